```python
import math
import jax, jax.numpy as jnp
from jax import lax
import numpy as np

D_MODEL = 4096
BATCH = 4
SEQ = 4096
DEPTH = 1

MEM_TOKENS = 256
D_MIX = D_MODEL
D_RWKV = D_MIX // 2
D_S5 = D_MIX - D_RWKV
RWKV_HEAD = 64
RWKV_HEADS = D_RWKV // RWKV_HEAD
D_DECAY_LORA = 96
D_AAA_LORA = 96
D_GATE_LORA = 256
N_DIR = 2
OFF_K = D_RWKV
OFF_V = 2 * D_RWKV
OFF_W = 3 * D_RWKV
OFF_A = OFF_W + N_DIR * D_DECAY_LORA
OFF_G = OFF_A + N_DIR * D_AAA_LORA
RWKV_COLS = OFF_G + D_GATE_LORA
S5_CH = 16
S5_GROUPS = D_S5 // S5_CH
S5_STATE = 64
IN_COLS = RWKV_COLS + D_S5
ATTN_HEADS = 4
ATTN_HEAD_DIM = D_MODEL // ATTN_HEADS
ATTN_SCALE = ATTN_HEAD_DIM ** -0.5
N_EXPERT_GROUPS = 8
EXPERTS_PER_GROUP = 8
N_EXPERTS = N_EXPERT_GROUPS * EXPERTS_PER_GROUP
TOP_K = 2
D_EXPERT = D_MODEL // 8
MOE_BLOCK = 128
RMS_EPS = 1e-6
GN_EPS = 64e-5
L2_EPS = 1e-12

kernel_name = "bidir_hymba_rwkv7_s5_hmoe"


def rms_norm(x, gain):
    xf = x.astype(jnp.float32)
    y = xf * lax.rsqrt(jnp.mean(xf * xf, axis=-1, keepdims=True) + RMS_EPS)
    return (y * gain.astype(jnp.float32)).astype(x.dtype)


def centred_shift(p, taps):
    prev = jnp.pad(p[:, :-1], ((0, 0), (1, 0), (0, 0)))
    nxt = jnp.pad(p[:, 1:], ((0, 0), (0, 1), (0, 0)))
    return taps[0] * prev + taps[1] * p + taps[2] * nxt


def _rwkv7_step(state, inp):
    r_t, w_t, kk_t, kka_t, v_t, k_t = inp
    sa = jnp.einsum('dbhij,dbhj->dbhi', state, -kk_t)
    state = (state * w_t[..., None, :] + sa[..., :, None] * kka_t[..., None, :]
             + v_t[..., :, None] * k_t[..., None, :])
    return state, jnp.einsum('dbhij,dbhj->dbhi', state, r_t)


def _to_scan(z):
    b, s = z.shape[0], z.shape[1]
    z = jnp.stack([z[:, :, 0], jnp.flip(z[:, :, 1], axis=1)], axis=2)
    return z.astype(jnp.float32).reshape(b, s, N_DIR, RWKV_HEADS, RWKV_HEAD).transpose(1, 2, 0, 3, 4)


def rwkv7_bidir(p, w0, w2, a0, a2, g2, k_k, k_a, r_k, ln_w, ln_b):
    B, S, _ = p.shape
    H, N = RWKV_HEADS, RWKV_HEAD
    r = p[..., :OFF_K]
    k = p[..., OFF_K:OFF_V]
    v = p[..., OFF_V:OFF_W]
    xw = p[..., OFF_W:OFF_A].reshape(B, S, N_DIR, D_DECAY_LORA)
    xa = p[..., OFF_A:OFF_G].reshape(B, S, N_DIR, D_AAA_LORA)
    xg = p[..., OFF_G:]
    w_log = -jax.nn.softplus(-(w0 + jnp.einsum('bsdr,drc->bsdc', jnp.tanh(xw), w2))) - 0.5
    decay = jnp.exp(-jnp.exp(w_log.astype(jnp.float32)))
    a = jax.nn.sigmoid((a0 + jnp.einsum('bsdr,drc->bsdc', xa, a2)).astype(jnp.float32))
    g = jax.nn.sigmoid(xg) @ g2
    kk = (k * k_k).astype(jnp.float32).reshape(B, S, H, N)
    kk = kk / jnp.maximum(jnp.sqrt(jnp.sum(kk * kk, axis=-1, keepdims=True)), L2_EPS)
    kk = kk.reshape(B, S, 1, D_RWKV)
    k_dir = k.astype(jnp.float32)[:, :, None] * (1.0 + (a - 1.0) * k_a.astype(jnp.float32))
    bshape = (B, S, N_DIR, D_RWKV)
    r_b = jnp.broadcast_to(r[:, :, None], bshape)
    v_b = jnp.broadcast_to(v[:, :, None], bshape)
    kk_b = jnp.broadcast_to(kk, bshape)
    xs = (_to_scan(r_b), _to_scan(decay), _to_scan(kk_b), _to_scan(kk_b * a),
          _to_scan(v_b), _to_scan(k_dir))
    state0 = jnp.zeros((N_DIR, B, H, N, N), jnp.float32)
    _, ys = lax.scan(_rwkv7_step, state0, xs)
    ys = ys.transpose(2, 0, 1, 3, 4)
    y = ys[:, :, 0] + jnp.flip(ys[:, :, 1], axis=1)
    mu = jnp.mean(y, axis=-1, keepdims=True)
    var = jnp.mean(jnp.square(y - mu), axis=-1, keepdims=True)
    yn = (y - mu) * lax.rsqrt(var + GN_EPS)
    yn = yn * ln_w.astype(jnp.float32).reshape(H, N) + ln_b.astype(jnp.float32).reshape(H, N)
    r_h = r.astype(jnp.float32).reshape(B, S, 1, H, N)
    kd_h = k_dir.reshape(B, S, N_DIR, H, N)
    bonus = jnp.sum(r_h * kd_h * r_k.astype(jnp.float32), axis=(2, 4))[..., None] \
        * v.astype(jnp.float32).reshape(B, S, H, N)
    out = (yn + bonus).reshape(B, S, D_RWKV) * g.astype(jnp.float32)
    return out.astype(p.dtype)


def _ssm_combine(left, right):
    a_l, b_l = left
    a_r, b_r = right
    return a_r * a_l, a_r * b_l + b_r


def s5_bidir(u, lam_re, lam_im, log_step, b_re, b_im, c_re, c_im, d_skip, glu_w, glu_b):
    B, S, _ = u.shape
    uf = u.astype(jnp.float32)
    ug = uf.reshape(B, S, S5_GROUPS, S5_CH)
    y = d_skip.astype(jnp.float32) * uf
    for d in range(N_DIR):
        lam = lax.complex(lam_re[d].astype(jnp.float32), lam_im[d].astype(jnp.float32))
        dt = jnp.exp(log_step[d].astype(jnp.float32))[:, None]
        lam_bar = jnp.exp(lam * dt)
        b = lax.complex(b_re[d].astype(jnp.float32), b_im[d].astype(jnp.float32))
        b_bar = ((lam_bar - 1.0) / lam)[..., None] * b
        bu = jnp.einsum('bsgc,gpc->bsgp', ug, b_bar)
        a_el = jnp.broadcast_to(lam_bar, (1, S, S5_GROUPS, S5_STATE))
        _, xs = lax.associative_scan(_ssm_combine, (a_el, bu), reverse=(d == 1), axis=1)
        c = lax.complex(c_re[d].astype(jnp.float32), c_im[d].astype(jnp.float32))
        y = y + jnp.real(jnp.einsum('bsgp,gcp->bsgc', xs, c)).reshape(B, S, D_S5)
    y = y.astype(u.dtype)
    return y * jax.nn.sigmoid(jax.nn.gelu(y) @ glu_w + glu_b)


def memory_cross_attention(xn, memn, w_q, w_k, w_v, w_o):
    B, S, _ = xn.shape
    M = memn.shape[1]
    q = (xn @ w_q).reshape(B, S, ATTN_HEADS, ATTN_HEAD_DIM)
    k = (memn @ w_k).reshape(B, M, ATTN_HEADS, ATTN_HEAD_DIM)
    v = (memn @ w_v).reshape(B, M, ATTN_HEADS, ATTN_HEAD_DIM)
    s = jnp.einsum('bshd,bmhd->bhsm', q, k).astype(jnp.float32) * ATTN_SCALE
    pr = jax.nn.softmax(s, axis=-1).astype(v.dtype)
    o = jnp.einsum('bhsm,bmhd->bshd', pr, v).reshape(B, S, D_MODEL)
    return o @ w_o


def hierarchical_moe(xn, grp_w, grp_b, exp_w, exp_b, w_gate, w_up, w_down):
    B, S, D = xn.shape
    T = B * S
    xt = xn.reshape(T, D)
    grp_prob = jax.nn.softmax((xt @ grp_w + grp_b).astype(jnp.float32), axis=-1)
    gp, gi = lax.top_k(grp_prob, 1)
    e_logits = (xt @ exp_w + exp_b).astype(jnp.float32).reshape(T, N_EXPERT_GROUPS, EXPERTS_PER_GROUP)
    sel = jnp.take_along_axis(e_logits, gi[:, :, None], axis=1)[:, 0]
    ev, ei = lax.top_k(sel, TOP_K)
    gates = jax.nn.softmax(ev, axis=-1) * gp
    expert_id = (gi * EXPERTS_PER_GROUP + ei).reshape(-1)
    token_id = jnp.repeat(jnp.arange(T, dtype=jnp.int32), TOP_K)
    weight = gates.reshape(-1)
    A = T * TOP_K
    n_blocks = -(-A // MOE_BLOCK) + N_EXPERTS
    cap = n_blocks * MOE_BLOCK
    order = jnp.argsort(expert_id)
    e_sorted = expert_id[order]
    counts = jnp.zeros((N_EXPERTS,), jnp.int32).at[expert_id].add(1)
    starts = jnp.cumsum(counts) - counts
    padded = ((counts + MOE_BLOCK - 1) // MOE_BLOCK) * MOE_BLOCK
    pstarts = jnp.cumsum(padded) - padded
    pends = pstarts + padded
    dest = pstarts[e_sorted] + (jnp.arange(A, dtype=jnp.int32) - starts[e_sorted])
    tok_buf = jnp.zeros((cap,), jnp.int32).at[dest].set(token_id[order])
    wt_buf = jnp.zeros((cap,), xt.dtype).at[dest].set(weight[order].astype(xt.dtype))
    block_start = jnp.arange(n_blocks, dtype=jnp.int32) * MOE_BLOCK
    block_expert = jnp.minimum(jnp.sum(block_start[:, None] >= pends[None, :], axis=1), N_EXPERTS - 1)
    x_buf = xt[tok_buf].reshape(n_blocks, MOE_BLOCK, D)

    def expert_block(args):
        xb, e = args
        hb = jax.nn.silu(xb @ w_gate[e]) * (xb @ w_up[e])
        return hb @ w_down[e]

    y_buf = lax.map(expert_block, (x_buf, block_expert)).reshape(cap, D)
    out = jax.ops.segment_sum(y_buf * wt_buf[:, None], tok_buf, num_segments=T)
    return out.reshape(B, S, D)


def setup_inputs(seed: int = 0) -> dict:
    key = jax.random.key(seed)
    ks = iter(jax.random.split(key, 48))

    def nrm(shape, scale):
        return jax.random.normal(next(ks), shape, jnp.float32) * scale

    def gain(shape):
        return 1.0 + nrm(shape, 0.02)

    L = DEPTH
    lam_im_base = math.pi * jnp.arange(S5_STATE, dtype=jnp.float32)
    return {
        "x": nrm((BATCH, SEQ, D_MODEL), 1.0),
        "mem": nrm((BATCH, MEM_TOKENS, D_MODEL), 1.0),
        "norm_mix": gain((L, D_MODEL)),
        "w_in": nrm((L, D_MODEL, IN_COLS), D_MODEL ** -0.5),
        "shift_taps": jnp.array([0.25, 0.5, 0.25], jnp.float32)[None, :, None] + nrm((L, 3, RWKV_COLS), 0.05),
        "rwkv_w0": jax.random.uniform(next(ks), (L, N_DIR, D_RWKV), jnp.float32, -6.0, -1.0),
        "rwkv_w2": nrm((L, N_DIR, D_DECAY_LORA, D_RWKV), D_DECAY_LORA ** -0.5),
        "rwkv_a0": nrm((L, N_DIR, D_RWKV), 0.1),
        "rwkv_a2": nrm((L, N_DIR, D_AAA_LORA, D_RWKV), D_AAA_LORA ** -0.5),
        "rwkv_g2": nrm((L, D_GATE_LORA, D_RWKV), D_GATE_LORA ** -0.5),
        "rwkv_k_k": 0.85 + nrm((L, D_RWKV), 0.02),
        "rwkv_k_a": gain((L, D_RWKV)),
        "rwkv_r_k": nrm((L, RWKV_HEADS, RWKV_HEAD), 0.1),
        "rwkv_ln_w": gain((L, D_RWKV)),
        "rwkv_ln_b": nrm((L, D_RWKV), 0.02),
        "s5_lam_re": -0.5 + nrm((L, N_DIR, S5_GROUPS, S5_STATE), 0.01),
        "s5_lam_im": lam_im_base + nrm((L, N_DIR, S5_GROUPS, S5_STATE), 0.01),
        "s5_log_step": jax.random.uniform(next(ks), (L, N_DIR, S5_GROUPS), jnp.float32,
                                          math.log(1e-3), math.log(1e-1)),
        "s5_b_re": nrm((L, N_DIR, S5_GROUPS, S5_STATE, S5_CH), (2 * S5_CH) ** -0.5),
        "s5_b_im": nrm((L, N_DIR, S5_GROUPS, S5_STATE, S5_CH), (2 * S5_CH) ** -0.5),
        "s5_c_re": nrm((L, N_DIR, S5_GROUPS, S5_CH, S5_STATE), (2 * S5_STATE) ** -0.5),
        "s5_c_im": nrm((L, N_DIR, S5_GROUPS, S5_CH, S5_STATE), (2 * S5_STATE) ** -0.5),
        "s5_d": nrm((L, D_S5), 1.0),
        "s5_glu_w": nrm((L, D_S5, D_S5), D_S5 ** -0.5),
        "s5_glu_b": nrm((L, D_S5), 0.01),
        "w_out": nrm((L, D_MIX, D_MODEL), D_MIX ** -0.5),
        "norm_attn": gain((L, D_MODEL)),
        "norm_mem": gain((L, D_MODEL)),
        "w_q": nrm((L, D_MODEL, D_MODEL), D_MODEL ** -0.5),
        "w_k": nrm((L, D_MODEL, D_MODEL), D_MODEL ** -0.5),
        "w_v": nrm((L, D_MODEL, D_MODEL), D_MODEL ** -0.5),
        "w_o": nrm((L, D_MODEL, D_MODEL), D_MODEL ** -0.5),
        "norm_ffn": gain((L, D_MODEL)),
        "router_grp_w": nrm((L, D_MODEL, N_EXPERT_GROUPS), D_MODEL ** -0.5),
        "router_grp_b": nrm((L, N_EXPERT_GROUPS), 0.01),
        "router_exp_w": nrm((L, D_MODEL, N_EXPERTS), D_MODEL ** -0.5),
        "router_exp_b": nrm((L, N_EXPERTS), 0.01),
        "exp_w_gate": nrm((L, N_EXPERTS, D_MODEL, D_EXPERT), D_MODEL ** -0.5),
        "exp_w_up": nrm((L, N_EXPERTS, D_MODEL, D_EXPERT), D_MODEL ** -0.5),
        "exp_w_down": nrm((L, N_EXPERTS, D_EXPERT, D_MODEL), D_EXPERT ** -0.5),
        "norm_final": gain((D_MODEL,)),
    }


def reference(x, mem, norm_mix, w_in, shift_taps,
              rwkv_w0, rwkv_w2, rwkv_a0, rwkv_a2, rwkv_g2, rwkv_k_k, rwkv_k_a, rwkv_r_k,
              rwkv_ln_w, rwkv_ln_b,
              s5_lam_re, s5_lam_im, s5_log_step, s5_b_re, s5_b_im, s5_c_re, s5_c_im,
              s5_d, s5_glu_w, s5_glu_b,
              w_out, norm_attn, norm_mem, w_q, w_k, w_v, w_o,
              norm_ffn, router_grp_w, router_grp_b, router_exp_w, router_exp_b,
              exp_w_gate, exp_w_up, exp_w_down, norm_final):
    for l in range(DEPTH):
        h = rms_norm(x, norm_mix[l])
        proj = h @ w_in[l]
        p_rwkv = centred_shift(proj[..., :RWKV_COLS], shift_taps[l])
        u_s5 = proj[..., RWKV_COLS:]
        y_rwkv = rwkv7_bidir(p_rwkv, rwkv_w0[l], rwkv_w2[l], rwkv_a0[l], rwkv_a2[l], rwkv_g2[l],
                             rwkv_k_k[l], rwkv_k_a[l], rwkv_r_k[l], rwkv_ln_w[l], rwkv_ln_b[l])
        y_s5 = s5_bidir(u_s5, s5_lam_re[l], s5_lam_im[l], s5_log_step[l], s5_b_re[l], s5_b_im[l],
                        s5_c_re[l], s5_c_im[l], s5_d[l], s5_glu_w[l], s5_glu_b[l])
        x = x + jnp.concatenate([y_rwkv, y_s5], axis=-1) @ w_out[l]
        x = x + memory_cross_attention(rms_norm(x, norm_attn[l]), rms_norm(mem, norm_mem[l]),
                                       w_q[l], w_k[l], w_v[l], w_o[l])
        x = x + hierarchical_moe(rms_norm(x, norm_ffn[l]), router_grp_w[l], router_grp_b[l],
                                 router_exp_w[l], router_exp_b[l],
                                 exp_w_gate[l], exp_w_up[l], exp_w_down[l])
    return rms_norm(x, norm_final)
```

```python
import functools
import math

import jax
import jax.numpy as jnp
from jax import lax
from jax.experimental import pallas as pl
from jax.experimental.pallas import tpu as pltpu

F32 = jnp.float32
BF16 = jnp.bfloat16

D_MODEL = 4096
D_RWKV = 2048
D_S5 = 2048
HEAD = 64
N_HEADS = D_RWKV // HEAD
LORA_PAD = 128
N_LORA = 4 * LORA_PAD + 256
N_MAIN = 3 * D_RWKV + D_S5
S5_CH = 16
S5_GROUPS = D_S5 // S5_CH
S5_STATE = 64
S5_L = 16
ATTN_HEADS = 4
ATTN_HEAD_DIM = D_MODEL // ATTN_HEADS
ATTN_SCALE = ATTN_HEAD_DIM ** -0.5
N_GROUPS = 8
EPG = 8
N_EXPERTS = N_GROUPS * EPG
TOP_K = 2
D_EXPERT = D_MODEL // 8
RMS_EPS = 1e-6
GN_EPS = 64e-5
L2_EPS = 1e-12

CHUNK = 64
SLAB = 512
MOE_BLK = 256
COMB_TM = 128
VMEM_LIMIT = 56 * 1024 * 1024


def _cparams(sem):
    return pltpu.CompilerParams(dimension_semantics=sem, vmem_limit_bytes=VMEM_LIMIT)


def _bdot(a, b):
    return jnp.dot(a.astype(BF16), b.astype(BF16), preferred_element_type=F32)


def _bdot_nt(a, b):
    return lax.dot_general(a.astype(BF16), b.astype(BF16), (((1,), (1,)), ((), ())),
                           preferred_element_type=F32)


def _split2(x):
    hi = x.astype(BF16)
    lo = (x - hi.astype(F32)).astype(BF16)
    return hi, lo


def _split3(x):
    hi = x.astype(BF16)
    r1 = x - hi.astype(F32)
    mid = r1.astype(BF16)
    lo = (r1 - mid.astype(F32)).astype(BF16)
    return hi, mid, lo


def _dot_exact_rhs(x, e):
    h, m, l = _split3(x)
    return (jnp.dot(h, e, preferred_element_type=F32) + jnp.dot(m, e, preferred_element_type=F32)
            + jnp.dot(l, e, preferred_element_type=F32))


def _dot_exact_lhs(e, x):
    h, m, l = _split3(x)
    return (jnp.dot(e, h, preferred_element_type=F32) + jnp.dot(e, m, preferred_element_type=F32)
            + jnp.dot(e, l, preferred_element_type=F32))


def _rms_kernel(x_ref, g_ref, o_ref):
    x = x_ref[...]
    ms = jnp.mean(x * x, axis=-1, keepdims=True)
    o_ref[...] = (x * lax.rsqrt(ms + RMS_EPS) * g_ref[...]).astype(o_ref.dtype)


def _rmsnorm(x, gain, out_dtype, tm=256):
    t, d = x.shape
    return pl.pallas_call(
        _rms_kernel,
        grid=(t // tm,),
        in_specs=[pl.BlockSpec((tm, d), lambda i: (i, 0)), pl.BlockSpec((1, d), lambda i: (0, 0))],
        out_specs=pl.BlockSpec((tm, d), lambda i: (i, 0)),
        out_shape=jax.ShapeDtypeStruct((t, d), out_dtype),
        compiler_params=_cparams(("parallel",)),
        name="rmsnorm",
    )(x, gain.reshape(1, d))


def _mm_kernel(a_ref, b_ref, o_ref):
    o_ref[...] = jnp.dot(a_ref[...], b_ref[...], preferred_element_type=F32).astype(o_ref.dtype)


def _mm_res_kernel(a_ref, b_ref, r_ref, o_ref):
    o_ref[...] = (jnp.dot(a_ref[...], b_ref[...], preferred_element_type=F32)
                  + r_ref[...]).astype(o_ref.dtype)


def _mm2_res_kernel(a1_ref, a2_ref, b_ref, r_ref, o_ref):
    k1 = a1_ref.shape[1]
    acc = jnp.dot(a1_ref[...], b_ref[:k1, :], preferred_element_type=F32)
    acc = acc + jnp.dot(a2_ref[...], b_ref[k1:, :], preferred_element_type=F32)
    o_ref[...] = (acc + r_ref[...]).astype(o_ref.dtype)


def _matmul(a, b, out_dtype, tm, tn, res=None, name="matmul"):
    m, k = a.shape
    n = b.shape[1]
    tm, tn = min(tm, m), min(tn, n)
    in_specs = [pl.BlockSpec((tm, k), lambda i, j: (i, 0)), pl.BlockSpec((k, tn), lambda i, j: (0, j))]
    args = [a, b]
    kern = _mm_kernel
    if res is not None:
        in_specs.append(pl.BlockSpec((tm, tn), lambda i, j: (i, j)))
        args.append(res)
        kern = _mm_res_kernel
    return pl.pallas_call(
        kern,
        grid=(m // tm, n // tn),
        in_specs=in_specs,
        out_specs=pl.BlockSpec((tm, tn), lambda i, j: (i, j)),
        out_shape=jax.ShapeDtypeStruct((m, n), out_dtype),
        compiler_params=_cparams(("parallel", "parallel")),
        name=name,
    )(*args)


def _matmul2_res(a1, a2, b, res, tm, tn):
    m, k1 = a1.shape
    k2 = a2.shape[1]
    n = b.shape[1]
    tm = min(tm, m)
    return pl.pallas_call(
        _mm2_res_kernel,
        grid=(m // tm, n // tn),
        in_specs=[pl.BlockSpec((tm, k1), lambda i, j: (i, 0)),
                  pl.BlockSpec((tm, k2), lambda i, j: (i, 0)),
                  pl.BlockSpec((k1 + k2, tn), lambda i, j: (0, j)),
                  pl.BlockSpec((tm, tn), lambda i, j: (i, j))],
        out_specs=pl.BlockSpec((tm, tn), lambda i, j: (i, j)),
        out_shape=jax.ShapeDtypeStruct((m, n), F32),
        compiler_params=_cparams(("parallel", "parallel")),
        name="w_out",
    )(a1, a2, b, res)


def _head_sum_bcast(x, e_ref, et_ref):
    s = _dot_exact_rhs(x, e_ref[...])
    return _dot_exact_rhs(s, et_ref[...])


def _shift3(x, hp, hn, taps, first, last):
    tm = x.shape[0]
    row = lax.broadcasted_iota(jnp.int32, x.shape, 0)
    prev_edge = jnp.where(first, 0.0, hp[7:8, :])
    next_edge = jnp.where(last, 0.0, hn[0:1, :])
    prev = jnp.where(row == 0, prev_edge, pltpu.roll(x, 1, 0))
    nxt = jnp.where(row == tm - 1, next_edge, pltpu.roll(x, tm - 1, 0))
    return taps[0:1, :] * prev + taps[1:2, :] * x + taps[2:3, :] * nxt


def _prep_kernel(seq_tiles,
                 r_ref, rp_ref, rn_ref, k_ref, kp_ref, kn_ref, v_ref, vp_ref, vn_ref,
                 lo_ref, lop_ref, lon_ref,
                 tr_ref, tk_ref, tv_ref, tl_ref,
                 w0_ref, w2_ref, a0_ref, a2_ref, g2_ref, kk_ref_p, e_ref, et_ref,
                 ro_ref, ko_ref, vo_ref, kko_ref, lw_ref, a_ref, g_ref):
    i = pl.program_id(0)
    first = (i % seq_tiles) == 0
    last = (i % seq_tiles) == seq_tiles - 1
    r = _shift3(r_ref[...], rp_ref[...], rn_ref[...], tr_ref[...], first, last)
    k = _shift3(k_ref[...], kp_ref[...], kn_ref[...], tk_ref[...], first, last)
    v = _shift3(v_ref[...], vp_ref[...], vn_ref[...], tv_ref[...], first, last)
    lo = _shift3(lo_ref[...], lop_ref[...], lon_ref[...], tl_ref[...], first, last)
    ro_ref[...] = r
    ko_ref[...] = k
    vo_ref[...] = v
    for d in range(2):
        xw = lo[:, d * LORA_PAD:(d + 1) * LORA_PAD]
        xa = lo[:, (2 + d) * LORA_PAD:(3 + d) * LORA_PAD]
        wl = w0_ref[d:d + 1, :] + _bdot(jnp.tanh(xw), w2_ref[d])
        w_log = -jax.nn.softplus(-wl) - 0.5
        lw_ref[d] = -jnp.exp(w_log)
        a_ref[d] = jax.nn.sigmoid(a0_ref[d:d + 1, :] + _bdot(xa, a2_ref[d]))
    xg = lo[:, 4 * LORA_PAD:]
    g_ref[...] = _bdot(jax.nn.sigmoid(xg), g2_ref[...])
    kk = k * kk_ref_p[...]
    ssq = _head_sum_bcast(kk * kk, e_ref, et_ref)
    kko_ref[...] = kk / jnp.maximum(jnp.sqrt(ssq), L2_EPS)


def _rwkv_prep(p_main, p_lora, seq, taps_r, taps_k, taps_v, taps_l, w0, w2p, a0, a2p, g2, k_k, e_mat, et_mat,
               tm=128):
    t = p_main.shape[0]
    nt8 = t // 8
    seq_tiles = seq // tm
    c = D_RWKV

    def main_spec(col):
        return [pl.BlockSpec((tm, c), lambda i, col=col: (i, col)),
                pl.BlockSpec((8, c), lambda i, col=col: (jnp.maximum(i * (tm // 8) - 1, 0), col)),
                pl.BlockSpec((8, c), lambda i, col=col: (jnp.minimum((i + 1) * (tm // 8), nt8 - 1), col))]

    lora_spec = [pl.BlockSpec((tm, N_LORA), lambda i: (i, 0)),
                 pl.BlockSpec((8, N_LORA), lambda i: (jnp.maximum(i * (tm // 8) - 1, 0), 0)),
                 pl.BlockSpec((8, N_LORA), lambda i: (jnp.minimum((i + 1) * (tm // 8), nt8 - 1), 0))]

    def full(shape):
        nd = len(shape)
        return pl.BlockSpec(shape, lambda i, nd=nd: (0,) * nd)

    in_specs = (main_spec(0) + main_spec(1) + main_spec(2) + lora_spec
                + [full((3, c)), full((3, c)), full((3, c)), full((3, N_LORA)),
                   full((2, c)), full((2, LORA_PAD, c)), full((2, c)), full((2, LORA_PAD, c)),
                   full((256, c)), full((1, c)), full((c, 128)), full((128, c))])
    row = pl.BlockSpec((tm, c), lambda i: (i, 0))
    row2 = pl.BlockSpec((2, tm, c), lambda i: (0, i, 0))
    sd = jax.ShapeDtypeStruct
    return pl.pallas_call(
        functools.partial(_prep_kernel, seq_tiles),
        grid=(t // tm,),
        in_specs=in_specs,
        out_specs=[row, row, row, row, row2, row2, row],
        out_shape=[sd((t, c), F32), sd((t, c), F32), sd((t, c), F32), sd((t, c), F32),
                   sd((2, t, c), F32), sd((2, t, c), F32), sd((t, c), F32)],
        compiler_params=_cparams(("parallel",)),
        name="rwkv_prep",
    )(p_main, p_main, p_main, p_main, p_main, p_main, p_main, p_main, p_main,
      p_lora, p_lora, p_lora,
      taps_r, taps_k, taps_v, taps_l, w0, w2p, a0, a2p, g2, k_k, e_mat, et_mat)


def _scan_kernel(r_ref, k_ref, v_ref, kk_ref, lw_ref, a_ref, ka_ref, y_ref, st_ref):
    d = pl.program_id(0)
    c = pl.program_id(3)

    @pl.when(c == 0)
    def _():
        st_ref[...] = jnp.zeros_like(st_ref)

    ti = lax.broadcasted_iota(jnp.int32, (CHUNK, CHUNK), 0)
    si = lax.broadcasted_iota(jnp.int32, (CHUNK, CHUNK), 1)
    ahead = (ti - si) * (1 - 2 * d)
    incl = ahead >= 0
    strict = ahead > 0
    eye = (si == ti).astype(F32)

    lw = lw_ref[...]
    cum_incl = _dot_exact_lhs(incl.astype(BF16), lw)
    tot = jnp.sum(lw, axis=0, keepdims=True)
    e_incl = jnp.exp(cum_incl)
    e_excl = jnp.exp(cum_incl - lw)
    e_ninc = jnp.exp(-cum_incl)
    g_tot = jnp.exp(tot)

    a = a_ref[...]
    kk = kk_ref[...]
    kd = k_ref[...] * (1.0 + (a - 1.0) * ka_ref[...])
    at_all = kk * e_excl
    rt_all = r_ref[...] * e_incl
    bt_all = (kk * a) * e_ninc
    kt_all = kd * e_ninc
    bh_all = bt_all * g_tot
    kh_all = kt_all * g_tot
    v_all = v_ref[...]

    outs = []
    for h in range(SLAB // HEAD):
        sl = slice(h * HEAD, (h + 1) * HEAD)
        ar = jnp.concatenate([at_all[:, sl], rt_all[:, sl]], axis=0).astype(BF16)
        g1 = _bdot_nt(ar, bt_all[:, sl])
        g2 = _bdot_nt(ar, kt_all[:, sl])
        aab = jnp.where(strict, g1[:CHUNK], 0.0)
        aak = jnp.where(strict, g2[:CHUNK], 0.0)
        arb = jnp.where(incl, g1[CHUNK:], 0.0)
        ark = jnp.where(incl, g2[CHUNK:], 0.0)
        m = -aab
        tinv = eye + m
        for _ in range(5):
            m = _bdot(m, m)
            tinv = tinv + _bdot(tinv, m)
        v_h = v_all[:, sl]
        s_h = st_ref[h]
        x1 = _bdot_nt(ar, s_h)
        x2 = _bdot(jnp.concatenate([aak, ark], axis=0), v_h)
        u = -_bdot(tinv, x1[:CHUNK] + x2[:CHUNK])
        o = x1[CHUNK:] + x2[CHUNK:] + _bdot(arb, u)
        uv = jnp.concatenate([u, v_h], axis=0).astype(BF16)
        bk = jnp.concatenate([bh_all[:, sl], kh_all[:, sl]], axis=0).astype(BF16)
        upd = lax.dot_general(uv, bk, (((0,), (0,)), ((), ())), preferred_element_type=F32)
        st_ref[h] = s_h * g_tot[:, sl] + upd
        outs.append(o)
    y_ref[...] = jnp.concatenate(outs, axis=1)


def _rwkv_scan(r, k, v, kk, lw, a, k_a, batch, seq):
    nc = seq // CHUNK
    ns = D_RWKV // SLAB
    r4 = r.reshape(batch, seq, D_RWKV)
    k4 = k.reshape(batch, seq, D_RWKV)
    v4 = v.reshape(batch, seq, D_RWKV)
    kk4 = kk.reshape(batch, seq, D_RWKV)
    lw5 = lw.reshape(2, batch, seq, D_RWKV)
    a5 = a.reshape(2, batch, seq, D_RWKV)

    def tchunk(d, c):
        return c + d * (nc - 1 - 2 * c)

    shared = pl.BlockSpec((None, CHUNK, SLAB), lambda d, b, s, c: (b, tchunk(d, c), s))
    perdir = pl.BlockSpec((None, None, CHUNK, SLAB), lambda d, b, s, c: (d, b, tchunk(d, c), s))
    y = pl.pallas_call(
        _scan_kernel,
        grid=(2, batch, ns, nc),
        in_specs=[shared, shared, shared, shared, perdir, perdir,
                  pl.BlockSpec((1, SLAB), lambda d, b, s, c: (0, s))],
        out_specs=perdir,
        out_shape=jax.ShapeDtypeStruct((2, batch, seq, D_RWKV), F32),
        scratch_shapes=[pltpu.VMEM((SLAB // HEAD, HEAD, HEAD), F32)],
        compiler_params=_cparams(("parallel", "parallel", "parallel", "arbitrary")),
        name="rwkv_scan",
    )(r4, k4, v4, kk4, lw5, a5, k_a)
    return y.reshape(2, batch * seq, D_RWKV)


def _post_kernel(y_ref, r_ref, k_ref, v_ref, a_ref, g_ref, ka_ref, rk_ref, lnw_ref, lnb_ref,
                 e_ref, et_ref, o_ref):
    y = y_ref[0] + y_ref[1]
    mu = _head_sum_bcast(y, e_ref, et_ref) * (1.0 / HEAD)
    yc = y - mu
    var = _head_sum_bcast(yc * yc, e_ref, et_ref) * (1.0 / HEAD)
    yn = yc * lax.rsqrt(var + GN_EPS) * lnw_ref[...] + lnb_ref[...]
    ka = ka_ref[...]
    k = k_ref[...]
    kd_sum = k * (1.0 + (a_ref[0] - 1.0) * ka) + k * (1.0 + (a_ref[1] - 1.0) * ka)
    bonus = _head_sum_bcast(r_ref[...] * kd_sum * rk_ref[...], e_ref, et_ref) * v_ref[...]
    o_ref[...] = ((yn + bonus) * g_ref[...]).astype(o_ref.dtype)


def _rwkv_post(y, r, k, v, a, g, k_a, r_k, ln_w, ln_b, e_mat, et_mat, tm=128):
    t, c = r.shape
    row = pl.BlockSpec((tm, c), lambda i: (i, 0))
    row2 = pl.BlockSpec((2, tm, c), lambda i: (0, i, 0))
    par = pl.BlockSpec((1, c), lambda i: (0, 0))
    return pl.pallas_call(
        _post_kernel,
        grid=(t // tm,),
        in_specs=[row2, row, row, row, row2, row, par, par, par, par,
                  pl.BlockSpec((c, 128), lambda i: (0, 0)), pl.BlockSpec((128, c), lambda i: (0, 0))],
        out_specs=row,
        out_shape=jax.ShapeDtypeStruct((t, c), BF16),
        compiler_params=_cparams(("parallel",)),
        name="rwkv_post",
    )(y, r, k, v, a, g, k_a, r_k, ln_w, ln_b, e_mat, et_mat)


def _s5_tables(lam_re, lam_im, log_step, b_re, b_im, c_re, c_im, d_skip):
    L, P, CH, G = S5_L, S5_STATE, S5_CH, S5_GROUPS
    lam = lax.complex(lam_re.astype(F32), lam_im.astype(F32))
    dt = jnp.exp(log_step.astype(F32))[..., None]
    lam_dt = lam * dt
    lam_bar = jnp.exp(lam_dt)
    b = lax.complex(b_re.astype(F32), b_im.astype(F32))
    b_bar = ((lam_bar - 1.0) / lam)[..., None] * b
    c = lax.complex(c_re.astype(F32), c_im.astype(F32))
    taus = jnp.arange(L + 1, dtype=F32)
    pows = jnp.exp(lam_dt[:, :, None, :] * taus[None, None, :, None])
    kern = jnp.real(jnp.einsum('dgop,dgtp,dgpi->dgtoi', c, pows[:, :, :L], b_bar,
                               precision=lax.Precision.HIGHEST))
    j = jnp.arange(L)[:, None]
    t = jnp.arange(L)[None, :]
    lag_f = jnp.clip(t - j, 0, L - 1)
    lag_b = jnp.clip(j - t, 0, L - 1)
    tf = jnp.where((t >= j)[None, :, :, None, None], kern[0][:, lag_f], 0.0)
    tb = jnp.where((t <= j)[None, :, :, None, None], kern[1][:, lag_b], 0.0)
    skip = (jnp.eye(L)[None, :, :, None, None] * jnp.eye(CH)[None, None, None]
            * d_skip.astype(F32).reshape(G, 1, 1, CH, 1))
    tfull = (tf + tb + skip).transpose(0, 1, 4, 2, 3).reshape(G, L * CH, L * CH)
    zf = pows[0][:, ::-1][:, 1:][:, :, :, None] * b_bar[0][:, None]
    zb = pows[1][:, :L][:, :, :, None] * b_bar[1][:, None]
    def rows_jc(z):
        return z.transpose(0, 1, 3, 2).reshape(G, L * CH, P)
    wz = jnp.concatenate([jnp.real(rows_jc(zf)), jnp.imag(rows_jc(zf)),
                          jnp.real(rows_jc(zb)), jnp.imag(rows_jc(zb))], axis=-1)
    yf = c[0][:, None] * pows[0][:, 1:][:, :, None, :]
    yb = c[1][:, None] * pows[1][:, ::-1][:, :L][:, :, None, :]
    def cols_tc(y):
        return y.transpose(0, 3, 1, 2).reshape(G, P, L * CH)
    wy = jnp.concatenate([jnp.real(cols_tc(yf)), -jnp.imag(cols_tc(yf)),
                          jnp.real(cols_tc(yb)), -jnp.imag(cols_tc(yb))], axis=1)
    steps = (L * (2.0 ** jnp.arange(8, dtype=F32)))
    lp = jnp.exp(lam_dt[:, :, None, :] * steps[None, None, :, None])
    re, im = jnp.real(lp), jnp.imag(lp)
    m1 = jnp.concatenate([re, re], axis=-1)
    m2 = jnp.concatenate([-im, im], axis=-1)
    mult = jnp.stack([m1[0], m2[0], m1[1], m2[1]], axis=2)
    return tfull.astype(BF16), wz.astype(BF16), wy.astype(BF16), mult


def _s5_kernel(n_chunks, u_ref, t_ref, wz_ref, wy_ref, mult_ref, y_ref):
    u = u_ref[...]
    rows = u.shape[0]
    z = jnp.dot(u, wz_ref[...], preferred_element_type=F32)
    xf = z[:, :128]
    xb = z[:, 128:]
    cidx = lax.broadcasted_iota(jnp.int32, (rows, 128), 0) % n_chunks
    levels = int(math.log2(n_chunks))
    for i in range(levels):
        sh = 1 << i
        mf1 = mult_ref[i, 0:1, :]
        mf2 = mult_ref[i, 1:2, :]
        mb1 = mult_ref[i, 2:3, :]
        mb2 = mult_ref[i, 3:4, :]
        sf = jnp.where(cidx >= sh, pltpu.roll(xf, sh, 0), 0.0)
        xf = xf + sf * mf1 + pltpu.roll(sf, 64, 1) * mf2
        sb = jnp.where(cidx < n_chunks - sh, pltpu.roll(xb, rows - sh, 0), 0.0)
        xb = xb + sb * mb1 + pltpu.roll(sb, 64, 1) * mb2
    x_prev = jnp.where(cidx >= 1, pltpu.roll(xf, 1, 0), 0.0)
    x_next = jnp.where(cidx < n_chunks - 1, pltpu.roll(xb, rows - 1, 0), 0.0)
    xin = jnp.concatenate([x_prev, x_next], axis=1).astype(BF16)
    y = jnp.dot(u, t_ref[...], preferred_element_type=F32)
    y = y + jnp.dot(xin, wy_ref[...], preferred_element_type=F32)
    y_ref[...] = y


def _s5_mix(u_g, tfull, wz, wy, mult, n_chunks):
    g, rows, w = u_g.shape
    mat = pl.BlockSpec((None, w, w), lambda i: (i, 0, 0))
    return pl.pallas_call(
        functools.partial(_s5_kernel, n_chunks),
        grid=(g,),
        in_specs=[pl.BlockSpec((None, rows, w), lambda i: (i, 0, 0)), mat, mat, mat,
                  pl.BlockSpec((None, 8, 4, 128), lambda i: (i, 0, 0, 0))],
        out_specs=pl.BlockSpec((None, rows, w), lambda i: (i, 0, 0)),
        out_shape=jax.ShapeDtypeStruct((g, rows, w), F32),
        compiler_params=_cparams(("parallel",)),
        name="s5_mix",
    )(u_g, tfull, wz, wy, mult)


def _glu_kernel(y_ref, w_ref, b_ref, o_ref):
    y = y_ref[...]
    z = jnp.dot(jax.nn.gelu(y).astype(BF16), w_ref[...], preferred_element_type=F32) + b_ref[...]
    o_ref[...] = (y * jax.nn.sigmoid(z)).astype(o_ref.dtype)


def _glu(y, w, b, tm=512):
    t, c = y.shape
    tm = min(tm, t)
    return pl.pallas_call(
        _glu_kernel,
        grid=(t // tm,),
        in_specs=[pl.BlockSpec((tm, c), lambda i: (i, 0)), pl.BlockSpec((c, c), lambda i: (0, 0)),
                  pl.BlockSpec((1, c), lambda i: (0, 0))],
        out_specs=pl.BlockSpec((tm, c), lambda i: (i, 0)),
        out_shape=jax.ShapeDtypeStruct((t, c), BF16),
        compiler_params=_cparams(("parallel",)),
        name="s5_glu",
    )(y, w, b)


def _attn_kernel(q_ref, k_ref, v_ref, o_ref):
    for h in range(ATTN_HEADS):
        sl = slice(h * ATTN_HEAD_DIM, (h + 1) * ATTN_HEAD_DIM)
        s = lax.dot_general(q_ref[:, sl], k_ref[:, sl], (((1,), (1,)), ((), ())),
                            preferred_element_type=F32) * ATTN_SCALE
        s = s - jnp.max(s, axis=-1, keepdims=True)
        p = jnp.exp(s)
        p = p / jnp.sum(p, axis=-1, keepdims=True)
        o_ref[:, sl] = jnp.dot(p.astype(BF16), v_ref[:, sl], preferred_element_type=F32).astype(o_ref.dtype)


def _attention(q, k, v, batch, seq, mem, tm=512):
    tm = min(tm, seq)
    q3 = q.reshape(batch, seq, D_MODEL)
    k3 = k.reshape(batch, mem, D_MODEL)
    v3 = v.reshape(batch, mem, D_MODEL)
    kv = pl.BlockSpec((None, mem, D_MODEL), lambda b, i: (b, 0, 0))
    o = pl.pallas_call(
        _attn_kernel,
        grid=(batch, seq // tm),
        in_specs=[pl.BlockSpec((None, tm, D_MODEL), lambda b, i: (b, i, 0)), kv, kv],
        out_specs=pl.BlockSpec((None, tm, D_MODEL), lambda b, i: (b, i, 0)),
        out_shape=jax.ShapeDtypeStruct((batch, seq, D_MODEL), BF16),
        compiler_params=_cparams(("parallel", "parallel")),
        name="cross_attn",
    )(q3, k3, v3)
    return o.reshape(batch * seq, D_MODEL)


def _router_kernel(x_ref, g_ref, whi_ref, wlo_ref, b_ref, xn_ref, info_ref, cnt_ref, carry_ref):
    i = pl.program_id(0)

    @pl.when(i == 0)
    def _():
        carry_ref[...] = jnp.zeros_like(carry_ref)

    x = x_ref[...]
    tm = x.shape[0]
    ms = jnp.mean(x * x, axis=-1, keepdims=True)
    xn = x * lax.rsqrt(ms + RMS_EPS) * g_ref[...]
    xn_ref[...] = xn
    xh, xl = _split2(xn)
    whi = whi_ref[...]
    logits = (jnp.dot(xh, whi, preferred_element_type=F32) + jnp.dot(xl, whi, preferred_element_type=F32)
              + jnp.dot(xh, wlo_ref[...], preferred_element_type=F32)) + b_ref[...]
    li = lax.broadcasted_iota(jnp.int32, logits.shape, 1)
    neg = jnp.float32(-jnp.inf)
    is_g = li < N_GROUPS
    gl = jnp.where(is_g, logits, neg)
    gm = jnp.max(gl, axis=-1, keepdims=True)
    gi = jnp.min(jnp.where(is_g & (gl == gm), li, 128), axis=-1, keepdims=True)
    gp = 1.0 / jnp.sum(jnp.where(is_g, jnp.exp(gl - gm), 0.0), axis=-1, keepdims=True)
    lo_lane = N_GROUPS + EPG * gi
    sel = (li >= lo_lane) & (li < lo_lane + EPG)
    l1 = jnp.where(sel, logits, neg)
    e1 = jnp.max(l1, axis=-1, keepdims=True)
    i1 = jnp.min(jnp.where(sel & (l1 == e1), li, 128), axis=-1, keepdims=True)
    sel2 = sel & (li != i1)
    l2 = jnp.where(sel2, logits, neg)
    e2 = jnp.max(l2, axis=-1, keepdims=True)
    i2 = jnp.min(jnp.where(sel2 & (l2 == e2), li, 128), axis=-1, keepdims=True)
    ex = jnp.exp(e2 - e1)
    w1 = gp / (1.0 + ex)
    w2 = gp * ex / (1.0 + ex)
    x1 = i1 - N_GROUPS
    x2 = i2 - N_GROUPS
    oh1 = li == x1
    oh2 = li == x2
    oh = (oh1 | oh2).astype(BF16)
    ri = lax.broadcasted_iota(jnp.int32, (tm, tm), 0)
    ci = lax.broadcasted_iota(jnp.int32, (tm, tm), 1)
    before = jnp.dot((ci < ri).astype(BF16), oh, preferred_element_type=F32) + carry_ref[0:1, :]
    r1 = jnp.sum(jnp.where(oh1, before, 0.0), axis=-1, keepdims=True)
    r2 = jnp.sum(jnp.where(oh2, before, 0.0), axis=-1, keepdims=True)
    new_carry = carry_ref[0:1, :] + jnp.sum(oh.astype(F32), axis=0, keepdims=True)
    carry_ref[...] = jnp.broadcast_to(new_carry, carry_ref.shape)
    cnt_ref[...] = jnp.broadcast_to(new_carry, cnt_ref.shape)
    info = jnp.where(li == 0, x1.astype(F32), 0.0)
    info = jnp.where(li == 1, x2.astype(F32), info)
    info = jnp.where(li == 2, w1, info)
    info = jnp.where(li == 3, w2, info)
    info = jnp.where(li == 4, r1, info)
    info = jnp.where(li == 5, r2, info)
    info_ref[...] = info


def _router(x, gain, w_hi, w_lo, bias, tm=256):
    t, d = x.shape
    sd = jax.ShapeDtypeStruct
    return pl.pallas_call(
        _router_kernel,
        grid=(t // tm,),
        in_specs=[pl.BlockSpec((tm, d), lambda i: (i, 0)), pl.BlockSpec((1, d), lambda i: (0, 0)),
                  pl.BlockSpec((d, 128), lambda i: (0, 0)), pl.BlockSpec((d, 128), lambda i: (0, 0)),
                  pl.BlockSpec((1, 128), lambda i: (0, 0))],
        out_specs=[pl.BlockSpec((tm, d), lambda i: (i, 0)), pl.BlockSpec((tm, 128), lambda i: (i, 0)),
                   pl.BlockSpec((8, 128), lambda i: (0, 0))],
        out_shape=[sd((t, d), F32), sd((t, 128), F32), sd((8, 128), F32)],
        scratch_shapes=[pltpu.VMEM((8, 128), F32)],
        compiler_params=_cparams(("arbitrary",)),
        name="moe_router",
    )(x, gain.reshape(1, d), w_hi, w_lo, bias)


def _gather_rows(src_hbm, idx_ref, dst_ref, sem, n_rows):
    def body(j, carry):
        pltpu.make_async_copy(src_hbm.at[pl.ds(idx_ref[0, j], 1), :], dst_ref.at[pl.ds(j, 1), :], sem).start()
        return carry
    lax.fori_loop(0, n_rows, body, 0)


def _wait_rows(src_hbm, dst_ref, sem, n_rows):
    pltpu.make_async_copy(src_hbm.at[pl.ds(0, n_rows), :], dst_ref, sem).wait()


def _expert_kernel(be_ref, nused_ref, tok_ref, tokn_ref, x_hbm, wg_ref, wu_ref, wd_ref, y_ref, xbuf, sem):
    i = pl.program_id(0)
    n_used = nused_ref[0]
    slot = i % 2

    @pl.when(i == 0)
    def _():
        _gather_rows(x_hbm, tok_ref, xbuf.at[0], sem.at[0], MOE_BLK)

    @pl.when(i + 1 < n_used)
    def _():
        _gather_rows(x_hbm, tokn_ref, xbuf.at[1 - slot], sem.at[1 - slot], MOE_BLK)

    @pl.when(i < n_used)
    def _():
        _wait_rows(x_hbm, xbuf.at[slot], sem.at[slot], MOE_BLK)
        xb = xbuf[slot].astype(BF16)
        hg = jnp.dot(xb, wg_ref[...], preferred_element_type=F32)
        hu = jnp.dot(xb, wu_ref[...], preferred_element_type=F32)
        h = (jax.nn.silu(hg) * hu).astype(BF16)
        y_ref[...] = jnp.dot(h, wd_ref[...], preferred_element_type=F32)

    @pl.when(i >= n_used)
    def _():
        y_ref[...] = jnp.zeros_like(y_ref)


def _experts(block_expert, n_used, tok3, xn, wg, wu, wd):
    nb = tok3.shape[0]
    d = xn.shape[1]
    tok_spec = pl.BlockSpec((None, 1, MOE_BLK), lambda i, be, nu: (i, 0, 0), memory_space=pltpu.SMEM)
    tokn_spec = pl.BlockSpec((None, 1, MOE_BLK), lambda i, be, nu: (jnp.minimum(i + 1, nb - 1), 0, 0),
                             memory_space=pltpu.SMEM)
    grid_spec = pltpu.PrefetchScalarGridSpec(
        num_scalar_prefetch=2,
        grid=(nb,),
        in_specs=[tok_spec, tokn_spec, pl.BlockSpec(memory_space=pl.ANY),
                  pl.BlockSpec((None, d, D_EXPERT), lambda i, be, nu: (be[i], 0, 0)),
                  pl.BlockSpec((None, d, D_EXPERT), lambda i, be, nu: (be[i], 0, 0)),
                  pl.BlockSpec((None, D_EXPERT, d), lambda i, be, nu: (be[i], 0, 0))],
        out_specs=pl.BlockSpec((MOE_BLK, d), lambda i, be, nu: (i, 0)),
        scratch_shapes=[pltpu.VMEM((2, MOE_BLK, d), F32), pltpu.SemaphoreType.DMA((2,))],
    )
    return pl.pallas_call(
        _expert_kernel,
        grid_spec=grid_spec,
        out_shape=jax.ShapeDtypeStruct((nb * MOE_BLK, d), F32),
        compiler_params=_cparams(("arbitrary",)),
        name="moe_experts",
    )(block_expert, n_used, tok3, tok3, xn, wg, wu, wd)


def _combine_kernel(d1_ref, d1n_ref, d2_ref, d2n_ref, x_ref, info_ref, g_ref, y_hbm, o_ref, ybuf, sem):
    i = pl.program_id(0)
    n = pl.num_programs(0)
    slot = i % 2

    @pl.when(i == 0)
    def _():
        _gather_rows(y_hbm, d1_ref, ybuf.at[0, 0], sem.at[0], COMB_TM)
        _gather_rows(y_hbm, d2_ref, ybuf.at[0, 1], sem.at[0], COMB_TM)

    @pl.when(i + 1 < n)
    def _():
        _gather_rows(y_hbm, d1n_ref, ybuf.at[1 - slot, 0], sem.at[1 - slot], COMB_TM)
        _gather_rows(y_hbm, d2n_ref, ybuf.at[1 - slot, 1], sem.at[1 - slot], COMB_TM)

    _wait_rows(y_hbm, ybuf.at[slot, 0], sem.at[slot], COMB_TM)
    _wait_rows(y_hbm, ybuf.at[slot, 1], sem.at[slot], COMB_TM)
    info = info_ref[...]
    w1 = info[:, 2:3]
    w2 = info[:, 3:4]
    x = x_ref[...] + ybuf[slot, 0] * w1 + ybuf[slot, 1] * w2
    ms = jnp.mean(x * x, axis=-1, keepdims=True)
    o_ref[...] = x * lax.rsqrt(ms + RMS_EPS) * g_ref[...]


def _combine(dest1, dest2, x, info, gain, y_buf):
    t, d = x.shape
    nt = t // COMB_TM
    d1 = dest1.reshape(nt, 1, COMB_TM)
    d2 = dest2.reshape(nt, 1, COMB_TM)
    cur = pl.BlockSpec((None, 1, COMB_TM), lambda i: (i, 0, 0), memory_space=pltpu.SMEM)
    nxt = pl.BlockSpec((None, 1, COMB_TM), lambda i: (jnp.minimum(i + 1, nt - 1), 0, 0),
                       memory_space=pltpu.SMEM)
    return pl.pallas_call(
        _combine_kernel,
        grid=(nt,),
        in_specs=[cur, nxt, cur, nxt,
                  pl.BlockSpec((COMB_TM, d), lambda i: (i, 0)),
                  pl.BlockSpec((COMB_TM, 128), lambda i: (i, 0)),
                  pl.BlockSpec((1, d), lambda i: (0, 0)),
                  pl.BlockSpec(memory_space=pl.ANY)],
        out_specs=pl.BlockSpec((COMB_TM, d), lambda i: (i, 0)),
        out_shape=jax.ShapeDtypeStruct((t, d), F32),
        scratch_shapes=[pltpu.VMEM((2, 2, COMB_TM, d), F32), pltpu.SemaphoreType.DMA((2,))],
        compiler_params=_cparams(("arbitrary",)),
        name="moe_combine",
    )(d1, d1, d2, d2, x, info, gain.reshape(1, d), y_buf)


def _head_matrices():
    lane = jnp.arange(D_RWKV) // HEAD
    e = (lane[:, None] == jnp.arange(128)[None, :]).astype(BF16)
    return e, e.T


def _pad_lora_rows(w):
    return jnp.pad(w, ((0, 0), (0, LORA_PAD - w.shape[1]), (0, 0)))


def _layer(x, mem, norm_mix, w_in, shift_taps, rwkv_w0, rwkv_w2, rwkv_a0, rwkv_a2, rwkv_g2, rwkv_k_k,
           rwkv_k_a, rwkv_r_k, rwkv_ln_w, rwkv_ln_b, s5_lam_re, s5_lam_im, s5_log_step, s5_b_re, s5_b_im,
           s5_c_re, s5_c_im, s5_d, s5_glu_w, s5_glu_b, w_out, norm_attn, norm_mem, w_q, w_k, w_v, w_o,
           norm_ffn, router_grp_w, router_grp_b, router_exp_w, router_exp_b, exp_w_gate, exp_w_up,
           exp_w_down, out_gain):
    batch, seq, d = x.shape
    t = batch * seq
    mem_n = mem.shape[1]
    xt = x.reshape(t, d)
    c3 = 3 * D_RWKV
    off_g = c3 + 4 * 96

    def pad_cols(w, lo, width):
        return jnp.pad(w[:, lo:lo + width], ((0, 0), (0, LORA_PAD - width)))

    w_main = jnp.concatenate([w_in[:, :c3], w_in[:, off_g + 256:]], axis=1).astype(BF16)
    w_lora = jnp.concatenate([pad_cols(w_in, c3 + i * 96, 96) for i in range(4)]
                             + [w_in[:, off_g:off_g + 256]], axis=1).astype(BF16)
    taps_l = jnp.concatenate([pad_cols(shift_taps, c3 + i * 96, 96) for i in range(4)]
                             + [shift_taps[:, off_g:off_g + 256]], axis=1)
    h = _rmsnorm(xt, norm_mix, BF16)
    p_main = _matmul(h, w_main, F32, 1024, 512, name="w_in_main")
    p_lora = _matmul(h, w_lora, F32, 1024, N_LORA, name="w_in_lora")

    e_mat, et_mat = _head_matrices()
    k_a = rwkv_k_a.reshape(1, D_RWKV)
    r, k, v, kk, lw, a, g = _rwkv_prep(
        p_main, p_lora, seq, shift_taps[:, :D_RWKV], shift_taps[:, D_RWKV:2 * D_RWKV],
        shift_taps[:, 2 * D_RWKV:c3], taps_l, rwkv_w0, _pad_lora_rows(rwkv_w2).astype(BF16), rwkv_a0,
        _pad_lora_rows(rwkv_a2).astype(BF16), rwkv_g2.astype(BF16), rwkv_k_k.reshape(1, D_RWKV),
        e_mat, et_mat)
    y_scan = _rwkv_scan(r, k, v, kk, lw, a, k_a, batch, seq)
    y_rwkv = _rwkv_post(y_scan, r, k, v, a, g, k_a, rwkv_r_k.reshape(1, D_RWKV),
                        rwkv_ln_w.reshape(1, D_RWKV), rwkv_ln_b.reshape(1, D_RWKV), e_mat, et_mat)

    n_chunks = seq // S5_L
    tfull, wz, wy, mult = _s5_tables(s5_lam_re, s5_lam_im, s5_log_step, s5_b_re, s5_b_im,
                                     s5_c_re, s5_c_im, s5_d)
    u = p_main[:, c3:].astype(BF16)
    u_g = u.reshape(t // S5_L, S5_L, S5_GROUPS, S5_CH).transpose(2, 0, 1, 3).reshape(
        S5_GROUPS, t // S5_L, S5_L * S5_CH)
    y_g = _s5_mix(u_g, tfull, wz, wy, mult, n_chunks)
    y_s5 = y_g.reshape(S5_GROUPS, t // S5_L, S5_L, S5_CH).transpose(1, 2, 0, 3).reshape(t, D_S5)
    y_glu = _glu(y_s5, s5_glu_w.astype(BF16), s5_glu_b.reshape(1, D_S5))

    x1 = _matmul2_res(y_rwkv, y_glu, w_out.astype(BF16), xt, 1024, 512)

    xn = _rmsnorm(x1, norm_attn, BF16)
    memn = _rmsnorm(mem.reshape(batch * mem_n, d), norm_mem, BF16)
    q = _matmul(xn, w_q.astype(BF16), BF16, 1024, 512, name="w_q")
    wkv = jnp.concatenate([w_k, w_v], axis=1).astype(BF16)
    kv = _matmul(memn, wkv, BF16, 1024, 512, name="w_kv")
    o = _attention(q, kv[:, :d], kv[:, d:], batch, seq, mem_n)
    x2 = _matmul(o, w_o.astype(BF16), F32, 1024, 512, res=x1, name="w_o")

    w_r = jnp.concatenate([router_grp_w, router_exp_w,
                           jnp.zeros((d, 128 - N_GROUPS - N_EXPERTS), F32)], axis=1)
    w_r_hi = w_r.astype(BF16)
    w_r_lo = (w_r - w_r_hi.astype(F32)).astype(BF16)
    b_r = jnp.concatenate([router_grp_b, router_exp_b,
                           jnp.zeros((128 - N_GROUPS - N_EXPERTS,), F32)]).reshape(1, 128)
    xn3, info, cnt = _router(x2, norm_ffn, w_r_hi, w_r_lo, b_r)
    eid = info[:, 0:2].astype(jnp.int32)
    rank = info[:, 4:6].astype(jnp.int32)
    counts = cnt[0, :N_EXPERTS].astype(jnp.int32)
    nblk = (counts + MOE_BLK - 1) // MOE_BLK
    bstart = jnp.cumsum(nblk) - nblk
    n_used = jnp.sum(nblk)
    dest = bstart[eid] * MOE_BLK + rank
    nb = (t * TOP_K) // MOE_BLK + N_EXPERTS
    tok_buf = jnp.zeros((nb * MOE_BLK,), jnp.int32).at[dest.reshape(-1)].set(
        jnp.repeat(jnp.arange(t, dtype=jnp.int32), TOP_K))
    blk = jnp.arange(nb, dtype=jnp.int32)
    block_expert = jnp.sum(blk[:, None] >= (bstart + nblk)[None, :], axis=1).astype(jnp.int32)
    last_e = jnp.max(jnp.where(nblk > 0, jnp.arange(N_EXPERTS), 0)).astype(jnp.int32)
    block_expert = jnp.minimum(block_expert, last_e)
    y_buf = _experts(block_expert, n_used.reshape(1).astype(jnp.int32), tok_buf.reshape(nb, 1, MOE_BLK),
                     xn3, exp_w_gate.astype(BF16), exp_w_up.astype(BF16), exp_w_down.astype(BF16))
    out = _combine(dest[:, 0], dest[:, 1], x2, info, out_gain, y_buf)
    return out.reshape(batch, seq, d)


def kernel(x, mem, norm_mix, w_in, shift_taps, rwkv_w0, rwkv_w2, rwkv_a0, rwkv_a2, rwkv_g2, rwkv_k_k, rwkv_k_a, rwkv_r_k, rwkv_ln_w, rwkv_ln_b, s5_lam_re, s5_lam_im, s5_log_step, s5_b_re, s5_b_im, s5_c_re, s5_c_im, s5_d, s5_glu_w, s5_glu_b, w_out, norm_attn, norm_mem, w_q, w_k, w_v, w_o, norm_ffn, router_grp_w, router_grp_b, router_exp_w, router_exp_b, exp_w_gate, exp_w_up, exp_w_down, norm_final):
    return _layer(x, mem, norm_mix[0], w_in[0], shift_taps[0], rwkv_w0[0], rwkv_w2[0], rwkv_a0[0],
                  rwkv_a2[0], rwkv_g2[0], rwkv_k_k[0], rwkv_k_a[0], rwkv_r_k[0], rwkv_ln_w[0],
                  rwkv_ln_b[0], s5_lam_re[0], s5_lam_im[0], s5_log_step[0], s5_b_re[0], s5_b_im[0],
                  s5_c_re[0], s5_c_im[0], s5_d[0], s5_glu_w[0], s5_glu_b[0], w_out[0], norm_attn[0],
                  norm_mem[0], w_q[0], w_k[0], w_v[0], w_o[0], norm_ffn[0], router_grp_w[0],
                  router_grp_b[0], router_exp_w[0], router_exp_b[0], exp_w_gate[0], exp_w_up[0],
                  exp_w_down[0], norm_final)
```

```python
import functools
import math

import jax
import jax.numpy as jnp
from jax import lax
from jax.experimental import pallas as pl
from jax.experimental.pallas import tpu as pltpu

F32 = jnp.float32
BF16 = jnp.bfloat16

D_MODEL = 4096
D_RWKV = 2048
D_S5 = 2048
HEAD = 64
N_HEADS = D_RWKV // HEAD
LORA_PAD = 128
N_LORA = 4 * LORA_PAD + 256
N_MAIN = 3 * D_RWKV + D_S5
S5_CH = 16
S5_GROUPS = D_S5 // S5_CH
S5_STATE = 64
S5_L = 16
ATTN_HEADS = 4
ATTN_HEAD_DIM = D_MODEL // ATTN_HEADS
ATTN_SCALE = ATTN_HEAD_DIM ** -0.5
N_GROUPS = 8
EPG = 8
N_EXPERTS = N_GROUPS * EPG
TOP_K = 2
D_EXPERT = D_MODEL // 8
RMS_EPS = 1e-6
GN_EPS = 64e-5
L2_EPS = 1e-12

CHUNK = 64
SLAB = 2048
MOE_BLK = 256
COMB_TM = 128
VMEM_LIMIT = 56 * 1024 * 1024


def _cparams(sem):
    return pltpu.CompilerParams(dimension_semantics=sem, vmem_limit_bytes=VMEM_LIMIT)


def _bdot(a, b):
    return jnp.dot(a.astype(BF16), b.astype(BF16), preferred_element_type=F32)


def _bdot_nt(a, b):
    return lax.dot_general(a.astype(BF16), b.astype(BF16), (((1,), (1,)), ((), ())),
                           preferred_element_type=F32)


def _split2(x):
    hi = x.astype(BF16)
    lo = (x - hi.astype(F32)).astype(BF16)
    return hi, lo


def _split3(x):
    hi = x.astype(BF16)
    r1 = x - hi.astype(F32)
    mid = r1.astype(BF16)
    lo = (r1 - mid.astype(F32)).astype(BF16)
    return hi, mid, lo


def _dot_exact_rhs(x, e):
    h, m, l = _split3(x)
    return (jnp.dot(h, e, preferred_element_type=F32) + jnp.dot(m, e, preferred_element_type=F32)
            + jnp.dot(l, e, preferred_element_type=F32))


def _dot_exact_lhs(e, x):
    h, m, l = _split3(x)
    return (jnp.dot(e, h, preferred_element_type=F32) + jnp.dot(e, m, preferred_element_type=F32)
            + jnp.dot(e, l, preferred_element_type=F32))


def _rms_kernel(x_ref, g_ref, o_ref):
    x = x_ref[...]
    ms = jnp.mean(x * x, axis=-1, keepdims=True)
    o_ref[...] = (x * lax.rsqrt(ms + RMS_EPS) * g_ref[...]).astype(o_ref.dtype)


def _rmsnorm(x, gain, out_dtype, tm=256):
    t, d = x.shape
    return pl.pallas_call(
        _rms_kernel,
        grid=(t // tm,),
        in_specs=[pl.BlockSpec((tm, d), lambda i: (i, 0)), pl.BlockSpec((1, d), lambda i: (0, 0))],
        out_specs=pl.BlockSpec((tm, d), lambda i: (i, 0)),
        out_shape=jax.ShapeDtypeStruct((t, d), out_dtype),
        compiler_params=_cparams(("parallel",)),
        name="rmsnorm",
    )(x, gain.reshape(1, d))


def _mm_kernel(a_ref, b_ref, o_ref):
    o_ref[...] = jnp.dot(a_ref[...], b_ref[...], preferred_element_type=F32).astype(o_ref.dtype)


def _mm_res_kernel(a_ref, b_ref, r_ref, o_ref):
    o_ref[...] = (jnp.dot(a_ref[...], b_ref[...], preferred_element_type=F32)
                  + r_ref[...]).astype(o_ref.dtype)


def _mm2_res_kernel(a1_ref, a2_ref, b_ref, r_ref, o_ref):
    k1 = a1_ref.shape[1]
    acc = jnp.dot(a1_ref[...], b_ref[:k1, :], preferred_element_type=F32)
    acc = acc + jnp.dot(a2_ref[...], b_ref[k1:, :], preferred_element_type=F32)
    o_ref[...] = (acc + r_ref[...]).astype(o_ref.dtype)


def _matmul(a, b, out_dtype, tm, tn, res=None, name="matmul"):
    m, k = a.shape
    n = b.shape[1]
    tm, tn = min(tm, m), min(tn, n)
    in_specs = [pl.BlockSpec((tm, k), lambda i, j: (i, 0)), pl.BlockSpec((k, tn), lambda i, j: (0, j))]
    args = [a, b]
    kern = _mm_kernel
    if res is not None:
        in_specs.append(pl.BlockSpec((tm, tn), lambda i, j: (i, j)))
        args.append(res)
        kern = _mm_res_kernel
    return pl.pallas_call(
        kern,
        grid=(m // tm, n // tn),
        in_specs=in_specs,
        out_specs=pl.BlockSpec((tm, tn), lambda i, j: (i, j)),
        out_shape=jax.ShapeDtypeStruct((m, n), out_dtype),
        compiler_params=_cparams(("parallel", "parallel")),
        name=name,
    )(*args)


def _matmul2_res(a1, a2, b, res, tm, tn):
    m, k1 = a1.shape
    k2 = a2.shape[1]
    n = b.shape[1]
    tm = min(tm, m)
    return pl.pallas_call(
        _mm2_res_kernel,
        grid=(m // tm, n // tn),
        in_specs=[pl.BlockSpec((tm, k1), lambda i, j: (i, 0)),
                  pl.BlockSpec((tm, k2), lambda i, j: (i, 0)),
                  pl.BlockSpec((k1 + k2, tn), lambda i, j: (0, j)),
                  pl.BlockSpec((tm, tn), lambda i, j: (i, j))],
        out_specs=pl.BlockSpec((tm, tn), lambda i, j: (i, j)),
        out_shape=jax.ShapeDtypeStruct((m, n), F32),
        compiler_params=_cparams(("parallel", "parallel")),
        name="w_out",
    )(a1, a2, b, res)


def _head_sum_bcast(x, e_ref, et_ref):
    s = _dot_exact_rhs(x, e_ref[...])
    return _dot_exact_rhs(s, et_ref[...])


def _shift3(x, hp, hn, taps, first, last):
    tm = x.shape[0]
    row = lax.broadcasted_iota(jnp.int32, x.shape, 0)
    prev_edge = jnp.where(first, 0.0, hp[7:8, :])
    next_edge = jnp.where(last, 0.0, hn[0:1, :])
    prev = jnp.where(row == 0, prev_edge, pltpu.roll(x, 1, 0))
    nxt = jnp.where(row == tm - 1, next_edge, pltpu.roll(x, tm - 1, 0))
    return taps[0:1, :] * prev + taps[1:2, :] * x + taps[2:3, :] * nxt


def _prep_kernel(seq_tiles,
                 r_ref, rp_ref, rn_ref, k_ref, kp_ref, kn_ref, v_ref, vp_ref, vn_ref,
                 lo_ref, lop_ref, lon_ref,
                 tr_ref, tk_ref, tv_ref, tl_ref,
                 w0_ref, w2_ref, a0_ref, a2_ref, g2_ref, kk_ref_p, e_ref, et_ref,
                 ro_ref, ko_ref, vo_ref, kko_ref, lw_ref, a_ref, g_ref):
    i = pl.program_id(0)
    first = (i % seq_tiles) == 0
    last = (i % seq_tiles) == seq_tiles - 1
    r = _shift3(r_ref[...], rp_ref[...], rn_ref[...], tr_ref[...], first, last)
    k = _shift3(k_ref[...], kp_ref[...], kn_ref[...], tk_ref[...], first, last)
    v = _shift3(v_ref[...], vp_ref[...], vn_ref[...], tv_ref[...], first, last)
    lo = _shift3(lo_ref[...], lop_ref[...], lon_ref[...], tl_ref[...], first, last)
    ro_ref[...] = r
    ko_ref[...] = k
    vo_ref[...] = v
    for d in range(2):
        xw = lo[:, d * LORA_PAD:(d + 1) * LORA_PAD]
        xa = lo[:, (2 + d) * LORA_PAD:(3 + d) * LORA_PAD]
        wl = w0_ref[d:d + 1, :] + _bdot(jnp.tanh(xw), w2_ref[d])
        w_log = -jax.nn.softplus(-wl) - 0.5
        lw_ref[d] = -jnp.exp(w_log)
        a_ref[d] = jax.nn.sigmoid(a0_ref[d:d + 1, :] + _bdot(xa, a2_ref[d]))
    xg = lo[:, 4 * LORA_PAD:]
    g_ref[...] = _bdot(jax.nn.sigmoid(xg), g2_ref[...])
    kk = k * kk_ref_p[...]
    ssq = _head_sum_bcast(kk * kk, e_ref, et_ref)
    kko_ref[...] = kk / jnp.maximum(jnp.sqrt(ssq), L2_EPS)


def _rwkv_prep(p_main, p_lora, seq, taps_r, taps_k, taps_v, taps_l, w0, w2p, a0, a2p, g2, k_k, e_mat, et_mat,
               tm=128):
    t = p_main.shape[0]
    nt8 = t // 8
    seq_tiles = seq // tm
    c = D_RWKV

    def main_spec(col):
        return [pl.BlockSpec((tm, c), lambda i, col=col: (i, col)),
                pl.BlockSpec((8, c), lambda i, col=col: (jnp.maximum(i * (tm // 8) - 1, 0), col)),
                pl.BlockSpec((8, c), lambda i, col=col: (jnp.minimum((i + 1) * (tm // 8), nt8 - 1), col))]

    lora_spec = [pl.BlockSpec((tm, N_LORA), lambda i: (i, 0)),
                 pl.BlockSpec((8, N_LORA), lambda i: (jnp.maximum(i * (tm // 8) - 1, 0), 0)),
                 pl.BlockSpec((8, N_LORA), lambda i: (jnp.minimum((i + 1) * (tm // 8), nt8 - 1), 0))]

    def full(shape):
        nd = len(shape)
        return pl.BlockSpec(shape, lambda i, nd=nd: (0,) * nd)

    in_specs = (main_spec(0) + main_spec(1) + main_spec(2) + lora_spec
                + [full((3, c)), full((3, c)), full((3, c)), full((3, N_LORA)),
                   full((2, c)), full((2, LORA_PAD, c)), full((2, c)), full((2, LORA_PAD, c)),
                   full((256, c)), full((1, c)), full((c, 128)), full((128, c))])
    row = pl.BlockSpec((tm, c), lambda i: (i, 0))
    row2 = pl.BlockSpec((2, tm, c), lambda i: (0, i, 0))
    sd = jax.ShapeDtypeStruct
    return pl.pallas_call(
        functools.partial(_prep_kernel, seq_tiles),
        grid=(t // tm,),
        in_specs=in_specs,
        out_specs=[row, row, row, row, row2, row2, row],
        out_shape=[sd((t, c), F32), sd((t, c), F32), sd((t, c), F32), sd((t, c), F32),
                   sd((2, t, c), F32), sd((2, t, c), F32), sd((t, c), F32)],
        compiler_params=_cparams(("parallel",)),
        name="rwkv_prep",
    )(p_main, p_main, p_main, p_main, p_main, p_main, p_main, p_main, p_main,
      p_lora, p_lora, p_lora,
      taps_r, taps_k, taps_v, taps_l, w0, w2p, a0, a2p, g2, k_k, e_mat, et_mat)


def _scan_kernel(r_ref, k_ref, v_ref, kk_ref, lw_ref, a_ref, ka_ref, y_ref, st_ref):
    d = pl.program_id(0)
    c = pl.program_id(3)

    @pl.when(c == 0)
    def _():
        st_ref[...] = jnp.zeros_like(st_ref)

    ti = lax.broadcasted_iota(jnp.int32, (CHUNK, CHUNK), 0)
    si = lax.broadcasted_iota(jnp.int32, (CHUNK, CHUNK), 1)
    ahead = (ti - si) * (1 - 2 * d)
    incl = ahead >= 0
    strict = ahead > 0
    eye = (si == ti).astype(F32)

    lw = lw_ref[...]
    cum_incl = _dot_exact_lhs(incl.astype(BF16), lw)
    tot = jnp.sum(lw, axis=0, keepdims=True)
    e_incl = jnp.exp(cum_incl)
    e_excl = jnp.exp(cum_incl - lw)
    e_ninc = jnp.exp(-cum_incl)
    g_tot = jnp.exp(tot)

    a = a_ref[...]
    kk = kk_ref[...]
    kd = k_ref[...] * (1.0 + (a - 1.0) * ka_ref[...])
    at_all = kk * e_excl
    rt_all = r_ref[...] * e_incl
    bt_all = (kk * a) * e_ninc
    kt_all = kd * e_ninc
    bh_all = bt_all * g_tot
    kh_all = kt_all * g_tot
    v_all = v_ref[...]

    nh = SLAB // HEAD
    sls = [slice(h * HEAD, (h + 1) * HEAD) for h in range(nh)]
    ars = [jnp.concatenate([at_all[:, sl], rt_all[:, sl]], axis=0).astype(BF16) for sl in sls]
    g1s = [_bdot_nt(ars[h], bt_all[:, sls[h]]) for h in range(nh)]
    g2s = [_bdot_nt(ars[h], kt_all[:, sls[h]]) for h in range(nh)]
    ms = [jnp.where(strict, -g1s[h][:CHUNK], 0.0) for h in range(nh)]
    tinvs = [eye + ms[h] for h in range(nh)]
    for _ in range(5):
        ms = [_bdot(m, m) for m in ms]
        tinvs = [tinvs[h] + _bdot(tinvs[h], ms[h]) for h in range(nh)]
    s_hs = [st_ref[h] for h in range(nh)]
    x1s = [_bdot_nt(ars[h], s_hs[h]) for h in range(nh)]
    x2s = [_bdot(jnp.concatenate([jnp.where(strict, g2s[h][:CHUNK], 0.0),
                                  jnp.where(incl, g2s[h][CHUNK:], 0.0)], axis=0), v_all[:, sls[h]])
           for h in range(nh)]
    us = [-_bdot(tinvs[h], x1s[h][:CHUNK] + x2s[h][:CHUNK]) for h in range(nh)]
    outs = [x1s[h][CHUNK:] + x2s[h][CHUNK:] + _bdot(jnp.where(incl, g1s[h][CHUNK:], 0.0), us[h])
            for h in range(nh)]
    for h in range(nh):
        uv = jnp.concatenate([us[h], v_all[:, sls[h]]], axis=0).astype(BF16)
        bk = jnp.concatenate([bh_all[:, sls[h]], kh_all[:, sls[h]]], axis=0).astype(BF16)
        upd = lax.dot_general(uv, bk, (((0,), (0,)), ((), ())), preferred_element_type=F32)
        st_ref[h] = s_hs[h] * g_tot[:, sls[h]] + upd
    y_ref[...] = jnp.concatenate(outs, axis=1)


def _rwkv_scan(r, k, v, kk, lw, a, k_a, batch, seq):
    nc = seq // CHUNK
    ns = D_RWKV // SLAB
    r4 = r.reshape(batch, seq, D_RWKV)
    k4 = k.reshape(batch, seq, D_RWKV)
    v4 = v.reshape(batch, seq, D_RWKV)
    kk4 = kk.reshape(batch, seq, D_RWKV)
    lw5 = lw.reshape(2, batch, seq, D_RWKV)
    a5 = a.reshape(2, batch, seq, D_RWKV)

    def tchunk(d, c):
        return c + d * (nc - 1 - 2 * c)

    shared = pl.BlockSpec((None, CHUNK, SLAB), lambda d, b, s, c: (b, tchunk(d, c), s))
    perdir = pl.BlockSpec((None, None, CHUNK, SLAB), lambda d, b, s, c: (d, b, tchunk(d, c), s))
    y = pl.pallas_call(
        _scan_kernel,
        grid=(2, batch, ns, nc),
        in_specs=[shared, shared, shared, shared, perdir, perdir,
                  pl.BlockSpec((1, SLAB), lambda d, b, s, c: (0, s))],
        out_specs=perdir,
        out_shape=jax.ShapeDtypeStruct((2, batch, seq, D_RWKV), F32),
        scratch_shapes=[pltpu.VMEM((SLAB // HEAD, HEAD, HEAD), F32)],
        compiler_params=_cparams(("parallel", "parallel", "parallel", "arbitrary")),
        name="rwkv_scan",
    )(r4, k4, v4, kk4, lw5, a5, k_a)
    return y.reshape(2, batch * seq, D_RWKV)


def _post_kernel(y_ref, r_ref, k_ref, v_ref, a_ref, g_ref, ka_ref, rk_ref, lnw_ref, lnb_ref,
                 e_ref, et_ref, o_ref):
    y = y_ref[0] + y_ref[1]
    mu = _head_sum_bcast(y, e_ref, et_ref) * (1.0 / HEAD)
    yc = y - mu
    var = _head_sum_bcast(yc * yc, e_ref, et_ref) * (1.0 / HEAD)
    yn = yc * lax.rsqrt(var + GN_EPS) * lnw_ref[...] + lnb_ref[...]
    ka = ka_ref[...]
    k = k_ref[...]
    kd_sum = k * (1.0 + (a_ref[0] - 1.0) * ka) + k * (1.0 + (a_ref[1] - 1.0) * ka)
    bonus = _head_sum_bcast(r_ref[...] * kd_sum * rk_ref[...], e_ref, et_ref) * v_ref[...]
    o_ref[...] = ((yn + bonus) * g_ref[...]).astype(o_ref.dtype)


def _rwkv_post(y, r, k, v, a, g, k_a, r_k, ln_w, ln_b, e_mat, et_mat, tm=128):
    t, c = r.shape
    row = pl.BlockSpec((tm, c), lambda i: (i, 0))
    row2 = pl.BlockSpec((2, tm, c), lambda i: (0, i, 0))
    par = pl.BlockSpec((1, c), lambda i: (0, 0))
    return pl.pallas_call(
        _post_kernel,
        grid=(t // tm,),
        in_specs=[row2, row, row, row, row2, row, par, par, par, par,
                  pl.BlockSpec((c, 128), lambda i: (0, 0)), pl.BlockSpec((128, c), lambda i: (0, 0))],
        out_specs=row,
        out_shape=jax.ShapeDtypeStruct((t, c), BF16),
        compiler_params=_cparams(("parallel",)),
        name="rwkv_post",
    )(y, r, k, v, a, g, k_a, r_k, ln_w, ln_b, e_mat, et_mat)


def _s5_tables(lam_re, lam_im, log_step, b_re, b_im, c_re, c_im, d_skip):
    L, P, CH, G = S5_L, S5_STATE, S5_CH, S5_GROUPS
    lam = lax.complex(lam_re.astype(F32), lam_im.astype(F32))
    dt = jnp.exp(log_step.astype(F32))[..., None]
    lam_dt = lam * dt
    lam_bar = jnp.exp(lam_dt)
    b = lax.complex(b_re.astype(F32), b_im.astype(F32))
    b_bar = ((lam_bar - 1.0) / lam)[..., None] * b
    c = lax.complex(c_re.astype(F32), c_im.astype(F32))
    taus = jnp.arange(L + 1, dtype=F32)
    pows = jnp.exp(lam_dt[:, :, None, :] * taus[None, None, :, None])
    kern = jnp.real(jnp.einsum('dgop,dgtp,dgpi->dgtoi', c, pows[:, :, :L], b_bar,
                               precision=lax.Precision.HIGHEST))
    j = jnp.arange(L)[:, None]
    t = jnp.arange(L)[None, :]
    lag_f = jnp.clip(t - j, 0, L - 1)
    lag_b = jnp.clip(j - t, 0, L - 1)
    tf = jnp.where((t >= j)[None, :, :, None, None], kern[0][:, lag_f], 0.0)
    tb = jnp.where((t <= j)[None, :, :, None, None], kern[1][:, lag_b], 0.0)
    skip = (jnp.eye(L)[None, :, :, None, None] * jnp.eye(CH)[None, None, None]
            * d_skip.astype(F32).reshape(G, 1, 1, CH, 1))
    tfull = (tf + tb + skip).transpose(0, 1, 4, 2, 3).reshape(G, L * CH, L * CH)
    zf = pows[0][:, ::-1][:, 1:][:, :, :, None] * b_bar[0][:, None]
    zb = pows[1][:, :L][:, :, :, None] * b_bar[1][:, None]
    def rows_jc(z):
        return z.transpose(0, 1, 3, 2).reshape(G, L * CH, P)
    wz = jnp.concatenate([jnp.real(rows_jc(zf)), jnp.imag(rows_jc(zf)),
                          jnp.real(rows_jc(zb)), jnp.imag(rows_jc(zb))], axis=-1)
    yf = c[0][:, None] * pows[0][:, 1:][:, :, None, :]
    yb = c[1][:, None] * pows[1][:, ::-1][:, :L][:, :, None, :]
    def cols_tc(y):
        return y.transpose(0, 3, 1, 2).reshape(G, P, L * CH)
    wy = jnp.concatenate([jnp.real(cols_tc(yf)), -jnp.imag(cols_tc(yf)),
                          jnp.real(cols_tc(yb)), -jnp.imag(cols_tc(yb))], axis=1)
    steps = (L * (2.0 ** jnp.arange(8, dtype=F32)))
    lp = jnp.exp(lam_dt[:, :, None, :] * steps[None, None, :, None])
    re, im = jnp.real(lp), jnp.imag(lp)
    m1 = jnp.concatenate([re, re], axis=-1)
    m2 = jnp.concatenate([-im, im], axis=-1)
    mult = jnp.stack([m1[0], m2[0], m1[1], m2[1]], axis=2)
    return tfull.astype(BF16), wz.astype(BF16), wy.astype(BF16), mult


def _s5_kernel(n_chunks, u_ref, t_ref, wz_ref, wy_ref, mult_ref, y_ref):
    u = u_ref[...]
    rows = u.shape[0]
    z = jnp.dot(u, wz_ref[...], preferred_element_type=F32)
    xf = z[:, :128]
    xb = z[:, 128:]
    cidx = lax.broadcasted_iota(jnp.int32, (rows, 128), 0) % n_chunks
    levels = int(math.log2(n_chunks))
    for i in range(levels):
        sh = 1 << i
        mf1 = mult_ref[i, 0:1, :]
        mf2 = mult_ref[i, 1:2, :]
        mb1 = mult_ref[i, 2:3, :]
        mb2 = mult_ref[i, 3:4, :]
        sf = jnp.where(cidx >= sh, pltpu.roll(xf, sh, 0), 0.0)
        xf = xf + sf * mf1 + pltpu.roll(sf, 64, 1) * mf2
        sb = jnp.where(cidx < n_chunks - sh, pltpu.roll(xb, rows - sh, 0), 0.0)
        xb = xb + sb * mb1 + pltpu.roll(sb, 64, 1) * mb2
    x_prev = jnp.where(cidx >= 1, pltpu.roll(xf, 1, 0), 0.0)
    x_next = jnp.where(cidx < n_chunks - 1, pltpu.roll(xb, rows - 1, 0), 0.0)
    xin = jnp.concatenate([x_prev, x_next], axis=1).astype(BF16)
    y = jnp.dot(u, t_ref[...], preferred_element_type=F32)
    y = y + jnp.dot(xin, wy_ref[...], preferred_element_type=F32)
    y_ref[...] = y


def _s5_mix(u_g, tfull, wz, wy, mult, n_chunks):
    g, rows, w = u_g.shape
    mat = pl.BlockSpec((None, w, w), lambda i: (i, 0, 0))
    return pl.pallas_call(
        functools.partial(_s5_kernel, n_chunks),
        grid=(g,),
        in_specs=[pl.BlockSpec((None, rows, w), lambda i: (i, 0, 0)), mat, mat, mat,
                  pl.BlockSpec((None, 8, 4, 128), lambda i: (i, 0, 0, 0))],
        out_specs=pl.BlockSpec((None, rows, w), lambda i: (i, 0, 0)),
        out_shape=jax.ShapeDtypeStruct((g, rows, w), F32),
        compiler_params=_cparams(("parallel",)),
        name="s5_mix",
    )(u_g, tfull, wz, wy, mult)


def _glu_kernel(y_ref, w_ref, b_ref, o_ref):
    y = y_ref[...]
    z = jnp.dot(jax.nn.gelu(y).astype(BF16), w_ref[...], preferred_element_type=F32) + b_ref[...]
    o_ref[...] = (y * jax.nn.sigmoid(z)).astype(o_ref.dtype)


def _glu(y, w, b, tm=512):
    t, c = y.shape
    tm = min(tm, t)
    return pl.pallas_call(
        _glu_kernel,
        grid=(t // tm,),
        in_specs=[pl.BlockSpec((tm, c), lambda i: (i, 0)), pl.BlockSpec((c, c), lambda i: (0, 0)),
                  pl.BlockSpec((1, c), lambda i: (0, 0))],
        out_specs=pl.BlockSpec((tm, c), lambda i: (i, 0)),
        out_shape=jax.ShapeDtypeStruct((t, c), BF16),
        compiler_params=_cparams(("parallel",)),
        name="s5_glu",
    )(y, w, b)


def _attn_kernel(q_ref, k_ref, v_ref, o_ref):
    for h in range(ATTN_HEADS):
        sl = slice(h * ATTN_HEAD_DIM, (h + 1) * ATTN_HEAD_DIM)
        s = lax.dot_general(q_ref[:, sl], k_ref[:, sl], (((1,), (1,)), ((), ())),
                            preferred_element_type=F32) * ATTN_SCALE
        s = s - jnp.max(s, axis=-1, keepdims=True)
        p = jnp.exp(s)
        p = p / jnp.sum(p, axis=-1, keepdims=True)
        o_ref[:, sl] = jnp.dot(p.astype(BF16), v_ref[:, sl], preferred_element_type=F32).astype(o_ref.dtype)


def _attention(q, k, v, batch, seq, mem, tm=512):
    tm = min(tm, seq)
    q3 = q.reshape(batch, seq, D_MODEL)
    k3 = k.reshape(batch, mem, D_MODEL)
    v3 = v.reshape(batch, mem, D_MODEL)
    kv = pl.BlockSpec((None, mem, D_MODEL), lambda b, i: (b, 0, 0))
    o = pl.pallas_call(
        _attn_kernel,
        grid=(batch, seq // tm),
        in_specs=[pl.BlockSpec((None, tm, D_MODEL), lambda b, i: (b, i, 0)), kv, kv],
        out_specs=pl.BlockSpec((None, tm, D_MODEL), lambda b, i: (b, i, 0)),
        out_shape=jax.ShapeDtypeStruct((batch, seq, D_MODEL), BF16),
        compiler_params=_cparams(("parallel", "parallel")),
        name="cross_attn",
    )(q3, k3, v3)
    return o.reshape(batch * seq, D_MODEL)


def _router_kernel(x_ref, g_ref, whi_ref, wlo_ref, b_ref, xn_ref, info_ref, cnt_ref, carry_ref):
    i = pl.program_id(0)

    @pl.when(i == 0)
    def _():
        carry_ref[...] = jnp.zeros_like(carry_ref)

    x = x_ref[...]
    tm = x.shape[0]
    ms = jnp.mean(x * x, axis=-1, keepdims=True)
    xn = x * lax.rsqrt(ms + RMS_EPS) * g_ref[...]
    xn_ref[...] = xn
    xh, xl = _split2(xn)
    whi = whi_ref[...]
    logits = (jnp.dot(xh, whi, preferred_element_type=F32) + jnp.dot(xl, whi, preferred_element_type=F32)
              + jnp.dot(xh, wlo_ref[...], preferred_element_type=F32)) + b_ref[...]
    li = lax.broadcasted_iota(jnp.int32, logits.shape, 1)
    neg = jnp.float32(-jnp.inf)
    is_g = li < N_GROUPS
    gl = jnp.where(is_g, logits, neg)
    gm = jnp.max(gl, axis=-1, keepdims=True)
    gi = jnp.min(jnp.where(is_g & (gl == gm), li, 128), axis=-1, keepdims=True)
    gp = 1.0 / jnp.sum(jnp.where(is_g, jnp.exp(gl - gm), 0.0), axis=-1, keepdims=True)
    lo_lane = N_GROUPS + EPG * gi
    sel = (li >= lo_lane) & (li < lo_lane + EPG)
    l1 = jnp.where(sel, logits, neg)
    e1 = jnp.max(l1, axis=-1, keepdims=True)
    i1 = jnp.min(jnp.where(sel & (l1 == e1), li, 128), axis=-1, keepdims=True)
    sel2 = sel & (li != i1)
    l2 = jnp.where(sel2, logits, neg)
    e2 = jnp.max(l2, axis=-1, keepdims=True)
    i2 = jnp.min(jnp.where(sel2 & (l2 == e2), li, 128), axis=-1, keepdims=True)
    ex = jnp.exp(e2 - e1)
    w1 = gp / (1.0 + ex)
    w2 = gp * ex / (1.0 + ex)
    x1 = i1 - N_GROUPS
    x2 = i2 - N_GROUPS
    oh1 = li == x1
    oh2 = li == x2
    oh = (oh1 | oh2).astype(BF16)
    ri = lax.broadcasted_iota(jnp.int32, (tm, tm), 0)
    ci = lax.broadcasted_iota(jnp.int32, (tm, tm), 1)
    before = jnp.dot((ci < ri).astype(BF16), oh, preferred_element_type=F32) + carry_ref[0:1, :]
    r1 = jnp.sum(jnp.where(oh1, before, 0.0), axis=-1, keepdims=True)
    r2 = jnp.sum(jnp.where(oh2, before, 0.0), axis=-1, keepdims=True)
    new_carry = carry_ref[0:1, :] + jnp.sum(oh.astype(F32), axis=0, keepdims=True)
    carry_ref[...] = jnp.broadcast_to(new_carry, carry_ref.shape)
    cnt_ref[...] = jnp.broadcast_to(new_carry, cnt_ref.shape)
    info = jnp.where(li == 0, x1.astype(F32), 0.0)
    info = jnp.where(li == 1, x2.astype(F32), info)
    info = jnp.where(li == 2, w1, info)
    info = jnp.where(li == 3, w2, info)
    info = jnp.where(li == 4, r1, info)
    info = jnp.where(li == 5, r2, info)
    info_ref[...] = info


def _router(x, gain, w_hi, w_lo, bias, tm=256):
    t, d = x.shape
    sd = jax.ShapeDtypeStruct
    return pl.pallas_call(
        _router_kernel,
        grid=(t // tm,),
        in_specs=[pl.BlockSpec((tm, d), lambda i: (i, 0)), pl.BlockSpec((1, d), lambda i: (0, 0)),
                  pl.BlockSpec((d, 128), lambda i: (0, 0)), pl.BlockSpec((d, 128), lambda i: (0, 0)),
                  pl.BlockSpec((1, 128), lambda i: (0, 0))],
        out_specs=[pl.BlockSpec((tm, d), lambda i: (i, 0)), pl.BlockSpec((tm, 128), lambda i: (i, 0)),
                   pl.BlockSpec((8, 128), lambda i: (0, 0))],
        out_shape=[sd((t, d), F32), sd((t, 128), F32), sd((8, 128), F32)],
        scratch_shapes=[pltpu.VMEM((8, 128), F32)],
        compiler_params=_cparams(("arbitrary",)),
        name="moe_router",
    )(x, gain.reshape(1, d), w_hi, w_lo, bias)


def _gather_rows(src_hbm, idx_ref, dst_ref, sem, n_rows):
    def body(j, carry):
        pltpu.make_async_copy(src_hbm.at[pl.ds(idx_ref[0, j], 1), :], dst_ref.at[pl.ds(j, 1), :], sem).start()
        return carry
    lax.fori_loop(0, n_rows, body, 0)


def _wait_rows(src_hbm, dst_ref, sem, n_rows):
    pltpu.make_async_copy(src_hbm.at[pl.ds(0, n_rows), :], dst_ref, sem).wait()


def _expert_kernel(be_ref, nused_ref, tok_ref, tokn_ref, x_hbm, wg_ref, wu_ref, wd_ref, y_ref, xbuf, sem):
    i = pl.program_id(0)
    n_used = nused_ref[0]
    slot = i % 2

    @pl.when(i == 0)
    def _():
        _gather_rows(x_hbm, tok_ref, xbuf.at[0], sem.at[0], MOE_BLK)

    @pl.when(i + 1 < n_used)
    def _():
        _gather_rows(x_hbm, tokn_ref, xbuf.at[1 - slot], sem.at[1 - slot], MOE_BLK)

    @pl.when(i < n_used)
    def _():
        _wait_rows(x_hbm, xbuf.at[slot], sem.at[slot], MOE_BLK)
        xb = xbuf[slot].astype(BF16)
        hg = jnp.dot(xb, wg_ref[...], preferred_element_type=F32)
        hu = jnp.dot(xb, wu_ref[...], preferred_element_type=F32)
        h = (jax.nn.silu(hg) * hu).astype(BF16)
        y_ref[...] = jnp.dot(h, wd_ref[...], preferred_element_type=F32)

    @pl.when(i >= n_used)
    def _():
        y_ref[...] = jnp.zeros_like(y_ref)


def _experts(block_expert, n_used, tok3, xn, wg, wu, wd):
    nb = tok3.shape[0]
    d = xn.shape[1]
    tok_spec = pl.BlockSpec((None, 1, MOE_BLK), lambda i, be, nu: (i, 0, 0), memory_space=pltpu.SMEM)
    tokn_spec = pl.BlockSpec((None, 1, MOE_BLK), lambda i, be, nu: (jnp.minimum(i + 1, nb - 1), 0, 0),
                             memory_space=pltpu.SMEM)
    grid_spec = pltpu.PrefetchScalarGridSpec(
        num_scalar_prefetch=2,
        grid=(nb,),
        in_specs=[tok_spec, tokn_spec, pl.BlockSpec(memory_space=pl.ANY),
                  pl.BlockSpec((None, d, D_EXPERT), lambda i, be, nu: (be[i], 0, 0)),
                  pl.BlockSpec((None, d, D_EXPERT), lambda i, be, nu: (be[i], 0, 0)),
                  pl.BlockSpec((None, D_EXPERT, d), lambda i, be, nu: (be[i], 0, 0))],
        out_specs=pl.BlockSpec((MOE_BLK, d), lambda i, be, nu: (i, 0)),
        scratch_shapes=[pltpu.VMEM((2, MOE_BLK, d), F32), pltpu.SemaphoreType.DMA((2,))],
    )
    return pl.pallas_call(
        _expert_kernel,
        grid_spec=grid_spec,
        out_shape=jax.ShapeDtypeStruct((nb * MOE_BLK, d), F32),
        compiler_params=_cparams(("arbitrary",)),
        name="moe_experts",
    )(block_expert, n_used, tok3, tok3, xn, wg, wu, wd)


def _combine_kernel(d1_ref, d1n_ref, d2_ref, d2n_ref, x_ref, info_ref, g_ref, y_hbm, o_ref, ybuf, sem):
    i = pl.program_id(0)
    n = pl.num_programs(0)
    slot = i % 2

    @pl.when(i == 0)
    def _():
        _gather_rows(y_hbm, d1_ref, ybuf.at[0, 0], sem.at[0], COMB_TM)
        _gather_rows(y_hbm, d2_ref, ybuf.at[0, 1], sem.at[0], COMB_TM)

    @pl.when(i + 1 < n)
    def _():
        _gather_rows(y_hbm, d1n_ref, ybuf.at[1 - slot, 0], sem.at[1 - slot], COMB_TM)
        _gather_rows(y_hbm, d2n_ref, ybuf.at[1 - slot, 1], sem.at[1 - slot], COMB_TM)

    _wait_rows(y_hbm, ybuf.at[slot, 0], sem.at[slot], COMB_TM)
    _wait_rows(y_hbm, ybuf.at[slot, 1], sem.at[slot], COMB_TM)
    info = info_ref[...]
    w1 = info[:, 2:3]
    w2 = info[:, 3:4]
    x = x_ref[...] + ybuf[slot, 0] * w1 + ybuf[slot, 1] * w2
    ms = jnp.mean(x * x, axis=-1, keepdims=True)
    o_ref[...] = x * lax.rsqrt(ms + RMS_EPS) * g_ref[...]


def _combine(dest1, dest2, x, info, gain, y_buf):
    t, d = x.shape
    nt = t // COMB_TM
    d1 = dest1.reshape(nt, 1, COMB_TM)
    d2 = dest2.reshape(nt, 1, COMB_TM)
    cur = pl.BlockSpec((None, 1, COMB_TM), lambda i: (i, 0, 0), memory_space=pltpu.SMEM)
    nxt = pl.BlockSpec((None, 1, COMB_TM), lambda i: (jnp.minimum(i + 1, nt - 1), 0, 0),
                       memory_space=pltpu.SMEM)
    return pl.pallas_call(
        _combine_kernel,
        grid=(nt,),
        in_specs=[cur, nxt, cur, nxt,
                  pl.BlockSpec((COMB_TM, d), lambda i: (i, 0)),
                  pl.BlockSpec((COMB_TM, 128), lambda i: (i, 0)),
                  pl.BlockSpec((1, d), lambda i: (0, 0)),
                  pl.BlockSpec(memory_space=pl.ANY)],
        out_specs=pl.BlockSpec((COMB_TM, d), lambda i: (i, 0)),
        out_shape=jax.ShapeDtypeStruct((t, d), F32),
        scratch_shapes=[pltpu.VMEM((2, 2, COMB_TM, d), F32), pltpu.SemaphoreType.DMA((2,))],
        compiler_params=_cparams(("arbitrary",)),
        name="moe_combine",
    )(d1, d1, d2, d2, x, info, gain.reshape(1, d), y_buf)


def _head_matrices():
    lane = jnp.arange(D_RWKV) // HEAD
    e = (lane[:, None] == jnp.arange(128)[None, :]).astype(BF16)
    return e, e.T


def _pad_lora_rows(w):
    return jnp.pad(w, ((0, 0), (0, LORA_PAD - w.shape[1]), (0, 0)))


def _layer(x, mem, norm_mix, w_in, shift_taps, rwkv_w0, rwkv_w2, rwkv_a0, rwkv_a2, rwkv_g2, rwkv_k_k,
           rwkv_k_a, rwkv_r_k, rwkv_ln_w, rwkv_ln_b, s5_lam_re, s5_lam_im, s5_log_step, s5_b_re, s5_b_im,
           s5_c_re, s5_c_im, s5_d, s5_glu_w, s5_glu_b, w_out, norm_attn, norm_mem, w_q, w_k, w_v, w_o,
           norm_ffn, router_grp_w, router_grp_b, router_exp_w, router_exp_b, exp_w_gate, exp_w_up,
           exp_w_down, out_gain):
    batch, seq, d = x.shape
    t = batch * seq
    mem_n = mem.shape[1]
    xt = x.reshape(t, d)
    c3 = 3 * D_RWKV
    off_g = c3 + 4 * 96

    def pad_cols(w, lo, width):
        return jnp.pad(w[:, lo:lo + width], ((0, 0), (0, LORA_PAD - width)))

    w_main = jnp.concatenate([w_in[:, :c3], w_in[:, off_g + 256:]], axis=1).astype(BF16)
    w_lora = jnp.concatenate([pad_cols(w_in, c3 + i * 96, 96) for i in range(4)]
                             + [w_in[:, off_g:off_g + 256]], axis=1).astype(BF16)
    taps_l = jnp.concatenate([pad_cols(shift_taps, c3 + i * 96, 96) for i in range(4)]
                             + [shift_taps[:, off_g:off_g + 256]], axis=1)
    h = _rmsnorm(xt, norm_mix, BF16)
    p_main = _matmul(h, w_main, F32, 1024, 512, name="w_in_main")
    p_lora = _matmul(h, w_lora, F32, 1024, N_LORA, name="w_in_lora")

    e_mat, et_mat = _head_matrices()
    k_a = rwkv_k_a.reshape(1, D_RWKV)
    r, k, v, kk, lw, a, g = _rwkv_prep(
        p_main, p_lora, seq, shift_taps[:, :D_RWKV], shift_taps[:, D_RWKV:2 * D_RWKV],
        shift_taps[:, 2 * D_RWKV:c3], taps_l, rwkv_w0, _pad_lora_rows(rwkv_w2).astype(BF16), rwkv_a0,
        _pad_lora_rows(rwkv_a2).astype(BF16), rwkv_g2.astype(BF16), rwkv_k_k.reshape(1, D_RWKV),
        e_mat, et_mat)
    y_scan = _rwkv_scan(r, k, v, kk, lw, a, k_a, batch, seq)
    y_rwkv = _rwkv_post(y_scan, r, k, v, a, g, k_a, rwkv_r_k.reshape(1, D_RWKV),
                        rwkv_ln_w.reshape(1, D_RWKV), rwkv_ln_b.reshape(1, D_RWKV), e_mat, et_mat)

    n_chunks = seq // S5_L
    tfull, wz, wy, mult = _s5_tables(s5_lam_re, s5_lam_im, s5_log_step, s5_b_re, s5_b_im,
                                     s5_c_re, s5_c_im, s5_d)
    u = p_main[:, c3:].astype(BF16)
    u_g = u.reshape(t // S5_L, S5_L, S5_GROUPS, S5_CH).transpose(2, 0, 1, 3).reshape(
        S5_GROUPS, t // S5_L, S5_L * S5_CH)
    y_g = _s5_mix(u_g, tfull, wz, wy, mult, n_chunks)
    y_s5 = y_g.reshape(S5_GROUPS, t // S5_L, S5_L, S5_CH).transpose(1, 2, 0, 3).reshape(t, D_S5)
    y_glu = _glu(y_s5, s5_glu_w.astype(BF16), s5_glu_b.reshape(1, D_S5))

    x1 = _matmul2_res(y_rwkv, y_glu, w_out.astype(BF16), xt, 1024, 512)

    xn = _rmsnorm(x1, norm_attn, BF16)
    memn = _rmsnorm(mem.reshape(batch * mem_n, d), norm_mem, BF16)
    q = _matmul(xn, w_q.astype(BF16), BF16, 1024, 512, name="w_q")
    wkv = jnp.concatenate([w_k, w_v], axis=1).astype(BF16)
    kv = _matmul(memn, wkv, BF16, 1024, 512, name="w_kv")
    o = _attention(q, kv[:, :d], kv[:, d:], batch, seq, mem_n)
    x2 = _matmul(o, w_o.astype(BF16), F32, 1024, 512, res=x1, name="w_o")

    w_r = jnp.concatenate([router_grp_w, router_exp_w,
                           jnp.zeros((d, 128 - N_GROUPS - N_EXPERTS), F32)], axis=1)
    w_r_hi = w_r.astype(BF16)
    w_r_lo = (w_r - w_r_hi.astype(F32)).astype(BF16)
    b_r = jnp.concatenate([router_grp_b, router_exp_b,
                           jnp.zeros((128 - N_GROUPS - N_EXPERTS,), F32)]).reshape(1, 128)
    xn3, info, cnt = _router(x2, norm_ffn, w_r_hi, w_r_lo, b_r)
    eid = info[:, 0:2].astype(jnp.int32)
    rank = info[:, 4:6].astype(jnp.int32)
    counts = cnt[0, :N_EXPERTS].astype(jnp.int32)
    nblk = (counts + MOE_BLK - 1) // MOE_BLK
    bstart = jnp.cumsum(nblk) - nblk
    n_used = jnp.sum(nblk)
    dest = bstart[eid] * MOE_BLK + rank
    nb = (t * TOP_K) // MOE_BLK + N_EXPERTS
    tok_buf = jnp.zeros((nb * MOE_BLK,), jnp.int32).at[dest.reshape(-1)].set(
        jnp.repeat(jnp.arange(t, dtype=jnp.int32), TOP_K))
    blk = jnp.arange(nb, dtype=jnp.int32)
    block_expert = jnp.sum(blk[:, None] >= (bstart + nblk)[None, :], axis=1).astype(jnp.int32)
    last_e = jnp.max(jnp.where(nblk > 0, jnp.arange(N_EXPERTS), 0)).astype(jnp.int32)
    block_expert = jnp.minimum(block_expert, last_e)
    y_buf = _experts(block_expert, n_used.reshape(1).astype(jnp.int32), tok_buf.reshape(nb, 1, MOE_BLK),
                     xn3, exp_w_gate.astype(BF16), exp_w_up.astype(BF16), exp_w_down.astype(BF16))
    out = _combine(dest[:, 0], dest[:, 1], x2, info, out_gain, y_buf)
    return out.reshape(batch, seq, d)


def kernel(x, mem, norm_mix, w_in, shift_taps, rwkv_w0, rwkv_w2, rwkv_a0, rwkv_a2, rwkv_g2, rwkv_k_k, rwkv_k_a, rwkv_r_k, rwkv_ln_w, rwkv_ln_b, s5_lam_re, s5_lam_im, s5_log_step, s5_b_re, s5_b_im, s5_c_re, s5_c_im, s5_d, s5_glu_w, s5_glu_b, w_out, norm_attn, norm_mem, w_q, w_k, w_v, w_o, norm_ffn, router_grp_w, router_grp_b, router_exp_w, router_exp_b, exp_w_gate, exp_w_up, exp_w_down, norm_final):
    return _layer(x, mem, norm_mix[0], w_in[0], shift_taps[0], rwkv_w0[0], rwkv_w2[0], rwkv_a0[0],
                  rwkv_a2[0], rwkv_g2[0], rwkv_k_k[0], rwkv_k_a[0], rwkv_r_k[0], rwkv_ln_w[0],
                  rwkv_ln_b[0], s5_lam_re[0], s5_lam_im[0], s5_log_step[0], s5_b_re[0], s5_b_im[0],
                  s5_c_re[0], s5_c_im[0], s5_d[0], s5_glu_w[0], s5_glu_b[0], w_out[0], norm_attn[0],
                  norm_mem[0], w_q[0], w_k[0], w_v[0], w_o[0], norm_ffn[0], router_grp_w[0],
                  router_grp_b[0], router_exp_w[0], router_exp_b[0], exp_w_gate[0], exp_w_up[0],
                  exp_w_down[0], norm_final)
```

```python
import functools
import math

import jax
import jax.numpy as jnp
from jax import lax
from jax.experimental import pallas as pl
from jax.experimental.pallas import tpu as pltpu

F32 = jnp.float32
BF16 = jnp.bfloat16

D_MODEL = 4096
D_RWKV = 2048
D_S5 = 2048
HEAD = 64
N_HEADS = D_RWKV // HEAD
LORA_PAD = 128
N_LORA = 4 * LORA_PAD + 256
N_MAIN = 3 * D_RWKV + D_S5
S5_CH = 16
S5_GROUPS = D_S5 // S5_CH
S5_STATE = 64
S5_L = 16
ATTN_HEADS = 4
ATTN_HEAD_DIM = D_MODEL // ATTN_HEADS
ATTN_SCALE = ATTN_HEAD_DIM ** -0.5
N_GROUPS = 8
EPG = 8
N_EXPERTS = N_GROUPS * EPG
TOP_K = 2
D_EXPERT = D_MODEL // 8
RMS_EPS = 1e-6
GN_EPS = 64e-5
L2_EPS = 1e-12

CHUNK = 64
SLAB = 2048
MOE_BLK = 256
COMB_TM = 128
EXPERT_KC = 1024
VMEM_LIMIT = 56 * 1024 * 1024


def _cparams(sem):
    return pltpu.CompilerParams(dimension_semantics=sem, vmem_limit_bytes=VMEM_LIMIT)


def _bdot(a, b):
    return jnp.dot(a.astype(BF16), b.astype(BF16), preferred_element_type=F32)


def _bdot_nt(a, b):
    return lax.dot_general(a.astype(BF16), b.astype(BF16), (((1,), (1,)), ((), ())),
                           preferred_element_type=F32)


def _split2(x):
    hi = x.astype(BF16)
    lo = (x - hi.astype(F32)).astype(BF16)
    return hi, lo


def _split3(x):
    hi = x.astype(BF16)
    r1 = x - hi.astype(F32)
    mid = r1.astype(BF16)
    lo = (r1 - mid.astype(F32)).astype(BF16)
    return hi, mid, lo


def _dot_exact_rhs(x, e):
    h, m, l = _split3(x)
    return (jnp.dot(h, e, preferred_element_type=F32) + jnp.dot(m, e, preferred_element_type=F32)
            + jnp.dot(l, e, preferred_element_type=F32))


def _dot_exact_lhs(e, x):
    h, m, l = _split3(x)
    return (jnp.dot(e, h, preferred_element_type=F32) + jnp.dot(e, m, preferred_element_type=F32)
            + jnp.dot(e, l, preferred_element_type=F32))


def _rms_kernel(x_ref, g_ref, o_ref):
    x = x_ref[...]
    ms = jnp.mean(x * x, axis=-1, keepdims=True)
    o_ref[...] = (x * lax.rsqrt(ms + RMS_EPS) * g_ref[...]).astype(o_ref.dtype)


def _rmsnorm(x, gain, out_dtype, tm=256):
    t, d = x.shape
    return pl.pallas_call(
        _rms_kernel,
        grid=(t // tm,),
        in_specs=[pl.BlockSpec((tm, d), lambda i: (i, 0)), pl.BlockSpec((1, d), lambda i: (0, 0))],
        out_specs=pl.BlockSpec((tm, d), lambda i: (i, 0)),
        out_shape=jax.ShapeDtypeStruct((t, d), out_dtype),
        compiler_params=_cparams(("parallel",)),
        name="rmsnorm",
    )(x, gain.reshape(1, d))


def _mm_kernel(a_ref, b_ref, o_ref):
    o_ref[...] = jnp.dot(a_ref[...], b_ref[...], preferred_element_type=F32).astype(o_ref.dtype)


def _mm_res_kernel(a_ref, b_ref, r_ref, o_ref):
    o_ref[...] = (jnp.dot(a_ref[...], b_ref[...], preferred_element_type=F32)
                  + r_ref[...]).astype(o_ref.dtype)


def _mm2_res_kernel(a1_ref, a2_ref, b_ref, r_ref, o_ref):
    k1 = a1_ref.shape[1]
    acc = jnp.dot(a1_ref[...], b_ref[:k1, :], preferred_element_type=F32)
    acc = acc + jnp.dot(a2_ref[...], b_ref[k1:, :], preferred_element_type=F32)
    o_ref[...] = (acc + r_ref[...]).astype(o_ref.dtype)


def _matmul(a, b, out_dtype, tm, tn, res=None, name="matmul"):
    m, k = a.shape
    n = b.shape[1]
    tm, tn = min(tm, m), min(tn, n)
    in_specs = [pl.BlockSpec((tm, k), lambda i, j: (i, 0)), pl.BlockSpec((k, tn), lambda i, j: (0, j))]
    args = [a, b]
    kern = _mm_kernel
    if res is not None:
        in_specs.append(pl.BlockSpec((tm, tn), lambda i, j: (i, j)))
        args.append(res)
        kern = _mm_res_kernel
    return pl.pallas_call(
        kern,
        grid=(m // tm, n // tn),
        in_specs=in_specs,
        out_specs=pl.BlockSpec((tm, tn), lambda i, j: (i, j)),
        out_shape=jax.ShapeDtypeStruct((m, n), out_dtype),
        compiler_params=_cparams(("parallel", "parallel")),
        name=name,
    )(*args)


def _matmul2_res(a1, a2, b, res, tm, tn):
    m, k1 = a1.shape
    k2 = a2.shape[1]
    n = b.shape[1]
    tm = min(tm, m)
    return pl.pallas_call(
        _mm2_res_kernel,
        grid=(m // tm, n // tn),
        in_specs=[pl.BlockSpec((tm, k1), lambda i, j: (i, 0)),
                  pl.BlockSpec((tm, k2), lambda i, j: (i, 0)),
                  pl.BlockSpec((k1 + k2, tn), lambda i, j: (0, j)),
                  pl.BlockSpec((tm, tn), lambda i, j: (i, j))],
        out_specs=pl.BlockSpec((tm, tn), lambda i, j: (i, j)),
        out_shape=jax.ShapeDtypeStruct((m, n), F32),
        compiler_params=_cparams(("parallel", "parallel")),
        name="w_out",
    )(a1, a2, b, res)


def _head_sum_bcast(x, e_ref, et_ref):
    s = _dot_exact_rhs(x, e_ref[...])
    return _dot_exact_rhs(s, et_ref[...])


def _shift3(x, hp, hn, taps, first, last):
    tm = x.shape[0]
    row = lax.broadcasted_iota(jnp.int32, x.shape, 0)
    prev_edge = jnp.where(first, 0.0, hp[7:8, :])
    next_edge = jnp.where(last, 0.0, hn[0:1, :])
    prev = jnp.where(row == 0, prev_edge, pltpu.roll(x, 1, 0))
    nxt = jnp.where(row == tm - 1, next_edge, pltpu.roll(x, tm - 1, 0))
    return taps[0:1, :] * prev + taps[1:2, :] * x + taps[2:3, :] * nxt


def _prep_kernel(seq_tiles,
                 r_ref, rp_ref, rn_ref, k_ref, kp_ref, kn_ref, v_ref, vp_ref, vn_ref,
                 lo_ref, lop_ref, lon_ref,
                 tr_ref, tk_ref, tv_ref, tl_ref,
                 w0_ref, w2_ref, a0_ref, a2_ref, g2_ref, kk_ref_p, e_ref, et_ref,
                 ro_ref, ko_ref, vo_ref, kko_ref, lw_ref, a_ref, g_ref):
    i = pl.program_id(0)
    first = (i % seq_tiles) == 0
    last = (i % seq_tiles) == seq_tiles - 1
    r = _shift3(r_ref[...], rp_ref[...], rn_ref[...], tr_ref[...], first, last)
    k = _shift3(k_ref[...], kp_ref[...], kn_ref[...], tk_ref[...], first, last)
    v = _shift3(v_ref[...], vp_ref[...], vn_ref[...], tv_ref[...], first, last)
    lo = _shift3(lo_ref[...], lop_ref[...], lon_ref[...], tl_ref[...], first, last)
    ro_ref[...] = r.astype(ro_ref.dtype)
    ko_ref[...] = k.astype(ko_ref.dtype)
    vo_ref[...] = v.astype(vo_ref.dtype)
    for d in range(2):
        xw = lo[:, d * LORA_PAD:(d + 1) * LORA_PAD]
        xa = lo[:, (2 + d) * LORA_PAD:(3 + d) * LORA_PAD]
        wl = w0_ref[d:d + 1, :] + _bdot(jnp.tanh(xw), w2_ref[d])
        w_log = -jax.nn.softplus(-wl) - 0.5
        lw_ref[d] = -jnp.exp(w_log)
        a_ref[d] = jax.nn.sigmoid(a0_ref[d:d + 1, :] + _bdot(xa, a2_ref[d])).astype(a_ref.dtype)
    xg = lo[:, 4 * LORA_PAD:]
    g_ref[...] = _bdot(jax.nn.sigmoid(xg), g2_ref[...]).astype(g_ref.dtype)
    kk = k * kk_ref_p[...]
    ssq = _head_sum_bcast(kk * kk, e_ref, et_ref)
    kko_ref[...] = (kk / jnp.maximum(jnp.sqrt(ssq), L2_EPS)).astype(kko_ref.dtype)


def _rwkv_prep(p_main, p_lora, seq, taps_r, taps_k, taps_v, taps_l, w0, w2p, a0, a2p, g2, k_k, e_mat, et_mat,
               tm=128):
    t = p_main.shape[0]
    nt8 = t // 8
    seq_tiles = seq // tm
    c = D_RWKV

    def main_spec(col):
        return [pl.BlockSpec((tm, c), lambda i, col=col: (i, col)),
                pl.BlockSpec((8, c), lambda i, col=col: (jnp.maximum(i * (tm // 8) - 1, 0), col)),
                pl.BlockSpec((8, c), lambda i, col=col: (jnp.minimum((i + 1) * (tm // 8), nt8 - 1), col))]

    lora_spec = [pl.BlockSpec((tm, N_LORA), lambda i: (i, 0)),
                 pl.BlockSpec((8, N_LORA), lambda i: (jnp.maximum(i * (tm // 8) - 1, 0), 0)),
                 pl.BlockSpec((8, N_LORA), lambda i: (jnp.minimum((i + 1) * (tm // 8), nt8 - 1), 0))]

    def full(shape):
        nd = len(shape)
        return pl.BlockSpec(shape, lambda i, nd=nd: (0,) * nd)

    in_specs = (main_spec(0) + main_spec(1) + main_spec(2) + lora_spec
                + [full((3, c)), full((3, c)), full((3, c)), full((3, N_LORA)),
                   full((2, c)), full((2, LORA_PAD, c)), full((2, c)), full((2, LORA_PAD, c)),
                   full((256, c)), full((1, c)), full((c, 128)), full((128, c))])
    row = pl.BlockSpec((tm, c), lambda i: (i, 0))
    row2 = pl.BlockSpec((2, tm, c), lambda i: (0, i, 0))
    sd = jax.ShapeDtypeStruct
    return pl.pallas_call(
        functools.partial(_prep_kernel, seq_tiles),
        grid=(t // tm,),
        in_specs=in_specs,
        out_specs=[row, row, row, row, row2, row2, row],
        out_shape=[sd((t, c), BF16), sd((t, c), BF16), sd((t, c), BF16), sd((t, c), BF16),
                   sd((2, t, c), F32), sd((2, t, c), BF16), sd((t, c), BF16)],
        compiler_params=_cparams(("parallel",)),
        name="rwkv_prep",
    )(p_main, p_main, p_main, p_main, p_main, p_main, p_main, p_main, p_main,
      p_lora, p_lora, p_lora,
      taps_r, taps_k, taps_v, taps_l, w0, w2p, a0, a2p, g2, k_k, e_mat, et_mat)


def _scan_kernel(r_ref, k_ref, v_ref, kk_ref, lw_ref, a_ref, ka_ref, y_ref, st_ref):
    d = pl.program_id(0)
    c = pl.program_id(3)

    @pl.when(c == 0)
    def _():
        st_ref[...] = jnp.zeros_like(st_ref)

    ti = lax.broadcasted_iota(jnp.int32, (CHUNK, CHUNK), 0)
    si = lax.broadcasted_iota(jnp.int32, (CHUNK, CHUNK), 1)
    ahead = (ti - si) * (1 - 2 * d)
    incl = ahead >= 0
    strict = ahead > 0
    eye = (si == ti).astype(F32)

    lw = lw_ref[...]
    cum_incl = _dot_exact_lhs(incl.astype(BF16), lw)
    tot = jnp.sum(lw, axis=0, keepdims=True)
    e_incl = jnp.exp(cum_incl)
    e_excl = jnp.exp(cum_incl - lw)
    e_ninc = jnp.exp(-cum_incl)
    g_tot = jnp.exp(tot)

    a = a_ref[...].astype(F32)
    kk = kk_ref[...].astype(F32)
    kd = k_ref[...].astype(F32) * (1.0 + (a - 1.0) * ka_ref[...])
    at_all = kk * e_excl
    rt_all = r_ref[...].astype(F32) * e_incl
    bt_all = (kk * a) * e_ninc
    kt_all = kd * e_ninc
    bh_all = bt_all * g_tot
    kh_all = kt_all * g_tot
    v_all = v_ref[...].astype(F32)

    nh = SLAB // HEAD
    sls = [slice(h * HEAD, (h + 1) * HEAD) for h in range(nh)]
    ars = [jnp.concatenate([at_all[:, sl], rt_all[:, sl]], axis=0).astype(BF16) for sl in sls]
    g1s = [_bdot_nt(ars[h], bt_all[:, sls[h]]) for h in range(nh)]
    g2s = [_bdot_nt(ars[h], kt_all[:, sls[h]]) for h in range(nh)]
    ms = [jnp.where(strict, -g1s[h][:CHUNK], 0.0) for h in range(nh)]
    tinvs = [eye + ms[h] for h in range(nh)]
    for _ in range(5):
        ms = [_bdot(m, m) for m in ms]
        tinvs = [tinvs[h] + _bdot(tinvs[h], ms[h]) for h in range(nh)]
    s_hs = [st_ref[h] for h in range(nh)]
    x1s = [_bdot_nt(ars[h], s_hs[h]) for h in range(nh)]
    x2s = [_bdot(jnp.concatenate([jnp.where(strict, g2s[h][:CHUNK], 0.0),
                                  jnp.where(incl, g2s[h][CHUNK:], 0.0)], axis=0), v_all[:, sls[h]])
           for h in range(nh)]
    us = [-_bdot(tinvs[h], x1s[h][:CHUNK] + x2s[h][:CHUNK]) for h in range(nh)]
    outs = [x1s[h][CHUNK:] + x2s[h][CHUNK:] + _bdot(jnp.where(incl, g1s[h][CHUNK:], 0.0), us[h])
            for h in range(nh)]
    for h in range(nh):
        uv = jnp.concatenate([us[h], v_all[:, sls[h]]], axis=0).astype(BF16)
        bk = jnp.concatenate([bh_all[:, sls[h]], kh_all[:, sls[h]]], axis=0).astype(BF16)
        upd = lax.dot_general(uv, bk, (((0,), (0,)), ((), ())), preferred_element_type=F32)
        st_ref[h] = s_hs[h] * g_tot[:, sls[h]] + upd
    y_ref[...] = jnp.concatenate(outs, axis=1)


def _rwkv_scan(r, k, v, kk, lw, a, k_a, batch, seq):
    nc = seq // CHUNK
    ns = D_RWKV // SLAB
    r4 = r.reshape(batch, seq, D_RWKV)
    k4 = k.reshape(batch, seq, D_RWKV)
    v4 = v.reshape(batch, seq, D_RWKV)
    kk4 = kk.reshape(batch, seq, D_RWKV)
    lw5 = lw.reshape(2, batch, seq, D_RWKV)
    a5 = a.reshape(2, batch, seq, D_RWKV)

    def tchunk(d, c):
        return c + d * (nc - 1 - 2 * c)

    shared = pl.BlockSpec((None, CHUNK, SLAB), lambda d, b, s, c: (b, tchunk(d, c), s))
    perdir = pl.BlockSpec((None, None, CHUNK, SLAB), lambda d, b, s, c: (d, b, tchunk(d, c), s))
    y = pl.pallas_call(
        _scan_kernel,
        grid=(2, batch, ns, nc),
        in_specs=[shared, shared, shared, shared, perdir, perdir,
                  pl.BlockSpec((1, SLAB), lambda d, b, s, c: (0, s))],
        out_specs=perdir,
        out_shape=jax.ShapeDtypeStruct((2, batch, seq, D_RWKV), F32),
        scratch_shapes=[pltpu.VMEM((SLAB // HEAD, HEAD, HEAD), F32)],
        compiler_params=_cparams(("parallel", "parallel", "parallel", "arbitrary")),
        name="rwkv_scan",
    )(r4, k4, v4, kk4, lw5, a5, k_a)
    return y.reshape(2, batch * seq, D_RWKV)


def _post_kernel(y_ref, r_ref, k_ref, v_ref, a_ref, g_ref, ka_ref, rk_ref, lnw_ref, lnb_ref,
                 e_ref, et_ref, o_ref):
    y = y_ref[0] + y_ref[1]
    mu = _head_sum_bcast(y, e_ref, et_ref) * (1.0 / HEAD)
    yc = y - mu
    var = _head_sum_bcast(yc * yc, e_ref, et_ref) * (1.0 / HEAD)
    yn = yc * lax.rsqrt(var + GN_EPS) * lnw_ref[...] + lnb_ref[...]
    ka = ka_ref[...]
    k = k_ref[...].astype(F32)
    kd_sum = (k * (1.0 + (a_ref[0].astype(F32) - 1.0) * ka)
              + k * (1.0 + (a_ref[1].astype(F32) - 1.0) * ka))
    bonus = (_head_sum_bcast(r_ref[...].astype(F32) * kd_sum * rk_ref[...], e_ref, et_ref)
             * v_ref[...].astype(F32))
    o_ref[...] = ((yn + bonus) * g_ref[...].astype(F32)).astype(o_ref.dtype)


def _rwkv_post(y, r, k, v, a, g, k_a, r_k, ln_w, ln_b, e_mat, et_mat, tm=128):
    t, c = r.shape
    row = pl.BlockSpec((tm, c), lambda i: (i, 0))
    row2 = pl.BlockSpec((2, tm, c), lambda i: (0, i, 0))
    par = pl.BlockSpec((1, c), lambda i: (0, 0))
    return pl.pallas_call(
        _post_kernel,
        grid=(t // tm,),
        in_specs=[row2, row, row, row, row2, row, par, par, par, par,
                  pl.BlockSpec((c, 128), lambda i: (0, 0)), pl.BlockSpec((128, c), lambda i: (0, 0))],
        out_specs=row,
        out_shape=jax.ShapeDtypeStruct((t, c), BF16),
        compiler_params=_cparams(("parallel",)),
        name="rwkv_post",
    )(y, r, k, v, a, g, k_a, r_k, ln_w, ln_b, e_mat, et_mat)


def _s5_tables(lam_re, lam_im, log_step, b_re, b_im, c_re, c_im, d_skip):
    L, P, CH, G = S5_L, S5_STATE, S5_CH, S5_GROUPS
    lam = lax.complex(lam_re.astype(F32), lam_im.astype(F32))
    dt = jnp.exp(log_step.astype(F32))[..., None]
    lam_dt = lam * dt
    lam_bar = jnp.exp(lam_dt)
    b = lax.complex(b_re.astype(F32), b_im.astype(F32))
    b_bar = ((lam_bar - 1.0) / lam)[..., None] * b
    c = lax.complex(c_re.astype(F32), c_im.astype(F32))
    taus = jnp.arange(L + 1, dtype=F32)
    pows = jnp.exp(lam_dt[:, :, None, :] * taus[None, None, :, None])
    kern = jnp.real(jnp.einsum('dgop,dgtp,dgpi->dgtoi', c, pows[:, :, :L], b_bar,
                               precision=lax.Precision.HIGHEST))
    j = jnp.arange(L)[:, None]
    t = jnp.arange(L)[None, :]
    lag_f = jnp.clip(t - j, 0, L - 1)
    lag_b = jnp.clip(j - t, 0, L - 1)
    tf = jnp.where((t >= j)[None, :, :, None, None], kern[0][:, lag_f], 0.0)
    tb = jnp.where((t <= j)[None, :, :, None, None], kern[1][:, lag_b], 0.0)
    skip = (jnp.eye(L)[None, :, :, None, None] * jnp.eye(CH)[None, None, None]
            * d_skip.astype(F32).reshape(G, 1, 1, CH, 1))
    tfull = (tf + tb + skip).transpose(0, 1, 4, 2, 3).reshape(G, L * CH, L * CH)
    zf = pows[0][:, ::-1][:, 1:][:, :, :, None] * b_bar[0][:, None]
    zb = pows[1][:, :L][:, :, :, None] * b_bar[1][:, None]
    def rows_jc(z):
        return z.transpose(0, 1, 3, 2).reshape(G, L * CH, P)
    wz = jnp.concatenate([jnp.real(rows_jc(zf)), jnp.imag(rows_jc(zf)),
                          jnp.real(rows_jc(zb)), jnp.imag(rows_jc(zb))], axis=-1)
    yf = c[0][:, None] * pows[0][:, 1:][:, :, None, :]
    yb = c[1][:, None] * pows[1][:, ::-1][:, :L][:, :, None, :]
    def cols_tc(y):
        return y.transpose(0, 3, 1, 2).reshape(G, P, L * CH)
    wy = jnp.concatenate([jnp.real(cols_tc(yf)), -jnp.imag(cols_tc(yf)),
                          jnp.real(cols_tc(yb)), -jnp.imag(cols_tc(yb))], axis=1)
    steps = (L * (2.0 ** jnp.arange(8, dtype=F32)))
    lp = jnp.exp(lam_dt[:, :, None, :] * steps[None, None, :, None])
    re, im = jnp.real(lp), jnp.imag(lp)
    m1 = jnp.concatenate([re, re], axis=-1)
    m2 = jnp.concatenate([-im, im], axis=-1)
    mult = jnp.stack([m1[0], m2[0], m1[1], m2[1]], axis=2)
    return tfull.astype(BF16), wz.astype(BF16), wy.astype(BF16), mult


def _s5_kernel(n_chunks, u_ref, t_ref, wz_ref, wy_ref, mult_ref, y_ref):
    u = u_ref[...]
    rows = u.shape[0]
    z = jnp.dot(u, wz_ref[...], preferred_element_type=F32)
    xf = z[:, :128]
    xb = z[:, 128:]
    cidx = lax.broadcasted_iota(jnp.int32, (rows, 128), 0) % n_chunks
    levels = int(math.log2(n_chunks))
    for i in range(levels):
        sh = 1 << i
        mf1 = mult_ref[i, 0:1, :]
        mf2 = mult_ref[i, 1:2, :]
        mb1 = mult_ref[i, 2:3, :]
        mb2 = mult_ref[i, 3:4, :]
        sf = jnp.where(cidx >= sh, pltpu.roll(xf, sh, 0), 0.0)
        xf = xf + sf * mf1 + pltpu.roll(sf, 64, 1) * mf2
        sb = jnp.where(cidx < n_chunks - sh, pltpu.roll(xb, rows - sh, 0), 0.0)
        xb = xb + sb * mb1 + pltpu.roll(sb, 64, 1) * mb2
    x_prev = jnp.where(cidx >= 1, pltpu.roll(xf, 1, 0), 0.0)
    x_next = jnp.where(cidx < n_chunks - 1, pltpu.roll(xb, rows - 1, 0), 0.0)
    xin = jnp.concatenate([x_prev, x_next], axis=1).astype(BF16)
    y = jnp.dot(u, t_ref[...], preferred_element_type=F32)
    y = y + jnp.dot(xin, wy_ref[...], preferred_element_type=F32)
    y_ref[...] = y


def _s5_mix(u_g, tfull, wz, wy, mult, n_chunks):
    g, rows, w = u_g.shape
    mat = pl.BlockSpec((None, w, w), lambda i: (i, 0, 0))
    return pl.pallas_call(
        functools.partial(_s5_kernel, n_chunks),
        grid=(g,),
        in_specs=[pl.BlockSpec((None, rows, w), lambda i: (i, 0, 0)), mat, mat, mat,
                  pl.BlockSpec((None, 8, 4, 128), lambda i: (i, 0, 0, 0))],
        out_specs=pl.BlockSpec((None, rows, w), lambda i: (i, 0, 0)),
        out_shape=jax.ShapeDtypeStruct((g, rows, w), F32),
        compiler_params=_cparams(("parallel",)),
        name="s5_mix",
    )(u_g, tfull, wz, wy, mult)


def _glu_kernel(y_ref, w_ref, b_ref, o_ref):
    y = y_ref[...]
    z = jnp.dot(jax.nn.gelu(y).astype(BF16), w_ref[...], preferred_element_type=F32) + b_ref[...]
    o_ref[...] = (y * jax.nn.sigmoid(z)).astype(o_ref.dtype)


def _glu(y, w, b, tm=512):
    t, c = y.shape
    tm = min(tm, t)
    return pl.pallas_call(
        _glu_kernel,
        grid=(t // tm,),
        in_specs=[pl.BlockSpec((tm, c), lambda i: (i, 0)), pl.BlockSpec((c, c), lambda i: (0, 0)),
                  pl.BlockSpec((1, c), lambda i: (0, 0))],
        out_specs=pl.BlockSpec((tm, c), lambda i: (i, 0)),
        out_shape=jax.ShapeDtypeStruct((t, c), BF16),
        compiler_params=_cparams(("parallel",)),
        name="s5_glu",
    )(y, w, b)


def _wqk_kernel(wq_ref, k_ref, o_ref):
    acc = lax.dot_general(wq_ref[...], k_ref[...], (((1,), (1,)), ((), ())), preferred_element_type=F32)
    o_ref[...] = (acc * ATTN_SCALE).astype(o_ref.dtype)


def _vwo_kernel(v_ref, wo_ref, o_ref):
    o_ref[...] = jnp.dot(v_ref[...], wo_ref[...], preferred_element_type=F32).astype(o_ref.dtype)


def _attn_kernel(mem, xn_ref, wqk_ref, vwo_ref, res_ref, o_ref, p_ref):
    @pl.when(pl.program_id(2) == 0)
    def _():
        s = jnp.dot(xn_ref[...], wqk_ref[...], preferred_element_type=F32)
        for h in range(ATTN_HEADS):
            sh = s[:, h * mem:(h + 1) * mem]
            sh = sh - jnp.max(sh, axis=-1, keepdims=True)
            p = jnp.exp(sh)
            p_ref[:, h * mem:(h + 1) * mem] = (p / jnp.sum(p, axis=-1, keepdims=True)).astype(BF16)

    o_ref[...] = jnp.dot(p_ref[...], vwo_ref[...], preferred_element_type=F32) + res_ref[...]


def _cross_attention(xn, kv, w_q, w_o, res, batch, seq, mem, tm=512, tn=1024):
    d = D_MODEL
    hm = ATTN_HEADS * mem
    tm = min(tm, seq)
    wqk = pl.pallas_call(
        _wqk_kernel,
        grid=(batch, ATTN_HEADS),
        in_specs=[pl.BlockSpec((d, ATTN_HEAD_DIM), lambda b, h: (0, h)),
                  pl.BlockSpec((mem, ATTN_HEAD_DIM), lambda b, h: (b, h))],
        out_specs=pl.BlockSpec((None, d, mem), lambda b, h: (b, 0, h)),
        out_shape=jax.ShapeDtypeStruct((batch, d, hm), BF16),
        compiler_params=_cparams(("parallel", "parallel")),
        name="attn_wqk",
    )(w_q, kv)
    vwo = pl.pallas_call(
        _vwo_kernel,
        grid=(batch, ATTN_HEADS),
        in_specs=[pl.BlockSpec((mem, ATTN_HEAD_DIM), lambda b, h: (b, ATTN_HEADS + h)),
                  pl.BlockSpec((ATTN_HEAD_DIM, d), lambda b, h: (h, 0))],
        out_specs=pl.BlockSpec((None, mem, d), lambda b, h: (b, h, 0)),
        out_shape=jax.ShapeDtypeStruct((batch, hm, d), BF16),
        compiler_params=_cparams(("parallel", "parallel")),
        name="attn_vwo",
    )(kv, w_o)
    out = pl.pallas_call(
        functools.partial(_attn_kernel, mem),
        grid=(batch, seq // tm, d // tn),
        in_specs=[pl.BlockSpec((None, tm, d), lambda b, i, j: (b, i, 0)),
                  pl.BlockSpec((None, d, hm), lambda b, i, j: (b, 0, 0)),
                  pl.BlockSpec((None, hm, tn), lambda b, i, j: (b, 0, j)),
                  pl.BlockSpec((None, tm, tn), lambda b, i, j: (b, i, j))],
        out_specs=pl.BlockSpec((None, tm, tn), lambda b, i, j: (b, i, j)),
        out_shape=jax.ShapeDtypeStruct((batch, seq, d), F32),
        scratch_shapes=[pltpu.VMEM((tm, hm), BF16)],
        compiler_params=_cparams(("parallel", "parallel", "arbitrary")),
        name="cross_attn",
    )(xn.reshape(batch, seq, d), wqk, vwo, res.reshape(batch, seq, d))
    return out.reshape(batch * seq, d)


def _router_kernel(x_ref, g_ref, whi_ref, wlo_ref, b_ref, xn_ref, info_ref, cnt_ref, carry_ref):
    i = pl.program_id(0)

    @pl.when(i == 0)
    def _():
        carry_ref[...] = jnp.zeros_like(carry_ref)

    x = x_ref[...]
    tm = x.shape[0]
    ms = jnp.mean(x * x, axis=-1, keepdims=True)
    xn = x * lax.rsqrt(ms + RMS_EPS) * g_ref[...]
    xn_ref[...] = xn
    xh, xl = _split2(xn)
    whi = whi_ref[...]
    logits = (jnp.dot(xh, whi, preferred_element_type=F32) + jnp.dot(xl, whi, preferred_element_type=F32)
              + jnp.dot(xh, wlo_ref[...], preferred_element_type=F32)) + b_ref[...]
    li = lax.broadcasted_iota(jnp.int32, logits.shape, 1)
    neg = jnp.float32(-jnp.inf)
    is_g = li < N_GROUPS
    gl = jnp.where(is_g, logits, neg)
    gm = jnp.max(gl, axis=-1, keepdims=True)
    gi = jnp.min(jnp.where(is_g & (gl == gm), li, 128), axis=-1, keepdims=True)
    gp = 1.0 / jnp.sum(jnp.where(is_g, jnp.exp(gl - gm), 0.0), axis=-1, keepdims=True)
    lo_lane = N_GROUPS + EPG * gi
    sel = (li >= lo_lane) & (li < lo_lane + EPG)
    l1 = jnp.where(sel, logits, neg)
    e1 = jnp.max(l1, axis=-1, keepdims=True)
    i1 = jnp.min(jnp.where(sel & (l1 == e1), li, 128), axis=-1, keepdims=True)
    sel2 = sel & (li != i1)
    l2 = jnp.where(sel2, logits, neg)
    e2 = jnp.max(l2, axis=-1, keepdims=True)
    i2 = jnp.min(jnp.where(sel2 & (l2 == e2), li, 128), axis=-1, keepdims=True)
    ex = jnp.exp(e2 - e1)
    w1 = gp / (1.0 + ex)
    w2 = gp * ex / (1.0 + ex)
    x1 = i1 - N_GROUPS
    x2 = i2 - N_GROUPS
    oh1 = li == x1
    oh2 = li == x2
    oh = (oh1 | oh2).astype(BF16)
    ri = lax.broadcasted_iota(jnp.int32, (tm, tm), 0)
    ci = lax.broadcasted_iota(jnp.int32, (tm, tm), 1)
    before = jnp.dot((ci < ri).astype(BF16), oh, preferred_element_type=F32) + carry_ref[0:1, :]
    r1 = jnp.sum(jnp.where(oh1, before, 0.0), axis=-1, keepdims=True)
    r2 = jnp.sum(jnp.where(oh2, before, 0.0), axis=-1, keepdims=True)
    new_carry = carry_ref[0:1, :] + jnp.sum(oh.astype(F32), axis=0, keepdims=True)
    carry_ref[...] = jnp.broadcast_to(new_carry, carry_ref.shape)
    cnt_ref[...] = jnp.broadcast_to(new_carry, cnt_ref.shape)
    info = jnp.where(li == 0, x1.astype(F32), 0.0)
    info = jnp.where(li == 1, x2.astype(F32), info)
    info = jnp.where(li == 2, w1, info)
    info = jnp.where(li == 3, w2, info)
    info = jnp.where(li == 4, r1, info)
    info = jnp.where(li == 5, r2, info)
    info_ref[...] = info


def _router(x, gain, w_hi, w_lo, bias, tm=256):
    t, d = x.shape
    sd = jax.ShapeDtypeStruct
    return pl.pallas_call(
        _router_kernel,
        grid=(t // tm,),
        in_specs=[pl.BlockSpec((tm, d), lambda i: (i, 0)), pl.BlockSpec((1, d), lambda i: (0, 0)),
                  pl.BlockSpec((d, 128), lambda i: (0, 0)), pl.BlockSpec((d, 128), lambda i: (0, 0)),
                  pl.BlockSpec((1, 128), lambda i: (0, 0))],
        out_specs=[pl.BlockSpec((tm, d), lambda i: (i, 0)), pl.BlockSpec((tm, 128), lambda i: (i, 0)),
                   pl.BlockSpec((8, 128), lambda i: (0, 0))],
        out_shape=[sd((t, d), F32), sd((t, 128), F32), sd((8, 128), F32)],
        scratch_shapes=[pltpu.VMEM((8, 128), F32)],
        compiler_params=_cparams(("arbitrary",)),
        name="moe_router",
    )(x, gain.reshape(1, d), w_hi, w_lo, bias)


def _gather_rows(src_hbm, idx_ref, dst_ref, sem, n_rows):
    def body(j, carry):
        pltpu.make_async_copy(src_hbm.at[pl.ds(idx_ref[0, j], 1), :], dst_ref.at[pl.ds(j, 1), :], sem).start()
        return carry
    lax.fori_loop(0, n_rows, body, 0)


def _wait_rows(src_hbm, dst_ref, sem, n_rows):
    pltpu.make_async_copy(src_hbm.at[pl.ds(0, n_rows), :], dst_ref, sem).wait()


def _expert_up_kernel(be_ref, nused_ref, tok_ref, tokn_ref, x_hbm, wg_ref, wu_ref, h_ref, xbuf, sem):
    i = pl.program_id(0)
    n_used = nused_ref[0]
    slot = i % 2

    @pl.when(i == 0)
    def _():
        _gather_rows(x_hbm, tok_ref, xbuf.at[0], sem.at[0], MOE_BLK)

    @pl.when(i + 1 < n_used)
    def _():
        _gather_rows(x_hbm, tokn_ref, xbuf.at[1 - slot], sem.at[1 - slot], MOE_BLK)

    @pl.when(i < n_used)
    def _():
        _wait_rows(x_hbm, xbuf.at[slot], sem.at[slot], MOE_BLK)
        d = xbuf.shape[2]
        hg = jnp.zeros((MOE_BLK, D_EXPERT), F32)
        hu = jnp.zeros((MOE_BLK, D_EXPERT), F32)
        for kc in range(d // EXPERT_KC):
            ks = slice(kc * EXPERT_KC, (kc + 1) * EXPERT_KC)
            xb = xbuf[slot, :, ks].astype(BF16)
            hg = hg + jnp.dot(xb, wg_ref[ks, :].astype(BF16), preferred_element_type=F32)
            hu = hu + jnp.dot(xb, wu_ref[ks, :].astype(BF16), preferred_element_type=F32)
        h_ref[...] = (jax.nn.silu(hg) * hu).astype(h_ref.dtype)

    @pl.when(i >= n_used)
    def _():
        h_ref[...] = jnp.zeros_like(h_ref)


def _expert_down_kernel(be_ref, nused_ref, h_ref, wd_ref, y_ref):
    i = pl.program_id(0)
    n_used = nused_ref[0]

    @pl.when(i < n_used)
    def _():
        h = h_ref[...]
        d = y_ref.shape[1]
        for nc in range(d // EXPERT_KC):
            ns = slice(nc * EXPERT_KC, (nc + 1) * EXPERT_KC)
            y_ref[:, ns] = jnp.dot(h, wd_ref[:, ns].astype(BF16), preferred_element_type=F32)

    @pl.when(i >= n_used)
    def _():
        y_ref[...] = jnp.zeros_like(y_ref)


def _experts(block_expert, n_used, tok3, xn, wg, wu, wd):
    nb = tok3.shape[0]
    d = xn.shape[1]
    tok_spec = pl.BlockSpec((None, 1, MOE_BLK), lambda i, be, nu: (i, 0, 0), memory_space=pltpu.SMEM)
    tokn_spec = pl.BlockSpec((None, 1, MOE_BLK), lambda i, be, nu: (jnp.minimum(i + 1, nb - 1), 0, 0),
                             memory_space=pltpu.SMEM)
    h_buf = pl.pallas_call(
        _expert_up_kernel,
        grid_spec=pltpu.PrefetchScalarGridSpec(
            num_scalar_prefetch=2,
            grid=(nb,),
            in_specs=[tok_spec, tokn_spec, pl.BlockSpec(memory_space=pl.ANY),
                      pl.BlockSpec((None, d, D_EXPERT), lambda i, be, nu: (be[i], 0, 0)),
                      pl.BlockSpec((None, d, D_EXPERT), lambda i, be, nu: (be[i], 0, 0))],
            out_specs=pl.BlockSpec((MOE_BLK, D_EXPERT), lambda i, be, nu: (i, 0)),
            scratch_shapes=[pltpu.VMEM((2, MOE_BLK, d), F32), pltpu.SemaphoreType.DMA((2,))],
        ),
        out_shape=jax.ShapeDtypeStruct((nb * MOE_BLK, D_EXPERT), BF16),
        compiler_params=_cparams(("arbitrary",)),
        name="moe_up",
    )(block_expert, n_used, tok3, tok3, xn, wg, wu)
    return pl.pallas_call(
        _expert_down_kernel,
        grid_spec=pltpu.PrefetchScalarGridSpec(
            num_scalar_prefetch=2,
            grid=(nb,),
            in_specs=[pl.BlockSpec((MOE_BLK, D_EXPERT), lambda i, be, nu: (i, 0)),
                      pl.BlockSpec((None, D_EXPERT, d), lambda i, be, nu: (be[i], 0, 0))],
            out_specs=pl.BlockSpec((MOE_BLK, d), lambda i, be, nu: (i, 0)),
        ),
        out_shape=jax.ShapeDtypeStruct((nb * MOE_BLK, d), F32),
        compiler_params=_cparams(("arbitrary",)),
        name="moe_down",
    )(block_expert, n_used, h_buf, wd)


def _combine_kernel(d1_ref, d1n_ref, d2_ref, d2n_ref, x_ref, info_ref, g_ref, y_hbm, o_ref, ybuf, sem):
    i = pl.program_id(0)
    n = pl.num_programs(0)
    slot = i % 2

    @pl.when(i == 0)
    def _():
        _gather_rows(y_hbm, d1_ref, ybuf.at[0, 0], sem.at[0], COMB_TM)
        _gather_rows(y_hbm, d2_ref, ybuf.at[0, 1], sem.at[0], COMB_TM)

    @pl.when(i + 1 < n)
    def _():
        _gather_rows(y_hbm, d1n_ref, ybuf.at[1 - slot, 0], sem.at[1 - slot], COMB_TM)
        _gather_rows(y_hbm, d2n_ref, ybuf.at[1 - slot, 1], sem.at[1 - slot], COMB_TM)

    _wait_rows(y_hbm, ybuf.at[slot, 0], sem.at[slot], COMB_TM)
    _wait_rows(y_hbm, ybuf.at[slot, 1], sem.at[slot], COMB_TM)
    info = info_ref[...]
    w1 = info[:, 2:3]
    w2 = info[:, 3:4]
    x = x_ref[...] + ybuf[slot, 0] * w1 + ybuf[slot, 1] * w2
    ms = jnp.mean(x * x, axis=-1, keepdims=True)
    o_ref[...] = x * lax.rsqrt(ms + RMS_EPS) * g_ref[...]


def _combine(dest1, dest2, x, info, gain, y_buf):
    t, d = x.shape
    nt = t // COMB_TM
    d1 = dest1.reshape(nt, 1, COMB_TM)
    d2 = dest2.reshape(nt, 1, COMB_TM)
    cur = pl.BlockSpec((None, 1, COMB_TM), lambda i: (i, 0, 0), memory_space=pltpu.SMEM)
    nxt = pl.BlockSpec((None, 1, COMB_TM), lambda i: (jnp.minimum(i + 1, nt - 1), 0, 0),
                       memory_space=pltpu.SMEM)
    return pl.pallas_call(
        _combine_kernel,
        grid=(nt,),
        in_specs=[cur, nxt, cur, nxt,
                  pl.BlockSpec((COMB_TM, d), lambda i: (i, 0)),
                  pl.BlockSpec((COMB_TM, 128), lambda i: (i, 0)),
                  pl.BlockSpec((1, d), lambda i: (0, 0)),
                  pl.BlockSpec(memory_space=pl.ANY)],
        out_specs=pl.BlockSpec((COMB_TM, d), lambda i: (i, 0)),
        out_shape=jax.ShapeDtypeStruct((t, d), F32),
        scratch_shapes=[pltpu.VMEM((2, 2, COMB_TM, d), F32), pltpu.SemaphoreType.DMA((2,))],
        compiler_params=_cparams(("arbitrary",)),
        name="moe_combine",
    )(d1, d1, d2, d2, x, info, gain.reshape(1, d), y_buf)


def _head_matrices():
    lane = jnp.arange(D_RWKV) // HEAD
    e = (lane[:, None] == jnp.arange(128)[None, :]).astype(BF16)
    return e, e.T


def _pad_lora_rows(w):
    return jnp.pad(w, ((0, 0), (0, LORA_PAD - w.shape[1]), (0, 0)))


def _layer(x, mem, norm_mix, w_in, shift_taps, rwkv_w0, rwkv_w2, rwkv_a0, rwkv_a2, rwkv_g2, rwkv_k_k,
           rwkv_k_a, rwkv_r_k, rwkv_ln_w, rwkv_ln_b, s5_lam_re, s5_lam_im, s5_log_step, s5_b_re, s5_b_im,
           s5_c_re, s5_c_im, s5_d, s5_glu_w, s5_glu_b, w_out, norm_attn, norm_mem, w_q, w_k, w_v, w_o,
           norm_ffn, router_grp_w, router_grp_b, router_exp_w, router_exp_b, exp_w_gate, exp_w_up,
           exp_w_down, out_gain):
    batch, seq, d = x.shape
    t = batch * seq
    mem_n = mem.shape[1]
    xt = x.reshape(t, d)
    c3 = 3 * D_RWKV
    off_g = c3 + 4 * 96

    def pad_cols(w, lo, width):
        return jnp.pad(w[:, lo:lo + width], ((0, 0), (0, LORA_PAD - width)))

    w_main = jnp.concatenate([w_in[:, :c3], w_in[:, off_g + 256:]], axis=1).astype(BF16)
    w_lora = jnp.concatenate([pad_cols(w_in, c3 + i * 96, 96) for i in range(4)]
                             + [w_in[:, off_g:off_g + 256]], axis=1).astype(BF16)
    taps_l = jnp.concatenate([pad_cols(shift_taps, c3 + i * 96, 96) for i in range(4)]
                             + [shift_taps[:, off_g:off_g + 256]], axis=1)
    h = _rmsnorm(xt, norm_mix, BF16)
    p_main = _matmul(h, w_main, F32, 1024, 512, name="w_in_main")
    p_lora = _matmul(h, w_lora, F32, 1024, N_LORA, name="w_in_lora")

    e_mat, et_mat = _head_matrices()
    k_a = rwkv_k_a.reshape(1, D_RWKV)
    r, k, v, kk, lw, a, g = _rwkv_prep(
        p_main, p_lora, seq, shift_taps[:, :D_RWKV], shift_taps[:, D_RWKV:2 * D_RWKV],
        shift_taps[:, 2 * D_RWKV:c3], taps_l, rwkv_w0, _pad_lora_rows(rwkv_w2).astype(BF16), rwkv_a0,
        _pad_lora_rows(rwkv_a2).astype(BF16), rwkv_g2.astype(BF16), rwkv_k_k.reshape(1, D_RWKV),
        e_mat, et_mat)
    y_scan = _rwkv_scan(r, k, v, kk, lw, a, k_a, batch, seq)
    y_rwkv = _rwkv_post(y_scan, r, k, v, a, g, k_a, rwkv_r_k.reshape(1, D_RWKV),
                        rwkv_ln_w.reshape(1, D_RWKV), rwkv_ln_b.reshape(1, D_RWKV), e_mat, et_mat)

    n_chunks = seq // S5_L
    tfull, wz, wy, mult = _s5_tables(s5_lam_re, s5_lam_im, s5_log_step, s5_b_re, s5_b_im,
                                     s5_c_re, s5_c_im, s5_d)
    u = p_main[:, c3:].astype(BF16)
    u_g = u.reshape(t // S5_L, S5_L, S5_GROUPS, S5_CH).transpose(2, 0, 1, 3).reshape(
        S5_GROUPS, t // S5_L, S5_L * S5_CH)
    y_g = _s5_mix(u_g, tfull, wz, wy, mult, n_chunks)
    y_s5 = y_g.reshape(S5_GROUPS, t // S5_L, S5_L, S5_CH).transpose(1, 2, 0, 3).reshape(t, D_S5)
    y_glu = _glu(y_s5, s5_glu_w.astype(BF16), s5_glu_b.reshape(1, D_S5))

    x1 = _matmul2_res(y_rwkv, y_glu, w_out.astype(BF16), xt, 1024, 512)

    xn = _rmsnorm(x1, norm_attn, BF16)
    memn = _rmsnorm(mem.reshape(batch * mem_n, d), norm_mem, BF16)
    wkv = jnp.concatenate([w_k, w_v], axis=1).astype(BF16)
    kv = _matmul(memn, wkv, BF16, 1024, 512, name="w_kv")
    x2 = _cross_attention(xn, kv, w_q.astype(BF16), w_o.astype(BF16), x1, batch, seq, mem_n)

    w_r = jnp.concatenate([router_grp_w, router_exp_w,
                           jnp.zeros((d, 128 - N_GROUPS - N_EXPERTS), F32)], axis=1)
    w_r_hi = w_r.astype(BF16)
    w_r_lo = (w_r - w_r_hi.astype(F32)).astype(BF16)
    b_r = jnp.concatenate([router_grp_b, router_exp_b,
                           jnp.zeros((128 - N_GROUPS - N_EXPERTS,), F32)]).reshape(1, 128)
    xn3, info, cnt = _router(x2, norm_ffn, w_r_hi, w_r_lo, b_r)
    eid = info[:, 0:2].astype(jnp.int32)
    rank = info[:, 4:6].astype(jnp.int32)
    counts = cnt[0, :N_EXPERTS].astype(jnp.int32)
    nblk = (counts + MOE_BLK - 1) // MOE_BLK
    bstart = jnp.cumsum(nblk) - nblk
    n_used = jnp.sum(nblk)
    first = jnp.sum(jnp.where(eid[:, :, None] == jnp.arange(N_EXPERTS)[None, None, :], bstart, 0), axis=-1)
    dest = first * MOE_BLK + rank
    nb = (t * TOP_K) // MOE_BLK + N_EXPERTS
    tok_buf = jnp.zeros((nb * MOE_BLK,), jnp.int32).at[dest.reshape(-1)].set(
        jnp.repeat(jnp.arange(t, dtype=jnp.int32), TOP_K))
    blk = jnp.arange(nb, dtype=jnp.int32)
    block_expert = jnp.sum(blk[:, None] >= (bstart + nblk)[None, :], axis=1).astype(jnp.int32)
    last_e = jnp.max(jnp.where(nblk > 0, jnp.arange(N_EXPERTS), 0)).astype(jnp.int32)
    block_expert = jnp.minimum(block_expert, last_e)
    y_buf = _experts(block_expert, n_used.reshape(1).astype(jnp.int32), tok_buf.reshape(nb, 1, MOE_BLK),
                     xn3, exp_w_gate, exp_w_up, exp_w_down)
    out = _combine(dest[:, 0], dest[:, 1], x2, info, out_gain, y_buf)
    return out.reshape(batch, seq, d)


def kernel(x, mem, norm_mix, w_in, shift_taps, rwkv_w0, rwkv_w2, rwkv_a0, rwkv_a2, rwkv_g2, rwkv_k_k, rwkv_k_a, rwkv_r_k, rwkv_ln_w, rwkv_ln_b, s5_lam_re, s5_lam_im, s5_log_step, s5_b_re, s5_b_im, s5_c_re, s5_c_im, s5_d, s5_glu_w, s5_glu_b, w_out, norm_attn, norm_mem, w_q, w_k, w_v, w_o, norm_ffn, router_grp_w, router_grp_b, router_exp_w, router_exp_b, exp_w_gate, exp_w_up, exp_w_down, norm_final):
    return _layer(x, mem, norm_mix[0], w_in[0], shift_taps[0], rwkv_w0[0], rwkv_w2[0], rwkv_a0[0],
                  rwkv_a2[0], rwkv_g2[0], rwkv_k_k[0], rwkv_k_a[0], rwkv_r_k[0], rwkv_ln_w[0],
                  rwkv_ln_b[0], s5_lam_re[0], s5_lam_im[0], s5_log_step[0], s5_b_re[0], s5_b_im[0],
                  s5_c_re[0], s5_c_im[0], s5_d[0], s5_glu_w[0], s5_glu_b[0], w_out[0], norm_attn[0],
                  norm_mem[0], w_q[0], w_k[0], w_v[0], w_o[0], norm_ffn[0], router_grp_w[0],
                  router_grp_b[0], router_exp_w[0], router_exp_b[0], exp_w_gate[0], exp_w_up[0],
                  exp_w_down[0], norm_final)
```

```python
import functools
import math

import jax
import jax.numpy as jnp
from jax import lax
from jax.experimental import pallas as pl
from jax.experimental.pallas import tpu as pltpu

F32 = jnp.float32
BF16 = jnp.bfloat16

D_MODEL = 4096
D_RWKV = 2048
D_S5 = 2048
HEAD = 64
PAIR = 2 * HEAD
N_HEADS = D_RWKV // HEAD
LORA_PAD = 128
N_LORA = 4 * LORA_PAD + 256
N_MAIN = 3 * D_RWKV + D_S5
S5_CH = 16
S5_GROUPS = D_S5 // S5_CH
S5_STATE = 64
S5_L = 8
S5_GB = 8
ATTN_HEADS = 4
ATTN_HEAD_DIM = D_MODEL // ATTN_HEADS
ATTN_SCALE = ATTN_HEAD_DIM ** -0.5
N_GROUPS = 8
EPG = 8
N_EXPERTS = N_GROUPS * EPG
TOP_K = 2
D_EXPERT = D_MODEL // 8
RMS_EPS = 1e-6
GN_EPS = 64e-5
L2_EPS = 1e-12

CHUNK = 64
SLAB = 2048
MOE_BLK = 256
COMB_TM = 128
EXPERT_KC = 1024
VMEM_LIMIT = 56 * 1024 * 1024


def _cparams(sem):
    return pltpu.CompilerParams(dimension_semantics=sem, vmem_limit_bytes=VMEM_LIMIT)


def _bdot(a, b):
    return jnp.dot(a.astype(BF16), b.astype(BF16), preferred_element_type=F32)


def _bdot_nt(a, b):
    return lax.dot_general(a.astype(BF16), b.astype(BF16), (((1,), (1,)), ((), ())),
                           preferred_element_type=F32)


def _split2(x):
    hi = x.astype(BF16)
    lo = (x - hi.astype(F32)).astype(BF16)
    return hi, lo


def _split3(x):
    hi = x.astype(BF16)
    r1 = x - hi.astype(F32)
    mid = r1.astype(BF16)
    lo = (r1 - mid.astype(F32)).astype(BF16)
    return hi, mid, lo


def _dot_exact_rhs(x, e):
    h, m, l = _split3(x)
    return (jnp.dot(h, e, preferred_element_type=F32) + jnp.dot(m, e, preferred_element_type=F32)
            + jnp.dot(l, e, preferred_element_type=F32))


def _dot_exact_lhs(e, x):
    h, m, l = _split3(x)
    return (jnp.dot(e, h, preferred_element_type=F32) + jnp.dot(e, m, preferred_element_type=F32)
            + jnp.dot(e, l, preferred_element_type=F32))


def _rms_kernel(x_ref, g_ref, o_ref):
    x = x_ref[...]
    ms = jnp.mean(x * x, axis=-1, keepdims=True)
    o_ref[...] = (x * lax.rsqrt(ms + RMS_EPS) * g_ref[...]).astype(o_ref.dtype)


def _rmsnorm(x, gain, out_dtype, tm=256):
    t, d = x.shape
    return pl.pallas_call(
        _rms_kernel,
        grid=(t // tm,),
        in_specs=[pl.BlockSpec((tm, d), lambda i: (i, 0)), pl.BlockSpec((1, d), lambda i: (0, 0))],
        out_specs=pl.BlockSpec((tm, d), lambda i: (i, 0)),
        out_shape=jax.ShapeDtypeStruct((t, d), out_dtype),
        compiler_params=_cparams(("parallel",)),
        name="rmsnorm",
    )(x, gain.reshape(1, d))


def _mm_kernel(a_ref, b_ref, o_ref):
    o_ref[...] = jnp.dot(a_ref[...], b_ref[...], preferred_element_type=F32).astype(o_ref.dtype)


def _mm_res_kernel(a_ref, b_ref, r_ref, o_ref):
    o_ref[...] = (jnp.dot(a_ref[...], b_ref[...], preferred_element_type=F32)
                  + r_ref[...]).astype(o_ref.dtype)


def _mm2_res_kernel(a1_ref, a2_ref, b_ref, r_ref, o_ref):
    k1 = a1_ref.shape[1]
    acc = jnp.dot(a1_ref[...], b_ref[:k1, :], preferred_element_type=F32)
    acc = acc + jnp.dot(a2_ref[...], b_ref[k1:, :], preferred_element_type=F32)
    o_ref[...] = (acc + r_ref[...]).astype(o_ref.dtype)


def _matmul(a, b, out_dtype, tm, tn, res=None, name="matmul"):
    m, k = a.shape
    n = b.shape[1]
    tm, tn = min(tm, m), min(tn, n)
    in_specs = [pl.BlockSpec((tm, k), lambda i, j: (i, 0)), pl.BlockSpec((k, tn), lambda i, j: (0, j))]
    args = [a, b]
    kern = _mm_kernel
    if res is not None:
        in_specs.append(pl.BlockSpec((tm, tn), lambda i, j: (i, j)))
        args.append(res)
        kern = _mm_res_kernel
    return pl.pallas_call(
        kern,
        grid=(m // tm, n // tn),
        in_specs=in_specs,
        out_specs=pl.BlockSpec((tm, tn), lambda i, j: (i, j)),
        out_shape=jax.ShapeDtypeStruct((m, n), out_dtype),
        compiler_params=_cparams(("parallel", "parallel")),
        name=name,
    )(*args)


def _matmul2_res(a1, a2, b, res, tm, tn):
    m, k1 = a1.shape
    k2 = a2.shape[1]
    n = b.shape[1]
    tm = min(tm, m)
    return pl.pallas_call(
        _mm2_res_kernel,
        grid=(m // tm, n // tn),
        in_specs=[pl.BlockSpec((tm, k1), lambda i, j: (i, 0)),
                  pl.BlockSpec((tm, k2), lambda i, j: (i, 0)),
                  pl.BlockSpec((k1 + k2, tn), lambda i, j: (0, j)),
                  pl.BlockSpec((tm, tn), lambda i, j: (i, j))],
        out_specs=pl.BlockSpec((tm, tn), lambda i, j: (i, j)),
        out_shape=jax.ShapeDtypeStruct((m, n), F32),
        compiler_params=_cparams(("parallel", "parallel")),
        name="w_out",
    )(a1, a2, b, res)


def _head_sum_bcast(x, e_ref, et_ref):
    s = _dot_exact_rhs(x, e_ref[...])
    return _dot_exact_rhs(s, et_ref[...])


def _shift3(x, hp, hn, taps, first, last):
    tm = x.shape[0]
    row = lax.broadcasted_iota(jnp.int32, x.shape, 0)
    prev_edge = jnp.where(first, 0.0, hp[7:8, :])
    next_edge = jnp.where(last, 0.0, hn[0:1, :])
    prev = jnp.where(row == 0, prev_edge, pltpu.roll(x, 1, 0))
    nxt = jnp.where(row == tm - 1, next_edge, pltpu.roll(x, tm - 1, 0))
    return taps[0:1, :] * prev + taps[1:2, :] * x + taps[2:3, :] * nxt


def _prep_kernel(seq_tiles,
                 r_ref, rp_ref, rn_ref, k_ref, kp_ref, kn_ref, v_ref, vp_ref, vn_ref,
                 lo_ref, lop_ref, lon_ref,
                 tr_ref, tk_ref, tv_ref, tl_ref,
                 w0_ref, w2_ref, a0_ref, a2_ref, g2_ref, kk_ref_p, e_ref, et_ref,
                 ro_ref, ko_ref, vo_ref, kko_ref, lw_ref, a_ref, g_ref):
    i = pl.program_id(0)
    first = (i % seq_tiles) == 0
    last = (i % seq_tiles) == seq_tiles - 1
    r = _shift3(r_ref[...], rp_ref[...], rn_ref[...], tr_ref[...], first, last)
    k = _shift3(k_ref[...], kp_ref[...], kn_ref[...], tk_ref[...], first, last)
    v = _shift3(v_ref[...], vp_ref[...], vn_ref[...], tv_ref[...], first, last)
    lo = _shift3(lo_ref[...], lop_ref[...], lon_ref[...], tl_ref[...], first, last)
    ro_ref[...] = r.astype(ro_ref.dtype)
    ko_ref[...] = k.astype(ko_ref.dtype)
    vo_ref[...] = v.astype(vo_ref.dtype)
    for d in range(2):
        xw = lo[:, d * LORA_PAD:(d + 1) * LORA_PAD]
        xa = lo[:, (2 + d) * LORA_PAD:(3 + d) * LORA_PAD]
        wl = w0_ref[d:d + 1, :] + _bdot(jnp.tanh(xw), w2_ref[d])
        w_log = -jax.nn.softplus(-wl) - 0.5
        lw_ref[d] = -jnp.exp(w_log)
        a_ref[d] = jax.nn.sigmoid(a0_ref[d:d + 1, :] + _bdot(xa, a2_ref[d])).astype(a_ref.dtype)
    xg = lo[:, 4 * LORA_PAD:]
    g_ref[...] = _bdot(jax.nn.sigmoid(xg), g2_ref[...]).astype(g_ref.dtype)
    kk = k * kk_ref_p[...]
    ssq = _head_sum_bcast(kk * kk, e_ref, et_ref)
    kko_ref[...] = (kk / jnp.maximum(jnp.sqrt(ssq), L2_EPS)).astype(kko_ref.dtype)


def _rwkv_prep(p_main, p_lora, seq, taps_r, taps_k, taps_v, taps_l, w0, w2p, a0, a2p, g2, k_k, e_mat, et_mat,
               tm=128):
    t = p_main.shape[0]
    nt8 = t // 8
    seq_tiles = seq // tm
    c = D_RWKV

    def main_spec(col):
        return [pl.BlockSpec((tm, c), lambda i, col=col: (i, col)),
                pl.BlockSpec((8, c), lambda i, col=col: (jnp.maximum(i * (tm // 8) - 1, 0), col)),
                pl.BlockSpec((8, c), lambda i, col=col: (jnp.minimum((i + 1) * (tm // 8), nt8 - 1), col))]

    lora_spec = [pl.BlockSpec((tm, N_LORA), lambda i: (i, 0)),
                 pl.BlockSpec((8, N_LORA), lambda i: (jnp.maximum(i * (tm // 8) - 1, 0), 0)),
                 pl.BlockSpec((8, N_LORA), lambda i: (jnp.minimum((i + 1) * (tm // 8), nt8 - 1), 0))]

    def full(shape):
        nd = len(shape)
        return pl.BlockSpec(shape, lambda i, nd=nd: (0,) * nd)

    in_specs = (main_spec(0) + main_spec(1) + main_spec(2) + lora_spec
                + [full((3, c)), full((3, c)), full((3, c)), full((3, N_LORA)),
                   full((2, c)), full((2, LORA_PAD, c)), full((2, c)), full((2, LORA_PAD, c)),
                   full((256, c)), full((1, c)), full((c, 128)), full((128, c))])
    row = pl.BlockSpec((tm, c), lambda i: (i, 0))
    row2 = pl.BlockSpec((2, tm, c), lambda i: (0, i, 0))
    sd = jax.ShapeDtypeStruct
    return pl.pallas_call(
        functools.partial(_prep_kernel, seq_tiles),
        grid=(t // tm,),
        in_specs=in_specs,
        out_specs=[row, row, row, row, row2, row2, row],
        out_shape=[sd((t, c), BF16), sd((t, c), BF16), sd((t, c), BF16), sd((t, c), BF16),
                   sd((2, t, c), F32), sd((2, t, c), BF16), sd((t, c), BF16)],
        compiler_params=_cparams(("parallel",)),
        name="rwkv_prep",
    )(p_main, p_main, p_main, p_main, p_main, p_main, p_main, p_main, p_main,
      p_lora, p_lora, p_lora,
      taps_r, taps_k, taps_v, taps_l, w0, w2p, a0, a2p, g2, k_k, e_mat, et_mat)


def _scan_kernel(r_ref, k_ref, v_ref, kk_ref, lw_ref, a_ref, ka_ref, y_ref, st_ref):
    d = pl.program_id(0)
    c = pl.program_id(3)

    @pl.when(c == 0)
    def _():
        st_ref[...] = jnp.zeros_like(st_ref)

    ti = lax.broadcasted_iota(jnp.int32, (CHUNK, PAIR), 0)
    si = lax.broadcasted_iota(jnp.int32, (CHUNK, PAIR), 1) % HEAD
    ahead = (ti - si) * (1 - 2 * d)
    incl = ahead >= 0
    strict = ahead > 0
    eye = (si == ti).astype(F32)

    lw = lw_ref[...]
    cum_incl = _dot_exact_lhs(incl[:, :CHUNK].astype(BF16), lw)
    tot = jnp.sum(lw, axis=0, keepdims=True)
    e_incl = jnp.exp(cum_incl)
    e_excl = jnp.exp(cum_incl - lw)
    e_ninc = jnp.exp(-cum_incl)
    g_tot = jnp.exp(tot)

    a = a_ref[...].astype(F32)
    kk = kk_ref[...].astype(F32)
    kd = k_ref[...].astype(F32) * (1.0 + (a - 1.0) * ka_ref[...])
    at_all = kk * e_excl
    rt_all = r_ref[...].astype(F32) * e_incl
    bt_all = (kk * a) * e_ninc
    kt_all = kd * e_ninc
    bh_all = bt_all * g_tot
    kh_all = kt_all * g_tot
    v_all = v_ref[...].astype(F32)

    npair = SLAB // PAIR
    lane = lax.broadcasted_iota(jnp.int32, (1, PAIR), 1)
    head0 = lane < HEAD

    def bd(y):
        return jnp.concatenate([jnp.where(head0, y, 0.0), jnp.where(head0, 0.0, y)], axis=0)

    sls = [slice(p * PAIR, (p + 1) * PAIR) for p in range(npair)]
    ars = [jnp.concatenate([at_all[:, sl], rt_all[:, sl]], axis=0).astype(BF16) for sl in sls]
    s_ps = [st_ref[p] for p in range(npair)]
    gs = [_bdot_nt(ars[p], jnp.concatenate([bd(bt_all[:, sls[p]]), bd(kt_all[:, sls[p]]), s_ps[p]], axis=0))
          for p in range(npair)]
    ms = [jnp.where(strict, -gs[p][:CHUNK, :PAIR], 0.0) for p in range(npair)]
    tinvs = [eye + ms[p] for p in range(npair)]
    ms = [_bdot(m, bd(m)) for m in ms]
    for _ in range(4):
        ps = [_bdot(jnp.concatenate([tinvs[p], ms[p]], axis=0), bd(ms[p])) for p in range(npair)]
        tinvs = [tinvs[p] + ps[p][:CHUNK] for p in range(npair)]
        ms = [ps[p][CHUNK:] for p in range(npair)]
    tinvs = [tinvs[p] + _bdot(tinvs[p], bd(ms[p])) for p in range(npair)]
    x2s = [_bdot(jnp.concatenate([jnp.where(strict, gs[p][:CHUNK, PAIR:2 * PAIR], 0.0),
                                  jnp.where(incl, gs[p][CHUNK:, PAIR:2 * PAIR], 0.0)], axis=0),
                 bd(v_all[:, sls[p]])) for p in range(npair)]
    us = [-_bdot(tinvs[p], bd(gs[p][:CHUNK, 2 * PAIR:] + x2s[p][:CHUNK])) for p in range(npair)]
    outs = [gs[p][CHUNK:, 2 * PAIR:] + x2s[p][CHUNK:]
            + _bdot(jnp.where(incl, gs[p][CHUNK:, :PAIR], 0.0), bd(us[p])) for p in range(npair)]
    row = lax.broadcasted_iota(jnp.int32, (PAIR, PAIR), 0)
    same_head = (row < HEAD) == (lax.broadcasted_iota(jnp.int32, (PAIR, PAIR), 1) < HEAD)
    for p in range(npair):
        uv = jnp.concatenate([us[p], v_all[:, sls[p]]], axis=0).astype(BF16)
        bk = jnp.concatenate([bh_all[:, sls[p]], kh_all[:, sls[p]]], axis=0).astype(BF16)
        upd = lax.dot_general(uv, bk, (((0,), (0,)), ((), ())), preferred_element_type=F32)
        st_ref[p] = s_ps[p] * g_tot[:, sls[p]] + jnp.where(same_head, upd, 0.0)
    y_ref[...] = jnp.concatenate(outs, axis=1)


def _rwkv_scan(r, k, v, kk, lw, a, k_a, batch, seq):
    nc = seq // CHUNK
    ns = D_RWKV // SLAB
    r4 = r.reshape(batch, seq, D_RWKV)
    k4 = k.reshape(batch, seq, D_RWKV)
    v4 = v.reshape(batch, seq, D_RWKV)
    kk4 = kk.reshape(batch, seq, D_RWKV)
    lw5 = lw.reshape(2, batch, seq, D_RWKV)
    a5 = a.reshape(2, batch, seq, D_RWKV)

    def tchunk(d, c):
        return c + d * (nc - 1 - 2 * c)

    shared = pl.BlockSpec((None, CHUNK, SLAB), lambda d, b, s, c: (b, tchunk(d, c), s))
    perdir = pl.BlockSpec((None, None, CHUNK, SLAB), lambda d, b, s, c: (d, b, tchunk(d, c), s))
    y = pl.pallas_call(
        _scan_kernel,
        grid=(2, batch, ns, nc),
        in_specs=[shared, shared, shared, shared, perdir, perdir,
                  pl.BlockSpec((1, SLAB), lambda d, b, s, c: (0, s))],
        out_specs=perdir,
        out_shape=jax.ShapeDtypeStruct((2, batch, seq, D_RWKV), F32),
        scratch_shapes=[pltpu.VMEM((SLAB // PAIR, PAIR, PAIR), F32)],
        compiler_params=_cparams(("parallel", "parallel", "parallel", "arbitrary")),
        name="rwkv_scan",
    )(r4, k4, v4, kk4, lw5, a5, k_a)
    return y.reshape(2, batch * seq, D_RWKV)


def _post_kernel(y_ref, r_ref, k_ref, v_ref, a_ref, g_ref, ka_ref, rk_ref, lnw_ref, lnb_ref,
                 e_ref, et_ref, o_ref):
    y = y_ref[0] + y_ref[1]
    mu = _head_sum_bcast(y, e_ref, et_ref) * (1.0 / HEAD)
    yc = y - mu
    var = _head_sum_bcast(yc * yc, e_ref, et_ref) * (1.0 / HEAD)
    yn = yc * lax.rsqrt(var + GN_EPS) * lnw_ref[...] + lnb_ref[...]
    ka = ka_ref[...]
    k = k_ref[...].astype(F32)
    kd_sum = (k * (1.0 + (a_ref[0].astype(F32) - 1.0) * ka)
              + k * (1.0 + (a_ref[1].astype(F32) - 1.0) * ka))
    bonus = (_head_sum_bcast(r_ref[...].astype(F32) * kd_sum * rk_ref[...], e_ref, et_ref)
             * v_ref[...].astype(F32))
    o_ref[...] = ((yn + bonus) * g_ref[...].astype(F32)).astype(o_ref.dtype)


def _rwkv_post(y, r, k, v, a, g, k_a, r_k, ln_w, ln_b, e_mat, et_mat, tm=128):
    t, c = r.shape
    row = pl.BlockSpec((tm, c), lambda i: (i, 0))
    row2 = pl.BlockSpec((2, tm, c), lambda i: (0, i, 0))
    par = pl.BlockSpec((1, c), lambda i: (0, 0))
    return pl.pallas_call(
        _post_kernel,
        grid=(t // tm,),
        in_specs=[row2, row, row, row, row2, row, par, par, par, par,
                  pl.BlockSpec((c, 128), lambda i: (0, 0)), pl.BlockSpec((128, c), lambda i: (0, 0))],
        out_specs=row,
        out_shape=jax.ShapeDtypeStruct((t, c), BF16),
        compiler_params=_cparams(("parallel",)),
        name="rwkv_post",
    )(y, r, k, v, a, g, k_a, r_k, ln_w, ln_b, e_mat, et_mat)


def _s5_tables(lam_re, lam_im, log_step, b_re, b_im, c_re, c_im, d_skip, n_levels):
    L, P, CH, G, GB = S5_L, S5_STATE, S5_CH, S5_GROUPS, S5_GB
    NB = G // GB
    hi = lax.Precision.HIGHEST
    lam = lax.complex(lam_re.astype(F32), lam_im.astype(F32))
    dt = jnp.exp(log_step.astype(F32))[..., None]
    lam_dt = lam * dt
    lam_bar = jnp.exp(lam_dt)
    b = lax.complex(b_re.astype(F32), b_im.astype(F32))
    b_bar = ((lam_bar - 1.0) / lam)[..., None] * b
    c = lax.complex(c_re.astype(F32), c_im.astype(F32))
    taus = jnp.arange(L + 1, dtype=F32)
    pows = jnp.exp(lam_dt[:, :, None, :] * taus[None, None, :, None])
    kern = jnp.real(jnp.einsum('dgop,dgtp,dgpi->dgtoi', c, pows[:, :, :L], b_bar, precision=hi))
    j = jnp.arange(L)[:, None]
    t = jnp.arange(L)[None, :]
    lag = jnp.arange(L)[None, None, :]
    sel_f = ((t - j)[:, :, None] == lag).astype(F32)
    sel_b = ((j - t)[:, :, None] == lag).astype(F32)
    skip = (jnp.eye(L)[None, :, None, :, None] * jnp.eye(CH)[None, None, :, None, :]
            * d_skip.astype(F32).reshape(G, 1, CH, 1, 1))
    tsmall = (jnp.einsum('jtl,gloi->gjito', sel_f, kern[0], precision=hi)
              + jnp.einsum('jtl,gloi->gjito', sel_b, kern[1], precision=hi) + skip)
    eye = jnp.eye(GB, dtype=F32)
    t8 = jnp.einsum('Jgaibo,gh->Jagibho', tsmall.reshape(NB, GB, L, CH, L, CH), eye)
    t8 = t8.reshape(NB, L * GB * CH, L * GB * CH)
    desc = jnp.exp(lam_dt[:, :, None, :] * (L - taus[:L])[None, None, :, None])
    desc1 = jnp.exp(lam_dt[0][:, None, :] * (L - 1 - taus[:L])[None, :, None])
    zf = desc1[:, :, :, None] * b_bar[0][:, None]
    zb = pows[1][:, :L][:, :, :, None] * b_bar[1][:, None]
    wzs = jnp.stack([jnp.real(zf), jnp.imag(zf), jnp.real(zb), jnp.imag(zb)], axis=2)
    wz8 = jnp.einsum('Jgaqpi,gh->Jagiqhp', wzs.reshape(NB, GB, L, 4, P, CH), eye)
    wz8 = wz8.reshape(NB, L * GB * CH, 4 * GB * P)
    yf = c[0][:, None] * pows[0][:, 1:][:, :, None, :]
    yb = c[1][:, None] * desc[1][:, :, None, :]
    wys = jnp.stack([jnp.real(yf), -jnp.imag(yf), jnp.real(yb), -jnp.imag(yb)], axis=1)
    wy8 = jnp.einsum('Jgqbop,gh->Jqgpbho', wys.reshape(NB, GB, 4, L, CH, P), eye)
    wy8 = wy8.reshape(NB, 4 * GB * P, L * GB * CH)
    steps = L * (2.0 ** jnp.arange(n_levels, dtype=F32))
    lp = jnp.exp(lam_dt[:, :, None, :] * steps[None, None, :, None])
    m = jnp.stack([jnp.real(lp[0]), jnp.imag(lp[0]), jnp.real(lp[1]), jnp.imag(lp[1])], axis=0)
    mult = m.reshape(4, NB, GB, n_levels, P).transpose(1, 3, 0, 2, 4).reshape(NB, n_levels, 4, GB * P)
    return t8.astype(BF16), wz8.astype(BF16), wy8.astype(BF16), mult


def _s5_kernel(n_levels, x_ref, t_ref, wz_ref, wy_ref, mult_ref, y_ref):
    nck = x_ref.shape[0] // S5_L
    lanes = x_ref.shape[1]
    w = S5_GB * S5_STATE
    xcat = jnp.concatenate([x_ref[pl.ds(tl, nck, stride=S5_L), :].astype(BF16) for tl in range(S5_L)],
                           axis=1)
    z = jnp.dot(xcat, wz_ref[...], preferred_element_type=F32)
    fre, fim, bre, bim = z[:, :w], z[:, w:2 * w], z[:, 2 * w:3 * w], z[:, 3 * w:]
    cidx = lax.broadcasted_iota(jnp.int32, (nck, w), 0)
    for i in range(n_levels):
        sh = 1 << i
        lfr, lfi = mult_ref[i, 0:1, :], mult_ref[i, 1:2, :]
        lbr, lbi = mult_ref[i, 2:3, :], mult_ref[i, 3:4, :]
        keep = cidx >= sh
        sr = jnp.where(keep, pltpu.roll(fre, sh, 0), 0.0)
        si = jnp.where(keep, pltpu.roll(fim, sh, 0), 0.0)
        fre, fim = fre + sr * lfr - si * lfi, fim + sr * lfi + si * lfr
        keep = cidx < nck - sh
        sr = jnp.where(keep, pltpu.roll(bre, nck - sh, 0), 0.0)
        si = jnp.where(keep, pltpu.roll(bim, nck - sh, 0), 0.0)
        bre, bim = bre + sr * lbr - si * lbi, bim + sr * lbi + si * lbr
    has_prev = cidx >= 1
    has_next = cidx < nck - 1
    xin = jnp.concatenate([jnp.where(has_prev, pltpu.roll(fre, 1, 0), 0.0),
                           jnp.where(has_prev, pltpu.roll(fim, 1, 0), 0.0),
                           jnp.where(has_next, pltpu.roll(bre, nck - 1, 0), 0.0),
                           jnp.where(has_next, pltpu.roll(bim, nck - 1, 0), 0.0)], axis=1).astype(BF16)
    y = jnp.dot(xcat, t_ref[...], preferred_element_type=F32)
    y = y + jnp.dot(xin, wy_ref[...], preferred_element_type=F32)
    for tl in range(S5_L):
        y_ref[pl.ds(tl, nck, stride=S5_L), :] = y[:, tl * lanes:(tl + 1) * lanes]


def _s5_mix(p_main, col0, t8, wz8, wy8, mult, batch, seq, n_levels):
    nb = t8.shape[0]
    lanes = S5_GB * S5_CH
    kw = S5_L * lanes
    sw = 4 * S5_GB * S5_STATE
    return pl.pallas_call(
        functools.partial(_s5_kernel, n_levels),
        grid=(nb, batch),
        in_specs=[pl.BlockSpec((seq, lanes), lambda j, b: (b, col0 + j)),
                  pl.BlockSpec((None, kw, kw), lambda j, b: (j, 0, 0)),
                  pl.BlockSpec((None, kw, sw), lambda j, b: (j, 0, 0)),
                  pl.BlockSpec((None, sw, kw), lambda j, b: (j, 0, 0)),
                  pl.BlockSpec((None, n_levels, 4, S5_GB * S5_STATE), lambda j, b: (j, 0, 0, 0))],
        out_specs=pl.BlockSpec((seq, lanes), lambda j, b: (b, j)),
        out_shape=jax.ShapeDtypeStruct((batch * seq, D_S5), F32),
        compiler_params=_cparams(("parallel", "parallel")),
        name="s5_mix",
    )(p_main, t8, wz8, wy8, mult)


def _glu_kernel(y_ref, w_ref, b_ref, o_ref):
    y = y_ref[...]
    z = jnp.dot(jax.nn.gelu(y).astype(BF16), w_ref[...], preferred_element_type=F32) + b_ref[...]
    o_ref[...] = (y * jax.nn.sigmoid(z)).astype(o_ref.dtype)


def _glu(y, w, b, tm=512):
    t, c = y.shape
    tm = min(tm, t)
    return pl.pallas_call(
        _glu_kernel,
        grid=(t // tm,),
        in_specs=[pl.BlockSpec((tm, c), lambda i: (i, 0)), pl.BlockSpec((c, c), lambda i: (0, 0)),
                  pl.BlockSpec((1, c), lambda i: (0, 0))],
        out_specs=pl.BlockSpec((tm, c), lambda i: (i, 0)),
        out_shape=jax.ShapeDtypeStruct((t, c), BF16),
        compiler_params=_cparams(("parallel",)),
        name="s5_glu",
    )(y, w, b)


def _wqk_kernel(wq_ref, k_ref, o_ref):
    acc = lax.dot_general(wq_ref[...], k_ref[...], (((1,), (1,)), ((), ())), preferred_element_type=F32)
    o_ref[...] = (acc * ATTN_SCALE).astype(o_ref.dtype)


def _vwo_kernel(v_ref, wo_ref, o_ref):
    o_ref[...] = jnp.dot(v_ref[...], wo_ref[...], preferred_element_type=F32).astype(o_ref.dtype)


def _attn_kernel(mem, xn_ref, wqk_ref, vwo_ref, res_ref, o_ref, p_ref):
    @pl.when(pl.program_id(2) == 0)
    def _():
        s = jnp.dot(xn_ref[...], wqk_ref[...], preferred_element_type=F32)
        for h in range(ATTN_HEADS):
            sh = s[:, h * mem:(h + 1) * mem]
            sh = sh - jnp.max(sh, axis=-1, keepdims=True)
            p = jnp.exp(sh)
            p_ref[:, h * mem:(h + 1) * mem] = (p / jnp.sum(p, axis=-1, keepdims=True)).astype(BF16)

    o_ref[...] = jnp.dot(p_ref[...], vwo_ref[...], preferred_element_type=F32) + res_ref[...]


def _cross_attention(xn, kv, w_q, w_o, res, batch, seq, mem, tm=512, tn=1024):
    d = D_MODEL
    hm = ATTN_HEADS * mem
    tm = min(tm, seq)
    wqk = pl.pallas_call(
        _wqk_kernel,
        grid=(batch, ATTN_HEADS),
        in_specs=[pl.BlockSpec((d, ATTN_HEAD_DIM), lambda b, h: (0, h)),
                  pl.BlockSpec((mem, ATTN_HEAD_DIM), lambda b, h: (b, h))],
        out_specs=pl.BlockSpec((None, d, mem), lambda b, h: (b, 0, h)),
        out_shape=jax.ShapeDtypeStruct((batch, d, hm), BF16),
        compiler_params=_cparams(("parallel", "parallel")),
        name="attn_wqk",
    )(w_q, kv)
    vwo = pl.pallas_call(
        _vwo_kernel,
        grid=(batch, ATTN_HEADS),
        in_specs=[pl.BlockSpec((mem, ATTN_HEAD_DIM), lambda b, h: (b, ATTN_HEADS + h)),
                  pl.BlockSpec((ATTN_HEAD_DIM, d), lambda b, h: (h, 0))],
        out_specs=pl.BlockSpec((None, mem, d), lambda b, h: (b, h, 0)),
        out_shape=jax.ShapeDtypeStruct((batch, hm, d), BF16),
        compiler_params=_cparams(("parallel", "parallel")),
        name="attn_vwo",
    )(kv, w_o)
    out = pl.pallas_call(
        functools.partial(_attn_kernel, mem),
        grid=(batch, seq // tm, d // tn),
        in_specs=[pl.BlockSpec((None, tm, d), lambda b, i, j: (b, i, 0)),
                  pl.BlockSpec((None, d, hm), lambda b, i, j: (b, 0, 0)),
                  pl.BlockSpec((None, hm, tn), lambda b, i, j: (b, 0, j)),
                  pl.BlockSpec((None, tm, tn), lambda b, i, j: (b, i, j))],
        out_specs=pl.BlockSpec((None, tm, tn), lambda b, i, j: (b, i, j)),
        out_shape=jax.ShapeDtypeStruct((batch, seq, d), F32),
        scratch_shapes=[pltpu.VMEM((tm, hm), BF16)],
        compiler_params=_cparams(("parallel", "parallel", "arbitrary")),
        name="cross_attn",
    )(xn.reshape(batch, seq, d), wqk, vwo, res.reshape(batch, seq, d))
    return out.reshape(batch * seq, d)


def _router_kernel(x_ref, g_ref, whi_ref, wlo_ref, b_ref, xn_ref, info_ref, cnt_ref, carry_ref):
    i = pl.program_id(0)

    @pl.when(i == 0)
    def _():
        carry_ref[...] = jnp.zeros_like(carry_ref)

    x = x_ref[...]
    tm = x.shape[0]
    ms = jnp.mean(x * x, axis=-1, keepdims=True)
    xn = x * lax.rsqrt(ms + RMS_EPS) * g_ref[...]
    xn_ref[...] = xn
    xh, xl = _split2(xn)
    whi = whi_ref[...]
    logits = (jnp.dot(xh, whi, preferred_element_type=F32) + jnp.dot(xl, whi, preferred_element_type=F32)
              + jnp.dot(xh, wlo_ref[...], preferred_element_type=F32)) + b_ref[...]
    li = lax.broadcasted_iota(jnp.int32, logits.shape, 1)
    neg = jnp.float32(-jnp.inf)
    is_g = li < N_GROUPS
    gl = jnp.where(is_g, logits, neg)
    gm = jnp.max(gl, axis=-1, keepdims=True)
    gi = jnp.min(jnp.where(is_g & (gl == gm), li, 128), axis=-1, keepdims=True)
    gp = 1.0 / jnp.sum(jnp.where(is_g, jnp.exp(gl - gm), 0.0), axis=-1, keepdims=True)
    lo_lane = N_GROUPS + EPG * gi
    sel = (li >= lo_lane) & (li < lo_lane + EPG)
    l1 = jnp.where(sel, logits, neg)
    e1 = jnp.max(l1, axis=-1, keepdims=True)
    i1 = jnp.min(jnp.where(sel & (l1 == e1), li, 128), axis=-1, keepdims=True)
    sel2 = sel & (li != i1)
    l2 = jnp.where(sel2, logits, neg)
    e2 = jnp.max(l2, axis=-1, keepdims=True)
    i2 = jnp.min(jnp.where(sel2 & (l2 == e2), li, 128), axis=-1, keepdims=True)
    ex = jnp.exp(e2 - e1)
    w1 = gp / (1.0 + ex)
    w2 = gp * ex / (1.0 + ex)
    x1 = i1 - N_GROUPS
    x2 = i2 - N_GROUPS
    oh1 = li == x1
    oh2 = li == x2
    oh = (oh1 | oh2).astype(BF16)
    ri = lax.broadcasted_iota(jnp.int32, (tm, tm), 0)
    ci = lax.broadcasted_iota(jnp.int32, (tm, tm), 1)
    before = jnp.dot((ci < ri).astype(BF16), oh, preferred_element_type=F32) + carry_ref[0:1, :]
    r1 = jnp.sum(jnp.where(oh1, before, 0.0), axis=-1, keepdims=True)
    r2 = jnp.sum(jnp.where(oh2, before, 0.0), axis=-1, keepdims=True)
    new_carry = carry_ref[0:1, :] + jnp.sum(oh.astype(F32), axis=0, keepdims=True)
    carry_ref[...] = jnp.broadcast_to(new_carry, carry_ref.shape)
    cnt_ref[...] = jnp.broadcast_to(new_carry, cnt_ref.shape)
    info = jnp.where(li == 0, x1.astype(F32), 0.0)
    info = jnp.where(li == 1, x2.astype(F32), info)
    info = jnp.where(li == 2, w1, info)
    info = jnp.where(li == 3, w2, info)
    info = jnp.where(li == 4, r1, info)
    info = jnp.where(li == 5, r2, info)
    info_ref[...] = info


def _router(x, gain, w_hi, w_lo, bias, tm=256):
    t, d = x.shape
    sd = jax.ShapeDtypeStruct
    return pl.pallas_call(
        _router_kernel,
        grid=(t // tm,),
        in_specs=[pl.BlockSpec((tm, d), lambda i: (i, 0)), pl.BlockSpec((1, d), lambda i: (0, 0)),
                  pl.BlockSpec((d, 128), lambda i: (0, 0)), pl.BlockSpec((d, 128), lambda i: (0, 0)),
                  pl.BlockSpec((1, 128), lambda i: (0, 0))],
        out_specs=[pl.BlockSpec((tm, d), lambda i: (i, 0)), pl.BlockSpec((tm, 128), lambda i: (i, 0)),
                   pl.BlockSpec((8, 128), lambda i: (0, 0))],
        out_shape=[sd((t, d), F32), sd((t, 128), F32), sd((8, 128), F32)],
        scratch_shapes=[pltpu.VMEM((8, 128), F32)],
        compiler_params=_cparams(("arbitrary",)),
        name="moe_router",
    )(x, gain.reshape(1, d), w_hi, w_lo, bias)


def _gather_rows(src_hbm, idx_ref, dst_ref, sem, n_rows):
    def body(j, carry):
        pltpu.make_async_copy(src_hbm.at[pl.ds(idx_ref[0, j], 1), :], dst_ref.at[pl.ds(j, 1), :], sem).start()
        return carry
    lax.fori_loop(0, n_rows, body, 0)


def _wait_rows(src_hbm, dst_ref, sem, n_rows):
    pltpu.make_async_copy(src_hbm.at[pl.ds(0, n_rows), :], dst_ref, sem).wait()


def _expert_up_kernel(be_ref, nused_ref, tok_ref, tokn_ref, x_hbm, wg_ref, wu_ref, h_ref, xbuf, sem):
    i = pl.program_id(0)
    n_used = nused_ref[0]
    slot = i % 2

    @pl.when(i == 0)
    def _():
        _gather_rows(x_hbm, tok_ref, xbuf.at[0], sem.at[0], MOE_BLK)

    @pl.when(i + 1 < n_used)
    def _():
        _gather_rows(x_hbm, tokn_ref, xbuf.at[1 - slot], sem.at[1 - slot], MOE_BLK)

    @pl.when(i < n_used)
    def _():
        _wait_rows(x_hbm, xbuf.at[slot], sem.at[slot], MOE_BLK)
        d = xbuf.shape[2]
        hg = jnp.zeros((MOE_BLK, D_EXPERT), F32)
        hu = jnp.zeros((MOE_BLK, D_EXPERT), F32)
        for kc in range(d // EXPERT_KC):
            ks = slice(kc * EXPERT_KC, (kc + 1) * EXPERT_KC)
            xb = xbuf[slot, :, ks].astype(BF16)
            hg = hg + jnp.dot(xb, wg_ref[ks, :].astype(BF16), preferred_element_type=F32)
            hu = hu + jnp.dot(xb, wu_ref[ks, :].astype(BF16), preferred_element_type=F32)
        h_ref[...] = (jax.nn.silu(hg) * hu).astype(h_ref.dtype)

    @pl.when(i >= n_used)
    def _():
        h_ref[...] = jnp.zeros_like(h_ref)


def _expert_down_kernel(be_ref, nused_ref, h_ref, wd_ref, y_ref):
    i = pl.program_id(0)
    n_used = nused_ref[0]

    @pl.when(i < n_used)
    def _():
        h = h_ref[...]
        d = y_ref.shape[1]
        for nc in range(d // EXPERT_KC):
            ns = slice(nc * EXPERT_KC, (nc + 1) * EXPERT_KC)
            y_ref[:, ns] = jnp.dot(h, wd_ref[:, ns].astype(BF16), preferred_element_type=F32)

    @pl.when(i >= n_used)
    def _():
        y_ref[...] = jnp.zeros_like(y_ref)


def _experts(block_expert, n_used, tok3, xn, wg, wu, wd):
    nb = tok3.shape[0]
    d = xn.shape[1]
    tok_spec = pl.BlockSpec((None, 1, MOE_BLK), lambda i, be, nu: (i, 0, 0), memory_space=pltpu.SMEM)
    tokn_spec = pl.BlockSpec((None, 1, MOE_BLK), lambda i, be, nu: (jnp.minimum(i + 1, nb - 1), 0, 0),
                             memory_space=pltpu.SMEM)
    h_buf = pl.pallas_call(
        _expert_up_kernel,
        grid_spec=pltpu.PrefetchScalarGridSpec(
            num_scalar_prefetch=2,
            grid=(nb,),
            in_specs=[tok_spec, tokn_spec, pl.BlockSpec(memory_space=pl.ANY),
                      pl.BlockSpec((None, d, D_EXPERT), lambda i, be, nu: (be[i], 0, 0)),
                      pl.BlockSpec((None, d, D_EXPERT), lambda i, be, nu: (be[i], 0, 0))],
            out_specs=pl.BlockSpec((MOE_BLK, D_EXPERT), lambda i, be, nu: (i, 0)),
            scratch_shapes=[pltpu.VMEM((2, MOE_BLK, d), F32), pltpu.SemaphoreType.DMA((2,))],
        ),
        out_shape=jax.ShapeDtypeStruct((nb * MOE_BLK, D_EXPERT), BF16),
        compiler_params=_cparams(("arbitrary",)),
        name="moe_up",
    )(block_expert, n_used, tok3, tok3, xn, wg, wu)
    return pl.pallas_call(
        _expert_down_kernel,
        grid_spec=pltpu.PrefetchScalarGridSpec(
            num_scalar_prefetch=2,
            grid=(nb,),
            in_specs=[pl.BlockSpec((MOE_BLK, D_EXPERT), lambda i, be, nu: (i, 0)),
                      pl.BlockSpec((None, D_EXPERT, d), lambda i, be, nu: (be[i], 0, 0))],
            out_specs=pl.BlockSpec((MOE_BLK, d), lambda i, be, nu: (i, 0)),
        ),
        out_shape=jax.ShapeDtypeStruct((nb * MOE_BLK, d), F32),
        compiler_params=_cparams(("arbitrary",)),
        name="moe_down",
    )(block_expert, n_used, h_buf, wd)


def _combine_kernel(d1_ref, d1n_ref, d2_ref, d2n_ref, x_ref, info_ref, g_ref, y_hbm, o_ref, ybuf, sem):
    i = pl.program_id(0)
    n = pl.num_programs(0)
    slot = i % 2

    @pl.when(i == 0)
    def _():
        _gather_rows(y_hbm, d1_ref, ybuf.at[0, 0], sem.at[0], COMB_TM)
        _gather_rows(y_hbm, d2_ref, ybuf.at[0, 1], sem.at[0], COMB_TM)

    @pl.when(i + 1 < n)
    def _():
        _gather_rows(y_hbm, d1n_ref, ybuf.at[1 - slot, 0], sem.at[1 - slot], COMB_TM)
        _gather_rows(y_hbm, d2n_ref, ybuf.at[1 - slot, 1], sem.at[1 - slot], COMB_TM)

    _wait_rows(y_hbm, ybuf.at[slot, 0], sem.at[slot], COMB_TM)
    _wait_rows(y_hbm, ybuf.at[slot, 1], sem.at[slot], COMB_TM)
    info = info_ref[...]
    w1 = info[:, 2:3]
    w2 = info[:, 3:4]
    x = x_ref[...] + ybuf[slot, 0] * w1 + ybuf[slot, 1] * w2
    ms = jnp.mean(x * x, axis=-1, keepdims=True)
    o_ref[...] = x * lax.rsqrt(ms + RMS_EPS) * g_ref[...]


def _combine(dest1, dest2, x, info, gain, y_buf):
    t, d = x.shape
    nt = t // COMB_TM
    d1 = dest1.reshape(nt, 1, COMB_TM)
    d2 = dest2.reshape(nt, 1, COMB_TM)
    cur = pl.BlockSpec((None, 1, COMB_TM), lambda i: (i, 0, 0), memory_space=pltpu.SMEM)
    nxt = pl.BlockSpec((None, 1, COMB_TM), lambda i: (jnp.minimum(i + 1, nt - 1), 0, 0),
                       memory_space=pltpu.SMEM)
    return pl.pallas_call(
        _combine_kernel,
        grid=(nt,),
        in_specs=[cur, nxt, cur, nxt,
                  pl.BlockSpec((COMB_TM, d), lambda i: (i, 0)),
                  pl.BlockSpec((COMB_TM, 128), lambda i: (i, 0)),
                  pl.BlockSpec((1, d), lambda i: (0, 0)),
                  pl.BlockSpec(memory_space=pl.ANY)],
        out_specs=pl.BlockSpec((COMB_TM, d), lambda i: (i, 0)),
        out_shape=jax.ShapeDtypeStruct((t, d), F32),
        scratch_shapes=[pltpu.VMEM((2, 2, COMB_TM, d), F32), pltpu.SemaphoreType.DMA((2,))],
        compiler_params=_cparams(("arbitrary",)),
        name="moe_combine",
    )(d1, d1, d2, d2, x, info, gain.reshape(1, d), y_buf)


def _head_matrices():
    lane = jnp.arange(D_RWKV) // HEAD
    e = (lane[:, None] == jnp.arange(128)[None, :]).astype(BF16)
    return e, e.T


def _pad_lora_rows(w):
    return jnp.pad(w, ((0, 0), (0, LORA_PAD - w.shape[1]), (0, 0)))


def _layer(x, mem, norm_mix, w_in, shift_taps, rwkv_w0, rwkv_w2, rwkv_a0, rwkv_a2, rwkv_g2, rwkv_k_k,
           rwkv_k_a, rwkv_r_k, rwkv_ln_w, rwkv_ln_b, s5_lam_re, s5_lam_im, s5_log_step, s5_b_re, s5_b_im,
           s5_c_re, s5_c_im, s5_d, s5_glu_w, s5_glu_b, w_out, norm_attn, norm_mem, w_q, w_k, w_v, w_o,
           norm_ffn, router_grp_w, router_grp_b, router_exp_w, router_exp_b, exp_w_gate, exp_w_up,
           exp_w_down, out_gain):
    batch, seq, d = x.shape
    t = batch * seq
    mem_n = mem.shape[1]
    xt = x.reshape(t, d)
    c3 = 3 * D_RWKV
    off_g = c3 + 4 * 96

    def pad_cols(w, lo, width):
        return jnp.pad(w[:, lo:lo + width], ((0, 0), (0, LORA_PAD - width)))

    w_main = jnp.concatenate([w_in[:, :c3], w_in[:, off_g + 256:]], axis=1).astype(BF16)
    w_lora = jnp.concatenate([pad_cols(w_in, c3 + i * 96, 96) for i in range(4)]
                             + [w_in[:, off_g:off_g + 256]], axis=1).astype(BF16)
    taps_l = jnp.concatenate([pad_cols(shift_taps, c3 + i * 96, 96) for i in range(4)]
                             + [shift_taps[:, off_g:off_g + 256]], axis=1)
    h = _rmsnorm(xt, norm_mix, BF16)
    p_main = _matmul(h, w_main, F32, 1024, 512, name="w_in_main")
    p_lora = _matmul(h, w_lora, F32, 1024, N_LORA, name="w_in_lora")

    e_mat, et_mat = _head_matrices()
    k_a = rwkv_k_a.reshape(1, D_RWKV)
    r, k, v, kk, lw, a, g = _rwkv_prep(
        p_main, p_lora, seq, shift_taps[:, :D_RWKV], shift_taps[:, D_RWKV:2 * D_RWKV],
        shift_taps[:, 2 * D_RWKV:c3], taps_l, rwkv_w0, _pad_lora_rows(rwkv_w2).astype(BF16), rwkv_a0,
        _pad_lora_rows(rwkv_a2).astype(BF16), rwkv_g2.astype(BF16), rwkv_k_k.reshape(1, D_RWKV),
        e_mat, et_mat)
    y_scan = _rwkv_scan(r, k, v, kk, lw, a, k_a, batch, seq)
    y_rwkv = _rwkv_post(y_scan, r, k, v, a, g, k_a, rwkv_r_k.reshape(1, D_RWKV),
                        rwkv_ln_w.reshape(1, D_RWKV), rwkv_ln_b.reshape(1, D_RWKV), e_mat, et_mat)

    n_levels = int(math.log2(seq // S5_L))
    t8, wz8, wy8, mult = _s5_tables(s5_lam_re, s5_lam_im, s5_log_step, s5_b_re, s5_b_im,
                                    s5_c_re, s5_c_im, s5_d, n_levels)
    y_s5 = _s5_mix(p_main, c3 // (S5_GB * S5_CH), t8, wz8, wy8, mult, batch, seq, n_levels)
    y_glu = _glu(y_s5, s5_glu_w.astype(BF16), s5_glu_b.reshape(1, D_S5))

    x1 = _matmul2_res(y_rwkv, y_glu, w_out.astype(BF16), xt, 1024, 512)

    xn = _rmsnorm(x1, norm_attn, BF16)
    memn = _rmsnorm(mem.reshape(batch * mem_n, d), norm_mem, BF16)
    wkv = jnp.concatenate([w_k, w_v], axis=1).astype(BF16)
    kv = _matmul(memn, wkv, BF16, 1024, 512, name="w_kv")
    x2 = _cross_attention(xn, kv, w_q.astype(BF16), w_o.astype(BF16), x1, batch, seq, mem_n)

    w_r = jnp.concatenate([router_grp_w, router_exp_w,
                           jnp.zeros((d, 128 - N_GROUPS - N_EXPERTS), F32)], axis=1)
    w_r_hi = w_r.astype(BF16)
    w_r_lo = (w_r - w_r_hi.astype(F32)).astype(BF16)
    b_r = jnp.concatenate([router_grp_b, router_exp_b,
                           jnp.zeros((128 - N_GROUPS - N_EXPERTS,), F32)]).reshape(1, 128)
    xn3, info, cnt = _router(x2, norm_ffn, w_r_hi, w_r_lo, b_r)
    eid = info[:, 0:2].astype(jnp.int32)
    rank = info[:, 4:6].astype(jnp.int32)
    counts = cnt[0, :N_EXPERTS].astype(jnp.int32)
    nblk = (counts + MOE_BLK - 1) // MOE_BLK
    bstart = jnp.cumsum(nblk) - nblk
    n_used = jnp.sum(nblk)
    first = jnp.sum(jnp.where(eid[:, :, None] == jnp.arange(N_EXPERTS)[None, None, :], bstart, 0), axis=-1)
    dest = first * MOE_BLK + rank
    nb = (t * TOP_K) // MOE_BLK + N_EXPERTS
    tok_buf = jnp.zeros((nb * MOE_BLK,), jnp.int32).at[dest.reshape(-1)].set(
        jnp.repeat(jnp.arange(t, dtype=jnp.int32), TOP_K))
    blk = jnp.arange(nb, dtype=jnp.int32)
    block_expert = jnp.sum(blk[:, None] >= (bstart + nblk)[None, :], axis=1).astype(jnp.int32)
    last_e = jnp.max(jnp.where(nblk > 0, jnp.arange(N_EXPERTS), 0)).astype(jnp.int32)
    block_expert = jnp.minimum(block_expert, last_e)
    y_buf = _experts(block_expert, n_used.reshape(1).astype(jnp.int32), tok_buf.reshape(nb, 1, MOE_BLK),
                     xn3, exp_w_gate, exp_w_up, exp_w_down)
    out = _combine(dest[:, 0], dest[:, 1], x2, info, out_gain, y_buf)
    return out.reshape(batch, seq, d)


def kernel(x, mem, norm_mix, w_in, shift_taps, rwkv_w0, rwkv_w2, rwkv_a0, rwkv_a2, rwkv_g2, rwkv_k_k, rwkv_k_a, rwkv_r_k, rwkv_ln_w, rwkv_ln_b, s5_lam_re, s5_lam_im, s5_log_step, s5_b_re, s5_b_im, s5_c_re, s5_c_im, s5_d, s5_glu_w, s5_glu_b, w_out, norm_attn, norm_mem, w_q, w_k, w_v, w_o, norm_ffn, router_grp_w, router_grp_b, router_exp_w, router_exp_b, exp_w_gate, exp_w_up, exp_w_down, norm_final):
    return _layer(x, mem, norm_mix[0], w_in[0], shift_taps[0], rwkv_w0[0], rwkv_w2[0], rwkv_a0[0],
                  rwkv_a2[0], rwkv_g2[0], rwkv_k_k[0], rwkv_k_a[0], rwkv_r_k[0], rwkv_ln_w[0],
                  rwkv_ln_b[0], s5_lam_re[0], s5_lam_im[0], s5_log_step[0], s5_b_re[0], s5_b_im[0],
                  s5_c_re[0], s5_c_im[0], s5_d[0], s5_glu_w[0], s5_glu_b[0], w_out[0], norm_attn[0],
                  norm_mem[0], w_q[0], w_k[0], w_v[0], w_o[0], norm_ffn[0], router_grp_w[0],
                  router_grp_b[0], router_exp_w[0], router_exp_b[0], exp_w_gate[0], exp_w_up[0],
                  exp_w_down[0], norm_final)
```

```python
import functools
import math

import jax
import jax.numpy as jnp
from jax import lax
from jax.experimental import pallas as pl
from jax.experimental.pallas import tpu as pltpu

F32 = jnp.float32
BF16 = jnp.bfloat16

D_MODEL = 4096
D_RWKV = 2048
D_S5 = 2048
HEAD = 64
PAIR = 2 * HEAD
N_HEADS = D_RWKV // HEAD
LORA_PAD = 128
N_LORA = 4 * LORA_PAD + 256
N_MAIN = 3 * D_RWKV + D_S5
S5_CH = 16
S5_GROUPS = D_S5 // S5_CH
S5_STATE = 64
S5_L = 8
S5_GB = 8
ATTN_HEADS = 4
ATTN_HEAD_DIM = D_MODEL // ATTN_HEADS
ATTN_SCALE = ATTN_HEAD_DIM ** -0.5
N_GROUPS = 8
EPG = 8
N_EXPERTS = N_GROUPS * EPG
TOP_K = 2
D_EXPERT = D_MODEL // 8
RMS_EPS = 1e-6
GN_EPS = 64e-5
L2_EPS = 1e-12

CHUNK = 64
SLAB = 2048
MOE_BLK = 256
COMB_TM = 128
EXPERT_KC = 1024
VMEM_LIMIT = 56 * 1024 * 1024


def _cparams(sem):
    return pltpu.CompilerParams(dimension_semantics=sem, vmem_limit_bytes=VMEM_LIMIT)


def _bdot(a, b):
    return jnp.dot(a.astype(BF16), b.astype(BF16), preferred_element_type=F32)


def _bdot_nt(a, b):
    return lax.dot_general(a.astype(BF16), b.astype(BF16), (((1,), (1,)), ((), ())),
                           preferred_element_type=F32)


def _split2(x):
    hi = x.astype(BF16)
    lo = (x - hi.astype(F32)).astype(BF16)
    return hi, lo


def _split3(x):
    hi = x.astype(BF16)
    r1 = x - hi.astype(F32)
    mid = r1.astype(BF16)
    lo = (r1 - mid.astype(F32)).astype(BF16)
    return hi, mid, lo


def _dot_split2_rhs(x, e):
    h, l = _split2(x)
    return jnp.dot(h, e, preferred_element_type=F32) + jnp.dot(l, e, preferred_element_type=F32)


def _dot_exact_lhs(e, x):
    h, m, l = _split3(x)
    return (jnp.dot(e, h, preferred_element_type=F32) + jnp.dot(e, m, preferred_element_type=F32)
            + jnp.dot(e, l, preferred_element_type=F32))


def _rms_kernel(x_ref, g_ref, o_ref):
    x = x_ref[...]
    ms = jnp.mean(x * x, axis=-1, keepdims=True)
    o_ref[...] = (x * lax.rsqrt(ms + RMS_EPS) * g_ref[...]).astype(o_ref.dtype)


def _rmsnorm(x, gain, out_dtype, tm=256):
    t, d = x.shape
    return pl.pallas_call(
        _rms_kernel,
        grid=(t // tm,),
        in_specs=[pl.BlockSpec((tm, d), lambda i: (i, 0)), pl.BlockSpec((1, d), lambda i: (0, 0))],
        out_specs=pl.BlockSpec((tm, d), lambda i: (i, 0)),
        out_shape=jax.ShapeDtypeStruct((t, d), out_dtype),
        compiler_params=_cparams(("parallel",)),
        name="rmsnorm",
    )(x, gain.reshape(1, d))


def _mm_kernel(a_ref, b_ref, o_ref):
    o_ref[...] = jnp.dot(a_ref[...], b_ref[...], preferred_element_type=F32).astype(o_ref.dtype)


def _mm_res_kernel(a_ref, b_ref, r_ref, o_ref):
    o_ref[...] = (jnp.dot(a_ref[...], b_ref[...], preferred_element_type=F32)
                  + r_ref[...]).astype(o_ref.dtype)


def _mm2_res_kernel(a1_ref, a2_ref, b_ref, r_ref, o_ref):
    k1 = a1_ref.shape[1]
    acc = jnp.dot(a1_ref[...], b_ref[:k1, :], preferred_element_type=F32)
    acc = acc + jnp.dot(a2_ref[...], b_ref[k1:, :], preferred_element_type=F32)
    o_ref[...] = (acc + r_ref[...]).astype(o_ref.dtype)


def _matmul(a, b, out_dtype, tm, tn, res=None, name="matmul"):
    m, k = a.shape
    n = b.shape[1]
    tm, tn = min(tm, m), min(tn, n)
    in_specs = [pl.BlockSpec((tm, k), lambda i, j: (i, 0)), pl.BlockSpec((k, tn), lambda i, j: (0, j))]
    args = [a, b]
    kern = _mm_kernel
    if res is not None:
        in_specs.append(pl.BlockSpec((tm, tn), lambda i, j: (i, j)))
        args.append(res)
        kern = _mm_res_kernel
    return pl.pallas_call(
        kern,
        grid=(m // tm, n // tn),
        in_specs=in_specs,
        out_specs=pl.BlockSpec((tm, tn), lambda i, j: (i, j)),
        out_shape=jax.ShapeDtypeStruct((m, n), out_dtype),
        compiler_params=_cparams(("parallel", "parallel")),
        name=name,
    )(*args)


def _matmul2_res(a1, a2, b, res, tm, tn):
    m, k1 = a1.shape
    k2 = a2.shape[1]
    n = b.shape[1]
    tm = min(tm, m)
    return pl.pallas_call(
        _mm2_res_kernel,
        grid=(m // tm, n // tn),
        in_specs=[pl.BlockSpec((tm, k1), lambda i, j: (i, 0)),
                  pl.BlockSpec((tm, k2), lambda i, j: (i, 0)),
                  pl.BlockSpec((k1 + k2, tn), lambda i, j: (0, j)),
                  pl.BlockSpec((tm, tn), lambda i, j: (i, j))],
        out_specs=pl.BlockSpec((tm, tn), lambda i, j: (i, j)),
        out_shape=jax.ShapeDtypeStruct((m, n), F32),
        compiler_params=_cparams(("parallel", "parallel")),
        name="w_out",
    )(a1, a2, b, res)


def _head_sum_bcast(x, e_ref, et_ref):
    s = _dot_split2_rhs(x, e_ref[...])
    return _dot_split2_rhs(s, et_ref[...])


def _shift3(x, hp, hn, taps, first, last):
    tm = x.shape[0]
    row = lax.broadcasted_iota(jnp.int32, x.shape, 0)
    prev_edge = jnp.where(first, 0.0, hp[7:8, :])
    next_edge = jnp.where(last, 0.0, hn[0:1, :])
    prev = jnp.where(row == 0, prev_edge, pltpu.roll(x, 1, 0))
    nxt = jnp.where(row == tm - 1, next_edge, pltpu.roll(x, tm - 1, 0))
    return taps[0:1, :] * prev + taps[1:2, :] * x + taps[2:3, :] * nxt


def _prep_kernel(seq_tiles,
                 r_ref, rp_ref, rn_ref, k_ref, kp_ref, kn_ref, v_ref, vp_ref, vn_ref,
                 lo_ref, lop_ref, lon_ref,
                 tr_ref, tk_ref, tv_ref, tl_ref,
                 w0_ref, w2_ref, a0_ref, a2_ref, g2_ref, kk_ref_p, e_ref, et_ref,
                 ro_ref, ko_ref, vo_ref, kko_ref, lw_ref, a_ref, g_ref):
    i = pl.program_id(0)
    first = (i % seq_tiles) == 0
    last = (i % seq_tiles) == seq_tiles - 1
    r = _shift3(r_ref[...], rp_ref[...], rn_ref[...], tr_ref[...], first, last)
    k = _shift3(k_ref[...], kp_ref[...], kn_ref[...], tk_ref[...], first, last)
    v = _shift3(v_ref[...], vp_ref[...], vn_ref[...], tv_ref[...], first, last)
    lo = _shift3(lo_ref[...], lop_ref[...], lon_ref[...], tl_ref[...], first, last)
    ro_ref[...] = r.astype(ro_ref.dtype)
    ko_ref[...] = k.astype(ko_ref.dtype)
    vo_ref[...] = v.astype(vo_ref.dtype)
    for d in range(2):
        xw = lo[:, d * LORA_PAD:(d + 1) * LORA_PAD]
        xa = lo[:, (2 + d) * LORA_PAD:(3 + d) * LORA_PAD]
        wl = w0_ref[d:d + 1, :] + _bdot(jnp.tanh(xw), w2_ref[d])
        w_log = -jax.nn.softplus(-wl) - 0.5
        lw_ref[d] = -jnp.exp(w_log)
        a_ref[d] = jax.nn.sigmoid(a0_ref[d:d + 1, :] + _bdot(xa, a2_ref[d])).astype(a_ref.dtype)
    xg = lo[:, 4 * LORA_PAD:]
    g_ref[...] = _bdot(jax.nn.sigmoid(xg), g2_ref[...]).astype(g_ref.dtype)
    kk = k * kk_ref_p[...]
    ssq = _head_sum_bcast(kk * kk, e_ref, et_ref)
    kko_ref[...] = (kk / jnp.maximum(jnp.sqrt(ssq), L2_EPS)).astype(kko_ref.dtype)


def _rwkv_prep(p_main, p_lora, seq, taps_r, taps_k, taps_v, taps_l, w0, w2p, a0, a2p, g2, k_k, e_mat, et_mat,
               tm=128):
    t = p_main.shape[0]
    nt8 = t // 8
    seq_tiles = seq // tm
    c = D_RWKV

    def main_spec(col):
        return [pl.BlockSpec((tm, c), lambda i, col=col: (i, col)),
                pl.BlockSpec((8, c), lambda i, col=col: (jnp.maximum(i * (tm // 8) - 1, 0), col)),
                pl.BlockSpec((8, c), lambda i, col=col: (jnp.minimum((i + 1) * (tm // 8), nt8 - 1), col))]

    lora_spec = [pl.BlockSpec((tm, N_LORA), lambda i: (i, 0)),
                 pl.BlockSpec((8, N_LORA), lambda i: (jnp.maximum(i * (tm // 8) - 1, 0), 0)),
                 pl.BlockSpec((8, N_LORA), lambda i: (jnp.minimum((i + 1) * (tm // 8), nt8 - 1), 0))]

    def full(shape):
        nd = len(shape)
        return pl.BlockSpec(shape, lambda i, nd=nd: (0,) * nd)

    in_specs = (main_spec(0) + main_spec(1) + main_spec(2) + lora_spec
                + [full((3, c)), full((3, c)), full((3, c)), full((3, N_LORA)),
                   full((2, c)), full((2, LORA_PAD, c)), full((2, c)), full((2, LORA_PAD, c)),
                   full((256, c)), full((1, c)), full((c, 128)), full((128, c))])
    row = pl.BlockSpec((tm, c), lambda i: (i, 0))
    row2 = pl.BlockSpec((2, tm, c), lambda i: (0, i, 0))
    sd = jax.ShapeDtypeStruct
    return pl.pallas_call(
        functools.partial(_prep_kernel, seq_tiles),
        grid=(t // tm,),
        in_specs=in_specs,
        out_specs=[row, row, row, row, row2, row2, row],
        out_shape=[sd((t, c), BF16), sd((t, c), BF16), sd((t, c), BF16), sd((t, c), BF16),
                   sd((2, t, c), F32), sd((2, t, c), BF16), sd((t, c), BF16)],
        compiler_params=_cparams(("parallel",)),
        name="rwkv_prep",
    )(p_main, p_main, p_main, p_main, p_main, p_main, p_main, p_main, p_main,
      p_lora, p_lora, p_lora,
      taps_r, taps_k, taps_v, taps_l, w0, w2p, a0, a2p, g2, k_k, e_mat, et_mat)


def _scan_kernel(r_ref, k_ref, v_ref, kk_ref, lw_ref, a_ref, ka_ref, y_ref, st_ref):
    d = pl.program_id(0)
    c = pl.program_id(3)

    @pl.when(c == 0)
    def _():
        st_ref[...] = jnp.zeros_like(st_ref)

    ti = lax.broadcasted_iota(jnp.int32, (CHUNK, PAIR), 0)
    si = lax.broadcasted_iota(jnp.int32, (CHUNK, PAIR), 1) % HEAD
    ahead = (ti - si) * (1 - 2 * d)
    incl = ahead >= 0
    strict = ahead > 0
    eye = (si == ti).astype(F32)

    lw = lw_ref[...]
    cum_incl = _dot_exact_lhs(incl[:, :CHUNK].astype(BF16), lw)
    tot = jnp.sum(lw, axis=0, keepdims=True)
    e_incl = jnp.exp(cum_incl)
    e_excl = jnp.exp(cum_incl - lw)
    e_ninc = jnp.exp(-cum_incl)
    g_tot = jnp.exp(tot)

    a = a_ref[...].astype(F32)
    kk = kk_ref[...].astype(F32)
    kd = k_ref[...].astype(F32) * (1.0 + (a - 1.0) * ka_ref[...])
    at_all = kk * e_excl
    rt_all = r_ref[...].astype(F32) * e_incl
    bt_all = (kk * a) * e_ninc
    kt_all = kd * e_ninc
    bh_all = bt_all * g_tot
    kh_all = kt_all * g_tot
    v_all = v_ref[...].astype(F32)

    npair = SLAB // PAIR
    lane = lax.broadcasted_iota(jnp.int32, (1, PAIR), 1)
    head0 = lane < HEAD

    def bd(y):
        return jnp.concatenate([jnp.where(head0, y, 0.0), jnp.where(head0, 0.0, y)], axis=0)

    sls = [slice(p * PAIR, (p + 1) * PAIR) for p in range(npair)]
    ars = [jnp.concatenate([at_all[:, sl], rt_all[:, sl]], axis=0).astype(BF16) for sl in sls]
    s_ps = [st_ref[p] for p in range(npair)]
    gs = [_bdot_nt(ars[p], jnp.concatenate([bd(bt_all[:, sls[p]]), bd(kt_all[:, sls[p]]), s_ps[p]], axis=0))
          for p in range(npair)]
    ms = [jnp.where(strict, -gs[p][:CHUNK, :PAIR], 0.0) for p in range(npair)]
    tinvs = [eye + ms[p] for p in range(npair)]
    ms = [_bdot(m, bd(m)) for m in ms]
    for _ in range(4):
        ps = [_bdot(jnp.concatenate([tinvs[p], ms[p]], axis=0), bd(ms[p])) for p in range(npair)]
        tinvs = [tinvs[p] + ps[p][:CHUNK] for p in range(npair)]
        ms = [ps[p][CHUNK:] for p in range(npair)]
    tinvs = [tinvs[p] + _bdot(tinvs[p], bd(ms[p])) for p in range(npair)]
    x2s = [_bdot(jnp.concatenate([jnp.where(strict, gs[p][:CHUNK, PAIR:2 * PAIR], 0.0),
                                  jnp.where(incl, gs[p][CHUNK:, PAIR:2 * PAIR], 0.0)], axis=0),
                 bd(v_all[:, sls[p]])) for p in range(npair)]
    us = [-_bdot(tinvs[p], bd(gs[p][:CHUNK, 2 * PAIR:] + x2s[p][:CHUNK])) for p in range(npair)]
    outs = [gs[p][CHUNK:, 2 * PAIR:] + x2s[p][CHUNK:]
            + _bdot(jnp.where(incl, gs[p][CHUNK:, :PAIR], 0.0), bd(us[p])) for p in range(npair)]
    row = lax.broadcasted_iota(jnp.int32, (PAIR, PAIR), 0)
    same_head = (row < HEAD) == (lax.broadcasted_iota(jnp.int32, (PAIR, PAIR), 1) < HEAD)
    for p in range(npair):
        uv = jnp.concatenate([us[p], v_all[:, sls[p]]], axis=0).astype(BF16)
        bk = jnp.concatenate([bh_all[:, sls[p]], kh_all[:, sls[p]]], axis=0).astype(BF16)
        upd = lax.dot_general(uv, bk, (((0,), (0,)), ((), ())), preferred_element_type=F32)
        st_ref[p] = s_ps[p] * g_tot[:, sls[p]] + jnp.where(same_head, upd, 0.0)
    y_ref[...] = jnp.concatenate(outs, axis=1)


def _rwkv_scan(r, k, v, kk, lw, a, k_a, batch, seq):
    nc = seq // CHUNK
    ns = D_RWKV // SLAB
    r4 = r.reshape(batch, seq, D_RWKV)
    k4 = k.reshape(batch, seq, D_RWKV)
    v4 = v.reshape(batch, seq, D_RWKV)
    kk4 = kk.reshape(batch, seq, D_RWKV)
    lw5 = lw.reshape(2, batch, seq, D_RWKV)
    a5 = a.reshape(2, batch, seq, D_RWKV)

    def tchunk(d, c):
        return c + d * (nc - 1 - 2 * c)

    shared = pl.BlockSpec((None, CHUNK, SLAB), lambda d, b, s, c: (b, tchunk(d, c), s))
    perdir = pl.BlockSpec((None, None, CHUNK, SLAB), lambda d, b, s, c: (d, b, tchunk(d, c), s))
    y = pl.pallas_call(
        _scan_kernel,
        grid=(2, batch, ns, nc),
        in_specs=[shared, shared, shared, shared, perdir, perdir,
                  pl.BlockSpec((1, SLAB), lambda d, b, s, c: (0, s))],
        out_specs=perdir,
        out_shape=jax.ShapeDtypeStruct((2, batch, seq, D_RWKV), F32),
        scratch_shapes=[pltpu.VMEM((SLAB // PAIR, PAIR, PAIR), F32)],
        compiler_params=_cparams(("parallel", "parallel", "parallel", "arbitrary")),
        name="rwkv_scan",
    )(r4, k4, v4, kk4, lw5, a5, k_a)
    return y.reshape(2, batch * seq, D_RWKV)


def _post_kernel(y_ref, r_ref, k_ref, v_ref, a_ref, g_ref, ka_ref, rk_ref, lnw_ref, lnb_ref,
                 e_ref, et_ref, o_ref):
    y = y_ref[0] + y_ref[1]
    mu = _head_sum_bcast(y, e_ref, et_ref) * (1.0 / HEAD)
    yc = y - mu
    var = _head_sum_bcast(yc * yc, e_ref, et_ref) * (1.0 / HEAD)
    yn = yc * lax.rsqrt(var + GN_EPS) * lnw_ref[...] + lnb_ref[...]
    ka = ka_ref[...]
    k = k_ref[...].astype(F32)
    kd_sum = (k * (1.0 + (a_ref[0].astype(F32) - 1.0) * ka)
              + k * (1.0 + (a_ref[1].astype(F32) - 1.0) * ka))
    bonus = (_head_sum_bcast(r_ref[...].astype(F32) * kd_sum * rk_ref[...], e_ref, et_ref)
             * v_ref[...].astype(F32))
    o_ref[...] = ((yn + bonus) * g_ref[...].astype(F32)).astype(o_ref.dtype)


def _rwkv_post(y, r, k, v, a, g, k_a, r_k, ln_w, ln_b, e_mat, et_mat, tm=128):
    t, c = r.shape
    row = pl.BlockSpec((tm, c), lambda i: (i, 0))
    row2 = pl.BlockSpec((2, tm, c), lambda i: (0, i, 0))
    par = pl.BlockSpec((1, c), lambda i: (0, 0))
    return pl.pallas_call(
        _post_kernel,
        grid=(t // tm,),
        in_specs=[row2, row, row, row, row2, row, par, par, par, par,
                  pl.BlockSpec((c, 128), lambda i: (0, 0)), pl.BlockSpec((128, c), lambda i: (0, 0))],
        out_specs=row,
        out_shape=jax.ShapeDtypeStruct((t, c), BF16),
        compiler_params=_cparams(("parallel",)),
        name="rwkv_post",
    )(y, r, k, v, a, g, k_a, r_k, ln_w, ln_b, e_mat, et_mat)


def _s5_tables(lam_re, lam_im, log_step, b_re, b_im, c_re, c_im, d_skip, n_levels):
    L, P, CH, G, GB = S5_L, S5_STATE, S5_CH, S5_GROUPS, S5_GB
    NB = G // GB
    hi = lax.Precision.HIGHEST
    lam = lax.complex(lam_re.astype(F32), lam_im.astype(F32))
    dt = jnp.exp(log_step.astype(F32))[..., None]
    lam_dt = lam * dt
    lam_bar = jnp.exp(lam_dt)
    b = lax.complex(b_re.astype(F32), b_im.astype(F32))
    b_bar = ((lam_bar - 1.0) / lam)[..., None] * b
    c = lax.complex(c_re.astype(F32), c_im.astype(F32))
    taus = jnp.arange(L + 1, dtype=F32)
    pows = jnp.exp(lam_dt[:, :, None, :] * taus[None, None, :, None])
    kern = jnp.real(jnp.einsum('dgop,dgtp,dgpi->dgtoi', c, pows[:, :, :L], b_bar, precision=hi))
    j = jnp.arange(L)[:, None]
    t = jnp.arange(L)[None, :]
    lag = jnp.arange(L)[None, None, :]
    sel_f = ((t - j)[:, :, None] == lag).astype(F32)
    sel_b = ((j - t)[:, :, None] == lag).astype(F32)
    skip = (jnp.eye(L)[None, :, None, :, None] * jnp.eye(CH)[None, None, :, None, :]
            * d_skip.astype(F32).reshape(G, 1, CH, 1, 1))
    tsmall = (jnp.einsum('jtl,gloi->gjito', sel_f, kern[0], precision=hi)
              + jnp.einsum('jtl,gloi->gjito', sel_b, kern[1], precision=hi) + skip)
    def by_block(x, lead):
        x = x.reshape((NB, GB) + x.shape[1:])
        perm = (0,) + tuple(range(2, 2 + lead)) + (1,) + tuple(range(2 + lead, x.ndim))
        return x.transpose(perm)

    a_t = by_block(tsmall.reshape(G, L, CH, L * CH), 1).reshape(NB, L * GB * CH, L * CH)
    desc = jnp.exp(lam_dt[:, :, None, :] * (L - taus[:L])[None, None, :, None])
    desc1 = jnp.exp(lam_dt[0][:, None, :] * (L - 1 - taus[:L])[None, :, None])
    bb_t = b_bar.transpose(0, 1, 3, 2)
    zf = desc1[:, :, None, :] * bb_t[0][:, None]
    zb = pows[1][:, :L][:, :, None, :] * bb_t[1][:, None]
    wzs = jnp.stack([jnp.real(zf), jnp.imag(zf), jnp.real(zb), jnp.imag(zb)], axis=3)
    a_z = by_block(wzs.reshape(G, L, CH, 4 * P), 1).reshape(NB, L * GB * CH, 4 * P)
    c_t = c.transpose(0, 1, 3, 2)
    yf = c_t[0][:, :, None, :] * pows[0][:, 1:].transpose(0, 2, 1)[:, :, :, None]
    yb = c_t[1][:, :, None, :] * desc[1].transpose(0, 2, 1)[:, :, :, None]
    wys = jnp.stack([jnp.real(yf), -jnp.imag(yf), jnp.real(yb), -jnp.imag(yb)], axis=1)
    a_y = by_block(wys.reshape(G, 4, P, L * CH), 1).reshape(NB, 4 * GB * P, L * CH)
    steps = L * (2.0 ** jnp.arange(n_levels, dtype=F32))
    lp = jnp.exp(lam_dt[:, :, None, :] * steps[None, None, :, None])
    m = jnp.stack([jnp.real(lp[0]), jnp.imag(lp[0]), jnp.real(lp[1]), jnp.imag(lp[1])], axis=0)
    mult = m.reshape(4, NB, GB, n_levels, P).transpose(1, 3, 0, 2, 4).reshape(NB, n_levels, 4, GB * P)
    return a_t.astype(BF16), a_z.astype(BF16), a_y.astype(BF16), mult


def _group_of(shape, dim, width):
    return (lax.broadcasted_iota(jnp.int32, shape, dim) // width) % S5_GB


def _s5_kernel(n_levels, x_ref, at_ref, az_ref, ay_ref, rt_ref, rz_ref, mult_ref, y_ref, t_s, wz_s, wy_s):
    nck = x_ref.shape[0] // S5_L
    lanes = x_ref.shape[1]
    w = S5_GB * S5_STATE

    @pl.when(pl.program_id(1) == 0)
    def _():
        t = jnp.dot(at_ref[...], rt_ref[...], preferred_element_type=F32)
        keep = _group_of(t.shape, 0, S5_CH) == _group_of(t.shape, 1, S5_CH)
        t_s[...] = jnp.where(keep, t, 0.0).astype(BF16)
        z = jnp.dot(az_ref[...], rz_ref[...], preferred_element_type=F32)
        keep = _group_of(z.shape, 0, S5_CH) == _group_of(z.shape, 1, S5_STATE)
        wz_s[...] = jnp.where(keep, z, 0.0).astype(BF16)
        yy = jnp.dot(ay_ref[...], rt_ref[...], preferred_element_type=F32)
        keep = _group_of(yy.shape, 0, S5_STATE) == _group_of(yy.shape, 1, S5_CH)
        wy_s[...] = jnp.where(keep, yy, 0.0).astype(BF16)

    xcat = jnp.concatenate([x_ref[pl.ds(tl, nck, stride=S5_L), :].astype(BF16) for tl in range(S5_L)],
                           axis=1)
    z = jnp.dot(xcat, wz_s[...], preferred_element_type=F32)
    fre, fim, bre, bim = z[:, :w], z[:, w:2 * w], z[:, 2 * w:3 * w], z[:, 3 * w:]
    cidx = lax.broadcasted_iota(jnp.int32, (nck, w), 0)
    for i in range(n_levels):
        sh = 1 << i
        lfr, lfi = mult_ref[i, 0:1, :], mult_ref[i, 1:2, :]
        lbr, lbi = mult_ref[i, 2:3, :], mult_ref[i, 3:4, :]
        keep = cidx >= sh
        sr = jnp.where(keep, pltpu.roll(fre, sh, 0), 0.0)
        si = jnp.where(keep, pltpu.roll(fim, sh, 0), 0.0)
        fre, fim = fre + sr * lfr - si * lfi, fim + sr * lfi + si * lfr
        keep = cidx < nck - sh
        sr = jnp.where(keep, pltpu.roll(bre, nck - sh, 0), 0.0)
        si = jnp.where(keep, pltpu.roll(bim, nck - sh, 0), 0.0)
        bre, bim = bre + sr * lbr - si * lbi, bim + sr * lbi + si * lbr
    has_prev = cidx >= 1
    has_next = cidx < nck - 1
    xin = jnp.concatenate([jnp.where(has_prev, pltpu.roll(fre, 1, 0), 0.0),
                           jnp.where(has_prev, pltpu.roll(fim, 1, 0), 0.0),
                           jnp.where(has_next, pltpu.roll(bre, nck - 1, 0), 0.0),
                           jnp.where(has_next, pltpu.roll(bim, nck - 1, 0), 0.0)], axis=1).astype(BF16)
    y = jnp.dot(xcat, t_s[...], preferred_element_type=F32)
    y = y + jnp.dot(xin, wy_s[...], preferred_element_type=F32)
    for tl in range(S5_L):
        y_ref[pl.ds(tl, nck, stride=S5_L), :] = y[:, tl * lanes:(tl + 1) * lanes]


def _s5_mix(p_main, col0, a_t, a_z, a_y, mult, batch, seq, n_levels):
    nb = a_t.shape[0]
    lanes = S5_GB * S5_CH
    kw = S5_L * lanes
    sw = 4 * S5_GB * S5_STATE
    r_t = (jnp.arange(S5_L * S5_CH)[:, None] ==
           (jnp.arange(kw)[None, :] // lanes) * S5_CH + jnp.arange(kw)[None, :] % S5_CH).astype(BF16)
    r_z = (jnp.arange(4 * S5_STATE)[:, None] ==
           (jnp.arange(sw)[None, :] // (S5_GB * S5_STATE)) * S5_STATE + jnp.arange(sw)[None, :] % S5_STATE
           ).astype(BF16)
    return pl.pallas_call(
        functools.partial(_s5_kernel, n_levels),
        grid=(nb, batch),
        in_specs=[pl.BlockSpec((seq, lanes), lambda j, b: (b, col0 + j)),
                  pl.BlockSpec((None, kw, S5_L * S5_CH), lambda j, b: (j, 0, 0)),
                  pl.BlockSpec((None, kw, 4 * S5_STATE), lambda j, b: (j, 0, 0)),
                  pl.BlockSpec((None, sw, S5_L * S5_CH), lambda j, b: (j, 0, 0)),
                  pl.BlockSpec((S5_L * S5_CH, kw), lambda j, b: (0, 0)),
                  pl.BlockSpec((4 * S5_STATE, sw), lambda j, b: (0, 0)),
                  pl.BlockSpec((None, n_levels, 4, S5_GB * S5_STATE), lambda j, b: (j, 0, 0, 0))],
        out_specs=pl.BlockSpec((seq, lanes), lambda j, b: (b, j)),
        out_shape=jax.ShapeDtypeStruct((batch * seq, D_S5), F32),
        scratch_shapes=[pltpu.VMEM((kw, kw), BF16), pltpu.VMEM((kw, sw), BF16), pltpu.VMEM((sw, kw), BF16)],
        compiler_params=_cparams(("parallel", "arbitrary")),
        name="s5_mix",
    )(p_main, a_t, a_z, a_y, r_t, r_z, mult)


def _glu_kernel(y_ref, w_ref, b_ref, o_ref):
    y = y_ref[...]
    z = jnp.dot(jax.nn.gelu(y).astype(BF16), w_ref[...], preferred_element_type=F32) + b_ref[...]
    o_ref[...] = (y * jax.nn.sigmoid(z)).astype(o_ref.dtype)


def _glu(y, w, b, tm=512):
    t, c = y.shape
    tm = min(tm, t)
    return pl.pallas_call(
        _glu_kernel,
        grid=(t // tm,),
        in_specs=[pl.BlockSpec((tm, c), lambda i: (i, 0)), pl.BlockSpec((c, c), lambda i: (0, 0)),
                  pl.BlockSpec((1, c), lambda i: (0, 0))],
        out_specs=pl.BlockSpec((tm, c), lambda i: (i, 0)),
        out_shape=jax.ShapeDtypeStruct((t, c), BF16),
        compiler_params=_cparams(("parallel",)),
        name="s5_glu",
    )(y, w, b)


def _wqk_kernel(wq_ref, k_ref, o_ref):
    acc = lax.dot_general(wq_ref[...], k_ref[...], (((1,), (1,)), ((), ())), preferred_element_type=F32)
    o_ref[...] = (acc * ATTN_SCALE).astype(o_ref.dtype)


def _vwo_kernel(v_ref, wo_ref, o_ref):
    o_ref[...] = jnp.dot(v_ref[...], wo_ref[...], preferred_element_type=F32).astype(o_ref.dtype)


def _attn_kernel(mem, xn_ref, wqk_ref, vwo_ref, res_ref, o_ref, p_ref):
    @pl.when(pl.program_id(2) == 0)
    def _():
        s = jnp.dot(xn_ref[...], wqk_ref[...], preferred_element_type=F32)
        for h in range(ATTN_HEADS):
            sh = s[:, h * mem:(h + 1) * mem]
            sh = sh - jnp.max(sh, axis=-1, keepdims=True)
            p = jnp.exp(sh)
            p_ref[:, h * mem:(h + 1) * mem] = (p / jnp.sum(p, axis=-1, keepdims=True)).astype(BF16)

    o_ref[...] = jnp.dot(p_ref[...], vwo_ref[...], preferred_element_type=F32) + res_ref[...]


def _cross_attention(xn, kv, w_q, w_o, res, batch, seq, mem, tm=512, tn=1024):
    d = D_MODEL
    hm = ATTN_HEADS * mem
    tm = min(tm, seq)
    wqk = pl.pallas_call(
        _wqk_kernel,
        grid=(batch, ATTN_HEADS),
        in_specs=[pl.BlockSpec((d, ATTN_HEAD_DIM), lambda b, h: (0, h)),
                  pl.BlockSpec((mem, ATTN_HEAD_DIM), lambda b, h: (b, h))],
        out_specs=pl.BlockSpec((None, d, mem), lambda b, h: (b, 0, h)),
        out_shape=jax.ShapeDtypeStruct((batch, d, hm), BF16),
        compiler_params=_cparams(("parallel", "parallel")),
        name="attn_wqk",
    )(w_q, kv)
    vwo = pl.pallas_call(
        _vwo_kernel,
        grid=(batch, ATTN_HEADS),
        in_specs=[pl.BlockSpec((mem, ATTN_HEAD_DIM), lambda b, h: (b, ATTN_HEADS + h)),
                  pl.BlockSpec((ATTN_HEAD_DIM, d), lambda b, h: (h, 0))],
        out_specs=pl.BlockSpec((None, mem, d), lambda b, h: (b, h, 0)),
        out_shape=jax.ShapeDtypeStruct((batch, hm, d), BF16),
        compiler_params=_cparams(("parallel", "parallel")),
        name="attn_vwo",
    )(kv, w_o)
    out = pl.pallas_call(
        functools.partial(_attn_kernel, mem),
        grid=(batch, seq // tm, d // tn),
        in_specs=[pl.BlockSpec((None, tm, d), lambda b, i, j: (b, i, 0)),
                  pl.BlockSpec((None, d, hm), lambda b, i, j: (b, 0, 0)),
                  pl.BlockSpec((None, hm, tn), lambda b, i, j: (b, 0, j)),
                  pl.BlockSpec((None, tm, tn), lambda b, i, j: (b, i, j))],
        out_specs=pl.BlockSpec((None, tm, tn), lambda b, i, j: (b, i, j)),
        out_shape=jax.ShapeDtypeStruct((batch, seq, d), F32),
        scratch_shapes=[pltpu.VMEM((tm, hm), BF16)],
        compiler_params=_cparams(("parallel", "parallel", "arbitrary")),
        name="cross_attn",
    )(xn.reshape(batch, seq, d), wqk, vwo, res.reshape(batch, seq, d))
    return out.reshape(batch * seq, d)


def _router_kernel(x_ref, g_ref, whi_ref, wlo_ref, b_ref, xn_ref, info_ref, cnt_ref, carry_ref):
    i = pl.program_id(0)

    @pl.when(i == 0)
    def _():
        carry_ref[...] = jnp.zeros_like(carry_ref)

    x = x_ref[...]
    tm = x.shape[0]
    ms = jnp.mean(x * x, axis=-1, keepdims=True)
    xn = x * lax.rsqrt(ms + RMS_EPS) * g_ref[...]
    xn_ref[...] = xn
    xh, xl = _split2(xn)
    whi = whi_ref[...]
    logits = (jnp.dot(xh, whi, preferred_element_type=F32) + jnp.dot(xl, whi, preferred_element_type=F32)
              + jnp.dot(xh, wlo_ref[...], preferred_element_type=F32)) + b_ref[...]
    li = lax.broadcasted_iota(jnp.int32, logits.shape, 1)
    neg = jnp.float32(-jnp.inf)
    is_g = li < N_GROUPS
    gl = jnp.where(is_g, logits, neg)
    gm = jnp.max(gl, axis=-1, keepdims=True)
    gi = jnp.min(jnp.where(is_g & (gl == gm), li, 128), axis=-1, keepdims=True)
    gp = 1.0 / jnp.sum(jnp.where(is_g, jnp.exp(gl - gm), 0.0), axis=-1, keepdims=True)
    lo_lane = N_GROUPS + EPG * gi
    sel = (li >= lo_lane) & (li < lo_lane + EPG)
    l1 = jnp.where(sel, logits, neg)
    e1 = jnp.max(l1, axis=-1, keepdims=True)
    i1 = jnp.min(jnp.where(sel & (l1 == e1), li, 128), axis=-1, keepdims=True)
    sel2 = sel & (li != i1)
    l2 = jnp.where(sel2, logits, neg)
    e2 = jnp.max(l2, axis=-1, keepdims=True)
    i2 = jnp.min(jnp.where(sel2 & (l2 == e2), li, 128), axis=-1, keepdims=True)
    ex = jnp.exp(e2 - e1)
    w1 = gp / (1.0 + ex)
    w2 = gp * ex / (1.0 + ex)
    x1 = i1 - N_GROUPS
    x2 = i2 - N_GROUPS
    oh1 = li == x1
    oh2 = li == x2
    oh = (oh1 | oh2).astype(BF16)
    ri = lax.broadcasted_iota(jnp.int32, (tm, tm), 0)
    ci = lax.broadcasted_iota(jnp.int32, (tm, tm), 1)
    before = jnp.dot((ci < ri).astype(BF16), oh, preferred_element_type=F32) + carry_ref[0:1, :]
    r1 = jnp.sum(jnp.where(oh1, before, 0.0), axis=-1, keepdims=True)
    r2 = jnp.sum(jnp.where(oh2, before, 0.0), axis=-1, keepdims=True)
    new_carry = carry_ref[0:1, :] + jnp.sum(oh.astype(F32), axis=0, keepdims=True)
    carry_ref[...] = jnp.broadcast_to(new_carry, carry_ref.shape)
    cnt_ref[...] = jnp.broadcast_to(new_carry, cnt_ref.shape)
    info = jnp.where(li == 0, x1.astype(F32), 0.0)
    info = jnp.where(li == 1, x2.astype(F32), info)
    info = jnp.where(li == 2, w1, info)
    info = jnp.where(li == 3, w2, info)
    info = jnp.where(li == 4, r1, info)
    info = jnp.where(li == 5, r2, info)
    info_ref[...] = info


def _router(x, gain, w_hi, w_lo, bias, tm=256):
    t, d = x.shape
    sd = jax.ShapeDtypeStruct
    return pl.pallas_call(
        _router_kernel,
        grid=(t // tm,),
        in_specs=[pl.BlockSpec((tm, d), lambda i: (i, 0)), pl.BlockSpec((1, d), lambda i: (0, 0)),
                  pl.BlockSpec((d, 128), lambda i: (0, 0)), pl.BlockSpec((d, 128), lambda i: (0, 0)),
                  pl.BlockSpec((1, 128), lambda i: (0, 0))],
        out_specs=[pl.BlockSpec((tm, d), lambda i: (i, 0)), pl.BlockSpec((tm, 128), lambda i: (i, 0)),
                   pl.BlockSpec((8, 128), lambda i: (0, 0))],
        out_shape=[sd((t, d), F32), sd((t, 128), F32), sd((8, 128), F32)],
        scratch_shapes=[pltpu.VMEM((8, 128), F32)],
        compiler_params=_cparams(("arbitrary",)),
        name="moe_router",
    )(x, gain.reshape(1, d), w_hi, w_lo, bias)


def _gather_rows(src_hbm, idx_ref, dst_ref, sem, n_rows):
    def body(j, carry):
        pltpu.make_async_copy(src_hbm.at[pl.ds(idx_ref[0, j], 1), :], dst_ref.at[pl.ds(j, 1), :], sem).start()
        return carry
    lax.fori_loop(0, n_rows, body, 0, unroll=8)


def _wait_rows(src_hbm, dst_ref, sem, n_rows):
    pltpu.make_async_copy(src_hbm.at[pl.ds(0, n_rows), :], dst_ref, sem).wait()


def _expert_up_kernel(be_ref, nused_ref, tok_ref, tokn_ref, x_hbm, wg_ref, wu_ref, h_ref, xbuf, sem):
    i = pl.program_id(0)
    n_used = nused_ref[0]
    slot = i % 2

    @pl.when(i == 0)
    def _():
        _gather_rows(x_hbm, tok_ref, xbuf.at[0], sem.at[0], MOE_BLK)

    @pl.when(i + 1 < n_used)
    def _():
        _gather_rows(x_hbm, tokn_ref, xbuf.at[1 - slot], sem.at[1 - slot], MOE_BLK)

    @pl.when(i < n_used)
    def _():
        _wait_rows(x_hbm, xbuf.at[slot], sem.at[slot], MOE_BLK)
        d = xbuf.shape[2]
        hg = jnp.zeros((MOE_BLK, D_EXPERT), F32)
        hu = jnp.zeros((MOE_BLK, D_EXPERT), F32)
        for kc in range(d // EXPERT_KC):
            ks = slice(kc * EXPERT_KC, (kc + 1) * EXPERT_KC)
            xb = xbuf[slot, :, ks].astype(BF16)
            hg = hg + jnp.dot(xb, wg_ref[ks, :].astype(BF16), preferred_element_type=F32)
            hu = hu + jnp.dot(xb, wu_ref[ks, :].astype(BF16), preferred_element_type=F32)
        h_ref[...] = (jax.nn.silu(hg) * hu).astype(h_ref.dtype)

    @pl.when(i >= n_used)
    def _():
        h_ref[...] = jnp.zeros_like(h_ref)


def _expert_down_kernel(be_ref, nused_ref, h_ref, wd_ref, y_ref):
    i = pl.program_id(0)
    n_used = nused_ref[0]

    @pl.when(i < n_used)
    def _():
        h = h_ref[...]
        d = y_ref.shape[1]
        for nc in range(d // EXPERT_KC):
            ns = slice(nc * EXPERT_KC, (nc + 1) * EXPERT_KC)
            y_ref[:, ns] = jnp.dot(h, wd_ref[:, ns].astype(BF16), preferred_element_type=F32)

    @pl.when(i >= n_used)
    def _():
        y_ref[...] = jnp.zeros_like(y_ref)


def _experts(block_expert, n_used, tok3, xn, wg, wu, wd):
    nb = tok3.shape[0]
    d = xn.shape[1]
    tok_spec = pl.BlockSpec((None, 1, MOE_BLK), lambda i, be, nu: (i, 0, 0), memory_space=pltpu.SMEM)
    tokn_spec = pl.BlockSpec((None, 1, MOE_BLK), lambda i, be, nu: (jnp.minimum(i + 1, nb - 1), 0, 0),
                             memory_space=pltpu.SMEM)
    h_buf = pl.pallas_call(
        _expert_up_kernel,
        grid_spec=pltpu.PrefetchScalarGridSpec(
            num_scalar_prefetch=2,
            grid=(nb,),
            in_specs=[tok_spec, tokn_spec, pl.BlockSpec(memory_space=pl.ANY),
                      pl.BlockSpec((None, d, D_EXPERT), lambda i, be, nu: (be[i], 0, 0)),
                      pl.BlockSpec((None, d, D_EXPERT), lambda i, be, nu: (be[i], 0, 0))],
            out_specs=pl.BlockSpec((MOE_BLK, D_EXPERT), lambda i, be, nu: (i, 0)),
            scratch_shapes=[pltpu.VMEM((2, MOE_BLK, d), F32), pltpu.SemaphoreType.DMA((2,))],
        ),
        out_shape=jax.ShapeDtypeStruct((nb * MOE_BLK, D_EXPERT), BF16),
        compiler_params=_cparams(("arbitrary",)),
        name="moe_up",
    )(block_expert, n_used, tok3, tok3, xn, wg, wu)
    return pl.pallas_call(
        _expert_down_kernel,
        grid_spec=pltpu.PrefetchScalarGridSpec(
            num_scalar_prefetch=2,
            grid=(nb,),
            in_specs=[pl.BlockSpec((MOE_BLK, D_EXPERT), lambda i, be, nu: (i, 0)),
                      pl.BlockSpec((None, D_EXPERT, d), lambda i, be, nu: (be[i], 0, 0))],
            out_specs=pl.BlockSpec((MOE_BLK, d), lambda i, be, nu: (i, 0)),
        ),
        out_shape=jax.ShapeDtypeStruct((nb * MOE_BLK, d), F32),
        compiler_params=_cparams(("arbitrary",)),
        name="moe_down",
    )(block_expert, n_used, h_buf, wd)


def _combine_kernel(d1_ref, d1n_ref, d2_ref, d2n_ref, x_ref, info_ref, g_ref, y_hbm, o_ref, ybuf, sem):
    i = pl.program_id(0)
    n = pl.num_programs(0)
    slot = i % 2

    @pl.when(i == 0)
    def _():
        _gather_rows(y_hbm, d1_ref, ybuf.at[0, 0], sem.at[0], COMB_TM)
        _gather_rows(y_hbm, d2_ref, ybuf.at[0, 1], sem.at[0], COMB_TM)

    @pl.when(i + 1 < n)
    def _():
        _gather_rows(y_hbm, d1n_ref, ybuf.at[1 - slot, 0], sem.at[1 - slot], COMB_TM)
        _gather_rows(y_hbm, d2n_ref, ybuf.at[1 - slot, 1], sem.at[1 - slot], COMB_TM)

    _wait_rows(y_hbm, ybuf.at[slot, 0], sem.at[slot], COMB_TM)
    _wait_rows(y_hbm, ybuf.at[slot, 1], sem.at[slot], COMB_TM)
    info = info_ref[...]
    w1 = info[:, 2:3]
    w2 = info[:, 3:4]
    x = x_ref[...] + ybuf[slot, 0] * w1 + ybuf[slot, 1] * w2
    ms = jnp.mean(x * x, axis=-1, keepdims=True)
    o_ref[...] = x * lax.rsqrt(ms + RMS_EPS) * g_ref[...]


def _combine(dest1, dest2, x, info, gain, y_buf):
    t, d = x.shape
    nt = t // COMB_TM
    d1 = dest1.reshape(nt, 1, COMB_TM)
    d2 = dest2.reshape(nt, 1, COMB_TM)
    cur = pl.BlockSpec((None, 1, COMB_TM), lambda i: (i, 0, 0), memory_space=pltpu.SMEM)
    nxt = pl.BlockSpec((None, 1, COMB_TM), lambda i: (jnp.minimum(i + 1, nt - 1), 0, 0),
                       memory_space=pltpu.SMEM)
    return pl.pallas_call(
        _combine_kernel,
        grid=(nt,),
        in_specs=[cur, nxt, cur, nxt,
                  pl.BlockSpec((COMB_TM, d), lambda i: (i, 0)),
                  pl.BlockSpec((COMB_TM, 128), lambda i: (i, 0)),
                  pl.BlockSpec((1, d), lambda i: (0, 0)),
                  pl.BlockSpec(memory_space=pl.ANY)],
        out_specs=pl.BlockSpec((COMB_TM, d), lambda i: (i, 0)),
        out_shape=jax.ShapeDtypeStruct((t, d), F32),
        scratch_shapes=[pltpu.VMEM((2, 2, COMB_TM, d), F32), pltpu.SemaphoreType.DMA((2,))],
        compiler_params=_cparams(("arbitrary",)),
        name="moe_combine",
    )(d1, d1, d2, d2, x, info, gain.reshape(1, d), y_buf)


def _head_matrices():
    lane = jnp.arange(D_RWKV) // HEAD
    e = (lane[:, None] == jnp.arange(128)[None, :]).astype(BF16)
    return e, e.T


def _pad_lora_rows(w):
    return jnp.pad(w, ((0, 0), (0, LORA_PAD - w.shape[1]), (0, 0)))


def _layer(x, mem, norm_mix, w_in, shift_taps, rwkv_w0, rwkv_w2, rwkv_a0, rwkv_a2, rwkv_g2, rwkv_k_k,
           rwkv_k_a, rwkv_r_k, rwkv_ln_w, rwkv_ln_b, s5_lam_re, s5_lam_im, s5_log_step, s5_b_re, s5_b_im,
           s5_c_re, s5_c_im, s5_d, s5_glu_w, s5_glu_b, w_out, norm_attn, norm_mem, w_q, w_k, w_v, w_o,
           norm_ffn, router_grp_w, router_grp_b, router_exp_w, router_exp_b, exp_w_gate, exp_w_up,
           exp_w_down, out_gain):
    batch, seq, d = x.shape
    t = batch * seq
    mem_n = mem.shape[1]
    xt = x.reshape(t, d)
    c3 = 3 * D_RWKV
    off_g = c3 + 4 * 96

    def pad_cols(w, lo, width):
        return jnp.pad(w[:, lo:lo + width], ((0, 0), (0, LORA_PAD - width)))

    w_main = jnp.concatenate([w_in[:, :c3], w_in[:, off_g + 256:]], axis=1).astype(BF16)
    w_lora = jnp.concatenate([pad_cols(w_in, c3 + i * 96, 96) for i in range(4)]
                             + [w_in[:, off_g:off_g + 256]], axis=1).astype(BF16)
    taps_l = jnp.concatenate([pad_cols(shift_taps, c3 + i * 96, 96) for i in range(4)]
                             + [shift_taps[:, off_g:off_g + 256]], axis=1)
    h = _rmsnorm(xt, norm_mix, BF16)
    p_main = _matmul(h, w_main, F32, 1024, 512, name="w_in_main")
    p_lora = _matmul(h, w_lora, F32, 1024, N_LORA, name="w_in_lora")

    e_mat, et_mat = _head_matrices()
    k_a = rwkv_k_a.reshape(1, D_RWKV)
    r, k, v, kk, lw, a, g = _rwkv_prep(
        p_main, p_lora, seq, shift_taps[:, :D_RWKV], shift_taps[:, D_RWKV:2 * D_RWKV],
        shift_taps[:, 2 * D_RWKV:c3], taps_l, rwkv_w0, _pad_lora_rows(rwkv_w2).astype(BF16), rwkv_a0,
        _pad_lora_rows(rwkv_a2).astype(BF16), rwkv_g2.astype(BF16), rwkv_k_k.reshape(1, D_RWKV),
        e_mat, et_mat)
    y_scan = _rwkv_scan(r, k, v, kk, lw, a, k_a, batch, seq)
    y_rwkv = _rwkv_post(y_scan, r, k, v, a, g, k_a, rwkv_r_k.reshape(1, D_RWKV),
                        rwkv_ln_w.reshape(1, D_RWKV), rwkv_ln_b.reshape(1, D_RWKV), e_mat, et_mat)

    n_levels = int(math.log2(seq // S5_L))
    a_t, a_z, a_y, mult = _s5_tables(s5_lam_re, s5_lam_im, s5_log_step, s5_b_re, s5_b_im,
                                    s5_c_re, s5_c_im, s5_d, n_levels)
    y_s5 = _s5_mix(p_main, c3 // (S5_GB * S5_CH), a_t, a_z, a_y, mult, batch, seq, n_levels)
    y_glu = _glu(y_s5, s5_glu_w.astype(BF16), s5_glu_b.reshape(1, D_S5))

    x1 = _matmul2_res(y_rwkv, y_glu, w_out.astype(BF16), xt, 1024, 512)

    xn = _rmsnorm(x1, norm_attn, BF16)
    memn = _rmsnorm(mem.reshape(batch * mem_n, d), norm_mem, BF16)
    wkv = jnp.concatenate([w_k, w_v], axis=1).astype(BF16)
    kv = _matmul(memn, wkv, BF16, 1024, 512, name="w_kv")
    x2 = _cross_attention(xn, kv, w_q.astype(BF16), w_o.astype(BF16), x1, batch, seq, mem_n)

    w_r = jnp.concatenate([router_grp_w, router_exp_w,
                           jnp.zeros((d, 128 - N_GROUPS - N_EXPERTS), F32)], axis=1)
    w_r_hi = w_r.astype(BF16)
    w_r_lo = (w_r - w_r_hi.astype(F32)).astype(BF16)
    b_r = jnp.concatenate([router_grp_b, router_exp_b,
                           jnp.zeros((128 - N_GROUPS - N_EXPERTS,), F32)]).reshape(1, 128)
    xn3, info, cnt = _router(x2, norm_ffn, w_r_hi, w_r_lo, b_r)
    eid = info[:, 0:2].astype(jnp.int32)
    rank = info[:, 4:6].astype(jnp.int32)
    counts = cnt[0, :N_EXPERTS].astype(jnp.int32)
    nblk = (counts + MOE_BLK - 1) // MOE_BLK
    bstart = jnp.cumsum(nblk) - nblk
    n_used = jnp.sum(nblk)
    first = jnp.sum(jnp.where(eid[:, :, None] == jnp.arange(N_EXPERTS)[None, None, :], bstart, 0), axis=-1)
    dest = first * MOE_BLK + rank
    nb = (t * TOP_K) // MOE_BLK + N_EXPERTS
    tok_buf = jnp.zeros((nb * MOE_BLK,), jnp.int32).at[dest.reshape(-1)].set(
        jnp.repeat(jnp.arange(t, dtype=jnp.int32), TOP_K))
    blk = jnp.arange(nb, dtype=jnp.int32)
    block_expert = jnp.sum(blk[:, None] >= (bstart + nblk)[None, :], axis=1).astype(jnp.int32)
    last_e = jnp.max(jnp.where(nblk > 0, jnp.arange(N_EXPERTS), 0)).astype(jnp.int32)
    block_expert = jnp.minimum(block_expert, last_e)
    y_buf = _experts(block_expert, n_used.reshape(1).astype(jnp.int32), tok_buf.reshape(nb, 1, MOE_BLK),
                     xn3, exp_w_gate, exp_w_up, exp_w_down)
    out = _combine(dest[:, 0], dest[:, 1], x2, info, out_gain, y_buf)
    return out.reshape(batch, seq, d)


def kernel(x, mem, norm_mix, w_in, shift_taps, rwkv_w0, rwkv_w2, rwkv_a0, rwkv_a2, rwkv_g2, rwkv_k_k, rwkv_k_a, rwkv_r_k, rwkv_ln_w, rwkv_ln_b, s5_lam_re, s5_lam_im, s5_log_step, s5_b_re, s5_b_im, s5_c_re, s5_c_im, s5_d, s5_glu_w, s5_glu_b, w_out, norm_attn, norm_mem, w_q, w_k, w_v, w_o, norm_ffn, router_grp_w, router_grp_b, router_exp_w, router_exp_b, exp_w_gate, exp_w_up, exp_w_down, norm_final):
    return _layer(x, mem, norm_mix[0], w_in[0], shift_taps[0], rwkv_w0[0], rwkv_w2[0], rwkv_a0[0],
                  rwkv_a2[0], rwkv_g2[0], rwkv_k_k[0], rwkv_k_a[0], rwkv_r_k[0], rwkv_ln_w[0],
                  rwkv_ln_b[0], s5_lam_re[0], s5_lam_im[0], s5_log_step[0], s5_b_re[0], s5_b_im[0],
                  s5_c_re[0], s5_c_im[0], s5_d[0], s5_glu_w[0], s5_glu_b[0], w_out[0], norm_attn[0],
                  norm_mem[0], w_q[0], w_k[0], w_v[0], w_o[0], norm_ffn[0], router_grp_w[0],
                  router_grp_b[0], router_exp_w[0], router_exp_b[0], exp_w_gate[0], exp_w_up[0],
                  exp_w_down[0], norm_final)
```

```python
import functools
import math

import jax
import jax.numpy as jnp
from jax import lax
from jax.experimental import pallas as pl
from jax.experimental.pallas import tpu as pltpu

F32 = jnp.float32
BF16 = jnp.bfloat16

D_MODEL = 4096
D_RWKV = 2048
D_S5 = 2048
HEAD = 64
PAIR = 2 * HEAD
N_HEADS = D_RWKV // HEAD
LORA_PAD = 128
N_LORA = 4 * LORA_PAD + 256
N_MAIN = 3 * D_RWKV + D_S5
S5_CH = 16
S5_GROUPS = D_S5 // S5_CH
S5_STATE = 64
S5_L = 8
S5_GB = 8
ATTN_HEADS = 4
ATTN_HEAD_DIM = D_MODEL // ATTN_HEADS
ATTN_SCALE = ATTN_HEAD_DIM ** -0.5
N_GROUPS = 8
EPG = 8
N_EXPERTS = N_GROUPS * EPG
TOP_K = 2
D_EXPERT = D_MODEL // 8
RMS_EPS = 1e-6
GN_EPS = 64e-5
L2_EPS = 1e-12

CHUNK = 64
SLAB = 2048
MOE_BLK = 256
COMB_TM = 128
EXPERT_KC = 1024
VMEM_LIMIT = 56 * 1024 * 1024


def _cparams(sem):
    return pltpu.CompilerParams(dimension_semantics=sem, vmem_limit_bytes=VMEM_LIMIT)


def _bdot(a, b):
    return jnp.dot(a.astype(BF16), b.astype(BF16), preferred_element_type=F32)


def _bdot_nt(a, b):
    return lax.dot_general(a.astype(BF16), b.astype(BF16), (((1,), (1,)), ((), ())),
                           preferred_element_type=F32)


def _split2(x):
    hi = x.astype(BF16)
    lo = (x - hi.astype(F32)).astype(BF16)
    return hi, lo


def _split3(x):
    hi = x.astype(BF16)
    r1 = x - hi.astype(F32)
    mid = r1.astype(BF16)
    lo = (r1 - mid.astype(F32)).astype(BF16)
    return hi, mid, lo


def _pack_bf16_pair(lo, hi):
    lo_b = lax.bitcast_convert_type(lo.astype(BF16).astype(F32), jnp.uint32) >> 16
    hi_b = lax.bitcast_convert_type(hi.astype(BF16).astype(F32), jnp.uint32) & jnp.uint32(0xFFFF0000)
    return hi_b | lo_b


def _unpack_bf16_pair(w):
    return (lax.bitcast_convert_type(w << 16, F32),
            lax.bitcast_convert_type(w & jnp.uint32(0xFFFF0000), F32))


def _dot_split2_rhs(x, e):
    h, l = _split2(x)
    return jnp.dot(h, e, preferred_element_type=F32) + jnp.dot(l, e, preferred_element_type=F32)


def _dot_exact_lhs(e, x):
    h, m, l = _split3(x)
    return (jnp.dot(e, h, preferred_element_type=F32) + jnp.dot(e, m, preferred_element_type=F32)
            + jnp.dot(e, l, preferred_element_type=F32))


def _rms_kernel(x_ref, g_ref, o_ref):
    x = x_ref[...]
    ms = jnp.mean(x * x, axis=-1, keepdims=True)
    o_ref[...] = (x * lax.rsqrt(ms + RMS_EPS) * g_ref[...]).astype(o_ref.dtype)


def _rmsnorm(x, gain, out_dtype, tm=256):
    t, d = x.shape
    return pl.pallas_call(
        _rms_kernel,
        grid=(t // tm,),
        in_specs=[pl.BlockSpec((tm, d), lambda i: (i, 0)), pl.BlockSpec((1, d), lambda i: (0, 0))],
        out_specs=pl.BlockSpec((tm, d), lambda i: (i, 0)),
        out_shape=jax.ShapeDtypeStruct((t, d), out_dtype),
        compiler_params=_cparams(("parallel",)),
        name="rmsnorm",
    )(x, gain.reshape(1, d))


def _mm_kernel(a_ref, b_ref, o_ref):
    o_ref[...] = jnp.dot(a_ref[...], b_ref[...], preferred_element_type=F32).astype(o_ref.dtype)


def _mm_res_kernel(a_ref, b_ref, r_ref, o_ref):
    o_ref[...] = (jnp.dot(a_ref[...], b_ref[...], preferred_element_type=F32)
                  + r_ref[...]).astype(o_ref.dtype)


def _mm2_res_kernel(a1_ref, a2_ref, b_ref, r_ref, o_ref):
    k1 = a1_ref.shape[1]
    acc = jnp.dot(a1_ref[...], b_ref[:k1, :], preferred_element_type=F32)
    acc = acc + jnp.dot(a2_ref[...], b_ref[k1:, :], preferred_element_type=F32)
    o_ref[...] = (acc + r_ref[...]).astype(o_ref.dtype)


def _matmul(a, b, out_dtype, tm, tn, res=None, name="matmul"):
    m, k = a.shape
    n = b.shape[1]
    tm, tn = min(tm, m), min(tn, n)
    in_specs = [pl.BlockSpec((tm, k), lambda i, j: (i, 0)), pl.BlockSpec((k, tn), lambda i, j: (0, j))]
    args = [a, b]
    kern = _mm_kernel
    if res is not None:
        in_specs.append(pl.BlockSpec((tm, tn), lambda i, j: (i, j)))
        args.append(res)
        kern = _mm_res_kernel
    return pl.pallas_call(
        kern,
        grid=(m // tm, n // tn),
        in_specs=in_specs,
        out_specs=pl.BlockSpec((tm, tn), lambda i, j: (i, j)),
        out_shape=jax.ShapeDtypeStruct((m, n), out_dtype),
        compiler_params=_cparams(("parallel", "parallel")),
        name=name,
    )(*args)


def _matmul2_res(a1, a2, b, res, tm, tn):
    m, k1 = a1.shape
    k2 = a2.shape[1]
    n = b.shape[1]
    tm = min(tm, m)
    return pl.pallas_call(
        _mm2_res_kernel,
        grid=(m // tm, n // tn),
        in_specs=[pl.BlockSpec((tm, k1), lambda i, j: (i, 0)),
                  pl.BlockSpec((tm, k2), lambda i, j: (i, 0)),
                  pl.BlockSpec((k1 + k2, tn), lambda i, j: (0, j)),
                  pl.BlockSpec((tm, tn), lambda i, j: (i, j))],
        out_specs=pl.BlockSpec((tm, tn), lambda i, j: (i, j)),
        out_shape=jax.ShapeDtypeStruct((m, n), F32),
        compiler_params=_cparams(("parallel", "parallel")),
        name="w_out",
    )(a1, a2, b, res)


def _head_sum_bcast(x, e_ref, et_ref):
    s = _dot_split2_rhs(x, e_ref[...])
    return _dot_split2_rhs(s, et_ref[...])


def _shift3(x, hp, hn, taps, first, last):
    tm = x.shape[0]
    row = lax.broadcasted_iota(jnp.int32, x.shape, 0)
    prev_edge = jnp.where(first, 0.0, hp[7:8, :])
    next_edge = jnp.where(last, 0.0, hn[0:1, :])
    prev = jnp.where(row == 0, prev_edge, pltpu.roll(x, 1, 0))
    nxt = jnp.where(row == tm - 1, next_edge, pltpu.roll(x, tm - 1, 0))
    return taps[0:1, :] * prev + taps[1:2, :] * x + taps[2:3, :] * nxt


def _prep_kernel(seq_tiles,
                 r_ref, rp_ref, rn_ref, k_ref, kp_ref, kn_ref, v_ref, vp_ref, vn_ref,
                 lo_ref, lop_ref, lon_ref,
                 tr_ref, tk_ref, tv_ref, tl_ref,
                 w0_ref, w2_ref, a0_ref, a2_ref, g2_ref, kk_ref_p, e_ref, et_ref,
                 ro_ref, ko_ref, vo_ref, kko_ref, lw_ref, a_ref, g_ref):
    i = pl.program_id(0)
    first = (i % seq_tiles) == 0
    last = (i % seq_tiles) == seq_tiles - 1
    r = _shift3(r_ref[...], rp_ref[...], rn_ref[...], tr_ref[...], first, last)
    k = _shift3(k_ref[...], kp_ref[...], kn_ref[...], tk_ref[...], first, last)
    v = _shift3(v_ref[...], vp_ref[...], vn_ref[...], tv_ref[...], first, last)
    lo = _shift3(lo_ref[...], lop_ref[...], lon_ref[...], tl_ref[...], first, last)
    ro_ref[...] = r.astype(ro_ref.dtype)
    ko_ref[...] = k.astype(ko_ref.dtype)
    vo_ref[...] = v.astype(vo_ref.dtype)
    for d in range(2):
        xw = lo[:, d * LORA_PAD:(d + 1) * LORA_PAD]
        xa = lo[:, (2 + d) * LORA_PAD:(3 + d) * LORA_PAD]
        wl = w0_ref[d:d + 1, :] + _bdot(jnp.tanh(xw), w2_ref[d])
        w_log = -jax.nn.softplus(-wl) - 0.5
        lw_ref[d] = -jnp.exp(w_log)
        a_ref[d] = jax.nn.sigmoid(a0_ref[d:d + 1, :] + _bdot(xa, a2_ref[d])).astype(a_ref.dtype)
    xg = lo[:, 4 * LORA_PAD:]
    g_ref[...] = _bdot(jax.nn.sigmoid(xg), g2_ref[...]).astype(g_ref.dtype)
    kk = k * kk_ref_p[...]
    ssq = _head_sum_bcast(kk * kk, e_ref, et_ref)
    kko_ref[...] = (kk / jnp.maximum(jnp.sqrt(ssq), L2_EPS)).astype(kko_ref.dtype)


def _rwkv_prep(p_main, p_lora, seq, taps_r, taps_k, taps_v, taps_l, w0, w2p, a0, a2p, g2, k_k, e_mat, et_mat,
               tm=128):
    t = p_main.shape[0]
    nt8 = t // 8
    seq_tiles = seq // tm
    c = D_RWKV

    def main_spec(col):
        return [pl.BlockSpec((tm, c), lambda i, col=col: (i, col)),
                pl.BlockSpec((8, c), lambda i, col=col: (jnp.maximum(i * (tm // 8) - 1, 0), col)),
                pl.BlockSpec((8, c), lambda i, col=col: (jnp.minimum((i + 1) * (tm // 8), nt8 - 1), col))]

    lora_spec = [pl.BlockSpec((tm, N_LORA), lambda i: (i, 0)),
                 pl.BlockSpec((8, N_LORA), lambda i: (jnp.maximum(i * (tm // 8) - 1, 0), 0)),
                 pl.BlockSpec((8, N_LORA), lambda i: (jnp.minimum((i + 1) * (tm // 8), nt8 - 1), 0))]

    def full(shape):
        nd = len(shape)
        return pl.BlockSpec(shape, lambda i, nd=nd: (0,) * nd)

    in_specs = (main_spec(0) + main_spec(1) + main_spec(2) + lora_spec
                + [full((3, c)), full((3, c)), full((3, c)), full((3, N_LORA)),
                   full((2, c)), full((2, LORA_PAD, c)), full((2, c)), full((2, LORA_PAD, c)),
                   full((256, c)), full((1, c)), full((c, 128)), full((128, c))])
    row = pl.BlockSpec((tm, c), lambda i: (i, 0))
    row2 = pl.BlockSpec((2, tm, c), lambda i: (0, i, 0))
    sd = jax.ShapeDtypeStruct
    return pl.pallas_call(
        functools.partial(_prep_kernel, seq_tiles),
        grid=(t // tm,),
        in_specs=in_specs,
        out_specs=[row, row, row, row, row2, row2, row],
        out_shape=[sd((t, c), BF16), sd((t, c), BF16), sd((t, c), BF16), sd((t, c), BF16),
                   sd((2, t, c), F32), sd((2, t, c), BF16), sd((t, c), BF16)],
        compiler_params=_cparams(("parallel",)),
        name="rwkv_prep",
    )(p_main, p_main, p_main, p_main, p_main, p_main, p_main, p_main, p_main,
      p_lora, p_lora, p_lora,
      taps_r, taps_k, taps_v, taps_l, w0, w2p, a0, a2p, g2, k_k, e_mat, et_mat)


def _scan_kernel(r_ref, k_ref, v_ref, kk_ref, lw_ref, a_ref, ka_ref, y_ref, st_ref):
    d = pl.program_id(0)
    c = pl.program_id(3)

    @pl.when(c == 0)
    def _():
        st_ref[...] = jnp.zeros_like(st_ref)

    ti = lax.broadcasted_iota(jnp.int32, (CHUNK, PAIR), 0)
    si = lax.broadcasted_iota(jnp.int32, (CHUNK, PAIR), 1) % HEAD
    ahead = (ti - si) * (1 - 2 * d)
    incl = ahead >= 0
    strict = ahead > 0
    eye = (si == ti).astype(F32)

    lw = lw_ref[...]
    cum_incl = _dot_exact_lhs(incl[:, :CHUNK].astype(BF16), lw)
    tot = jnp.sum(lw, axis=0, keepdims=True)
    e_incl = jnp.exp(cum_incl)
    e_excl = jnp.exp(cum_incl - lw)
    e_ninc = jnp.exp(-cum_incl)
    g_tot = jnp.exp(tot)

    a = a_ref[...].astype(F32)
    kk = kk_ref[...].astype(F32)
    kd = k_ref[...].astype(F32) * (1.0 + (a - 1.0) * ka_ref[...])
    at_all = kk * e_excl
    rt_all = r_ref[...].astype(F32) * e_incl
    bt_all = (kk * a) * e_ninc
    kt_all = kd * e_ninc
    bh_all = bt_all * g_tot
    kh_all = kt_all * g_tot
    v_all = v_ref[...].astype(F32)

    npair = SLAB // PAIR
    lane = lax.broadcasted_iota(jnp.int32, (1, PAIR), 1)
    head0 = lane < HEAD

    def bd(y):
        return jnp.concatenate([jnp.where(head0, y, 0.0), jnp.where(head0, 0.0, y)], axis=0)

    sls = [slice(p * PAIR, (p + 1) * PAIR) for p in range(npair)]
    ars = [jnp.concatenate([at_all[:, sl], rt_all[:, sl]], axis=0).astype(BF16) for sl in sls]
    s_ps = [st_ref[p] for p in range(npair)]
    gs = [_bdot_nt(ars[p], jnp.concatenate([bd(bt_all[:, sls[p]]), bd(kt_all[:, sls[p]]), s_ps[p]], axis=0))
          for p in range(npair)]
    ms = [jnp.where(strict, -gs[p][:CHUNK, :PAIR], 0.0) for p in range(npair)]
    tinvs = [eye + ms[p] for p in range(npair)]
    ms = [_bdot(m, bd(m)) for m in ms]
    for _ in range(4):
        ps = [_bdot(jnp.concatenate([tinvs[p], ms[p]], axis=0), bd(ms[p])) for p in range(npair)]
        tinvs = [tinvs[p] + ps[p][:CHUNK] for p in range(npair)]
        ms = [ps[p][CHUNK:] for p in range(npair)]
    tinvs = [tinvs[p] + _bdot(tinvs[p], bd(ms[p])) for p in range(npair)]
    x2s = [_bdot(jnp.concatenate([jnp.where(strict, gs[p][:CHUNK, PAIR:2 * PAIR], 0.0),
                                  jnp.where(incl, gs[p][CHUNK:, PAIR:2 * PAIR], 0.0)], axis=0),
                 bd(v_all[:, sls[p]])) for p in range(npair)]
    us = [-_bdot(tinvs[p], bd(gs[p][:CHUNK, 2 * PAIR:] + x2s[p][:CHUNK])) for p in range(npair)]
    outs = [gs[p][CHUNK:, 2 * PAIR:] + x2s[p][CHUNK:]
            + _bdot(jnp.where(incl, gs[p][CHUNK:, :PAIR], 0.0), bd(us[p])) for p in range(npair)]
    row = lax.broadcasted_iota(jnp.int32, (PAIR, PAIR), 0)
    same_head = (row < HEAD) == (lax.broadcasted_iota(jnp.int32, (PAIR, PAIR), 1) < HEAD)
    for p in range(npair):
        uv = jnp.concatenate([us[p], v_all[:, sls[p]]], axis=0).astype(BF16)
        bk = jnp.concatenate([bh_all[:, sls[p]], kh_all[:, sls[p]]], axis=0).astype(BF16)
        upd = lax.dot_general(uv, bk, (((0,), (0,)), ((), ())), preferred_element_type=F32)
        st_ref[p] = s_ps[p] * g_tot[:, sls[p]] + jnp.where(same_head, upd, 0.0)
    y_ref[...] = jnp.concatenate(outs, axis=1)


def _rwkv_scan(r, k, v, kk, lw, a, k_a, batch, seq):
    nc = seq // CHUNK
    ns = D_RWKV // SLAB
    r4 = r.reshape(batch, seq, D_RWKV)
    k4 = k.reshape(batch, seq, D_RWKV)
    v4 = v.reshape(batch, seq, D_RWKV)
    kk4 = kk.reshape(batch, seq, D_RWKV)
    lw5 = lw.reshape(2, batch, seq, D_RWKV)
    a5 = a.reshape(2, batch, seq, D_RWKV)

    def tchunk(d, c):
        return c + d * (nc - 1 - 2 * c)

    shared = pl.BlockSpec((None, CHUNK, SLAB), lambda d, b, s, c: (b, tchunk(d, c), s))
    perdir = pl.BlockSpec((None, None, CHUNK, SLAB), lambda d, b, s, c: (d, b, tchunk(d, c), s))
    y = pl.pallas_call(
        _scan_kernel,
        grid=(2, batch, ns, nc),
        in_specs=[shared, shared, shared, shared, perdir, perdir,
                  pl.BlockSpec((1, SLAB), lambda d, b, s, c: (0, s))],
        out_specs=perdir,
        out_shape=jax.ShapeDtypeStruct((2, batch, seq, D_RWKV), F32),
        scratch_shapes=[pltpu.VMEM((SLAB // PAIR, PAIR, PAIR), F32)],
        compiler_params=_cparams(("parallel", "parallel", "parallel", "arbitrary")),
        name="rwkv_scan",
    )(r4, k4, v4, kk4, lw5, a5, k_a)
    return y.reshape(2, batch * seq, D_RWKV)


def _post_kernel(y_ref, r_ref, k_ref, v_ref, a_ref, g_ref, ka_ref, rk_ref, lnw_ref, lnb_ref,
                 e_ref, et_ref, o_ref):
    y = y_ref[0] + y_ref[1]
    mu = _head_sum_bcast(y, e_ref, et_ref) * (1.0 / HEAD)
    yc = y - mu
    var = _head_sum_bcast(yc * yc, e_ref, et_ref) * (1.0 / HEAD)
    yn = yc * lax.rsqrt(var + GN_EPS) * lnw_ref[...] + lnb_ref[...]
    ka = ka_ref[...]
    k = k_ref[...].astype(F32)
    kd_sum = (k * (1.0 + (a_ref[0].astype(F32) - 1.0) * ka)
              + k * (1.0 + (a_ref[1].astype(F32) - 1.0) * ka))
    bonus = (_head_sum_bcast(r_ref[...].astype(F32) * kd_sum * rk_ref[...], e_ref, et_ref)
             * v_ref[...].astype(F32))
    o_ref[...] = ((yn + bonus) * g_ref[...].astype(F32)).astype(o_ref.dtype)


def _rwkv_post(y, r, k, v, a, g, k_a, r_k, ln_w, ln_b, e_mat, et_mat, tm=128):
    t, c = r.shape
    row = pl.BlockSpec((tm, c), lambda i: (i, 0))
    row2 = pl.BlockSpec((2, tm, c), lambda i: (0, i, 0))
    par = pl.BlockSpec((1, c), lambda i: (0, 0))
    return pl.pallas_call(
        _post_kernel,
        grid=(t // tm,),
        in_specs=[row2, row, row, row, row2, row, par, par, par, par,
                  pl.BlockSpec((c, 128), lambda i: (0, 0)), pl.BlockSpec((128, c), lambda i: (0, 0))],
        out_specs=row,
        out_shape=jax.ShapeDtypeStruct((t, c), BF16),
        compiler_params=_cparams(("parallel",)),
        name="rwkv_post",
    )(y, r, k, v, a, g, k_a, r_k, ln_w, ln_b, e_mat, et_mat)


def _s5_tables(lam_re, lam_im, log_step, b_re, b_im, c_re, c_im, d_skip, n_levels):
    L, P, CH, G, GB = S5_L, S5_STATE, S5_CH, S5_GROUPS, S5_GB
    NB = G // GB
    hi = lax.Precision.HIGHEST
    lam = lax.complex(lam_re.astype(F32), lam_im.astype(F32))
    dt = jnp.exp(log_step.astype(F32))[..., None]
    lam_dt = lam * dt
    lam_bar = jnp.exp(lam_dt)
    b = lax.complex(b_re.astype(F32), b_im.astype(F32))
    b_bar = ((lam_bar - 1.0) / lam)[..., None] * b
    c = lax.complex(c_re.astype(F32), c_im.astype(F32))
    taus = jnp.arange(L + 1, dtype=F32)
    pows = jnp.exp(lam_dt[:, :, None, :] * taus[None, None, :, None])
    kern = jnp.real(jnp.einsum('dgop,dgtp,dgpi->dgtoi', c, pows[:, :, :L], b_bar, precision=hi))
    j = jnp.arange(L)[:, None]
    t = jnp.arange(L)[None, :]
    lag = jnp.arange(L)[None, None, :]
    sel_f = ((t - j)[:, :, None] == lag).astype(F32)
    sel_b = ((j - t)[:, :, None] == lag).astype(F32)
    skip = (jnp.eye(L)[None, :, None, :, None] * jnp.eye(CH)[None, None, :, None, :]
            * d_skip.astype(F32).reshape(G, 1, CH, 1, 1))
    tsmall = (jnp.einsum('jtl,gloi->gjito', sel_f, kern[0], precision=hi)
              + jnp.einsum('jtl,gloi->gjito', sel_b, kern[1], precision=hi) + skip)
    def by_block(x, lead):
        x = x.reshape((NB, GB) + x.shape[1:])
        perm = (0,) + tuple(range(2, 2 + lead)) + (1,) + tuple(range(2 + lead, x.ndim))
        return x.transpose(perm)

    a_t = by_block(tsmall.reshape(G, L, CH, L * CH), 1).reshape(NB, L * GB * CH, L * CH)
    desc = jnp.exp(lam_dt[:, :, None, :] * (L - taus[:L])[None, None, :, None])
    desc1 = jnp.exp(lam_dt[0][:, None, :] * (L - 1 - taus[:L])[None, :, None])
    bb_t = b_bar.transpose(0, 1, 3, 2)
    zf = desc1[:, :, None, :] * bb_t[0][:, None]
    zb = pows[1][:, :L][:, :, None, :] * bb_t[1][:, None]
    wzs = jnp.stack([jnp.real(zf), jnp.imag(zf), jnp.real(zb), jnp.imag(zb)], axis=3)
    a_z = by_block(wzs.reshape(G, L, CH, 4 * P), 1).reshape(NB, L * GB * CH, 4 * P)
    c_t = c.transpose(0, 1, 3, 2)
    yf = c_t[0][:, :, None, :] * pows[0][:, 1:].transpose(0, 2, 1)[:, :, :, None]
    yb = c_t[1][:, :, None, :] * desc[1].transpose(0, 2, 1)[:, :, :, None]
    wys = jnp.stack([jnp.real(yf), -jnp.imag(yf), jnp.real(yb), -jnp.imag(yb)], axis=1)
    a_y = by_block(wys.reshape(G, 4, P, L * CH), 1).reshape(NB, 4 * GB * P, L * CH)
    steps = L * (2.0 ** jnp.arange(n_levels, dtype=F32))
    lp = jnp.exp(lam_dt[:, :, None, :] * steps[None, None, :, None])
    m = jnp.stack([jnp.real(lp[0]), jnp.imag(lp[0]), jnp.real(lp[1]), jnp.imag(lp[1])], axis=0)
    mult = m.reshape(4, NB, GB, n_levels, P).transpose(1, 3, 0, 2, 4).reshape(NB, n_levels, 4, GB * P)
    return a_t.astype(BF16), a_z.astype(BF16), a_y.astype(BF16), mult


def _group_of(shape, dim, width):
    return (lax.broadcasted_iota(jnp.int32, shape, dim) // width) % S5_GB


def _s5_kernel(n_levels, x_ref, at_ref, az_ref, ay_ref, rt_ref, rz_ref, mult_ref, y_ref, t_s, wz_s, wy_s):
    nck = x_ref.shape[0] // S5_L
    lanes = x_ref.shape[1]
    w = S5_GB * S5_STATE

    @pl.when(pl.program_id(1) == 0)
    def _():
        t = jnp.dot(at_ref[...], rt_ref[...], preferred_element_type=F32)
        keep = _group_of(t.shape, 0, S5_CH) == _group_of(t.shape, 1, S5_CH)
        t_s[...] = jnp.where(keep, t, 0.0).astype(BF16)
        z = jnp.dot(az_ref[...], rz_ref[...], preferred_element_type=F32)
        keep = _group_of(z.shape, 0, S5_CH) == _group_of(z.shape, 1, S5_STATE)
        wz_s[...] = jnp.where(keep, z, 0.0).astype(BF16)
        yy = jnp.dot(ay_ref[...], rt_ref[...], preferred_element_type=F32)
        keep = _group_of(yy.shape, 0, S5_STATE) == _group_of(yy.shape, 1, S5_CH)
        wy_s[...] = jnp.where(keep, yy, 0.0).astype(BF16)

    xcat = jnp.concatenate([x_ref[pl.ds(tl, nck, stride=S5_L), :].astype(BF16) for tl in range(S5_L)],
                           axis=1)
    z = jnp.dot(xcat, wz_s[...], preferred_element_type=F32)
    fre, fim, bre, bim = z[:, :w], z[:, w:2 * w], z[:, 2 * w:3 * w], z[:, 3 * w:]
    cidx = lax.broadcasted_iota(jnp.int32, (nck, w), 0)
    for i in range(n_levels):
        sh = 1 << i
        lfr, lfi = mult_ref[i, 0:1, :], mult_ref[i, 1:2, :]
        lbr, lbi = mult_ref[i, 2:3, :], mult_ref[i, 3:4, :]
        keep = cidx >= sh
        sr = jnp.where(keep, pltpu.roll(fre, sh, 0), 0.0)
        si = jnp.where(keep, pltpu.roll(fim, sh, 0), 0.0)
        fre, fim = fre + sr * lfr - si * lfi, fim + sr * lfi + si * lfr
        keep = cidx < nck - sh
        sr = jnp.where(keep, pltpu.roll(bre, nck - sh, 0), 0.0)
        si = jnp.where(keep, pltpu.roll(bim, nck - sh, 0), 0.0)
        bre, bim = bre + sr * lbr - si * lbi, bim + sr * lbi + si * lbr
    has_prev = cidx >= 1
    has_next = cidx < nck - 1
    xin = jnp.concatenate([jnp.where(has_prev, pltpu.roll(fre, 1, 0), 0.0),
                           jnp.where(has_prev, pltpu.roll(fim, 1, 0), 0.0),
                           jnp.where(has_next, pltpu.roll(bre, nck - 1, 0), 0.0),
                           jnp.where(has_next, pltpu.roll(bim, nck - 1, 0), 0.0)], axis=1).astype(BF16)
    y = jnp.dot(xcat, t_s[...], preferred_element_type=F32)
    y = y + jnp.dot(xin, wy_s[...], preferred_element_type=F32)
    for tl in range(S5_L):
        y_ref[pl.ds(tl, nck, stride=S5_L), :] = y[:, tl * lanes:(tl + 1) * lanes]


def _s5_mix(p_main, col0, a_t, a_z, a_y, mult, batch, seq, n_levels):
    nb = a_t.shape[0]
    lanes = S5_GB * S5_CH
    kw = S5_L * lanes
    sw = 4 * S5_GB * S5_STATE
    r_t = (jnp.arange(S5_L * S5_CH)[:, None] ==
           (jnp.arange(kw)[None, :] // lanes) * S5_CH + jnp.arange(kw)[None, :] % S5_CH).astype(BF16)
    r_z = (jnp.arange(4 * S5_STATE)[:, None] ==
           (jnp.arange(sw)[None, :] // (S5_GB * S5_STATE)) * S5_STATE + jnp.arange(sw)[None, :] % S5_STATE
           ).astype(BF16)
    return pl.pallas_call(
        functools.partial(_s5_kernel, n_levels),
        grid=(nb, batch),
        in_specs=[pl.BlockSpec((seq, lanes), lambda j, b: (b, col0 + j)),
                  pl.BlockSpec((None, kw, S5_L * S5_CH), lambda j, b: (j, 0, 0)),
                  pl.BlockSpec((None, kw, 4 * S5_STATE), lambda j, b: (j, 0, 0)),
                  pl.BlockSpec((None, sw, S5_L * S5_CH), lambda j, b: (j, 0, 0)),
                  pl.BlockSpec((S5_L * S5_CH, kw), lambda j, b: (0, 0)),
                  pl.BlockSpec((4 * S5_STATE, sw), lambda j, b: (0, 0)),
                  pl.BlockSpec((None, n_levels, 4, S5_GB * S5_STATE), lambda j, b: (j, 0, 0, 0))],
        out_specs=pl.BlockSpec((seq, lanes), lambda j, b: (b, j)),
        out_shape=jax.ShapeDtypeStruct((batch * seq, D_S5), F32),
        scratch_shapes=[pltpu.VMEM((kw, kw), BF16), pltpu.VMEM((kw, sw), BF16), pltpu.VMEM((sw, kw), BF16)],
        compiler_params=_cparams(("parallel", "arbitrary")),
        name="s5_mix",
    )(p_main, a_t, a_z, a_y, r_t, r_z, mult)


def _glu_kernel(y_ref, w_ref, b_ref, o_ref):
    y = y_ref[...]
    z = jnp.dot(jax.nn.gelu(y).astype(BF16), w_ref[...], preferred_element_type=F32) + b_ref[...]
    o_ref[...] = (y * jax.nn.sigmoid(z)).astype(o_ref.dtype)


def _glu(y, w, b, tm=512):
    t, c = y.shape
    tm = min(tm, t)
    return pl.pallas_call(
        _glu_kernel,
        grid=(t // tm,),
        in_specs=[pl.BlockSpec((tm, c), lambda i: (i, 0)), pl.BlockSpec((c, c), lambda i: (0, 0)),
                  pl.BlockSpec((1, c), lambda i: (0, 0))],
        out_specs=pl.BlockSpec((tm, c), lambda i: (i, 0)),
        out_shape=jax.ShapeDtypeStruct((t, c), BF16),
        compiler_params=_cparams(("parallel",)),
        name="s5_glu",
    )(y, w, b)


def _wqk_kernel(wq_ref, k_ref, o_ref):
    acc = lax.dot_general(wq_ref[...], k_ref[...], (((1,), (1,)), ((), ())), preferred_element_type=F32)
    o_ref[...] = (acc * ATTN_SCALE).astype(o_ref.dtype)


def _vwo_kernel(v_ref, wo_ref, o_ref):
    o_ref[...] = jnp.dot(v_ref[...], wo_ref[...], preferred_element_type=F32).astype(o_ref.dtype)


def _attn_kernel(mem, x_ref, wqk_ref, vwo_ref, o_ref, p_ref):
    j = pl.program_id(2)
    tn = o_ref.shape[1]

    @pl.when(j == 0)
    def _():
        x = x_ref[...]
        rstd = lax.rsqrt(jnp.mean(x * x, axis=-1, keepdims=True) + RMS_EPS)
        s = jnp.dot(x.astype(BF16), wqk_ref[...], preferred_element_type=F32) * rstd
        for h in range(ATTN_HEADS):
            sh = s[:, h * mem:(h + 1) * mem]
            sh = sh - jnp.max(sh, axis=-1, keepdims=True)
            p = jnp.exp(sh)
            p_ref[:, h * mem:(h + 1) * mem] = (p / jnp.sum(p, axis=-1, keepdims=True)).astype(BF16)

    res = x_ref[:, pl.ds(pl.multiple_of(j * tn, tn), tn)]
    o_ref[...] = jnp.dot(p_ref[...], vwo_ref[...], preferred_element_type=F32) + res


def _cross_attention(x, kv, w_q, w_o, batch, seq, mem, tm=512, tn=1024):
    d = D_MODEL
    hm = ATTN_HEADS * mem
    tm = min(tm, seq)
    wqk = pl.pallas_call(
        _wqk_kernel,
        grid=(batch, ATTN_HEADS),
        in_specs=[pl.BlockSpec((d, ATTN_HEAD_DIM), lambda b, h: (0, h)),
                  pl.BlockSpec((mem, ATTN_HEAD_DIM), lambda b, h: (b, h))],
        out_specs=pl.BlockSpec((None, d, mem), lambda b, h: (b, 0, h)),
        out_shape=jax.ShapeDtypeStruct((batch, d, hm), BF16),
        compiler_params=_cparams(("parallel", "parallel")),
        name="attn_wqk",
    )(w_q, kv)
    vwo = pl.pallas_call(
        _vwo_kernel,
        grid=(batch, ATTN_HEADS),
        in_specs=[pl.BlockSpec((mem, ATTN_HEAD_DIM), lambda b, h: (b, ATTN_HEADS + h)),
                  pl.BlockSpec((ATTN_HEAD_DIM, d), lambda b, h: (h, 0))],
        out_specs=pl.BlockSpec((None, mem, d), lambda b, h: (b, h, 0)),
        out_shape=jax.ShapeDtypeStruct((batch, hm, d), BF16),
        compiler_params=_cparams(("parallel", "parallel")),
        name="attn_vwo",
    )(kv, w_o)
    out = pl.pallas_call(
        functools.partial(_attn_kernel, mem),
        grid=(batch, seq // tm, d // tn),
        in_specs=[pl.BlockSpec((None, tm, d), lambda b, i, j: (b, i, 0)),
                  pl.BlockSpec((None, d, hm), lambda b, i, j: (b, 0, 0)),
                  pl.BlockSpec((None, hm, tn), lambda b, i, j: (b, 0, j))],
        out_specs=pl.BlockSpec((None, tm, tn), lambda b, i, j: (b, i, j)),
        out_shape=jax.ShapeDtypeStruct((batch, seq, d), F32),
        scratch_shapes=[pltpu.VMEM((tm, hm), BF16)],
        compiler_params=_cparams(("parallel", "parallel", "arbitrary")),
        name="cross_attn",
    )(x.reshape(batch, seq, d), wqk, vwo)
    return out.reshape(batch * seq, d)


def _router_kernel(x_ref, g_ref, whi_ref, wlo_ref, b_ref, xn_ref, info_ref, cnt_ref, carry_ref):
    i = pl.program_id(0)

    @pl.when(i == 0)
    def _():
        carry_ref[...] = jnp.zeros_like(carry_ref)

    x = x_ref[...]
    tm = x.shape[0]
    ms = jnp.mean(x * x, axis=-1, keepdims=True)
    xn = x * lax.rsqrt(ms + RMS_EPS) * g_ref[...]
    half = xn.shape[1] // 2
    xn_ref[...] = _pack_bf16_pair(xn[:, :half], xn[:, half:])
    xh, xl = _split2(xn)
    whi = whi_ref[...]
    logits = (jnp.dot(xh, whi, preferred_element_type=F32) + jnp.dot(xl, whi, preferred_element_type=F32)
              + jnp.dot(xh, wlo_ref[...], preferred_element_type=F32)) + b_ref[...]
    li = lax.broadcasted_iota(jnp.int32, logits.shape, 1)
    neg = jnp.float32(-jnp.inf)
    is_g = li < N_GROUPS
    gl = jnp.where(is_g, logits, neg)
    gm = jnp.max(gl, axis=-1, keepdims=True)
    gi = jnp.min(jnp.where(is_g & (gl == gm), li, 128), axis=-1, keepdims=True)
    gp = 1.0 / jnp.sum(jnp.where(is_g, jnp.exp(gl - gm), 0.0), axis=-1, keepdims=True)
    lo_lane = N_GROUPS + EPG * gi
    sel = (li >= lo_lane) & (li < lo_lane + EPG)
    l1 = jnp.where(sel, logits, neg)
    e1 = jnp.max(l1, axis=-1, keepdims=True)
    i1 = jnp.min(jnp.where(sel & (l1 == e1), li, 128), axis=-1, keepdims=True)
    sel2 = sel & (li != i1)
    l2 = jnp.where(sel2, logits, neg)
    e2 = jnp.max(l2, axis=-1, keepdims=True)
    i2 = jnp.min(jnp.where(sel2 & (l2 == e2), li, 128), axis=-1, keepdims=True)
    ex = jnp.exp(e2 - e1)
    w1 = gp / (1.0 + ex)
    w2 = gp * ex / (1.0 + ex)
    x1 = i1 - N_GROUPS
    x2 = i2 - N_GROUPS
    oh1 = li == x1
    oh2 = li == x2
    oh = (oh1 | oh2).astype(BF16)
    ri = lax.broadcasted_iota(jnp.int32, (tm, tm), 0)
    ci = lax.broadcasted_iota(jnp.int32, (tm, tm), 1)
    before = jnp.dot((ci < ri).astype(BF16), oh, preferred_element_type=F32) + carry_ref[0:1, :]
    r1 = jnp.sum(jnp.where(oh1, before, 0.0), axis=-1, keepdims=True)
    r2 = jnp.sum(jnp.where(oh2, before, 0.0), axis=-1, keepdims=True)
    new_carry = carry_ref[0:1, :] + jnp.sum(oh.astype(F32), axis=0, keepdims=True)
    carry_ref[...] = jnp.broadcast_to(new_carry, carry_ref.shape)
    cnt_ref[...] = jnp.broadcast_to(new_carry, cnt_ref.shape)
    info = jnp.where(li == 0, x1.astype(F32), 0.0)
    info = jnp.where(li == 1, x2.astype(F32), info)
    info = jnp.where(li == 2, w1, info)
    info = jnp.where(li == 3, w2, info)
    info = jnp.where(li == 4, r1, info)
    info = jnp.where(li == 5, r2, info)
    info_ref[...] = info


def _router(x, gain, w_hi, w_lo, bias, tm=256):
    t, d = x.shape
    sd = jax.ShapeDtypeStruct
    return pl.pallas_call(
        _router_kernel,
        grid=(t // tm,),
        in_specs=[pl.BlockSpec((tm, d), lambda i: (i, 0)), pl.BlockSpec((1, d), lambda i: (0, 0)),
                  pl.BlockSpec((d, 128), lambda i: (0, 0)), pl.BlockSpec((d, 128), lambda i: (0, 0)),
                  pl.BlockSpec((1, 128), lambda i: (0, 0))],
        out_specs=[pl.BlockSpec((tm, d // 2), lambda i: (i, 0)), pl.BlockSpec((tm, 128), lambda i: (i, 0)),
                   pl.BlockSpec((8, 128), lambda i: (0, 0))],
        out_shape=[sd((t, d // 2), jnp.uint32), sd((t, 128), F32), sd((8, 128), F32)],
        scratch_shapes=[pltpu.VMEM((8, 128), F32)],
        compiler_params=_cparams(("arbitrary",)),
        name="moe_router",
    )(x, gain.reshape(1, d), w_hi, w_lo, bias)


def _gather_rows(src_hbm, idx_ref, dst_ref, sem, n_rows):
    def body(j, carry):
        pltpu.make_async_copy(src_hbm.at[pl.ds(idx_ref[0, j], 1), :], dst_ref.at[pl.ds(j, 1), :], sem).start()
        return carry
    lax.fori_loop(0, n_rows, body, 0, unroll=8)


def _wait_rows(src_hbm, dst_ref, sem, n_rows):
    pltpu.make_async_copy(src_hbm.at[pl.ds(0, n_rows), :], dst_ref, sem).wait()


def _issue_rows(src_hbm, idx_ref, dst_ref, sem, lo, hi):
    for j in range(lo, hi):
        pltpu.make_async_copy(src_hbm.at[pl.ds(idx_ref[0, j], 1), :], dst_ref.at[pl.ds(j, 1), :], sem).start()


def _expert_up_kernel(be_ref, nused_ref, tok_ref, tokn_ref, x_hbm, wg_ref, wu_ref, h_ref, xbuf, sem):
    i = pl.program_id(0)
    nb = pl.num_programs(0)
    n_used = nused_ref[0]
    slot = i % 2

    @pl.when(i == 0)
    def _():
        _gather_rows(x_hbm, tok_ref, xbuf.at[0], sem.at[0], MOE_BLK)

    @pl.when(i < n_used)
    def _():
        _wait_rows(x_hbm, xbuf.at[slot], sem.at[slot], MOE_BLK)
        half = xbuf.shape[2]
        nk = 2 * half // EXPERT_KC
        per = MOE_BLK // nk
        hg = jnp.zeros((MOE_BLK, D_EXPERT), F32)
        hu = jnp.zeros((MOE_BLK, D_EXPERT), F32)
        for kc in range(nk):
            _issue_rows(x_hbm, tokn_ref, xbuf.at[1 - slot], sem.at[1 - slot], kc * per, (kc + 1) * per)
            ks = slice(kc * EXPERT_KC, (kc + 1) * EXPERT_KC)
            wc = (kc * EXPERT_KC) % half
            xb = _unpack_bf16_pair(xbuf[slot, :, wc:wc + EXPERT_KC])[(kc * EXPERT_KC) // half].astype(BF16)
            hg = hg + jnp.dot(xb, wg_ref[ks, :].astype(BF16), preferred_element_type=F32)
            hu = hu + jnp.dot(xb, wu_ref[ks, :].astype(BF16), preferred_element_type=F32)
        h_ref[...] = (jax.nn.silu(hg) * hu).astype(h_ref.dtype)

        @pl.when(i == nb - 1)
        def _():
            _wait_rows(x_hbm, xbuf.at[1 - slot], sem.at[1 - slot], MOE_BLK)

    @pl.when(i >= n_used)
    def _():
        @pl.when(i == n_used)
        def _():
            _wait_rows(x_hbm, xbuf.at[slot], sem.at[slot], MOE_BLK)

        h_ref[...] = jnp.zeros_like(h_ref)


def _expert_down_kernel(be_ref, nused_ref, h_ref, wd_ref, y_ref):
    i = pl.program_id(0)
    n_used = nused_ref[0]

    @pl.when(i < n_used)
    def _():
        h = h_ref[...]
        half = y_ref.shape[1]
        for nc in range(half // EXPERT_KC):
            ns = slice(nc * EXPERT_KC, (nc + 1) * EXPERT_KC)
            nh = slice(half + nc * EXPERT_KC, half + (nc + 1) * EXPERT_KC)
            lo = jnp.dot(h, wd_ref[:, ns].astype(BF16), preferred_element_type=F32)
            hi = jnp.dot(h, wd_ref[:, nh].astype(BF16), preferred_element_type=F32)
            y_ref[:, ns] = _pack_bf16_pair(lo, hi)

    @pl.when(i >= n_used)
    def _():
        y_ref[...] = jnp.zeros_like(y_ref)


def _experts(block_expert, n_used, tok3, xn, wg, wu, wd):
    nb = tok3.shape[0]
    d = 2 * xn.shape[1]
    tok_spec = pl.BlockSpec((None, 1, MOE_BLK), lambda i, be, nu: (i, 0, 0), memory_space=pltpu.SMEM)
    tokn_spec = pl.BlockSpec((None, 1, MOE_BLK), lambda i, be, nu: (jnp.minimum(i + 1, nb - 1), 0, 0),
                             memory_space=pltpu.SMEM)
    h_buf = pl.pallas_call(
        _expert_up_kernel,
        grid_spec=pltpu.PrefetchScalarGridSpec(
            num_scalar_prefetch=2,
            grid=(nb,),
            in_specs=[tok_spec, tokn_spec, pl.BlockSpec(memory_space=pl.ANY),
                      pl.BlockSpec((None, d, D_EXPERT), lambda i, be, nu: (be[i], 0, 0)),
                      pl.BlockSpec((None, d, D_EXPERT), lambda i, be, nu: (be[i], 0, 0))],
            out_specs=pl.BlockSpec((MOE_BLK, D_EXPERT), lambda i, be, nu: (i, 0)),
            scratch_shapes=[pltpu.VMEM((2, MOE_BLK, d // 2), jnp.uint32), pltpu.SemaphoreType.DMA((2,))],
        ),
        out_shape=jax.ShapeDtypeStruct((nb * MOE_BLK, D_EXPERT), BF16),
        compiler_params=_cparams(("arbitrary",)),
        name="moe_up",
    )(block_expert, n_used, tok3, tok3, xn, wg, wu)
    return pl.pallas_call(
        _expert_down_kernel,
        grid_spec=pltpu.PrefetchScalarGridSpec(
            num_scalar_prefetch=2,
            grid=(nb,),
            in_specs=[pl.BlockSpec((MOE_BLK, D_EXPERT), lambda i, be, nu: (i, 0)),
                      pl.BlockSpec((None, D_EXPERT, d), lambda i, be, nu: (be[i], 0, 0))],
            out_specs=pl.BlockSpec((MOE_BLK, d // 2), lambda i, be, nu: (i, 0)),
        ),
        out_shape=jax.ShapeDtypeStruct((nb * MOE_BLK, d // 2), jnp.uint32),
        compiler_params=_cparams(("arbitrary",)),
        name="moe_down",
    )(block_expert, n_used, h_buf, wd)


def _combine_kernel(d1_ref, d1n_ref, d2_ref, d2n_ref, x_ref, info_ref, g_ref, y_hbm, o_ref, ybuf, sem):
    i = pl.program_id(0)
    n = pl.num_programs(0)
    slot = i % 2

    @pl.when(i == 0)
    def _():
        _gather_rows(y_hbm, d1_ref, ybuf.at[0, 0], sem.at[0], COMB_TM)
        _gather_rows(y_hbm, d2_ref, ybuf.at[0, 1], sem.at[0], COMB_TM)

    _wait_rows(y_hbm, ybuf.at[slot, 0], sem.at[slot], COMB_TM)
    _wait_rows(y_hbm, ybuf.at[slot, 1], sem.at[slot], COMB_TM)
    _issue_rows(y_hbm, d1n_ref, ybuf.at[1 - slot, 0], sem.at[1 - slot], 0, COMB_TM)
    _issue_rows(y_hbm, d2n_ref, ybuf.at[1 - slot, 1], sem.at[1 - slot], 0, COMB_TM)
    info = info_ref[...]
    w1 = info[:, 2:3]
    w2 = info[:, 3:4]
    half = ybuf.shape[3]
    lo1, hi1 = _unpack_bf16_pair(ybuf[slot, 0])
    lo2, hi2 = _unpack_bf16_pair(ybuf[slot, 1])
    xlo = x_ref[:, :half] + lo1 * w1 + lo2 * w2
    xhi = x_ref[:, half:] + hi1 * w1 + hi2 * w2
    ms = (jnp.sum(xlo * xlo, axis=-1, keepdims=True) + jnp.sum(xhi * xhi, axis=-1, keepdims=True)) / (2 * half)
    rstd = lax.rsqrt(ms + RMS_EPS)
    o_ref[:, :half] = xlo * rstd * g_ref[:, :half]
    o_ref[:, half:] = xhi * rstd * g_ref[:, half:]

    @pl.when(i == n - 1)
    def _():
        _wait_rows(y_hbm, ybuf.at[1 - slot, 0], sem.at[1 - slot], COMB_TM)
        _wait_rows(y_hbm, ybuf.at[1 - slot, 1], sem.at[1 - slot], COMB_TM)


def _combine(dest1, dest2, x, info, gain, y_buf):
    t, d = x.shape
    nt = t // COMB_TM
    d1 = dest1.reshape(nt, 1, COMB_TM)
    d2 = dest2.reshape(nt, 1, COMB_TM)
    cur = pl.BlockSpec((None, 1, COMB_TM), lambda i: (i, 0, 0), memory_space=pltpu.SMEM)
    nxt = pl.BlockSpec((None, 1, COMB_TM), lambda i: (jnp.minimum(i + 1, nt - 1), 0, 0),
                       memory_space=pltpu.SMEM)
    return pl.pallas_call(
        _combine_kernel,
        grid=(nt,),
        in_specs=[cur, nxt, cur, nxt,
                  pl.BlockSpec((COMB_TM, d), lambda i: (i, 0)),
                  pl.BlockSpec((COMB_TM, 128), lambda i: (i, 0)),
                  pl.BlockSpec((1, d), lambda i: (0, 0)),
                  pl.BlockSpec(memory_space=pl.ANY)],
        out_specs=pl.BlockSpec((COMB_TM, d), lambda i: (i, 0)),
        out_shape=jax.ShapeDtypeStruct((t, d), F32),
        scratch_shapes=[pltpu.VMEM((2, 2, COMB_TM, d // 2), jnp.uint32), pltpu.SemaphoreType.DMA((2,))],
        compiler_params=_cparams(("arbitrary",)),
        name="moe_combine",
    )(d1, d1, d2, d2, x, info, gain.reshape(1, d), y_buf)


def _head_matrices():
    lane = jnp.arange(D_RWKV) // HEAD
    e = (lane[:, None] == jnp.arange(128)[None, :]).astype(BF16)
    return e, e.T


def _pad_lora_rows(w):
    return jnp.pad(w, ((0, 0), (0, LORA_PAD - w.shape[1]), (0, 0)))


def _layer(x, mem, norm_mix, w_in, shift_taps, rwkv_w0, rwkv_w2, rwkv_a0, rwkv_a2, rwkv_g2, rwkv_k_k,
           rwkv_k_a, rwkv_r_k, rwkv_ln_w, rwkv_ln_b, s5_lam_re, s5_lam_im, s5_log_step, s5_b_re, s5_b_im,
           s5_c_re, s5_c_im, s5_d, s5_glu_w, s5_glu_b, w_out, norm_attn, norm_mem, w_q, w_k, w_v, w_o,
           norm_ffn, router_grp_w, router_grp_b, router_exp_w, router_exp_b, exp_w_gate, exp_w_up,
           exp_w_down, out_gain):
    batch, seq, d = x.shape
    t = batch * seq
    mem_n = mem.shape[1]
    xt = x.reshape(t, d)
    c3 = 3 * D_RWKV
    off_g = c3 + 4 * 96

    def pad_cols(w, lo, width):
        return jnp.pad(w[:, lo:lo + width], ((0, 0), (0, LORA_PAD - width)))

    w_main = jnp.concatenate([w_in[:, :c3], w_in[:, off_g + 256:]], axis=1).astype(BF16)
    w_lora = jnp.concatenate([pad_cols(w_in, c3 + i * 96, 96) for i in range(4)]
                             + [w_in[:, off_g:off_g + 256]], axis=1).astype(BF16)
    taps_l = jnp.concatenate([pad_cols(shift_taps, c3 + i * 96, 96) for i in range(4)]
                             + [shift_taps[:, off_g:off_g + 256]], axis=1)
    h = _rmsnorm(xt, norm_mix, BF16)
    p_main = _matmul(h, w_main, F32, 1024, 512, name="w_in_main")
    p_lora = _matmul(h, w_lora, F32, 1024, N_LORA, name="w_in_lora")

    e_mat, et_mat = _head_matrices()
    k_a = rwkv_k_a.reshape(1, D_RWKV)
    r, k, v, kk, lw, a, g = _rwkv_prep(
        p_main, p_lora, seq, shift_taps[:, :D_RWKV], shift_taps[:, D_RWKV:2 * D_RWKV],
        shift_taps[:, 2 * D_RWKV:c3], taps_l, rwkv_w0, _pad_lora_rows(rwkv_w2).astype(BF16), rwkv_a0,
        _pad_lora_rows(rwkv_a2).astype(BF16), rwkv_g2.astype(BF16), rwkv_k_k.reshape(1, D_RWKV),
        e_mat, et_mat)
    y_scan = _rwkv_scan(r, k, v, kk, lw, a, k_a, batch, seq)
    y_rwkv = _rwkv_post(y_scan, r, k, v, a, g, k_a, rwkv_r_k.reshape(1, D_RWKV),
                        rwkv_ln_w.reshape(1, D_RWKV), rwkv_ln_b.reshape(1, D_RWKV), e_mat, et_mat)

    n_levels = int(math.log2(seq // S5_L))
    a_t, a_z, a_y, mult = _s5_tables(s5_lam_re, s5_lam_im, s5_log_step, s5_b_re, s5_b_im,
                                    s5_c_re, s5_c_im, s5_d, n_levels)
    y_s5 = _s5_mix(p_main, c3 // (S5_GB * S5_CH), a_t, a_z, a_y, mult, batch, seq, n_levels)
    y_glu = _glu(y_s5, s5_glu_w.astype(BF16), s5_glu_b.reshape(1, D_S5))

    x1 = _matmul2_res(y_rwkv, y_glu, w_out.astype(BF16), xt, 1024, 512)

    memn = _rmsnorm(mem.reshape(batch * mem_n, d), norm_mem, BF16)
    wkv = jnp.concatenate([w_k, w_v], axis=1).astype(BF16)
    kv = _matmul(memn, wkv, BF16, 1024, 512, name="w_kv")
    x2 = _cross_attention(x1, kv, (norm_attn[:, None] * w_q).astype(BF16), w_o.astype(BF16), batch, seq, mem_n)

    w_r = jnp.concatenate([router_grp_w, router_exp_w,
                           jnp.zeros((d, 128 - N_GROUPS - N_EXPERTS), F32)], axis=1)
    w_r_hi = w_r.astype(BF16)
    w_r_lo = (w_r - w_r_hi.astype(F32)).astype(BF16)
    b_r = jnp.concatenate([router_grp_b, router_exp_b,
                           jnp.zeros((128 - N_GROUPS - N_EXPERTS,), F32)]).reshape(1, 128)
    xn3, info, cnt = _router(x2, norm_ffn, w_r_hi, w_r_lo, b_r)
    eid = info[:, 0:2].astype(jnp.int32)
    rank = info[:, 4:6].astype(jnp.int32)
    counts = cnt[0, :N_EXPERTS].astype(jnp.int32)
    nblk = (counts + MOE_BLK - 1) // MOE_BLK
    bstart = jnp.cumsum(nblk) - nblk
    n_used = jnp.sum(nblk)
    first = jnp.sum(jnp.where(eid[:, :, None] == jnp.arange(N_EXPERTS)[None, None, :], bstart, 0), axis=-1)
    dest = first * MOE_BLK + rank
    nb = (t * TOP_K) // MOE_BLK + N_EXPERTS
    tok_buf = jnp.zeros((nb * MOE_BLK,), jnp.int32).at[dest.reshape(-1)].set(
        jnp.repeat(jnp.arange(t, dtype=jnp.int32), TOP_K))
    blk = jnp.arange(nb, dtype=jnp.int32)
    block_expert = jnp.sum(blk[:, None] >= (bstart + nblk)[None, :], axis=1).astype(jnp.int32)
    last_e = jnp.max(jnp.where(nblk > 0, jnp.arange(N_EXPERTS), 0)).astype(jnp.int32)
    block_expert = jnp.minimum(block_expert, last_e)
    y_buf = _experts(block_expert, n_used.reshape(1).astype(jnp.int32), tok_buf.reshape(nb, 1, MOE_BLK),
                     xn3, exp_w_gate, exp_w_up, exp_w_down)
    out = _combine(dest[:, 0], dest[:, 1], x2, info, out_gain, y_buf)
    return out.reshape(batch, seq, d)


def kernel(x, mem, norm_mix, w_in, shift_taps, rwkv_w0, rwkv_w2, rwkv_a0, rwkv_a2, rwkv_g2, rwkv_k_k, rwkv_k_a, rwkv_r_k, rwkv_ln_w, rwkv_ln_b, s5_lam_re, s5_lam_im, s5_log_step, s5_b_re, s5_b_im, s5_c_re, s5_c_im, s5_d, s5_glu_w, s5_glu_b, w_out, norm_attn, norm_mem, w_q, w_k, w_v, w_o, norm_ffn, router_grp_w, router_grp_b, router_exp_w, router_exp_b, exp_w_gate, exp_w_up, exp_w_down, norm_final):
    return _layer(x, mem, norm_mix[0], w_in[0], shift_taps[0], rwkv_w0[0], rwkv_w2[0], rwkv_a0[0],
                  rwkv_a2[0], rwkv_g2[0], rwkv_k_k[0], rwkv_k_a[0], rwkv_r_k[0], rwkv_ln_w[0],
                  rwkv_ln_b[0], s5_lam_re[0], s5_lam_im[0], s5_log_step[0], s5_b_re[0], s5_b_im[0],
                  s5_c_re[0], s5_c_im[0], s5_d[0], s5_glu_w[0], s5_glu_b[0], w_out[0], norm_attn[0],
                  norm_mem[0], w_q[0], w_k[0], w_v[0], w_o[0], norm_ffn[0], router_grp_w[0],
                  router_grp_b[0], router_exp_w[0], router_exp_b[0], exp_w_gate[0], exp_w_up[0],
                  exp_w_down[0], norm_final)
```

```python
import functools
import math

import jax
import jax.numpy as jnp
from jax import lax
from jax.experimental import pallas as pl
from jax.experimental.pallas import tpu as pltpu

F32 = jnp.float32
BF16 = jnp.bfloat16

D_MODEL = 4096
D_RWKV = 2048
D_S5 = 2048
HEAD = 64
PAIR = 2 * HEAD
N_HEADS = D_RWKV // HEAD
LORA_PAD = 128
N_LORA = 4 * LORA_PAD + 256
LORA_BLK = 1024
N_MAIN = 3 * D_RWKV + D_S5
S5_CH = 16
S5_GROUPS = D_S5 // S5_CH
S5_STATE = 64
S5_L = 8
S5_GB = 8
ATTN_HEADS = 4
ATTN_HEAD_DIM = D_MODEL // ATTN_HEADS
ATTN_SCALE = ATTN_HEAD_DIM ** -0.5
N_GROUPS = 8
EPG = 8
N_EXPERTS = N_GROUPS * EPG
TOP_K = 2
D_EXPERT = D_MODEL // 8
RMS_EPS = 1e-6
GN_EPS = 64e-5
L2_EPS = 1e-12

CHUNK = 64
SLAB = 2048
MOE_BLK = 256
COMB_TM = 128
EXPERT_KC = 1024
VMEM_LIMIT = 56 * 1024 * 1024


def _cparams(sem):
    return pltpu.CompilerParams(dimension_semantics=sem, vmem_limit_bytes=VMEM_LIMIT)


def _bdot(a, b):
    return jnp.dot(a.astype(BF16), b.astype(BF16), preferred_element_type=F32)


def _bdot_nt(a, b):
    return lax.dot_general(a.astype(BF16), b.astype(BF16), (((1,), (1,)), ((), ())),
                           preferred_element_type=F32)


def _split2(x):
    hi = x.astype(BF16)
    lo = (x - hi.astype(F32)).astype(BF16)
    return hi, lo


def _split3(x):
    hi = x.astype(BF16)
    r1 = x - hi.astype(F32)
    mid = r1.astype(BF16)
    lo = (r1 - mid.astype(F32)).astype(BF16)
    return hi, mid, lo


def _pack_bf16_pair(lo, hi):
    lo_b = lax.bitcast_convert_type(lo.astype(BF16).astype(F32), jnp.uint32) >> 16
    hi_b = lax.bitcast_convert_type(hi.astype(BF16).astype(F32), jnp.uint32) & jnp.uint32(0xFFFF0000)
    return hi_b | lo_b


def _unpack_bf16_pair(w):
    return (lax.bitcast_convert_type(w << 16, F32),
            lax.bitcast_convert_type(w & jnp.uint32(0xFFFF0000), F32))


def _dot_split2_rhs(x, e):
    h, l = _split2(x)
    return jnp.dot(h, e, preferred_element_type=F32) + jnp.dot(l, e, preferred_element_type=F32)


def _dot_exact_lhs(e, x):
    h, m, l = _split3(x)
    return (jnp.dot(e, h, preferred_element_type=F32) + jnp.dot(e, m, preferred_element_type=F32)
            + jnp.dot(e, l, preferred_element_type=F32))


def _rms_kernel(x_ref, g_ref, o_ref):
    x = x_ref[...]
    ms = jnp.mean(x * x, axis=-1, keepdims=True)
    o_ref[...] = (x * lax.rsqrt(ms + RMS_EPS) * g_ref[...]).astype(o_ref.dtype)


def _rmsnorm(x, gain, out_dtype, tm=256):
    t, d = x.shape
    return pl.pallas_call(
        _rms_kernel,
        grid=(t // tm,),
        in_specs=[pl.BlockSpec((tm, d), lambda i: (i, 0)), pl.BlockSpec((1, d), lambda i: (0, 0))],
        out_specs=pl.BlockSpec((tm, d), lambda i: (i, 0)),
        out_shape=jax.ShapeDtypeStruct((t, d), out_dtype),
        compiler_params=_cparams(("parallel",)),
        name="rmsnorm",
    )(x, gain.reshape(1, d))


def _mm_kernel(a_ref, b_ref, o_ref):
    o_ref[...] = jnp.dot(a_ref[...], b_ref[...], preferred_element_type=F32).astype(o_ref.dtype)


def _mm_res_kernel(a_ref, b_ref, r_ref, o_ref):
    o_ref[...] = (jnp.dot(a_ref[...], b_ref[...], preferred_element_type=F32)
                  + r_ref[...]).astype(o_ref.dtype)


def _mm2_res_kernel(a1_ref, a2_ref, b_ref, r_ref, o_ref):
    k1 = a1_ref.shape[1]
    acc = jnp.dot(a1_ref[...], b_ref[:k1, :], preferred_element_type=F32)
    acc = acc + jnp.dot(a2_ref[...], b_ref[k1:, :], preferred_element_type=F32)
    o_ref[...] = (acc + r_ref[...]).astype(o_ref.dtype)


def _matmul(a, b, out_dtype, tm, tn, res=None, name="matmul"):
    m, k = a.shape
    n = b.shape[1]
    tm, tn = min(tm, m), min(tn, n)
    in_specs = [pl.BlockSpec((tm, k), lambda i, j: (i, 0)), pl.BlockSpec((k, tn), lambda i, j: (0, j))]
    args = [a, b]
    kern = _mm_kernel
    if res is not None:
        in_specs.append(pl.BlockSpec((tm, tn), lambda i, j: (i, j)))
        args.append(res)
        kern = _mm_res_kernel
    return pl.pallas_call(
        kern,
        grid=(m // tm, n // tn),
        in_specs=in_specs,
        out_specs=pl.BlockSpec((tm, tn), lambda i, j: (i, j)),
        out_shape=jax.ShapeDtypeStruct((m, n), out_dtype),
        compiler_params=_cparams(("parallel", "parallel")),
        name=name,
    )(*args)


def _matmul2_res(a1, a2, b, res, tm, tn):
    m, k1 = a1.shape
    k2 = a2.shape[1]
    n = b.shape[1]
    tm = min(tm, m)
    return pl.pallas_call(
        _mm2_res_kernel,
        grid=(m // tm, n // tn),
        in_specs=[pl.BlockSpec((tm, k1), lambda i, j: (i, 0)),
                  pl.BlockSpec((tm, k2), lambda i, j: (i, 0)),
                  pl.BlockSpec((k1 + k2, tn), lambda i, j: (0, j)),
                  pl.BlockSpec((tm, tn), lambda i, j: (i, j))],
        out_specs=pl.BlockSpec((tm, tn), lambda i, j: (i, j)),
        out_shape=jax.ShapeDtypeStruct((m, n), F32),
        compiler_params=_cparams(("parallel", "parallel")),
        name="w_out",
    )(a1, a2, b, res)


def _head_sum_bcast(x, e_ref, et_ref):
    s = _dot_split2_rhs(x, e_ref[...])
    return _dot_split2_rhs(s, et_ref[...])


def _shift3(x, hp, hn, taps, first, last):
    tm = x.shape[0]
    row = lax.broadcasted_iota(jnp.int32, x.shape, 0)
    prev_edge = jnp.where(first, 0.0, hp[7:8, :])
    next_edge = jnp.where(last, 0.0, hn[0:1, :])
    prev = jnp.where(row == 0, prev_edge, pltpu.roll(x, 1, 0))
    nxt = jnp.where(row == tm - 1, next_edge, pltpu.roll(x, tm - 1, 0))
    return taps[0:1, :] * prev + taps[1:2, :] * x + taps[2:3, :] * nxt


def _prep_kernel(seq_tiles,
                 r_ref, rp_ref, rn_ref, k_ref, kp_ref, kn_ref, v_ref, vp_ref, vn_ref,
                 lo_ref, lop_ref, lon_ref,
                 tr_ref, tk_ref, tv_ref, tl_ref,
                 w0_ref, w2_ref, a0_ref, a2_ref, g2_ref, kk_ref_p, e_ref, et_ref,
                 ro_ref, ko_ref, vo_ref, kko_ref, lw_ref, a_ref, g_ref):
    i = pl.program_id(0)
    first = (i % seq_tiles) == 0
    last = (i % seq_tiles) == seq_tiles - 1
    r = _shift3(r_ref[...], rp_ref[...], rn_ref[...], tr_ref[...], first, last)
    k = _shift3(k_ref[...], kp_ref[...], kn_ref[...], tk_ref[...], first, last)
    v = _shift3(v_ref[...], vp_ref[...], vn_ref[...], tv_ref[...], first, last)
    lo = _shift3(lo_ref[...], lop_ref[...], lon_ref[...], tl_ref[...], first, last)
    ro_ref[...] = r.astype(ro_ref.dtype)
    ko_ref[...] = k.astype(ko_ref.dtype)
    vo_ref[...] = v.astype(vo_ref.dtype)
    for d in range(2):
        xw = lo[:, d * LORA_PAD:(d + 1) * LORA_PAD]
        xa = lo[:, (2 + d) * LORA_PAD:(3 + d) * LORA_PAD]
        wl = w0_ref[d:d + 1, :] + _bdot(jnp.tanh(xw), w2_ref[d])
        lw_ref[d] = -math.exp(-0.5) * jax.nn.sigmoid(wl)
        a_ref[d] = jax.nn.sigmoid(a0_ref[d:d + 1, :] + _bdot(xa, a2_ref[d])).astype(a_ref.dtype)
    xg = lo[:, 4 * LORA_PAD:N_LORA]
    g_ref[...] = _bdot(jax.nn.sigmoid(xg), g2_ref[...]).astype(g_ref.dtype)
    kk = k * kk_ref_p[...]
    ssq = _head_sum_bcast(kk * kk, e_ref, et_ref)
    kko_ref[...] = (kk / jnp.maximum(jnp.sqrt(ssq), L2_EPS)).astype(kko_ref.dtype)


def _rwkv_prep(p_all, seq, taps_r, taps_k, taps_v, taps_l, w0, w2p, a0, a2p, g2, k_k, e_mat, et_mat,
               tm=128):
    t = p_all.shape[0]
    nt8 = t // 8
    seq_tiles = seq // tm
    c = D_RWKV

    def main_spec(col):
        return [pl.BlockSpec((tm, c), lambda i, col=col: (i, col)),
                pl.BlockSpec((8, c), lambda i, col=col: (jnp.maximum(i * (tm // 8) - 1, 0), col)),
                pl.BlockSpec((8, c), lambda i, col=col: (jnp.minimum((i + 1) * (tm // 8), nt8 - 1), col))]

    lcol = N_MAIN // LORA_BLK
    lora_spec = [pl.BlockSpec((tm, LORA_BLK), lambda i: (i, lcol)),
                 pl.BlockSpec((8, LORA_BLK), lambda i: (jnp.maximum(i * (tm // 8) - 1, 0), lcol)),
                 pl.BlockSpec((8, LORA_BLK), lambda i: (jnp.minimum((i + 1) * (tm // 8), nt8 - 1), lcol))]

    def full(shape):
        nd = len(shape)
        return pl.BlockSpec(shape, lambda i, nd=nd: (0,) * nd)

    in_specs = (main_spec(0) + main_spec(1) + main_spec(2) + lora_spec
                + [full((3, c)), full((3, c)), full((3, c)), full((3, LORA_BLK)),
                   full((2, c)), full((2, LORA_PAD, c)), full((2, c)), full((2, LORA_PAD, c)),
                   full((256, c)), full((1, c)), full((c, 128)), full((128, c))])
    row = pl.BlockSpec((tm, c), lambda i: (i, 0))
    row2 = pl.BlockSpec((2, tm, c), lambda i: (0, i, 0))
    sd = jax.ShapeDtypeStruct
    return pl.pallas_call(
        functools.partial(_prep_kernel, seq_tiles),
        grid=(t // tm,),
        in_specs=in_specs,
        out_specs=[row, row, row, row, row2, row2, row],
        out_shape=[sd((t, c), BF16), sd((t, c), BF16), sd((t, c), BF16), sd((t, c), BF16),
                   sd((2, t, c), F32), sd((2, t, c), BF16), sd((t, c), BF16)],
        compiler_params=_cparams(("parallel",)),
        name="rwkv_prep",
    )(p_all, p_all, p_all, p_all, p_all, p_all, p_all, p_all, p_all,
      p_all, p_all, p_all,
      taps_r, taps_k, taps_v, taps_l, w0, w2p, a0, a2p, g2, k_k, e_mat, et_mat)


def _scan_kernel(r_ref, k_ref, v_ref, kk_ref, lw_ref, a_ref, ka_ref, y_ref, st_ref):
    d = pl.program_id(0)
    c = pl.program_id(3)

    @pl.when(c == 0)
    def _():
        st_ref[...] = jnp.zeros_like(st_ref)

    ti = lax.broadcasted_iota(jnp.int32, (CHUNK, PAIR), 0)
    si = lax.broadcasted_iota(jnp.int32, (CHUNK, PAIR), 1) % HEAD
    ahead = (ti - si) * (1 - 2 * d)
    incl = ahead >= 0
    strict = ahead > 0
    eye = (si == ti).astype(F32)

    lw = lw_ref[...]
    cum_incl = _dot_exact_lhs(incl[:, :CHUNK].astype(BF16), lw)
    tot = jnp.sum(lw, axis=0, keepdims=True)
    e_incl = jnp.exp(cum_incl)
    e_excl = jnp.exp(cum_incl - lw)
    e_ninc = jnp.exp(-cum_incl)
    g_tot = jnp.exp(tot)

    a = a_ref[...].astype(F32)
    kk = kk_ref[...].astype(F32)
    kd = k_ref[...].astype(F32) * (1.0 + (a - 1.0) * ka_ref[...])
    at_all = kk * e_excl
    rt_all = r_ref[...].astype(F32) * e_incl
    bt_all = (kk * a) * e_ninc
    kt_all = kd * e_ninc
    bh_all = bt_all * g_tot
    kh_all = kt_all * g_tot
    v_all = v_ref[...].astype(F32)

    npair = SLAB // PAIR
    lane = lax.broadcasted_iota(jnp.int32, (1, PAIR), 1)
    head0 = lane < HEAD

    def bd(y):
        return jnp.concatenate([jnp.where(head0, y, 0.0), jnp.where(head0, 0.0, y)], axis=0)

    sls = [slice(p * PAIR, (p + 1) * PAIR) for p in range(npair)]
    ars = [jnp.concatenate([at_all[:, sl], rt_all[:, sl]], axis=0).astype(BF16) for sl in sls]
    s_ps = [st_ref[p] for p in range(npair)]
    gs = [_bdot_nt(ars[p], jnp.concatenate([bd(bt_all[:, sls[p]]), bd(kt_all[:, sls[p]]), s_ps[p]], axis=0))
          for p in range(npair)]
    ms = [jnp.where(strict, -gs[p][:CHUNK, :PAIR], 0.0) for p in range(npair)]
    tinvs = [eye + ms[p] for p in range(npair)]
    ms = [_bdot(m, bd(m)) for m in ms]
    for _ in range(4):
        ps = [_bdot(jnp.concatenate([tinvs[p], ms[p]], axis=0), bd(ms[p])) for p in range(npair)]
        tinvs = [tinvs[p] + ps[p][:CHUNK] for p in range(npair)]
        ms = [ps[p][CHUNK:] for p in range(npair)]
    tinvs = [tinvs[p] + _bdot(tinvs[p], bd(ms[p])) for p in range(npair)]
    x2s = [_bdot(jnp.concatenate([jnp.where(strict, gs[p][:CHUNK, PAIR:2 * PAIR], 0.0),
                                  jnp.where(incl, gs[p][CHUNK:, PAIR:2 * PAIR], 0.0)], axis=0),
                 bd(v_all[:, sls[p]])) for p in range(npair)]
    us = [-_bdot(tinvs[p], bd(gs[p][:CHUNK, 2 * PAIR:] + x2s[p][:CHUNK])) for p in range(npair)]
    outs = [gs[p][CHUNK:, 2 * PAIR:] + x2s[p][CHUNK:]
            + _bdot(jnp.where(incl, gs[p][CHUNK:, :PAIR], 0.0), bd(us[p])) for p in range(npair)]
    row = lax.broadcasted_iota(jnp.int32, (PAIR, PAIR), 0)
    same_head = (row < HEAD) == (lax.broadcasted_iota(jnp.int32, (PAIR, PAIR), 1) < HEAD)
    for p in range(npair):
        uv = jnp.concatenate([us[p], v_all[:, sls[p]]], axis=0).astype(BF16)
        bk = jnp.concatenate([bh_all[:, sls[p]], kh_all[:, sls[p]]], axis=0).astype(BF16)
        upd = lax.dot_general(uv, bk, (((0,), (0,)), ((), ())), preferred_element_type=F32)
        st_ref[p] = s_ps[p] * g_tot[:, sls[p]] + jnp.where(same_head, upd, 0.0)
    y_ref[...] = jnp.concatenate(outs, axis=1)


def _rwkv_scan(r, k, v, kk, lw, a, k_a, batch, seq):
    nc = seq // CHUNK
    ns = D_RWKV // SLAB
    r4 = r.reshape(batch, seq, D_RWKV)
    k4 = k.reshape(batch, seq, D_RWKV)
    v4 = v.reshape(batch, seq, D_RWKV)
    kk4 = kk.reshape(batch, seq, D_RWKV)
    lw5 = lw.reshape(2, batch, seq, D_RWKV)
    a5 = a.reshape(2, batch, seq, D_RWKV)

    def tchunk(d, c):
        return c + d * (nc - 1 - 2 * c)

    shared = pl.BlockSpec((None, CHUNK, SLAB), lambda d, b, s, c: (b, tchunk(d, c), s))
    perdir = pl.BlockSpec((None, None, CHUNK, SLAB), lambda d, b, s, c: (d, b, tchunk(d, c), s))
    y = pl.pallas_call(
        _scan_kernel,
        grid=(2, batch, ns, nc),
        in_specs=[shared, shared, shared, shared, perdir, perdir,
                  pl.BlockSpec((1, SLAB), lambda d, b, s, c: (0, s))],
        out_specs=perdir,
        out_shape=jax.ShapeDtypeStruct((2, batch, seq, D_RWKV), F32),
        scratch_shapes=[pltpu.VMEM((SLAB // PAIR, PAIR, PAIR), F32)],
        compiler_params=_cparams(("parallel", "parallel", "parallel", "arbitrary")),
        name="rwkv_scan",
    )(r4, k4, v4, kk4, lw5, a5, k_a)
    return y.reshape(2, batch * seq, D_RWKV)


def _post_kernel(y_ref, r_ref, k_ref, v_ref, a_ref, g_ref, ka_ref, rk_ref, lnw_ref, lnb_ref,
                 e_ref, et_ref, o_ref):
    y = y_ref[0] + y_ref[1]
    mu = _head_sum_bcast(y, e_ref, et_ref) * (1.0 / HEAD)
    yc = y - mu
    var = _head_sum_bcast(yc * yc, e_ref, et_ref) * (1.0 / HEAD)
    yn = yc * lax.rsqrt(var + GN_EPS) * lnw_ref[...] + lnb_ref[...]
    ka = ka_ref[...]
    k = k_ref[...].astype(F32)
    kd_sum = (k * (1.0 + (a_ref[0].astype(F32) - 1.0) * ka)
              + k * (1.0 + (a_ref[1].astype(F32) - 1.0) * ka))
    bonus = (_head_sum_bcast(r_ref[...].astype(F32) * kd_sum * rk_ref[...], e_ref, et_ref)
             * v_ref[...].astype(F32))
    o_ref[...] = ((yn + bonus) * g_ref[...].astype(F32)).astype(o_ref.dtype)


def _rwkv_post(y, r, k, v, a, g, k_a, r_k, ln_w, ln_b, e_mat, et_mat, tm=128):
    t, c = r.shape
    row = pl.BlockSpec((tm, c), lambda i: (i, 0))
    row2 = pl.BlockSpec((2, tm, c), lambda i: (0, i, 0))
    par = pl.BlockSpec((1, c), lambda i: (0, 0))
    return pl.pallas_call(
        _post_kernel,
        grid=(t // tm,),
        in_specs=[row2, row, row, row, row2, row, par, par, par, par,
                  pl.BlockSpec((c, 128), lambda i: (0, 0)), pl.BlockSpec((128, c), lambda i: (0, 0))],
        out_specs=row,
        out_shape=jax.ShapeDtypeStruct((t, c), BF16),
        compiler_params=_cparams(("parallel",)),
        name="rwkv_post",
    )(y, r, k, v, a, g, k_a, r_k, ln_w, ln_b, e_mat, et_mat)


def _s5_tables(lam_re, lam_im, log_step, b_re, b_im, c_re, c_im, d_skip, n_levels):
    L, P, CH, G, GB = S5_L, S5_STATE, S5_CH, S5_GROUPS, S5_GB
    NB = G // GB
    hi = lax.Precision.HIGHEST
    lam = lax.complex(lam_re.astype(F32), lam_im.astype(F32))
    dt = jnp.exp(log_step.astype(F32))[..., None]
    lam_dt = lam * dt
    lam_bar = jnp.exp(lam_dt)
    b = lax.complex(b_re.astype(F32), b_im.astype(F32))
    b_bar = ((lam_bar - 1.0) / lam)[..., None] * b
    c = lax.complex(c_re.astype(F32), c_im.astype(F32))
    taus = jnp.arange(L + 1, dtype=F32)
    pows = jnp.exp(lam_dt[:, :, None, :] * taus[None, None, :, None])
    kern = jnp.real(jnp.einsum('dgop,dgtp,dgpi->dgtoi', c, pows[:, :, :L], b_bar, precision=hi))
    j = jnp.arange(L)[:, None]
    t = jnp.arange(L)[None, :]
    lag = jnp.arange(L)[None, None, :]
    sel_f = ((t - j)[:, :, None] == lag).astype(F32)
    sel_b = ((j - t)[:, :, None] == lag).astype(F32)
    skip = (jnp.eye(L)[None, :, None, :, None] * jnp.eye(CH)[None, None, :, None, :]
            * d_skip.astype(F32).reshape(G, 1, CH, 1, 1))
    tsmall = (jnp.einsum('jtl,gloi->gjito', sel_f, kern[0], precision=hi)
              + jnp.einsum('jtl,gloi->gjito', sel_b, kern[1], precision=hi) + skip)
    def by_block(x, lead):
        x = x.reshape((NB, GB) + x.shape[1:])
        perm = (0,) + tuple(range(2, 2 + lead)) + (1,) + tuple(range(2 + lead, x.ndim))
        return x.transpose(perm)

    a_t = by_block(tsmall.reshape(G, L, CH, L * CH), 1).reshape(NB, L * GB * CH, L * CH)
    desc = jnp.exp(lam_dt[:, :, None, :] * (L - taus[:L])[None, None, :, None])
    desc1 = jnp.exp(lam_dt[0][:, None, :] * (L - 1 - taus[:L])[None, :, None])
    bb_t = b_bar.transpose(0, 1, 3, 2)
    zf = desc1[:, :, None, :] * bb_t[0][:, None]
    zb = pows[1][:, :L][:, :, None, :] * bb_t[1][:, None]
    wzs = jnp.stack([jnp.real(zf), jnp.imag(zf), jnp.real(zb), jnp.imag(zb)], axis=3)
    a_z = by_block(wzs.reshape(G, L, CH, 4 * P), 1).reshape(NB, L * GB * CH, 4 * P)
    c_t = c.transpose(0, 1, 3, 2)
    yf = c_t[0][:, :, None, :] * pows[0][:, 1:].transpose(0, 2, 1)[:, :, :, None]
    yb = c_t[1][:, :, None, :] * desc[1].transpose(0, 2, 1)[:, :, :, None]
    wys = jnp.stack([jnp.real(yf), -jnp.imag(yf), jnp.real(yb), -jnp.imag(yb)], axis=1)
    a_y = by_block(wys.reshape(G, 4, P, L * CH), 1).reshape(NB, 4 * GB * P, L * CH)
    steps = L * (2.0 ** jnp.arange(n_levels, dtype=F32))
    lp = jnp.exp(lam_dt[:, :, None, :] * steps[None, None, :, None])
    m = jnp.stack([jnp.real(lp[0]), jnp.imag(lp[0]), jnp.real(lp[1]), jnp.imag(lp[1])], axis=0)
    mult = m.reshape(4, NB, GB, n_levels, P).transpose(1, 3, 0, 2, 4).reshape(NB, n_levels, 4, GB * P)
    return a_t.astype(BF16), a_z.astype(BF16), a_y.astype(BF16), mult


def _group_of(shape, dim, width):
    return (lax.broadcasted_iota(jnp.int32, shape, dim) // width) % S5_GB


def _s5_kernel(n_levels, x_ref, at_ref, az_ref, ay_ref, rt_ref, rz_ref, mult_ref, y_ref, t_s, wz_s, wy_s):
    nck = x_ref.shape[0] // S5_L
    lanes = x_ref.shape[1]
    w = S5_GB * S5_STATE

    @pl.when(pl.program_id(1) == 0)
    def _():
        t = jnp.dot(at_ref[...], rt_ref[...], preferred_element_type=F32)
        keep = _group_of(t.shape, 0, S5_CH) == _group_of(t.shape, 1, S5_CH)
        t_s[...] = jnp.where(keep, t, 0.0).astype(BF16)
        z = jnp.dot(az_ref[...], rz_ref[...], preferred_element_type=F32)
        keep = _group_of(z.shape, 0, S5_CH) == _group_of(z.shape, 1, S5_STATE)
        wz_s[...] = jnp.where(keep, z, 0.0).astype(BF16)
        yy = jnp.dot(ay_ref[...], rt_ref[...], preferred_element_type=F32)
        keep = _group_of(yy.shape, 0, S5_STATE) == _group_of(yy.shape, 1, S5_CH)
        wy_s[...] = jnp.where(keep, yy, 0.0).astype(BF16)

    xcat = jnp.concatenate([x_ref[pl.ds(tl, nck, stride=S5_L), :].astype(BF16) for tl in range(S5_L)],
                           axis=1)
    z = jnp.dot(xcat, wz_s[...], preferred_element_type=F32)
    fre, fim, bre, bim = z[:, :w], z[:, w:2 * w], z[:, 2 * w:3 * w], z[:, 3 * w:]
    cidx = lax.broadcasted_iota(jnp.int32, (nck, w), 0)
    for i in range(n_levels):
        sh = 1 << i
        lfr, lfi = mult_ref[i, 0:1, :], mult_ref[i, 1:2, :]
        lbr, lbi = mult_ref[i, 2:3, :], mult_ref[i, 3:4, :]
        keep = cidx >= sh
        sr = jnp.where(keep, pltpu.roll(fre, sh, 0), 0.0)
        si = jnp.where(keep, pltpu.roll(fim, sh, 0), 0.0)
        fre, fim = fre + sr * lfr - si * lfi, fim + sr * lfi + si * lfr
        keep = cidx < nck - sh
        sr = jnp.where(keep, pltpu.roll(bre, nck - sh, 0), 0.0)
        si = jnp.where(keep, pltpu.roll(bim, nck - sh, 0), 0.0)
        bre, bim = bre + sr * lbr - si * lbi, bim + sr * lbi + si * lbr
    has_prev = cidx >= 1
    has_next = cidx < nck - 1
    xin = jnp.concatenate([jnp.where(has_prev, pltpu.roll(fre, 1, 0), 0.0),
                           jnp.where(has_prev, pltpu.roll(fim, 1, 0), 0.0),
                           jnp.where(has_next, pltpu.roll(bre, nck - 1, 0), 0.0),
                           jnp.where(has_next, pltpu.roll(bim, nck - 1, 0), 0.0)], axis=1).astype(BF16)
    y = jnp.dot(xcat, t_s[...], preferred_element_type=F32)
    y = y + jnp.dot(xin, wy_s[...], preferred_element_type=F32)
    for tl in range(S5_L):
        y_ref[pl.ds(tl, nck, stride=S5_L), :] = y[:, tl * lanes:(tl + 1) * lanes]


def _s5_mix(p_main, col0, a_t, a_z, a_y, mult, batch, seq, n_levels):
    nb = a_t.shape[0]
    lanes = S5_GB * S5_CH
    kw = S5_L * lanes
    sw = 4 * S5_GB * S5_STATE
    r_t = (jnp.arange(S5_L * S5_CH)[:, None] ==
           (jnp.arange(kw)[None, :] // lanes) * S5_CH + jnp.arange(kw)[None, :] % S5_CH).astype(BF16)
    r_z = (jnp.arange(4 * S5_STATE)[:, None] ==
           (jnp.arange(sw)[None, :] // (S5_GB * S5_STATE)) * S5_STATE + jnp.arange(sw)[None, :] % S5_STATE
           ).astype(BF16)
    return pl.pallas_call(
        functools.partial(_s5_kernel, n_levels),
        grid=(nb, batch),
        in_specs=[pl.BlockSpec((seq, lanes), lambda j, b: (b, col0 + j)),
                  pl.BlockSpec((None, kw, S5_L * S5_CH), lambda j, b: (j, 0, 0)),
                  pl.BlockSpec((None, kw, 4 * S5_STATE), lambda j, b: (j, 0, 0)),
                  pl.BlockSpec((None, sw, S5_L * S5_CH), lambda j, b: (j, 0, 0)),
                  pl.BlockSpec((S5_L * S5_CH, kw), lambda j, b: (0, 0)),
                  pl.BlockSpec((4 * S5_STATE, sw), lambda j, b: (0, 0)),
                  pl.BlockSpec((None, n_levels, 4, S5_GB * S5_STATE), lambda j, b: (j, 0, 0, 0))],
        out_specs=pl.BlockSpec((seq, lanes), lambda j, b: (b, j)),
        out_shape=jax.ShapeDtypeStruct((batch * seq, D_S5), F32),
        scratch_shapes=[pltpu.VMEM((kw, kw), BF16), pltpu.VMEM((kw, sw), BF16), pltpu.VMEM((sw, kw), BF16)],
        compiler_params=_cparams(("parallel", "arbitrary")),
        name="s5_mix",
    )(p_main, a_t, a_z, a_y, r_t, r_z, mult)


def _glu_kernel(y_ref, w_ref, b_ref, o_ref):
    y = y_ref[...]
    z = jnp.dot(jax.nn.gelu(y).astype(BF16), w_ref[...], preferred_element_type=F32) + b_ref[...]
    o_ref[...] = (y * jax.nn.sigmoid(z)).astype(o_ref.dtype)


def _glu(y, w, b, tm=512):
    t, c = y.shape
    tm = min(tm, t)
    return pl.pallas_call(
        _glu_kernel,
        grid=(t // tm,),
        in_specs=[pl.BlockSpec((tm, c), lambda i: (i, 0)), pl.BlockSpec((c, c), lambda i: (0, 0)),
                  pl.BlockSpec((1, c), lambda i: (0, 0))],
        out_specs=pl.BlockSpec((tm, c), lambda i: (i, 0)),
        out_shape=jax.ShapeDtypeStruct((t, c), BF16),
        compiler_params=_cparams(("parallel",)),
        name="s5_glu",
    )(y, w, b)


def _wqk_kernel(wq_ref, k_ref, o_ref):
    acc = lax.dot_general(wq_ref[...], k_ref[...], (((1,), (1,)), ((), ())), preferred_element_type=F32)
    o_ref[...] = (acc * ATTN_SCALE).astype(o_ref.dtype)


def _vwo_kernel(v_ref, wo_ref, o_ref):
    o_ref[...] = jnp.dot(v_ref[...], wo_ref[...], preferred_element_type=F32).astype(o_ref.dtype)


def _attn_kernel(mem, x_ref, wqk_ref, vwo_ref, o_ref, p_ref):
    j = pl.program_id(2)
    tn = o_ref.shape[1]

    @pl.when(j == 0)
    def _():
        x = x_ref[...]
        rstd = lax.rsqrt(jnp.mean(x * x, axis=-1, keepdims=True) + RMS_EPS)
        s = jnp.dot(x.astype(BF16), wqk_ref[...], preferred_element_type=F32) * rstd
        for h in range(ATTN_HEADS):
            sh = s[:, h * mem:(h + 1) * mem]
            sh = sh - jnp.max(sh, axis=-1, keepdims=True)
            p = jnp.exp(sh)
            p_ref[:, h * mem:(h + 1) * mem] = (p / jnp.sum(p, axis=-1, keepdims=True)).astype(BF16)

    res = x_ref[:, pl.ds(pl.multiple_of(j * tn, tn), tn)]
    o_ref[...] = jnp.dot(p_ref[...], vwo_ref[...], preferred_element_type=F32) + res


def _cross_attention(x, kv, w_q, w_o, batch, seq, mem, tm=512, tn=1024):
    d = D_MODEL
    hm = ATTN_HEADS * mem
    tm = min(tm, seq)
    wqk = pl.pallas_call(
        _wqk_kernel,
        grid=(batch, ATTN_HEADS),
        in_specs=[pl.BlockSpec((d, ATTN_HEAD_DIM), lambda b, h: (0, h)),
                  pl.BlockSpec((mem, ATTN_HEAD_DIM), lambda b, h: (b, h))],
        out_specs=pl.BlockSpec((None, d, mem), lambda b, h: (b, 0, h)),
        out_shape=jax.ShapeDtypeStruct((batch, d, hm), BF16),
        compiler_params=_cparams(("parallel", "parallel")),
        name="attn_wqk",
    )(w_q, kv)
    vwo = pl.pallas_call(
        _vwo_kernel,
        grid=(batch, ATTN_HEADS),
        in_specs=[pl.BlockSpec((mem, ATTN_HEAD_DIM), lambda b, h: (b, ATTN_HEADS + h)),
                  pl.BlockSpec((ATTN_HEAD_DIM, d), lambda b, h: (h, 0))],
        out_specs=pl.BlockSpec((None, mem, d), lambda b, h: (b, h, 0)),
        out_shape=jax.ShapeDtypeStruct((batch, hm, d), BF16),
        compiler_params=_cparams(("parallel", "parallel")),
        name="attn_vwo",
    )(kv, w_o)
    out = pl.pallas_call(
        functools.partial(_attn_kernel, mem),
        grid=(batch, seq // tm, d // tn),
        in_specs=[pl.BlockSpec((None, tm, d), lambda b, i, j: (b, i, 0)),
                  pl.BlockSpec((None, d, hm), lambda b, i, j: (b, 0, 0)),
                  pl.BlockSpec((None, hm, tn), lambda b, i, j: (b, 0, j))],
        out_specs=pl.BlockSpec((None, tm, tn), lambda b, i, j: (b, i, j)),
        out_shape=jax.ShapeDtypeStruct((batch, seq, d), F32),
        scratch_shapes=[pltpu.VMEM((tm, hm), BF16)],
        compiler_params=_cparams(("parallel", "parallel", "arbitrary")),
        name="cross_attn",
    )(x.reshape(batch, seq, d), wqk, vwo)
    return out.reshape(batch * seq, d)


def _router_kernel(x_ref, g_ref, whi_ref, wlo_ref, b_ref, xn_ref, info_ref, cnt_ref, carry_ref):
    i = pl.program_id(0)

    @pl.when(i == 0)
    def _():
        carry_ref[...] = jnp.zeros_like(carry_ref)

    x = x_ref[...]
    tm = x.shape[0]
    ms = jnp.mean(x * x, axis=-1, keepdims=True)
    xn = x * lax.rsqrt(ms + RMS_EPS) * g_ref[...]
    half = xn.shape[1] // 2
    xn_ref[...] = _pack_bf16_pair(xn[:, :half], xn[:, half:])
    xh, xl = _split2(xn)
    whi = whi_ref[...]
    logits = (jnp.dot(xh, whi, preferred_element_type=F32) + jnp.dot(xl, whi, preferred_element_type=F32)
              + jnp.dot(xh, wlo_ref[...], preferred_element_type=F32)) + b_ref[...]
    li = lax.broadcasted_iota(jnp.int32, logits.shape, 1)
    neg = jnp.float32(-jnp.inf)
    is_g = li < N_GROUPS
    gl = jnp.where(is_g, logits, neg)
    gm = jnp.max(gl, axis=-1, keepdims=True)
    gi = jnp.min(jnp.where(is_g & (gl == gm), li, 128), axis=-1, keepdims=True)
    gp = 1.0 / jnp.sum(jnp.where(is_g, jnp.exp(gl - gm), 0.0), axis=-1, keepdims=True)
    lo_lane = N_GROUPS + EPG * gi
    sel = (li >= lo_lane) & (li < lo_lane + EPG)
    l1 = jnp.where(sel, logits, neg)
    e1 = jnp.max(l1, axis=-1, keepdims=True)
    i1 = jnp.min(jnp.where(sel & (l1 == e1), li, 128), axis=-1, keepdims=True)
    sel2 = sel & (li != i1)
    l2 = jnp.where(sel2, logits, neg)
    e2 = jnp.max(l2, axis=-1, keepdims=True)
    i2 = jnp.min(jnp.where(sel2 & (l2 == e2), li, 128), axis=-1, keepdims=True)
    ex = jnp.exp(e2 - e1)
    w1 = gp / (1.0 + ex)
    w2 = gp * ex / (1.0 + ex)
    x1 = i1 - N_GROUPS
    x2 = i2 - N_GROUPS
    oh1 = li == x1
    oh2 = li == x2
    oh = (oh1 | oh2).astype(BF16)
    ri = lax.broadcasted_iota(jnp.int32, (tm, tm), 0)
    ci = lax.broadcasted_iota(jnp.int32, (tm, tm), 1)
    before = jnp.dot((ci < ri).astype(BF16), oh, preferred_element_type=F32) + carry_ref[0:1, :]
    r1 = jnp.sum(jnp.where(oh1, before, 0.0), axis=-1, keepdims=True)
    r2 = jnp.sum(jnp.where(oh2, before, 0.0), axis=-1, keepdims=True)
    new_carry = carry_ref[0:1, :] + jnp.sum(oh.astype(F32), axis=0, keepdims=True)
    carry_ref[...] = jnp.broadcast_to(new_carry, carry_ref.shape)
    cnt_ref[...] = jnp.broadcast_to(new_carry, cnt_ref.shape)
    info = jnp.where(li == 0, x1.astype(F32), 0.0)
    info = jnp.where(li == 1, x2.astype(F32), info)
    info = jnp.where(li == 2, w1, info)
    info = jnp.where(li == 3, w2, info)
    info = jnp.where(li == 4, r1, info)
    info = jnp.where(li == 5, r2, info)
    info_ref[...] = info


def _router(x, gain, w_hi, w_lo, bias, tm=256):
    t, d = x.shape
    sd = jax.ShapeDtypeStruct
    return pl.pallas_call(
        _router_kernel,
        grid=(t // tm,),
        in_specs=[pl.BlockSpec((tm, d), lambda i: (i, 0)), pl.BlockSpec((1, d), lambda i: (0, 0)),
                  pl.BlockSpec((d, 128), lambda i: (0, 0)), pl.BlockSpec((d, 128), lambda i: (0, 0)),
                  pl.BlockSpec((1, 128), lambda i: (0, 0))],
        out_specs=[pl.BlockSpec((tm, d // 2), lambda i: (i, 0)), pl.BlockSpec((tm, 128), lambda i: (i, 0)),
                   pl.BlockSpec((8, 128), lambda i: (0, 0))],
        out_shape=[sd((t, d // 2), jnp.uint32), sd((t, 128), F32), sd((8, 128), F32)],
        scratch_shapes=[pltpu.VMEM((8, 128), F32)],
        compiler_params=_cparams(("arbitrary",)),
        name="moe_router",
    )(x, gain.reshape(1, d), w_hi, w_lo, bias)


def _gather_rows(src_hbm, idx_ref, dst_ref, sem, n_rows):
    def body(j, carry):
        pltpu.make_async_copy(src_hbm.at[pl.ds(idx_ref[0, j], 1), :], dst_ref.at[pl.ds(j, 1), :], sem).start()
        return carry
    lax.fori_loop(0, n_rows, body, 0, unroll=8)


def _wait_rows(src_hbm, dst_ref, sem, n_rows):
    pltpu.make_async_copy(src_hbm.at[pl.ds(0, n_rows), :], dst_ref, sem).wait()


def _issue_rows(src_hbm, idx_ref, dst_ref, sem, lo, hi):
    for j in range(lo, hi):
        pltpu.make_async_copy(src_hbm.at[pl.ds(idx_ref[0, j], 1), :], dst_ref.at[pl.ds(j, 1), :], sem).start()


def _expert_up_kernel(be_ref, nused_ref, tok_ref, tokn_ref, x_hbm, wg_ref, wu_ref, h_ref, xbuf, sem):
    i = pl.program_id(0)
    nb = pl.num_programs(0)
    n_used = nused_ref[0]
    slot = i % 2

    @pl.when(i == 0)
    def _():
        _gather_rows(x_hbm, tok_ref, xbuf.at[0], sem.at[0], MOE_BLK)

    @pl.when(i < n_used)
    def _():
        _wait_rows(x_hbm, xbuf.at[slot], sem.at[slot], MOE_BLK)
        half = xbuf.shape[2]
        nk = 2 * half // EXPERT_KC
        per = MOE_BLK // nk
        hg = jnp.zeros((MOE_BLK, D_EXPERT), F32)
        hu = jnp.zeros((MOE_BLK, D_EXPERT), F32)
        for kc in range(nk):
            _issue_rows(x_hbm, tokn_ref, xbuf.at[1 - slot], sem.at[1 - slot], kc * per, (kc + 1) * per)
            ks = slice(kc * EXPERT_KC, (kc + 1) * EXPERT_KC)
            wc = (kc * EXPERT_KC) % half
            xb = _unpack_bf16_pair(xbuf[slot, :, wc:wc + EXPERT_KC])[(kc * EXPERT_KC) // half].astype(BF16)
            hg = hg + jnp.dot(xb, wg_ref[ks, :].astype(BF16), preferred_element_type=F32)
            hu = hu + jnp.dot(xb, wu_ref[ks, :].astype(BF16), preferred_element_type=F32)
        h_ref[...] = (jax.nn.silu(hg) * hu).astype(h_ref.dtype)

        @pl.when(i == nb - 1)
        def _():
            _wait_rows(x_hbm, xbuf.at[1 - slot], sem.at[1 - slot], MOE_BLK)

    @pl.when(i >= n_used)
    def _():
        @pl.when(i == n_used)
        def _():
            _wait_rows(x_hbm, xbuf.at[slot], sem.at[slot], MOE_BLK)

        h_ref[...] = jnp.zeros_like(h_ref)


def _expert_down_kernel(be_ref, nused_ref, h_ref, wd_ref, y_ref):
    i = pl.program_id(0)
    n_used = nused_ref[0]

    @pl.when(i < n_used)
    def _():
        h = h_ref[...]
        half = y_ref.shape[1]
        for nc in range(half // EXPERT_KC):
            ns = slice(nc * EXPERT_KC, (nc + 1) * EXPERT_KC)
            nh = slice(half + nc * EXPERT_KC, half + (nc + 1) * EXPERT_KC)
            lo = jnp.dot(h, wd_ref[:, ns].astype(BF16), preferred_element_type=F32)
            hi = jnp.dot(h, wd_ref[:, nh].astype(BF16), preferred_element_type=F32)
            y_ref[:, ns] = _pack_bf16_pair(lo, hi)

    @pl.when(i >= n_used)
    def _():
        y_ref[...] = jnp.zeros_like(y_ref)


def _experts(block_expert, n_used, tok3, xn, wg, wu, wd):
    nb = tok3.shape[0]
    d = 2 * xn.shape[1]
    tok_spec = pl.BlockSpec((None, 1, MOE_BLK), lambda i, be, nu: (i, 0, 0), memory_space=pltpu.SMEM)
    tokn_spec = pl.BlockSpec((None, 1, MOE_BLK), lambda i, be, nu: (jnp.minimum(i + 1, nb - 1), 0, 0),
                             memory_space=pltpu.SMEM)
    h_buf = pl.pallas_call(
        _expert_up_kernel,
        grid_spec=pltpu.PrefetchScalarGridSpec(
            num_scalar_prefetch=2,
            grid=(nb,),
            in_specs=[tok_spec, tokn_spec, pl.BlockSpec(memory_space=pl.ANY),
                      pl.BlockSpec((None, d, D_EXPERT), lambda i, be, nu: (be[i], 0, 0)),
                      pl.BlockSpec((None, d, D_EXPERT), lambda i, be, nu: (be[i], 0, 0))],
            out_specs=pl.BlockSpec((MOE_BLK, D_EXPERT), lambda i, be, nu: (i, 0)),
            scratch_shapes=[pltpu.VMEM((2, MOE_BLK, d // 2), jnp.uint32), pltpu.SemaphoreType.DMA((2,))],
        ),
        out_shape=jax.ShapeDtypeStruct((nb * MOE_BLK, D_EXPERT), BF16),
        compiler_params=_cparams(("arbitrary",)),
        name="moe_up",
    )(block_expert, n_used, tok3, tok3, xn, wg, wu)
    return pl.pallas_call(
        _expert_down_kernel,
        grid_spec=pltpu.PrefetchScalarGridSpec(
            num_scalar_prefetch=2,
            grid=(nb,),
            in_specs=[pl.BlockSpec((MOE_BLK, D_EXPERT), lambda i, be, nu: (i, 0)),
                      pl.BlockSpec((None, D_EXPERT, d), lambda i, be, nu: (be[i], 0, 0))],
            out_specs=pl.BlockSpec((MOE_BLK, d // 2), lambda i, be, nu: (i, 0)),
        ),
        out_shape=jax.ShapeDtypeStruct((nb * MOE_BLK, d // 2), jnp.uint32),
        compiler_params=_cparams(("arbitrary",)),
        name="moe_down",
    )(block_expert, n_used, h_buf, wd)


def _combine_kernel(d1_ref, d1n_ref, d2_ref, d2n_ref, x_ref, info_ref, g_ref, y_hbm, o_ref, ybuf, sem):
    i = pl.program_id(0)
    n = pl.num_programs(0)
    slot = i % 2

    @pl.when(i == 0)
    def _():
        _gather_rows(y_hbm, d1_ref, ybuf.at[0, 0], sem.at[0], COMB_TM)
        _gather_rows(y_hbm, d2_ref, ybuf.at[0, 1], sem.at[0], COMB_TM)

    _wait_rows(y_hbm, ybuf.at[slot, 0], sem.at[slot], COMB_TM)
    _wait_rows(y_hbm, ybuf.at[slot, 1], sem.at[slot], COMB_TM)
    _issue_rows(y_hbm, d1n_ref, ybuf.at[1 - slot, 0], sem.at[1 - slot], 0, COMB_TM)
    _issue_rows(y_hbm, d2n_ref, ybuf.at[1 - slot, 1], sem.at[1 - slot], 0, COMB_TM)
    info = info_ref[...]
    w1 = info[:, 2:3]
    w2 = info[:, 3:4]
    half = ybuf.shape[3]
    lo1, hi1 = _unpack_bf16_pair(ybuf[slot, 0])
    lo2, hi2 = _unpack_bf16_pair(ybuf[slot, 1])
    xlo = x_ref[:, :half] + lo1 * w1 + lo2 * w2
    xhi = x_ref[:, half:] + hi1 * w1 + hi2 * w2
    ms = (jnp.sum(xlo * xlo, axis=-1, keepdims=True) + jnp.sum(xhi * xhi, axis=-1, keepdims=True)) / (2 * half)
    rstd = lax.rsqrt(ms + RMS_EPS)
    o_ref[:, :half] = xlo * rstd * g_ref[:, :half]
    o_ref[:, half:] = xhi * rstd * g_ref[:, half:]

    @pl.when(i == n - 1)
    def _():
        _wait_rows(y_hbm, ybuf.at[1 - slot, 0], sem.at[1 - slot], COMB_TM)
        _wait_rows(y_hbm, ybuf.at[1 - slot, 1], sem.at[1 - slot], COMB_TM)


def _combine(dest1, dest2, x, info, gain, y_buf):
    t, d = x.shape
    nt = t // COMB_TM
    d1 = dest1.reshape(nt, 1, COMB_TM)
    d2 = dest2.reshape(nt, 1, COMB_TM)
    cur = pl.BlockSpec((None, 1, COMB_TM), lambda i: (i, 0, 0), memory_space=pltpu.SMEM)
    nxt = pl.BlockSpec((None, 1, COMB_TM), lambda i: (jnp.minimum(i + 1, nt - 1), 0, 0),
                       memory_space=pltpu.SMEM)
    return pl.pallas_call(
        _combine_kernel,
        grid=(nt,),
        in_specs=[cur, nxt, cur, nxt,
                  pl.BlockSpec((COMB_TM, d), lambda i: (i, 0)),
                  pl.BlockSpec((COMB_TM, 128), lambda i: (i, 0)),
                  pl.BlockSpec((1, d), lambda i: (0, 0)),
                  pl.BlockSpec(memory_space=pl.ANY)],
        out_specs=pl.BlockSpec((COMB_TM, d), lambda i: (i, 0)),
        out_shape=jax.ShapeDtypeStruct((t, d), F32),
        scratch_shapes=[pltpu.VMEM((2, 2, COMB_TM, d // 2), jnp.uint32), pltpu.SemaphoreType.DMA((2,))],
        compiler_params=_cparams(("arbitrary",)),
        name="moe_combine",
    )(d1, d1, d2, d2, x, info, gain.reshape(1, d), y_buf)


def _head_matrices():
    lane = jnp.arange(D_RWKV) // HEAD
    e = (lane[:, None] == jnp.arange(128)[None, :]).astype(BF16)
    return e, e.T


def _pad_lora_rows(w):
    return jnp.pad(w, ((0, 0), (0, LORA_PAD - w.shape[1]), (0, 0)))


def _layer(x, mem, norm_mix, w_in, shift_taps, rwkv_w0, rwkv_w2, rwkv_a0, rwkv_a2, rwkv_g2, rwkv_k_k,
           rwkv_k_a, rwkv_r_k, rwkv_ln_w, rwkv_ln_b, s5_lam_re, s5_lam_im, s5_log_step, s5_b_re, s5_b_im,
           s5_c_re, s5_c_im, s5_d, s5_glu_w, s5_glu_b, w_out, norm_attn, norm_mem, w_q, w_k, w_v, w_o,
           norm_ffn, router_grp_w, router_grp_b, router_exp_w, router_exp_b, exp_w_gate, exp_w_up,
           exp_w_down, out_gain):
    batch, seq, d = x.shape
    t = batch * seq
    mem_n = mem.shape[1]
    xt = x.reshape(t, d)
    c3 = 3 * D_RWKV
    off_g = c3 + 4 * 96

    def pad_cols(w, lo, width):
        return jnp.pad(w[:, lo:lo + width], ((0, 0), (0, LORA_PAD - width)))

    w_all = jnp.concatenate([w_in[:, :c3], w_in[:, off_g + 256:]]
                            + [pad_cols(w_in, c3 + i * 96, 96) for i in range(4)]
                            + [w_in[:, off_g:off_g + 256], jnp.zeros((d, LORA_BLK - N_LORA), F32)],
                            axis=1).astype(BF16)
    taps_l = jnp.concatenate([pad_cols(shift_taps, c3 + i * 96, 96) for i in range(4)]
                             + [shift_taps[:, off_g:off_g + 256], jnp.zeros((3, LORA_BLK - N_LORA), F32)], axis=1)
    h = _rmsnorm(xt, norm_mix, BF16)
    p_all = _matmul(h, w_all, F32, 1024, LORA_BLK, name="w_in")

    e_mat, et_mat = _head_matrices()
    k_a = rwkv_k_a.reshape(1, D_RWKV)
    r, k, v, kk, lw, a, g = _rwkv_prep(
        p_all, seq, shift_taps[:, :D_RWKV], shift_taps[:, D_RWKV:2 * D_RWKV],
        shift_taps[:, 2 * D_RWKV:c3], taps_l, rwkv_w0, _pad_lora_rows(rwkv_w2).astype(BF16), rwkv_a0,
        _pad_lora_rows(rwkv_a2).astype(BF16), rwkv_g2.astype(BF16), rwkv_k_k.reshape(1, D_RWKV),
        e_mat, et_mat)
    y_scan = _rwkv_scan(r, k, v, kk, lw, a, k_a, batch, seq)
    y_rwkv = _rwkv_post(y_scan, r, k, v, a, g, k_a, rwkv_r_k.reshape(1, D_RWKV),
                        rwkv_ln_w.reshape(1, D_RWKV), rwkv_ln_b.reshape(1, D_RWKV), e_mat, et_mat)

    n_levels = int(math.log2(seq // S5_L))
    a_t, a_z, a_y, mult = _s5_tables(s5_lam_re, s5_lam_im, s5_log_step, s5_b_re, s5_b_im,
                                    s5_c_re, s5_c_im, s5_d, n_levels)
    y_s5 = _s5_mix(p_all, c3 // (S5_GB * S5_CH), a_t, a_z, a_y, mult, batch, seq, n_levels)
    y_glu = _glu(y_s5, s5_glu_w.astype(BF16), s5_glu_b.reshape(1, D_S5))

    x1 = _matmul2_res(y_rwkv, y_glu, w_out.astype(BF16), xt, 1024, 512)

    memn = _rmsnorm(mem.reshape(batch * mem_n, d), norm_mem, BF16)
    wkv = jnp.concatenate([w_k, w_v], axis=1).astype(BF16)
    kv = _matmul(memn, wkv, BF16, 1024, 512, name="w_kv")
    x2 = _cross_attention(x1, kv, (norm_attn[:, None] * w_q).astype(BF16), w_o.astype(BF16), batch, seq, mem_n)

    w_r = jnp.concatenate([router_grp_w, router_exp_w,
                           jnp.zeros((d, 128 - N_GROUPS - N_EXPERTS), F32)], axis=1)
    w_r_hi = w_r.astype(BF16)
    w_r_lo = (w_r - w_r_hi.astype(F32)).astype(BF16)
    b_r = jnp.concatenate([router_grp_b, router_exp_b,
                           jnp.zeros((128 - N_GROUPS - N_EXPERTS,), F32)]).reshape(1, 128)
    xn3, info, cnt = _router(x2, norm_ffn, w_r_hi, w_r_lo, b_r)
    eid = info[:, 0:2].astype(jnp.int32)
    rank = info[:, 4:6].astype(jnp.int32)
    counts = cnt[0, :N_EXPERTS].astype(jnp.int32)
    nblk = (counts + MOE_BLK - 1) // MOE_BLK
    bstart = jnp.cumsum(nblk) - nblk
    n_used = jnp.sum(nblk)
    first = jnp.sum(jnp.where(eid[:, :, None] == jnp.arange(N_EXPERTS)[None, None, :], bstart, 0), axis=-1)
    dest = first * MOE_BLK + rank
    nb = (t * TOP_K) // MOE_BLK + N_EXPERTS
    tok_buf = jnp.zeros((nb * MOE_BLK,), jnp.int32).at[dest.reshape(-1)].set(
        jnp.repeat(jnp.arange(t, dtype=jnp.int32), TOP_K))
    blk = jnp.arange(nb, dtype=jnp.int32)
    block_expert = jnp.sum(blk[:, None] >= (bstart + nblk)[None, :], axis=1).astype(jnp.int32)
    last_e = jnp.max(jnp.where(nblk > 0, jnp.arange(N_EXPERTS), 0)).astype(jnp.int32)
    block_expert = jnp.minimum(block_expert, last_e)
    y_buf = _experts(block_expert, n_used.reshape(1).astype(jnp.int32), tok_buf.reshape(nb, 1, MOE_BLK),
                     xn3, exp_w_gate, exp_w_up, exp_w_down)
    out = _combine(dest[:, 0], dest[:, 1], x2, info, out_gain, y_buf)
    return out.reshape(batch, seq, d)


def kernel(x, mem, norm_mix, w_in, shift_taps, rwkv_w0, rwkv_w2, rwkv_a0, rwkv_a2, rwkv_g2, rwkv_k_k, rwkv_k_a, rwkv_r_k, rwkv_ln_w, rwkv_ln_b, s5_lam_re, s5_lam_im, s5_log_step, s5_b_re, s5_b_im, s5_c_re, s5_c_im, s5_d, s5_glu_w, s5_glu_b, w_out, norm_attn, norm_mem, w_q, w_k, w_v, w_o, norm_ffn, router_grp_w, router_grp_b, router_exp_w, router_exp_b, exp_w_gate, exp_w_up, exp_w_down, norm_final):
    return _layer(x, mem, norm_mix[0], w_in[0], shift_taps[0], rwkv_w0[0], rwkv_w2[0], rwkv_a0[0],
                  rwkv_a2[0], rwkv_g2[0], rwkv_k_k[0], rwkv_k_a[0], rwkv_r_k[0], rwkv_ln_w[0],
                  rwkv_ln_b[0], s5_lam_re[0], s5_lam_im[0], s5_log_step[0], s5_b_re[0], s5_b_im[0],
                  s5_c_re[0], s5_c_im[0], s5_d[0], s5_glu_w[0], s5_glu_b[0], w_out[0], norm_attn[0],
                  norm_mem[0], w_q[0], w_k[0], w_v[0], w_o[0], norm_ffn[0], router_grp_w[0],
                  router_grp_b[0], router_exp_w[0], router_exp_b[0], exp_w_gate[0], exp_w_up[0],
                  exp_w_down[0], norm_final)
```

```python
import functools
import math

import jax
import jax.numpy as jnp
from jax import lax
from jax.experimental import pallas as pl
from jax.experimental.pallas import tpu as pltpu

F32 = jnp.float32
BF16 = jnp.bfloat16

D_MODEL = 4096
D_RWKV = 2048
D_S5 = 2048
HEAD = 64
PAIR = 2 * HEAD
LORA_PAD = 128
N_LORA = 4 * LORA_PAD + 256
LORA_BLK = 1024
N_MAIN = 3 * D_RWKV + D_S5
S5_CH = 16
S5_GROUPS = D_S5 // S5_CH
S5_STATE = 64
S5_L = 8
S5_GB = 8
ATTN_HEADS = 4
ATTN_HEAD_DIM = D_MODEL // ATTN_HEADS
ATTN_SCALE = ATTN_HEAD_DIM ** -0.5
N_GROUPS = 8
EPG = 8
N_EXPERTS = N_GROUPS * EPG
TOP_K = 2
D_EXPERT = D_MODEL // 8
RMS_EPS = 1e-6
GN_EPS = 64e-5
L2_EPS = 1e-12

CHUNK = 64
SLAB = 2048
MOE_BLK = 288
COMB_TM = 128
EXPERT_KC = 1024
VMEM_LIMIT = 56 * 1024 * 1024


def _cparams(sem):
    return pltpu.CompilerParams(dimension_semantics=sem, vmem_limit_bytes=VMEM_LIMIT)


def _bdot(a, b):
    return jnp.dot(a.astype(BF16), b.astype(BF16), preferred_element_type=F32)


def _bdot_nt(a, b):
    return lax.dot_general(a.astype(BF16), b.astype(BF16), (((1,), (1,)), ((), ())),
                           preferred_element_type=F32)


def _split2(x):
    hi = x.astype(BF16)
    lo = (x - hi.astype(F32)).astype(BF16)
    return hi, lo


def _split3(x):
    hi = x.astype(BF16)
    r1 = x - hi.astype(F32)
    mid = r1.astype(BF16)
    lo = (r1 - mid.astype(F32)).astype(BF16)
    return hi, mid, lo


def _pack_bf16_pair(lo, hi):
    lo_b = lax.bitcast_convert_type(lo.astype(BF16).astype(F32), jnp.uint32) >> 16
    hi_b = lax.bitcast_convert_type(hi.astype(BF16).astype(F32), jnp.uint32) & jnp.uint32(0xFFFF0000)
    return hi_b | lo_b


def _unpack_bf16_pair(w):
    return (lax.bitcast_convert_type(w << 16, F32),
            lax.bitcast_convert_type(w & jnp.uint32(0xFFFF0000), F32))


def _dot_split2_rhs(x, e):
    h, l = _split2(x)
    return jnp.dot(h, e, preferred_element_type=F32) + jnp.dot(l, e, preferred_element_type=F32)


def _dot_exact_lhs(e, x):
    h, m, l = _split3(x)
    return (jnp.dot(e, h, preferred_element_type=F32) + jnp.dot(e, m, preferred_element_type=F32)
            + jnp.dot(e, l, preferred_element_type=F32))


def _rms_kernel(x_ref, g_ref, o_ref):
    x = x_ref[...]
    ms = jnp.mean(x * x, axis=-1, keepdims=True)
    o_ref[...] = (x * lax.rsqrt(ms + RMS_EPS) * g_ref[...]).astype(o_ref.dtype)


def _rmsnorm(x, gain, out_dtype, tm=256):
    t, d = x.shape
    return pl.pallas_call(
        _rms_kernel,
        grid=(t // tm,),
        in_specs=[pl.BlockSpec((tm, d), lambda i: (i, 0)), pl.BlockSpec((1, d), lambda i: (0, 0))],
        out_specs=pl.BlockSpec((tm, d), lambda i: (i, 0)),
        out_shape=jax.ShapeDtypeStruct((t, d), out_dtype),
        compiler_params=_cparams(("parallel",)),
        name="rmsnorm",
    )(x, gain.reshape(1, d))


def _mm_kernel(a_ref, b_ref, o_ref):
    o_ref[...] = jnp.dot(a_ref[...], b_ref[...], preferred_element_type=F32).astype(o_ref.dtype)


def _mm2_res_kernel(a1_ref, a2_ref, b_ref, r_ref, o_ref):
    k1 = a1_ref.shape[1]
    acc = jnp.dot(a1_ref[...], b_ref[:k1, :], preferred_element_type=F32)
    acc = acc + jnp.dot(a2_ref[...], b_ref[k1:, :], preferred_element_type=F32)
    o_ref[...] = (acc + r_ref[...]).astype(o_ref.dtype)


def _matmul(a, b, out_dtype, tm, tn, name):
    m, k = a.shape
    n = b.shape[1]
    tm, tn = min(tm, m), min(tn, n)
    return pl.pallas_call(
        _mm_kernel,
        grid=(m // tm, n // tn),
        in_specs=[pl.BlockSpec((tm, k), lambda i, j: (i, 0)), pl.BlockSpec((k, tn), lambda i, j: (0, j))],
        out_specs=pl.BlockSpec((tm, tn), lambda i, j: (i, j)),
        out_shape=jax.ShapeDtypeStruct((m, n), out_dtype),
        compiler_params=_cparams(("parallel", "parallel")),
        name=name,
    )(a, b)


def _matmul2_res(a1, a2, b, res, tm, tn):
    m, k1 = a1.shape
    k2 = a2.shape[1]
    n = b.shape[1]
    tm = min(tm, m)
    return pl.pallas_call(
        _mm2_res_kernel,
        grid=(m // tm, n // tn),
        in_specs=[pl.BlockSpec((tm, k1), lambda i, j: (i, 0)),
                  pl.BlockSpec((tm, k2), lambda i, j: (i, 0)),
                  pl.BlockSpec((k1 + k2, tn), lambda i, j: (0, j)),
                  pl.BlockSpec((tm, tn), lambda i, j: (i, j))],
        out_specs=pl.BlockSpec((tm, tn), lambda i, j: (i, j)),
        out_shape=jax.ShapeDtypeStruct((m, n), F32),
        compiler_params=_cparams(("parallel", "parallel")),
        name="w_out",
    )(a1, a2, b, res)


def _head_sum_bcast(x, e_ref, et_ref):
    s = _dot_split2_rhs(x, e_ref[...])
    return _dot_split2_rhs(s, et_ref[...])


def _shift3(x, hp, hn, taps, first, last):
    tm = x.shape[0]
    row = lax.broadcasted_iota(jnp.int32, x.shape, 0)
    prev_edge = jnp.where(first, 0.0, hp[7:8, :])
    next_edge = jnp.where(last, 0.0, hn[0:1, :])
    prev = jnp.where(row == 0, prev_edge, pltpu.roll(x, 1, 0))
    nxt = jnp.where(row == tm - 1, next_edge, pltpu.roll(x, tm - 1, 0))
    return taps[0:1, :] * prev + taps[1:2, :] * x + taps[2:3, :] * nxt


def _prep_kernel(seq_tiles,
                 r_ref, rp_ref, rn_ref, k_ref, kp_ref, kn_ref, v_ref, vp_ref, vn_ref,
                 lo_ref, lop_ref, lon_ref,
                 tr_ref, tk_ref, tv_ref, tl_ref,
                 w0_ref, w2_ref, a0_ref, a2_ref, g2_ref, kk_ref_p, e_ref, et_ref,
                 ro_ref, ko_ref, vo_ref, kko_ref, lw_ref, a_ref, g_ref):
    i = pl.program_id(0)
    first = (i % seq_tiles) == 0
    last = (i % seq_tiles) == seq_tiles - 1
    r = _shift3(r_ref[...], rp_ref[...], rn_ref[...], tr_ref[...], first, last)
    k = _shift3(k_ref[...], kp_ref[...], kn_ref[...], tk_ref[...], first, last)
    v = _shift3(v_ref[...], vp_ref[...], vn_ref[...], tv_ref[...], first, last)
    lo = _shift3(lo_ref[...], lop_ref[...], lon_ref[...], tl_ref[...], first, last)
    ro_ref[...] = r.astype(ro_ref.dtype)
    ko_ref[...] = k.astype(ko_ref.dtype)
    vo_ref[...] = v.astype(vo_ref.dtype)
    for d in range(2):
        xw = lo[:, d * LORA_PAD:(d + 1) * LORA_PAD]
        xa = lo[:, (2 + d) * LORA_PAD:(3 + d) * LORA_PAD]
        wl = w0_ref[d:d + 1, :] + _bdot(jnp.tanh(xw), w2_ref[d])
        lw_ref[d] = -math.exp(-0.5) * jax.nn.sigmoid(wl)
        a_ref[d] = jax.nn.sigmoid(a0_ref[d:d + 1, :] + _bdot(xa, a2_ref[d])).astype(a_ref.dtype)
    xg = lo[:, 4 * LORA_PAD:N_LORA]
    g_ref[...] = _bdot(jax.nn.sigmoid(xg), g2_ref[...]).astype(g_ref.dtype)
    kk = k * kk_ref_p[...]
    ssq = _head_sum_bcast(kk * kk, e_ref, et_ref)
    kko_ref[...] = (kk / jnp.maximum(jnp.sqrt(ssq), L2_EPS)).astype(kko_ref.dtype)


def _rwkv_prep(p_all, seq, taps_r, taps_k, taps_v, taps_l, w0, w2p, a0, a2p, g2, k_k, e_mat, et_mat,
               tm=128):
    t = p_all.shape[0]
    nt8 = t // 8
    seq_tiles = seq // tm
    c = D_RWKV

    def main_spec(col):
        return [pl.BlockSpec((tm, c), lambda i, col=col: (i, col)),
                pl.BlockSpec((8, c), lambda i, col=col: (jnp.maximum(i * (tm // 8) - 1, 0), col)),
                pl.BlockSpec((8, c), lambda i, col=col: (jnp.minimum((i + 1) * (tm // 8), nt8 - 1), col))]

    lcol = N_MAIN // LORA_BLK
    lora_spec = [pl.BlockSpec((tm, LORA_BLK), lambda i: (i, lcol)),
                 pl.BlockSpec((8, LORA_BLK), lambda i: (jnp.maximum(i * (tm // 8) - 1, 0), lcol)),
                 pl.BlockSpec((8, LORA_BLK), lambda i: (jnp.minimum((i + 1) * (tm // 8), nt8 - 1), lcol))]

    def full(shape):
        nd = len(shape)
        return pl.BlockSpec(shape, lambda i, nd=nd: (0,) * nd)

    in_specs = (main_spec(0) + main_spec(1) + main_spec(2) + lora_spec
                + [full((3, c)), full((3, c)), full((3, c)), full((3, LORA_BLK)),
                   full((2, c)), full((2, LORA_PAD, c)), full((2, c)), full((2, LORA_PAD, c)),
                   full((256, c)), full((1, c)), full((c, 128)), full((128, c))])
    row = pl.BlockSpec((tm, c), lambda i: (i, 0))
    row2 = pl.BlockSpec((2, tm, c), lambda i: (0, i, 0))
    sd = jax.ShapeDtypeStruct
    return pl.pallas_call(
        functools.partial(_prep_kernel, seq_tiles),
        grid=(t // tm,),
        in_specs=in_specs,
        out_specs=[row, row, row, row, row2, row2, row],
        out_shape=[sd((t, c), BF16), sd((t, c), BF16), sd((t, c), BF16), sd((t, c), BF16),
                   sd((2, t, c), F32), sd((2, t, c), BF16), sd((t, c), BF16)],
        compiler_params=_cparams(("parallel",)),
        name="rwkv_prep",
    )(p_all, p_all, p_all, p_all, p_all, p_all, p_all, p_all, p_all,
      p_all, p_all, p_all,
      taps_r, taps_k, taps_v, taps_l, w0, w2p, a0, a2p, g2, k_k, e_mat, et_mat)


def _scan_kernel(r_ref, k_ref, v_ref, kk_ref, lw_ref, a_ref, ka_ref, y_ref, st_ref):
    d = pl.program_id(0)
    c = pl.program_id(3)

    @pl.when(c == 0)
    def _():
        st_ref[...] = jnp.zeros_like(st_ref)

    ti = lax.broadcasted_iota(jnp.int32, (CHUNK, PAIR), 0)
    si = lax.broadcasted_iota(jnp.int32, (CHUNK, PAIR), 1) % HEAD
    ahead = (ti - si) * (1 - 2 * d)
    incl = ahead >= 0
    strict = ahead > 0
    eye = (si == ti).astype(F32)

    lw = lw_ref[...]
    cum_incl = _dot_exact_lhs(incl[:, :CHUNK].astype(BF16), lw)
    tot = jnp.sum(lw, axis=0, keepdims=True)
    e_incl = jnp.exp(cum_incl)
    e_excl = jnp.exp(cum_incl - lw)
    e_ninc = jnp.exp(-cum_incl)
    g_tot = jnp.exp(tot)

    a = a_ref[...].astype(F32)
    kk = kk_ref[...].astype(F32)
    kd = k_ref[...].astype(F32) * (1.0 + (a - 1.0) * ka_ref[...])
    at_all = kk * e_excl
    rt_all = r_ref[...].astype(F32) * e_incl
    bt_all = (kk * a) * e_ninc
    kt_all = kd * e_ninc
    bh_all = bt_all * g_tot
    kh_all = kt_all * g_tot
    v_all = v_ref[...].astype(F32)

    npair = SLAB // PAIR
    lane = lax.broadcasted_iota(jnp.int32, (1, PAIR), 1)
    head0 = lane < HEAD

    def bd(y):
        return jnp.concatenate([jnp.where(head0, y, 0.0), jnp.where(head0, 0.0, y)], axis=0)

    sls = [slice(p * PAIR, (p + 1) * PAIR) for p in range(npair)]
    ars = [jnp.concatenate([at_all[:, sl], rt_all[:, sl]], axis=0).astype(BF16) for sl in sls]
    s_ps = [st_ref[p] for p in range(npair)]
    gs = [_bdot_nt(ars[p], jnp.concatenate([bd(bt_all[:, sls[p]]), bd(kt_all[:, sls[p]]), s_ps[p]], axis=0))
          for p in range(npair)]
    ms = [jnp.where(strict, -gs[p][:CHUNK, :PAIR], 0.0) for p in range(npair)]
    tinvs = [eye + ms[p] for p in range(npair)]
    ms = [_bdot(m, bd(m)) for m in ms]
    for _ in range(4):
        ps = [_bdot(jnp.concatenate([tinvs[p], ms[p]], axis=0), bd(ms[p])) for p in range(npair)]
        tinvs = [tinvs[p] + ps[p][:CHUNK] for p in range(npair)]
        ms = [ps[p][CHUNK:] for p in range(npair)]
    tinvs = [tinvs[p] + _bdot(tinvs[p], bd(ms[p])) for p in range(npair)]
    x2s = [_bdot(jnp.concatenate([jnp.where(strict, gs[p][:CHUNK, PAIR:2 * PAIR], 0.0),
                                  jnp.where(incl, gs[p][CHUNK:, PAIR:2 * PAIR], 0.0)], axis=0),
                 bd(v_all[:, sls[p]])) for p in range(npair)]
    us = [-_bdot(tinvs[p], bd(gs[p][:CHUNK, 2 * PAIR:] + x2s[p][:CHUNK])) for p in range(npair)]
    outs = [gs[p][CHUNK:, 2 * PAIR:] + x2s[p][CHUNK:]
            + _bdot(jnp.where(incl, gs[p][CHUNK:, :PAIR], 0.0), bd(us[p])) for p in range(npair)]
    row = lax.broadcasted_iota(jnp.int32, (PAIR, PAIR), 0)
    same_head = (row < HEAD) == (lax.broadcasted_iota(jnp.int32, (PAIR, PAIR), 1) < HEAD)
    for p in range(npair):
        uv = jnp.concatenate([us[p], v_all[:, sls[p]]], axis=0).astype(BF16)
        bk = jnp.concatenate([bh_all[:, sls[p]], kh_all[:, sls[p]]], axis=0).astype(BF16)
        upd = lax.dot_general(uv, bk, (((0,), (0,)), ((), ())), preferred_element_type=F32)
        st_ref[p] = s_ps[p] * g_tot[:, sls[p]] + jnp.where(same_head, upd, 0.0)
    y_ref[...] = jnp.concatenate(outs, axis=1)


def _rwkv_scan(r, k, v, kk, lw, a, k_a, batch, seq):
    nc = seq // CHUNK
    ns = D_RWKV // SLAB
    r4 = r.reshape(batch, seq, D_RWKV)
    k4 = k.reshape(batch, seq, D_RWKV)
    v4 = v.reshape(batch, seq, D_RWKV)
    kk4 = kk.reshape(batch, seq, D_RWKV)
    lw5 = lw.reshape(2, batch, seq, D_RWKV)
    a5 = a.reshape(2, batch, seq, D_RWKV)

    def tchunk(d, c):
        return c + d * (nc - 1 - 2 * c)

    shared = pl.BlockSpec((None, CHUNK, SLAB), lambda d, b, s, c: (b, tchunk(d, c), s))
    perdir = pl.BlockSpec((None, None, CHUNK, SLAB), lambda d, b, s, c: (d, b, tchunk(d, c), s))
    y = pl.pallas_call(
        _scan_kernel,
        grid=(2, batch, ns, nc),
        in_specs=[shared, shared, shared, shared, perdir, perdir,
                  pl.BlockSpec((1, SLAB), lambda d, b, s, c: (0, s))],
        out_specs=perdir,
        out_shape=jax.ShapeDtypeStruct((2, batch, seq, D_RWKV), F32),
        scratch_shapes=[pltpu.VMEM((SLAB // PAIR, PAIR, PAIR), F32)],
        compiler_params=_cparams(("parallel", "parallel", "parallel", "arbitrary")),
        name="rwkv_scan",
    )(r4, k4, v4, kk4, lw5, a5, k_a)
    return y.reshape(2, batch * seq, D_RWKV)


def _post_kernel(y_ref, r_ref, k_ref, v_ref, a_ref, g_ref, ka_ref, rk_ref, lnw_ref, lnb_ref,
                 e_ref, et_ref, o_ref):
    y = y_ref[0] + y_ref[1]
    mu = _head_sum_bcast(y, e_ref, et_ref) * (1.0 / HEAD)
    yc = y - mu
    var = _head_sum_bcast(yc * yc, e_ref, et_ref) * (1.0 / HEAD)
    yn = yc * lax.rsqrt(var + GN_EPS) * lnw_ref[...] + lnb_ref[...]
    ka = ka_ref[...]
    k = k_ref[...].astype(F32)
    kd_sum = (k * (1.0 + (a_ref[0].astype(F32) - 1.0) * ka)
              + k * (1.0 + (a_ref[1].astype(F32) - 1.0) * ka))
    bonus = (_head_sum_bcast(r_ref[...].astype(F32) * kd_sum * rk_ref[...], e_ref, et_ref)
             * v_ref[...].astype(F32))
    o_ref[...] = ((yn + bonus) * g_ref[...].astype(F32)).astype(o_ref.dtype)


def _rwkv_post(y, r, k, v, a, g, k_a, r_k, ln_w, ln_b, e_mat, et_mat, tm=128):
    t, c = r.shape
    row = pl.BlockSpec((tm, c), lambda i: (i, 0))
    row2 = pl.BlockSpec((2, tm, c), lambda i: (0, i, 0))
    par = pl.BlockSpec((1, c), lambda i: (0, 0))
    return pl.pallas_call(
        _post_kernel,
        grid=(t // tm,),
        in_specs=[row2, row, row, row, row2, row, par, par, par, par,
                  pl.BlockSpec((c, 128), lambda i: (0, 0)), pl.BlockSpec((128, c), lambda i: (0, 0))],
        out_specs=row,
        out_shape=jax.ShapeDtypeStruct((t, c), BF16),
        compiler_params=_cparams(("parallel",)),
        name="rwkv_post",
    )(y, r, k, v, a, g, k_a, r_k, ln_w, ln_b, e_mat, et_mat)


def _s5_tables(lam_re, lam_im, log_step, b_re, b_im, c_re, c_im, d_skip, n_levels):
    L, P, CH, G, GB = S5_L, S5_STATE, S5_CH, S5_GROUPS, S5_GB
    NB = G // GB
    hi = lax.Precision.HIGHEST
    lam = lax.complex(lam_re.astype(F32), lam_im.astype(F32))
    dt = jnp.exp(log_step.astype(F32))[..., None]
    lam_dt = lam * dt
    lam_bar = jnp.exp(lam_dt)
    b = lax.complex(b_re.astype(F32), b_im.astype(F32))
    b_bar = ((lam_bar - 1.0) / lam)[..., None] * b
    c = lax.complex(c_re.astype(F32), c_im.astype(F32))
    taus = jnp.arange(L + 1, dtype=F32)
    pows = jnp.exp(lam_dt[:, :, None, :] * taus[None, None, :, None])
    kern = jnp.real(jnp.einsum('dgop,dgtp,dgpi->dgtoi', c, pows[:, :, :L], b_bar, precision=hi))
    j = jnp.arange(L)[:, None]
    t = jnp.arange(L)[None, :]
    lag = jnp.arange(L)[None, None, :]
    sel_f = ((t - j)[:, :, None] == lag).astype(F32)
    sel_b = ((j - t)[:, :, None] == lag).astype(F32)
    skip = (jnp.eye(L)[None, :, None, :, None] * jnp.eye(CH)[None, None, :, None, :]
            * d_skip.astype(F32).reshape(G, 1, CH, 1, 1))
    tsmall = (jnp.einsum('jtl,gloi->gjito', sel_f, kern[0], precision=hi)
              + jnp.einsum('jtl,gloi->gjito', sel_b, kern[1], precision=hi) + skip)
    def by_block(x, lead):
        x = x.reshape((NB, GB) + x.shape[1:])
        perm = (0,) + tuple(range(2, 2 + lead)) + (1,) + tuple(range(2 + lead, x.ndim))
        return x.transpose(perm)

    a_t = by_block(tsmall.reshape(G, L, CH, L * CH), 1).reshape(NB, L * GB * CH, L * CH)
    desc = jnp.exp(lam_dt[:, :, None, :] * (L - taus[:L])[None, None, :, None])
    desc1 = jnp.exp(lam_dt[0][:, None, :] * (L - 1 - taus[:L])[None, :, None])
    bb_t = b_bar.transpose(0, 1, 3, 2)
    zf = desc1[:, :, None, :] * bb_t[0][:, None]
    zb = pows[1][:, :L][:, :, None, :] * bb_t[1][:, None]
    wzs = jnp.stack([jnp.real(zf), jnp.imag(zf), jnp.real(zb), jnp.imag(zb)], axis=3)
    a_z = by_block(wzs.reshape(G, L, CH, 4 * P), 1).reshape(NB, L * GB * CH, 4 * P)
    c_t = c.transpose(0, 1, 3, 2)
    yf = c_t[0][:, :, None, :] * pows[0][:, 1:].transpose(0, 2, 1)[:, :, :, None]
    yb = c_t[1][:, :, None, :] * desc[1].transpose(0, 2, 1)[:, :, :, None]
    wys = jnp.stack([jnp.real(yf), -jnp.imag(yf), jnp.real(yb), -jnp.imag(yb)], axis=1)
    a_y = by_block(wys.reshape(G, 4, P, L * CH), 1).reshape(NB, 4 * GB * P, L * CH)
    steps = L * (2.0 ** jnp.arange(n_levels, dtype=F32))
    lp = jnp.exp(lam_dt[:, :, None, :] * steps[None, None, :, None])
    m = jnp.stack([jnp.real(lp[0]), jnp.imag(lp[0]), jnp.real(lp[1]), jnp.imag(lp[1])], axis=0)
    mult = m.reshape(4, NB, GB, n_levels, P).transpose(1, 3, 0, 2, 4).reshape(NB, n_levels, 4, GB * P)
    return a_t.astype(BF16), a_z.astype(BF16), a_y.astype(BF16), mult


def _group_of(shape, dim, width):
    return (lax.broadcasted_iota(jnp.int32, shape, dim) // width) % S5_GB


def _s5_kernel(n_levels, x_ref, at_ref, az_ref, ay_ref, rt_ref, rz_ref, mult_ref, y_ref, t_s, wz_s, wy_s):
    nck = x_ref.shape[0] // S5_L
    lanes = x_ref.shape[1]
    w = S5_GB * S5_STATE

    @pl.when(pl.program_id(1) == 0)
    def _():
        t = jnp.dot(at_ref[...], rt_ref[...], preferred_element_type=F32)
        keep = _group_of(t.shape, 0, S5_CH) == _group_of(t.shape, 1, S5_CH)
        t_s[...] = jnp.where(keep, t, 0.0).astype(BF16)
        z = jnp.dot(az_ref[...], rz_ref[...], preferred_element_type=F32)
        keep = _group_of(z.shape, 0, S5_CH) == _group_of(z.shape, 1, S5_STATE)
        wz_s[...] = jnp.where(keep, z, 0.0).astype(BF16)
        yy = jnp.dot(ay_ref[...], rt_ref[...], preferred_element_type=F32)
        keep = _group_of(yy.shape, 0, S5_STATE) == _group_of(yy.shape, 1, S5_CH)
        wy_s[...] = jnp.where(keep, yy, 0.0).astype(BF16)

    xcat = jnp.concatenate([x_ref[pl.ds(tl, nck, stride=S5_L), :].astype(BF16) for tl in range(S5_L)],
                           axis=1)
    z = jnp.dot(xcat, wz_s[...], preferred_element_type=F32)
    fre, fim, bre, bim = z[:, :w], z[:, w:2 * w], z[:, 2 * w:3 * w], z[:, 3 * w:]
    cidx = lax.broadcasted_iota(jnp.int32, (nck, w), 0)
    for i in range(n_levels):
        sh = 1 << i
        lfr, lfi = mult_ref[i, 0:1, :], mult_ref[i, 1:2, :]
        lbr, lbi = mult_ref[i, 2:3, :], mult_ref[i, 3:4, :]
        keep = cidx >= sh
        sr = jnp.where(keep, pltpu.roll(fre, sh, 0), 0.0)
        si = jnp.where(keep, pltpu.roll(fim, sh, 0), 0.0)
        fre, fim = fre + sr * lfr - si * lfi, fim + sr * lfi + si * lfr
        keep = cidx < nck - sh
        sr = jnp.where(keep, pltpu.roll(bre, nck - sh, 0), 0.0)
        si = jnp.where(keep, pltpu.roll(bim, nck - sh, 0), 0.0)
        bre, bim = bre + sr * lbr - si * lbi, bim + sr * lbi + si * lbr
    has_prev = cidx >= 1
    has_next = cidx < nck - 1
    xin = jnp.concatenate([jnp.where(has_prev, pltpu.roll(fre, 1, 0), 0.0),
                           jnp.where(has_prev, pltpu.roll(fim, 1, 0), 0.0),
                           jnp.where(has_next, pltpu.roll(bre, nck - 1, 0), 0.0),
                           jnp.where(has_next, pltpu.roll(bim, nck - 1, 0), 0.0)], axis=1).astype(BF16)
    y = jnp.dot(xcat, t_s[...], preferred_element_type=F32)
    y = y + jnp.dot(xin, wy_s[...], preferred_element_type=F32)
    for tl in range(S5_L):
        y_ref[pl.ds(tl, nck, stride=S5_L), :] = y[:, tl * lanes:(tl + 1) * lanes]


def _s5_mix(p_main, col0, a_t, a_z, a_y, mult, batch, seq, n_levels):
    nb = a_t.shape[0]
    lanes = S5_GB * S5_CH
    kw = S5_L * lanes
    sw = 4 * S5_GB * S5_STATE
    r_t = (jnp.arange(S5_L * S5_CH)[:, None] ==
           (jnp.arange(kw)[None, :] // lanes) * S5_CH + jnp.arange(kw)[None, :] % S5_CH).astype(BF16)
    r_z = (jnp.arange(4 * S5_STATE)[:, None] ==
           (jnp.arange(sw)[None, :] // (S5_GB * S5_STATE)) * S5_STATE + jnp.arange(sw)[None, :] % S5_STATE
           ).astype(BF16)
    return pl.pallas_call(
        functools.partial(_s5_kernel, n_levels),
        grid=(nb, batch),
        in_specs=[pl.BlockSpec((seq, lanes), lambda j, b: (b, col0 + j)),
                  pl.BlockSpec((None, kw, S5_L * S5_CH), lambda j, b: (j, 0, 0)),
                  pl.BlockSpec((None, kw, 4 * S5_STATE), lambda j, b: (j, 0, 0)),
                  pl.BlockSpec((None, sw, S5_L * S5_CH), lambda j, b: (j, 0, 0)),
                  pl.BlockSpec((S5_L * S5_CH, kw), lambda j, b: (0, 0)),
                  pl.BlockSpec((4 * S5_STATE, sw), lambda j, b: (0, 0)),
                  pl.BlockSpec((None, n_levels, 4, S5_GB * S5_STATE), lambda j, b: (j, 0, 0, 0))],
        out_specs=pl.BlockSpec((seq, lanes), lambda j, b: (b, j)),
        out_shape=jax.ShapeDtypeStruct((batch * seq, D_S5), F32),
        scratch_shapes=[pltpu.VMEM((kw, kw), BF16), pltpu.VMEM((kw, sw), BF16), pltpu.VMEM((sw, kw), BF16)],
        compiler_params=_cparams(("parallel", "arbitrary")),
        name="s5_mix",
    )(p_main, a_t, a_z, a_y, r_t, r_z, mult)


def _glu_kernel(y_ref, w_ref, b_ref, o_ref):
    y = y_ref[...]
    z = jnp.dot(jax.nn.gelu(y).astype(BF16), w_ref[...], preferred_element_type=F32) + b_ref[...]
    o_ref[...] = (y * jax.nn.sigmoid(z)).astype(o_ref.dtype)


def _glu(y, w, b, tm=512):
    t, c = y.shape
    tm = min(tm, t)
    return pl.pallas_call(
        _glu_kernel,
        grid=(t // tm,),
        in_specs=[pl.BlockSpec((tm, c), lambda i: (i, 0)), pl.BlockSpec((c, c), lambda i: (0, 0)),
                  pl.BlockSpec((1, c), lambda i: (0, 0))],
        out_specs=pl.BlockSpec((tm, c), lambda i: (i, 0)),
        out_shape=jax.ShapeDtypeStruct((t, c), BF16),
        compiler_params=_cparams(("parallel",)),
        name="s5_glu",
    )(y, w, b)


def _wqk_kernel(wq_ref, k_ref, o_ref):
    acc = lax.dot_general(wq_ref[...], k_ref[...], (((1,), (1,)), ((), ())), preferred_element_type=F32)
    o_ref[...] = (acc * ATTN_SCALE).astype(o_ref.dtype)


def _vwo_kernel(v_ref, wo_ref, o_ref):
    o_ref[...] = jnp.dot(v_ref[...], wo_ref[...], preferred_element_type=F32).astype(o_ref.dtype)


def _attn_kernel(mem, x_ref, wqk_ref, vwo_ref, o_ref, p_ref):
    j = pl.program_id(2)
    tn = o_ref.shape[1]

    @pl.when(j == 0)
    def _():
        x = x_ref[...]
        rstd = lax.rsqrt(jnp.mean(x * x, axis=-1, keepdims=True) + RMS_EPS)
        s = jnp.dot(x.astype(BF16), wqk_ref[...], preferred_element_type=F32) * rstd
        for h in range(ATTN_HEADS):
            sh = s[:, h * mem:(h + 1) * mem]
            sh = sh - jnp.max(sh, axis=-1, keepdims=True)
            p = jnp.exp(sh)
            p_ref[:, h * mem:(h + 1) * mem] = (p / jnp.sum(p, axis=-1, keepdims=True)).astype(BF16)

    res = x_ref[:, pl.ds(pl.multiple_of(j * tn, tn), tn)]
    o_ref[...] = jnp.dot(p_ref[...], vwo_ref[...], preferred_element_type=F32) + res


def _cross_attention(x, kv, w_q, w_o, batch, seq, mem, tm=512, tn=1024):
    d = D_MODEL
    hm = ATTN_HEADS * mem
    tm = min(tm, seq)
    wqk = pl.pallas_call(
        _wqk_kernel,
        grid=(batch, ATTN_HEADS),
        in_specs=[pl.BlockSpec((d, ATTN_HEAD_DIM), lambda b, h: (0, h)),
                  pl.BlockSpec((mem, ATTN_HEAD_DIM), lambda b, h: (b, h))],
        out_specs=pl.BlockSpec((None, d, mem), lambda b, h: (b, 0, h)),
        out_shape=jax.ShapeDtypeStruct((batch, d, hm), BF16),
        compiler_params=_cparams(("parallel", "parallel")),
        name="attn_wqk",
    )(w_q, kv)
    vwo = pl.pallas_call(
        _vwo_kernel,
        grid=(batch, ATTN_HEADS),
        in_specs=[pl.BlockSpec((mem, ATTN_HEAD_DIM), lambda b, h: (b, ATTN_HEADS + h)),
                  pl.BlockSpec((ATTN_HEAD_DIM, d), lambda b, h: (h, 0))],
        out_specs=pl.BlockSpec((None, mem, d), lambda b, h: (b, h, 0)),
        out_shape=jax.ShapeDtypeStruct((batch, hm, d), BF16),
        compiler_params=_cparams(("parallel", "parallel")),
        name="attn_vwo",
    )(kv, w_o)
    out = pl.pallas_call(
        functools.partial(_attn_kernel, mem),
        grid=(batch, seq // tm, d // tn),
        in_specs=[pl.BlockSpec((None, tm, d), lambda b, i, j: (b, i, 0)),
                  pl.BlockSpec((None, d, hm), lambda b, i, j: (b, 0, 0)),
                  pl.BlockSpec((None, hm, tn), lambda b, i, j: (b, 0, j))],
        out_specs=pl.BlockSpec((None, tm, tn), lambda b, i, j: (b, i, j)),
        out_shape=jax.ShapeDtypeStruct((batch, seq, d), F32),
        scratch_shapes=[pltpu.VMEM((tm, hm), BF16)],
        compiler_params=_cparams(("parallel", "parallel", "arbitrary")),
        name="cross_attn",
    )(x.reshape(batch, seq, d), wqk, vwo)
    return out.reshape(batch * seq, d)


def _router_kernel(x_ref, g_ref, whi_ref, wlo_ref, b_ref, xn_ref, info_ref, cnt_ref, carry_ref):
    i = pl.program_id(0)

    @pl.when(i == 0)
    def _():
        carry_ref[...] = jnp.zeros_like(carry_ref)

    x = x_ref[...]
    tm = x.shape[0]
    ms = jnp.mean(x * x, axis=-1, keepdims=True)
    xn = x * lax.rsqrt(ms + RMS_EPS) * g_ref[...]
    half = xn.shape[1] // 2
    xn_ref[...] = _pack_bf16_pair(xn[:, :half], xn[:, half:])
    xh, xl = _split2(xn)
    whi = whi_ref[...]
    logits = (jnp.dot(xh, whi, preferred_element_type=F32) + jnp.dot(xl, whi, preferred_element_type=F32)
              + jnp.dot(xh, wlo_ref[...], preferred_element_type=F32)) + b_ref[...]
    li = lax.broadcasted_iota(jnp.int32, logits.shape, 1)
    neg = jnp.float32(-jnp.inf)
    is_g = li < N_GROUPS
    gl = jnp.where(is_g, logits, neg)
    gm = jnp.max(gl, axis=-1, keepdims=True)
    gi = jnp.min(jnp.where(is_g & (gl == gm), li, 128), axis=-1, keepdims=True)
    gp = 1.0 / jnp.sum(jnp.where(is_g, jnp.exp(gl - gm), 0.0), axis=-1, keepdims=True)
    lo_lane = N_GROUPS + EPG * gi
    sel = (li >= lo_lane) & (li < lo_lane + EPG)
    l1 = jnp.where(sel, logits, neg)
    e1 = jnp.max(l1, axis=-1, keepdims=True)
    i1 = jnp.min(jnp.where(sel & (l1 == e1), li, 128), axis=-1, keepdims=True)
    sel2 = sel & (li != i1)
    l2 = jnp.where(sel2, logits, neg)
    e2 = jnp.max(l2, axis=-1, keepdims=True)
    i2 = jnp.min(jnp.where(sel2 & (l2 == e2), li, 128), axis=-1, keepdims=True)
    ex = jnp.exp(e2 - e1)
    w1 = gp / (1.0 + ex)
    w2 = gp * ex / (1.0 + ex)
    x1 = i1 - N_GROUPS
    x2 = i2 - N_GROUPS
    oh1 = li == x1
    oh2 = li == x2
    oh = (oh1 | oh2).astype(BF16)
    ri = lax.broadcasted_iota(jnp.int32, (tm, tm), 0)
    ci = lax.broadcasted_iota(jnp.int32, (tm, tm), 1)
    before = jnp.dot((ci < ri).astype(BF16), oh, preferred_element_type=F32) + carry_ref[0:1, :]
    r1 = jnp.sum(jnp.where(oh1, before, 0.0), axis=-1, keepdims=True)
    r2 = jnp.sum(jnp.where(oh2, before, 0.0), axis=-1, keepdims=True)
    new_carry = carry_ref[0:1, :] + jnp.sum(oh.astype(F32), axis=0, keepdims=True)
    carry_ref[...] = jnp.broadcast_to(new_carry, carry_ref.shape)
    cnt_ref[...] = jnp.broadcast_to(new_carry, cnt_ref.shape)
    info = jnp.where(li == 0, x1.astype(F32), 0.0)
    info = jnp.where(li == 1, x2.astype(F32), info)
    info = jnp.where(li == 2, w1, info)
    info = jnp.where(li == 3, w2, info)
    info = jnp.where(li == 4, r1, info)
    info = jnp.where(li == 5, r2, info)
    info_ref[...] = info


def _router(x, gain, w_hi, w_lo, bias, tm=256):
    t, d = x.shape
    sd = jax.ShapeDtypeStruct
    return pl.pallas_call(
        _router_kernel,
        grid=(t // tm,),
        in_specs=[pl.BlockSpec((tm, d), lambda i: (i, 0)), pl.BlockSpec((1, d), lambda i: (0, 0)),
                  pl.BlockSpec((d, 128), lambda i: (0, 0)), pl.BlockSpec((d, 128), lambda i: (0, 0)),
                  pl.BlockSpec((1, 128), lambda i: (0, 0))],
        out_specs=[pl.BlockSpec((tm, d // 2), lambda i: (i, 0)), pl.BlockSpec((tm, 128), lambda i: (i, 0)),
                   pl.BlockSpec((8, 128), lambda i: (0, 0))],
        out_shape=[sd((t, d // 2), jnp.uint32), sd((t, 128), F32), sd((8, 128), F32)],
        scratch_shapes=[pltpu.VMEM((8, 128), F32)],
        compiler_params=_cparams(("arbitrary",)),
        name="moe_router",
    )(x, gain.reshape(1, d), w_hi, w_lo, bias)


def _gather_rows(src_hbm, idx_ref, dst_ref, sem, n_rows):
    def body(j, carry):
        pltpu.make_async_copy(src_hbm.at[pl.ds(idx_ref[0, j], 1), :], dst_ref.at[pl.ds(j, 1), :], sem).start()
        return carry
    lax.fori_loop(0, n_rows, body, 0, unroll=8)


def _wait_rows(src_hbm, dst_ref, sem, n_rows):
    pltpu.make_async_copy(src_hbm.at[pl.ds(0, n_rows), :], dst_ref, sem).wait()


def _issue_rows(src_hbm, idx_ref, dst_ref, sem, lo, hi):
    for j in range(lo, hi):
        pltpu.make_async_copy(src_hbm.at[pl.ds(idx_ref[0, j], 1), :], dst_ref.at[pl.ds(j, 1), :], sem).start()


def _expert_up_kernel(be_ref, nused_ref, tok_ref, tokn_ref, x_hbm, wg_ref, wu_ref, h_ref, xbuf, sem):
    i = pl.program_id(0)
    nb = pl.num_programs(0)
    n_used = nused_ref[0]
    slot = i % 2

    @pl.when(i == 0)
    def _():
        _gather_rows(x_hbm, tok_ref, xbuf.at[0], sem.at[0], MOE_BLK)

    @pl.when(i < n_used)
    def _():
        _wait_rows(x_hbm, xbuf.at[slot], sem.at[slot], MOE_BLK)
        half = xbuf.shape[2]
        nk = 2 * half // EXPERT_KC
        per = MOE_BLK // nk
        hg = jnp.zeros((MOE_BLK, D_EXPERT), F32)
        hu = jnp.zeros((MOE_BLK, D_EXPERT), F32)
        for kc in range(nk):
            _issue_rows(x_hbm, tokn_ref, xbuf.at[1 - slot], sem.at[1 - slot], kc * per, (kc + 1) * per)
            ks = slice(kc * EXPERT_KC, (kc + 1) * EXPERT_KC)
            wc = (kc * EXPERT_KC) % half
            xb = _unpack_bf16_pair(xbuf[slot, :, wc:wc + EXPERT_KC])[(kc * EXPERT_KC) // half].astype(BF16)
            hg = hg + jnp.dot(xb, wg_ref[ks, :].astype(BF16), preferred_element_type=F32)
            hu = hu + jnp.dot(xb, wu_ref[ks, :].astype(BF16), preferred_element_type=F32)
        h_ref[...] = (jax.nn.silu(hg) * hu).astype(h_ref.dtype)

        @pl.when(i == nb - 1)
        def _():
            _wait_rows(x_hbm, xbuf.at[1 - slot], sem.at[1 - slot], MOE_BLK)

    @pl.when(i >= n_used)
    def _():
        @pl.when(i == n_used)
        def _():
            _wait_rows(x_hbm, xbuf.at[slot], sem.at[slot], MOE_BLK)

        h_ref[...] = jnp.zeros_like(h_ref)


def _expert_down_kernel(be_ref, nused_ref, h_ref, wd_ref, y_ref):
    i = pl.program_id(0)
    n_used = nused_ref[0]

    @pl.when(i < n_used)
    def _():
        h = h_ref[...]
        half = y_ref.shape[1]
        for nc in range(half // EXPERT_KC):
            ns = slice(nc * EXPERT_KC, (nc + 1) * EXPERT_KC)
            nh = slice(half + nc * EXPERT_KC, half + (nc + 1) * EXPERT_KC)
            lo = jnp.dot(h, wd_ref[:, ns].astype(BF16), preferred_element_type=F32)
            hi = jnp.dot(h, wd_ref[:, nh].astype(BF16), preferred_element_type=F32)
            y_ref[:, ns] = _pack_bf16_pair(lo, hi)

    @pl.when(i >= n_used)
    def _():
        y_ref[...] = jnp.zeros_like(y_ref)


def _experts(block_expert, n_used, tok3, xn, wg, wu, wd):
    nb = tok3.shape[0]
    d = 2 * xn.shape[1]
    tok_spec = pl.BlockSpec((None, 1, MOE_BLK), lambda i, be, nu: (i, 0, 0), memory_space=pltpu.SMEM)
    tokn_spec = pl.BlockSpec((None, 1, MOE_BLK), lambda i, be, nu: (jnp.minimum(i + 1, nb - 1), 0, 0),
                             memory_space=pltpu.SMEM)
    h_buf = pl.pallas_call(
        _expert_up_kernel,
        grid_spec=pltpu.PrefetchScalarGridSpec(
            num_scalar_prefetch=2,
            grid=(nb,),
            in_specs=[tok_spec, tokn_spec, pl.BlockSpec(memory_space=pl.ANY),
                      pl.BlockSpec((None, d, D_EXPERT), lambda i, be, nu: (be[i], 0, 0)),
                      pl.BlockSpec((None, d, D_EXPERT), lambda i, be, nu: (be[i], 0, 0))],
            out_specs=pl.BlockSpec((MOE_BLK, D_EXPERT), lambda i, be, nu: (i, 0)),
            scratch_shapes=[pltpu.VMEM((2, MOE_BLK, d // 2), jnp.uint32), pltpu.SemaphoreType.DMA((2,))],
        ),
        out_shape=jax.ShapeDtypeStruct((nb * MOE_BLK, D_EXPERT), BF16),
        compiler_params=_cparams(("arbitrary",)),
        name="moe_up",
    )(block_expert, n_used, tok3, tok3, xn, wg, wu)
    return pl.pallas_call(
        _expert_down_kernel,
        grid_spec=pltpu.PrefetchScalarGridSpec(
            num_scalar_prefetch=2,
            grid=(nb,),
            in_specs=[pl.BlockSpec((MOE_BLK, D_EXPERT), lambda i, be, nu: (i, 0)),
                      pl.BlockSpec((None, D_EXPERT, d), lambda i, be, nu: (be[i], 0, 0))],
            out_specs=pl.BlockSpec((MOE_BLK, d // 2), lambda i, be, nu: (i, 0)),
        ),
        out_shape=jax.ShapeDtypeStruct((nb * MOE_BLK, d // 2), jnp.uint32),
        compiler_params=_cparams(("arbitrary",)),
        name="moe_down",
    )(block_expert, n_used, h_buf, wd)


def _combine_kernel(d1_ref, d1n_ref, d2_ref, d2n_ref, x_ref, info_ref, g_ref, y_hbm, o_ref, ybuf, sem):
    i = pl.program_id(0)
    n = pl.num_programs(0)
    slot = i % 2

    @pl.when(i == 0)
    def _():
        _gather_rows(y_hbm, d1_ref, ybuf.at[0, 0], sem.at[0], COMB_TM)
        _gather_rows(y_hbm, d2_ref, ybuf.at[0, 1], sem.at[0], COMB_TM)

    _wait_rows(y_hbm, ybuf.at[slot, 0], sem.at[slot], COMB_TM)
    _wait_rows(y_hbm, ybuf.at[slot, 1], sem.at[slot], COMB_TM)
    _issue_rows(y_hbm, d1n_ref, ybuf.at[1 - slot, 0], sem.at[1 - slot], 0, COMB_TM)
    _issue_rows(y_hbm, d2n_ref, ybuf.at[1 - slot, 1], sem.at[1 - slot], 0, COMB_TM)
    info = info_ref[...]
    w1 = info[:, 2:3]
    w2 = info[:, 3:4]
    half = ybuf.shape[3]
    lo1, hi1 = _unpack_bf16_pair(ybuf[slot, 0])
    lo2, hi2 = _unpack_bf16_pair(ybuf[slot, 1])
    xlo = x_ref[:, :half] + lo1 * w1 + lo2 * w2
    xhi = x_ref[:, half:] + hi1 * w1 + hi2 * w2
    ms = (jnp.sum(xlo * xlo, axis=-1, keepdims=True) + jnp.sum(xhi * xhi, axis=-1, keepdims=True)) / (2 * half)
    rstd = lax.rsqrt(ms + RMS_EPS)
    o_ref[:, :half] = xlo * rstd * g_ref[:, :half]
    o_ref[:, half:] = xhi * rstd * g_ref[:, half:]

    @pl.when(i == n - 1)
    def _():
        _wait_rows(y_hbm, ybuf.at[1 - slot, 0], sem.at[1 - slot], COMB_TM)
        _wait_rows(y_hbm, ybuf.at[1 - slot, 1], sem.at[1 - slot], COMB_TM)


def _combine(dest1, dest2, x, info, gain, y_buf):
    t, d = x.shape
    nt = t // COMB_TM
    d1 = dest1.reshape(nt, 1, COMB_TM)
    d2 = dest2.reshape(nt, 1, COMB_TM)
    cur = pl.BlockSpec((None, 1, COMB_TM), lambda i: (i, 0, 0), memory_space=pltpu.SMEM)
    nxt = pl.BlockSpec((None, 1, COMB_TM), lambda i: (jnp.minimum(i + 1, nt - 1), 0, 0),
                       memory_space=pltpu.SMEM)
    return pl.pallas_call(
        _combine_kernel,
        grid=(nt,),
        in_specs=[cur, nxt, cur, nxt,
                  pl.BlockSpec((COMB_TM, d), lambda i: (i, 0)),
                  pl.BlockSpec((COMB_TM, 128), lambda i: (i, 0)),
                  pl.BlockSpec((1, d), lambda i: (0, 0)),
                  pl.BlockSpec(memory_space=pl.ANY)],
        out_specs=pl.BlockSpec((COMB_TM, d), lambda i: (i, 0)),
        out_shape=jax.ShapeDtypeStruct((t, d), F32),
        scratch_shapes=[pltpu.VMEM((2, 2, COMB_TM, d // 2), jnp.uint32), pltpu.SemaphoreType.DMA((2,))],
        compiler_params=_cparams(("arbitrary",)),
        name="moe_combine",
    )(d1, d1, d2, d2, x, info, gain.reshape(1, d), y_buf)


def _head_matrices():
    lane = jnp.arange(D_RWKV) // HEAD
    e = (lane[:, None] == jnp.arange(128)[None, :]).astype(BF16)
    return e, e.T


def _pad_lora_rows(w):
    return jnp.pad(w, ((0, 0), (0, LORA_PAD - w.shape[1]), (0, 0)))


def _layer(x, mem, norm_mix, w_in, shift_taps, rwkv_w0, rwkv_w2, rwkv_a0, rwkv_a2, rwkv_g2, rwkv_k_k,
           rwkv_k_a, rwkv_r_k, rwkv_ln_w, rwkv_ln_b, s5_lam_re, s5_lam_im, s5_log_step, s5_b_re, s5_b_im,
           s5_c_re, s5_c_im, s5_d, s5_glu_w, s5_glu_b, w_out, norm_attn, norm_mem, w_q, w_k, w_v, w_o,
           norm_ffn, router_grp_w, router_grp_b, router_exp_w, router_exp_b, exp_w_gate, exp_w_up,
           exp_w_down, out_gain):
    batch, seq, d = x.shape
    t = batch * seq
    mem_n = mem.shape[1]
    xt = x.reshape(t, d)
    c3 = 3 * D_RWKV
    off_g = c3 + 4 * 96

    def pad_cols(w, lo, width):
        return jnp.pad(w[:, lo:lo + width], ((0, 0), (0, LORA_PAD - width)))

    w_all = jnp.concatenate([w_in[:, :c3], w_in[:, off_g + 256:]]
                            + [pad_cols(w_in, c3 + i * 96, 96) for i in range(4)]
                            + [w_in[:, off_g:off_g + 256], jnp.zeros((d, LORA_BLK - N_LORA), F32)],
                            axis=1).astype(BF16)
    taps_l = jnp.concatenate([pad_cols(shift_taps, c3 + i * 96, 96) for i in range(4)]
                             + [shift_taps[:, off_g:off_g + 256], jnp.zeros((3, LORA_BLK - N_LORA), F32)], axis=1)
    h = _rmsnorm(xt, norm_mix, BF16)
    p_all = _matmul(h, w_all, F32, 1024, LORA_BLK, name="w_in")

    e_mat, et_mat = _head_matrices()
    k_a = rwkv_k_a.reshape(1, D_RWKV)
    r, k, v, kk, lw, a, g = _rwkv_prep(
        p_all, seq, shift_taps[:, :D_RWKV], shift_taps[:, D_RWKV:2 * D_RWKV],
        shift_taps[:, 2 * D_RWKV:c3], taps_l, rwkv_w0, _pad_lora_rows(rwkv_w2).astype(BF16), rwkv_a0,
        _pad_lora_rows(rwkv_a2).astype(BF16), rwkv_g2.astype(BF16), rwkv_k_k.reshape(1, D_RWKV),
        e_mat, et_mat)
    y_scan = _rwkv_scan(r, k, v, kk, lw, a, k_a, batch, seq)
    y_rwkv = _rwkv_post(y_scan, r, k, v, a, g, k_a, rwkv_r_k.reshape(1, D_RWKV),
                        rwkv_ln_w.reshape(1, D_RWKV), rwkv_ln_b.reshape(1, D_RWKV), e_mat, et_mat)

    n_levels = int(math.log2(seq // S5_L))
    a_t, a_z, a_y, mult = _s5_tables(s5_lam_re, s5_lam_im, s5_log_step, s5_b_re, s5_b_im,
                                    s5_c_re, s5_c_im, s5_d, n_levels)
    y_s5 = _s5_mix(p_all, c3 // (S5_GB * S5_CH), a_t, a_z, a_y, mult, batch, seq, n_levels)
    y_glu = _glu(y_s5, s5_glu_w.astype(BF16), s5_glu_b.reshape(1, D_S5))

    x1 = _matmul2_res(y_rwkv, y_glu, w_out.astype(BF16), xt, 1024, 512)

    memn = _rmsnorm(mem.reshape(batch * mem_n, d), norm_mem, BF16)
    wkv = jnp.concatenate([w_k, w_v], axis=1).astype(BF16)
    kv = _matmul(memn, wkv, BF16, 1024, 512, name="w_kv")
    x2 = _cross_attention(x1, kv, (norm_attn[:, None] * w_q).astype(BF16), w_o.astype(BF16), batch, seq, mem_n)

    w_r = jnp.concatenate([router_grp_w, router_exp_w,
                           jnp.zeros((d, 128 - N_GROUPS - N_EXPERTS), F32)], axis=1)
    w_r_hi = w_r.astype(BF16)
    w_r_lo = (w_r - w_r_hi.astype(F32)).astype(BF16)
    b_r = jnp.concatenate([router_grp_b, router_exp_b,
                           jnp.zeros((128 - N_GROUPS - N_EXPERTS,), F32)]).reshape(1, 128)
    xn3, info, cnt = _router(x2, norm_ffn, w_r_hi, w_r_lo, b_r)
    eid = info[:, 0:2].astype(jnp.int32)
    rank = info[:, 4:6].astype(jnp.int32)
    counts = cnt[0, :N_EXPERTS].astype(jnp.int32)
    nblk = (counts + MOE_BLK - 1) // MOE_BLK
    bstart = jnp.cumsum(nblk) - nblk
    n_used = jnp.sum(nblk)
    first = jnp.sum(jnp.where(eid[:, :, None] == jnp.arange(N_EXPERTS)[None, None, :], bstart, 0), axis=-1)
    dest = first * MOE_BLK + rank
    nb = (t * TOP_K) // MOE_BLK + N_EXPERTS
    tok_buf = jnp.zeros((nb * MOE_BLK,), jnp.int32).at[dest.reshape(-1)].set(
        jnp.repeat(jnp.arange(t, dtype=jnp.int32), TOP_K))
    blk = jnp.arange(nb, dtype=jnp.int32)
    block_expert = jnp.sum(blk[:, None] >= (bstart + nblk)[None, :], axis=1).astype(jnp.int32)
    last_e = jnp.max(jnp.where(nblk > 0, jnp.arange(N_EXPERTS), 0)).astype(jnp.int32)
    block_expert = jnp.minimum(block_expert, last_e)
    y_buf = _experts(block_expert, n_used.reshape(1).astype(jnp.int32), tok_buf.reshape(nb, 1, MOE_BLK),
                     xn3, exp_w_gate, exp_w_up, exp_w_down)
    out = _combine(dest[:, 0], dest[:, 1], x2, info, out_gain, y_buf)
    return out.reshape(batch, seq, d)


def kernel(x, mem, norm_mix, w_in, shift_taps, rwkv_w0, rwkv_w2, rwkv_a0, rwkv_a2, rwkv_g2, rwkv_k_k, rwkv_k_a, rwkv_r_k, rwkv_ln_w, rwkv_ln_b, s5_lam_re, s5_lam_im, s5_log_step, s5_b_re, s5_b_im, s5_c_re, s5_c_im, s5_d, s5_glu_w, s5_glu_b, w_out, norm_attn, norm_mem, w_q, w_k, w_v, w_o, norm_ffn, router_grp_w, router_grp_b, router_exp_w, router_exp_b, exp_w_gate, exp_w_up, exp_w_down, norm_final):
    return _layer(x, mem, norm_mix[0], w_in[0], shift_taps[0], rwkv_w0[0], rwkv_w2[0], rwkv_a0[0],
                  rwkv_a2[0], rwkv_g2[0], rwkv_k_k[0], rwkv_k_a[0], rwkv_r_k[0], rwkv_ln_w[0],
                  rwkv_ln_b[0], s5_lam_re[0], s5_lam_im[0], s5_log_step[0], s5_b_re[0], s5_b_im[0],
                  s5_c_re[0], s5_c_im[0], s5_d[0], s5_glu_w[0], s5_glu_b[0], w_out[0], norm_attn[0],
                  norm_mem[0], w_q[0], w_k[0], w_v[0], w_o[0], norm_ffn[0], router_grp_w[0],
                  router_grp_b[0], router_exp_w[0], router_exp_b[0], exp_w_gate[0], exp_w_up[0],
                  exp_w_down[0], norm_final)
```

```python
import functools
import math

import jax
import jax.numpy as jnp
from jax import lax
from jax.experimental import pallas as pl
from jax.experimental.pallas import tpu as pltpu

F32 = jnp.float32
BF16 = jnp.bfloat16

D_MODEL = 4096
D_RWKV = 2048
D_S5 = 2048
HEAD = 64
PAIR = 2 * HEAD
LORA_PAD = 128
N_LORA = 4 * LORA_PAD + 256
LORA_BLK = 1024
N_MAIN = 3 * D_RWKV + D_S5
S5_CH = 16
S5_GROUPS = D_S5 // S5_CH
S5_STATE = 64
S5_L = 8
S5_GB = 8
ATTN_HEADS = 4
ATTN_HEAD_DIM = D_MODEL // ATTN_HEADS
ATTN_SCALE = ATTN_HEAD_DIM ** -0.5
N_GROUPS = 8
EPG = 8
N_EXPERTS = N_GROUPS * EPG
TOP_K = 2
D_EXPERT = D_MODEL // 8
RMS_EPS = 1e-6
GN_EPS = 64e-5
L2_EPS = 1e-12

CHUNK = 64
SLAB = 2048
MOE_BLK = 288
COMB_TM = 128
EXPERT_KC = 1024
VMEM_LIMIT = 56 * 1024 * 1024


def _cparams(sem):
    return pltpu.CompilerParams(dimension_semantics=sem, vmem_limit_bytes=VMEM_LIMIT)


def _bdot(a, b):
    return jnp.dot(a.astype(BF16), b.astype(BF16), preferred_element_type=F32)


def _bdot_nt(a, b):
    return lax.dot_general(a.astype(BF16), b.astype(BF16), (((1,), (1,)), ((), ())),
                           preferred_element_type=F32)


def _split2(x):
    hi = x.astype(BF16)
    lo = (x - hi.astype(F32)).astype(BF16)
    return hi, lo


def _split3(x):
    hi = x.astype(BF16)
    r1 = x - hi.astype(F32)
    mid = r1.astype(BF16)
    lo = (r1 - mid.astype(F32)).astype(BF16)
    return hi, mid, lo


def _pack_bf16_pair(lo, hi):
    lo_b = lax.bitcast_convert_type(lo.astype(BF16).astype(F32), jnp.uint32) >> 16
    hi_b = lax.bitcast_convert_type(hi.astype(BF16).astype(F32), jnp.uint32) & jnp.uint32(0xFFFF0000)
    return hi_b | lo_b


def _unpack_bf16_pair(w):
    return (lax.bitcast_convert_type(w << 16, F32),
            lax.bitcast_convert_type(w & jnp.uint32(0xFFFF0000), F32))


def _dot_split2_rhs(x, e):
    h, l = _split2(x)
    return jnp.dot(h, e, preferred_element_type=F32) + jnp.dot(l, e, preferred_element_type=F32)


def _dot_exact_lhs(e, x):
    h, m, l = _split3(x)
    return (jnp.dot(e, h, preferred_element_type=F32) + jnp.dot(e, m, preferred_element_type=F32)
            + jnp.dot(e, l, preferred_element_type=F32))


def _rms_kernel(x_ref, g_ref, o_ref):
    x = x_ref[...]
    ms = jnp.mean(x * x, axis=-1, keepdims=True)
    o_ref[...] = (x * lax.rsqrt(ms + RMS_EPS) * g_ref[...]).astype(o_ref.dtype)


def _rmsnorm(x, gain, out_dtype, tm=256):
    t, d = x.shape
    return pl.pallas_call(
        _rms_kernel,
        grid=(t // tm,),
        in_specs=[pl.BlockSpec((tm, d), lambda i: (i, 0)), pl.BlockSpec((1, d), lambda i: (0, 0))],
        out_specs=pl.BlockSpec((tm, d), lambda i: (i, 0)),
        out_shape=jax.ShapeDtypeStruct((t, d), out_dtype),
        compiler_params=_cparams(("parallel",)),
        name="rmsnorm",
    )(x, gain.reshape(1, d))


def _mm_kernel(a_ref, b_ref, o_ref):
    o_ref[...] = jnp.dot(a_ref[...], b_ref[...], preferred_element_type=F32).astype(o_ref.dtype)


def _mm2_res_kernel(a1_ref, a2_ref, b_ref, r_ref, o_ref):
    k1 = a1_ref.shape[1]
    acc = jnp.dot(a1_ref[...], b_ref[:k1, :], preferred_element_type=F32)
    acc = acc + jnp.dot(a2_ref[...], b_ref[k1:, :], preferred_element_type=F32)
    o_ref[...] = (acc + r_ref[...]).astype(o_ref.dtype)


def _matmul(a, b, out_dtype, tm, tn, name):
    m, k = a.shape
    n = b.shape[1]
    tm, tn = min(tm, m), min(tn, n)
    return pl.pallas_call(
        _mm_kernel,
        grid=(m // tm, n // tn),
        in_specs=[pl.BlockSpec((tm, k), lambda i, j: (i, 0)), pl.BlockSpec((k, tn), lambda i, j: (0, j))],
        out_specs=pl.BlockSpec((tm, tn), lambda i, j: (i, j)),
        out_shape=jax.ShapeDtypeStruct((m, n), out_dtype),
        compiler_params=_cparams(("parallel", "parallel")),
        name=name,
    )(a, b)


def _matmul2_res(a1, a2, b, res, tm, tn):
    m, k1 = a1.shape
    k2 = a2.shape[1]
    n = b.shape[1]
    tm = min(tm, m)
    return pl.pallas_call(
        _mm2_res_kernel,
        grid=(m // tm, n // tn),
        in_specs=[pl.BlockSpec((tm, k1), lambda i, j: (i, 0)),
                  pl.BlockSpec((tm, k2), lambda i, j: (i, 0)),
                  pl.BlockSpec((k1 + k2, tn), lambda i, j: (0, j)),
                  pl.BlockSpec((tm, tn), lambda i, j: (i, j))],
        out_specs=pl.BlockSpec((tm, tn), lambda i, j: (i, j)),
        out_shape=jax.ShapeDtypeStruct((m, n), F32),
        compiler_params=_cparams(("parallel", "parallel")),
        name="w_out",
    )(a1, a2, b, res)


def _head_sum_bcast(x, e_ref, et_ref):
    s = _dot_split2_rhs(x, e_ref[...])
    return _dot_split2_rhs(s, et_ref[...])


def _shift3(x, hp, hn, taps, first, last):
    tm = x.shape[0]
    row = lax.broadcasted_iota(jnp.int32, x.shape, 0)
    prev_edge = jnp.where(first, 0.0, hp[7:8, :])
    next_edge = jnp.where(last, 0.0, hn[0:1, :])
    prev = jnp.where(row == 0, prev_edge, pltpu.roll(x, 1, 0))
    nxt = jnp.where(row == tm - 1, next_edge, pltpu.roll(x, tm - 1, 0))
    return taps[0:1, :] * prev + taps[1:2, :] * x + taps[2:3, :] * nxt


def _prep_kernel(seq_tiles,
                 r_ref, rp_ref, rn_ref, k_ref, kp_ref, kn_ref, v_ref, vp_ref, vn_ref,
                 lo_ref, lop_ref, lon_ref,
                 tr_ref, tk_ref, tv_ref, tl_ref,
                 w0_ref, w2_ref, a0_ref, a2_ref, g2_ref, kk_ref_p, e_ref, et_ref,
                 ro_ref, ko_ref, vo_ref, kko_ref, lw_ref, a_ref, g_ref):
    i = pl.program_id(0)
    first = (i % seq_tiles) == 0
    last = (i % seq_tiles) == seq_tiles - 1
    r = _shift3(r_ref[...], rp_ref[...], rn_ref[...], tr_ref[...], first, last)
    k = _shift3(k_ref[...], kp_ref[...], kn_ref[...], tk_ref[...], first, last)
    v = _shift3(v_ref[...], vp_ref[...], vn_ref[...], tv_ref[...], first, last)
    lo = _shift3(lo_ref[...], lop_ref[...], lon_ref[...], tl_ref[...], first, last)
    ro_ref[...] = r.astype(ro_ref.dtype)
    ko_ref[...] = k.astype(ko_ref.dtype)
    vo_ref[...] = v.astype(vo_ref.dtype)
    for d in range(2):
        xw = lo[:, d * LORA_PAD:(d + 1) * LORA_PAD]
        xa = lo[:, (2 + d) * LORA_PAD:(3 + d) * LORA_PAD]
        wl = w0_ref[d:d + 1, :] + _bdot(jnp.tanh(xw), w2_ref[d])
        lw_ref[d] = -math.exp(-0.5) * jax.nn.sigmoid(wl)
        a_ref[d] = jax.nn.sigmoid(a0_ref[d:d + 1, :] + _bdot(xa, a2_ref[d])).astype(a_ref.dtype)
    xg = lo[:, 4 * LORA_PAD:N_LORA]
    g_ref[...] = _bdot(jax.nn.sigmoid(xg), g2_ref[...]).astype(g_ref.dtype)
    kk = k * kk_ref_p[...]
    ssq = _head_sum_bcast(kk * kk, e_ref, et_ref)
    kko_ref[...] = (kk / jnp.maximum(jnp.sqrt(ssq), L2_EPS)).astype(kko_ref.dtype)


def _rwkv_prep(p_all, seq, taps_r, taps_k, taps_v, taps_l, w0, w2p, a0, a2p, g2, k_k, e_mat, et_mat,
               tm=128):
    t = p_all.shape[0]
    nt8 = t // 8
    seq_tiles = seq // tm
    c = D_RWKV

    def main_spec(col):
        return [pl.BlockSpec((tm, c), lambda i, col=col: (i, col)),
                pl.BlockSpec((8, c), lambda i, col=col: (jnp.maximum(i * (tm // 8) - 1, 0), col)),
                pl.BlockSpec((8, c), lambda i, col=col: (jnp.minimum((i + 1) * (tm // 8), nt8 - 1), col))]

    lcol = N_MAIN // LORA_BLK
    lora_spec = [pl.BlockSpec((tm, LORA_BLK), lambda i: (i, lcol)),
                 pl.BlockSpec((8, LORA_BLK), lambda i: (jnp.maximum(i * (tm // 8) - 1, 0), lcol)),
                 pl.BlockSpec((8, LORA_BLK), lambda i: (jnp.minimum((i + 1) * (tm // 8), nt8 - 1), lcol))]

    def full(shape):
        nd = len(shape)
        return pl.BlockSpec(shape, lambda i, nd=nd: (0,) * nd)

    in_specs = (main_spec(0) + main_spec(1) + main_spec(2) + lora_spec
                + [full((3, c)), full((3, c)), full((3, c)), full((3, LORA_BLK)),
                   full((2, c)), full((2, LORA_PAD, c)), full((2, c)), full((2, LORA_PAD, c)),
                   full((256, c)), full((1, c)), full((c, 128)), full((128, c))])
    row = pl.BlockSpec((tm, c), lambda i: (i, 0))
    row2 = pl.BlockSpec((2, tm, c), lambda i: (0, i, 0))
    sd = jax.ShapeDtypeStruct
    return pl.pallas_call(
        functools.partial(_prep_kernel, seq_tiles),
        grid=(t // tm,),
        in_specs=in_specs,
        out_specs=[row, row, row, row, row2, row2, row],
        out_shape=[sd((t, c), BF16), sd((t, c), BF16), sd((t, c), BF16), sd((t, c), BF16),
                   sd((2, t, c), F32), sd((2, t, c), BF16), sd((t, c), BF16)],
        compiler_params=_cparams(("parallel",)),
        name="rwkv_prep",
    )(p_all, p_all, p_all, p_all, p_all, p_all, p_all, p_all, p_all,
      p_all, p_all, p_all,
      taps_r, taps_k, taps_v, taps_l, w0, w2p, a0, a2p, g2, k_k, e_mat, et_mat)


def _scan_kernel(r_ref, k_ref, v_ref, kk_ref, lw_ref, a_ref, ka_ref, y_ref, st_ref):
    d = pl.program_id(0)
    c = pl.program_id(3)

    @pl.when(c == 0)
    def _():
        st_ref[...] = jnp.zeros_like(st_ref)

    ti = lax.broadcasted_iota(jnp.int32, (CHUNK, PAIR), 0)
    si = lax.broadcasted_iota(jnp.int32, (CHUNK, PAIR), 1) % HEAD
    ahead = (ti - si) * (1 - 2 * d)
    incl = ahead >= 0
    strict = ahead > 0
    eye = (si == ti).astype(F32)

    lw = lw_ref[...]
    cum_incl = _dot_exact_lhs(incl[:, :CHUNK].astype(BF16), lw)
    tot = jnp.sum(lw, axis=0, keepdims=True)
    e_incl = jnp.exp(cum_incl)
    e_excl = jnp.exp(cum_incl - lw)
    e_ninc = jnp.exp(-cum_incl)
    g_tot = jnp.exp(tot)

    a = a_ref[...].astype(F32)
    kk = kk_ref[...].astype(F32)
    kd = k_ref[...].astype(F32) * (1.0 + (a - 1.0) * ka_ref[...])
    at_all = kk * e_excl
    rt_all = r_ref[...].astype(F32) * e_incl
    bt_all = (kk * a) * e_ninc
    kt_all = kd * e_ninc
    bh_all = bt_all * g_tot
    kh_all = kt_all * g_tot
    v_all = v_ref[...].astype(F32)

    npair = SLAB // PAIR
    lane = lax.broadcasted_iota(jnp.int32, (1, PAIR), 1)
    head0 = lane < HEAD

    def bd(y):
        return jnp.concatenate([jnp.where(head0, y, 0.0), jnp.where(head0, 0.0, y)], axis=0)

    sls = [slice(p * PAIR, (p + 1) * PAIR) for p in range(npair)]
    ars = [jnp.concatenate([at_all[:, sl], rt_all[:, sl]], axis=0).astype(BF16) for sl in sls]
    s_ps = [st_ref[p] for p in range(npair)]
    gs = [_bdot_nt(ars[p], jnp.concatenate([bd(bt_all[:, sls[p]]), bd(kt_all[:, sls[p]]), s_ps[p]], axis=0))
          for p in range(npair)]
    ms = [jnp.where(strict, -gs[p][:CHUNK, :PAIR], 0.0) for p in range(npair)]
    tinvs = [eye + ms[p] for p in range(npair)]
    ms = [_bdot(m, bd(m)) for m in ms]
    for _ in range(4):
        ps = [_bdot(jnp.concatenate([tinvs[p], ms[p]], axis=0), bd(ms[p])) for p in range(npair)]
        tinvs = [tinvs[p] + ps[p][:CHUNK] for p in range(npair)]
        ms = [ps[p][CHUNK:] for p in range(npair)]
    tinvs = [tinvs[p] + _bdot(tinvs[p], bd(ms[p])) for p in range(npair)]
    x2s = [_bdot(jnp.concatenate([jnp.where(strict, gs[p][:CHUNK, PAIR:2 * PAIR], 0.0),
                                  jnp.where(incl, gs[p][CHUNK:, PAIR:2 * PAIR], 0.0)], axis=0),
                 bd(v_all[:, sls[p]])) for p in range(npair)]
    us = [-_bdot(tinvs[p], bd(gs[p][:CHUNK, 2 * PAIR:] + x2s[p][:CHUNK])) for p in range(npair)]
    outs = [gs[p][CHUNK:, 2 * PAIR:] + x2s[p][CHUNK:]
            + _bdot(jnp.where(incl, gs[p][CHUNK:, :PAIR], 0.0), bd(us[p])) for p in range(npair)]
    row = lax.broadcasted_iota(jnp.int32, (PAIR, PAIR), 0)
    same_head = (row < HEAD) == (lax.broadcasted_iota(jnp.int32, (PAIR, PAIR), 1) < HEAD)
    for p in range(npair):
        uv = jnp.concatenate([us[p], v_all[:, sls[p]]], axis=0).astype(BF16)
        bk = jnp.concatenate([bh_all[:, sls[p]], kh_all[:, sls[p]]], axis=0).astype(BF16)
        upd = lax.dot_general(uv, bk, (((0,), (0,)), ((), ())), preferred_element_type=F32)
        st_ref[p] = s_ps[p] * g_tot[:, sls[p]] + jnp.where(same_head, upd, 0.0)
    y_ref[...] = jnp.concatenate(outs, axis=1)


def _rwkv_scan(r, k, v, kk, lw, a, k_a, batch, seq):
    nc = seq // CHUNK
    ns = D_RWKV // SLAB
    r4 = r.reshape(batch, seq, D_RWKV)
    k4 = k.reshape(batch, seq, D_RWKV)
    v4 = v.reshape(batch, seq, D_RWKV)
    kk4 = kk.reshape(batch, seq, D_RWKV)
    lw5 = lw.reshape(2, batch, seq, D_RWKV)
    a5 = a.reshape(2, batch, seq, D_RWKV)

    def tchunk(d, c):
        return c + d * (nc - 1 - 2 * c)

    shared = pl.BlockSpec((None, CHUNK, SLAB), lambda d, b, s, c: (b, tchunk(d, c), s))
    perdir = pl.BlockSpec((None, None, CHUNK, SLAB), lambda d, b, s, c: (d, b, tchunk(d, c), s))
    y = pl.pallas_call(
        _scan_kernel,
        grid=(2, batch, ns, nc),
        in_specs=[shared, shared, shared, shared, perdir, perdir,
                  pl.BlockSpec((1, SLAB), lambda d, b, s, c: (0, s))],
        out_specs=perdir,
        out_shape=jax.ShapeDtypeStruct((2, batch, seq, D_RWKV), F32),
        scratch_shapes=[pltpu.VMEM((SLAB // PAIR, PAIR, PAIR), F32)],
        compiler_params=_cparams(("parallel", "parallel", "parallel", "arbitrary")),
        name="rwkv_scan",
    )(r4, k4, v4, kk4, lw5, a5, k_a)
    return y.reshape(2, batch * seq, D_RWKV)


def _post_kernel(y_ref, r_ref, k_ref, v_ref, a_ref, g_ref, ka_ref, rk_ref, lnw_ref, lnb_ref,
                 e_ref, et_ref, o_ref):
    y = y_ref[0] + y_ref[1]
    mu = _head_sum_bcast(y, e_ref, et_ref) * (1.0 / HEAD)
    yc = y - mu
    var = _head_sum_bcast(yc * yc, e_ref, et_ref) * (1.0 / HEAD)
    yn = yc * lax.rsqrt(var + GN_EPS) * lnw_ref[...] + lnb_ref[...]
    ka = ka_ref[...]
    k = k_ref[...].astype(F32)
    kd_sum = (k * (1.0 + (a_ref[0].astype(F32) - 1.0) * ka)
              + k * (1.0 + (a_ref[1].astype(F32) - 1.0) * ka))
    bonus = (_head_sum_bcast(r_ref[...].astype(F32) * kd_sum * rk_ref[...], e_ref, et_ref)
             * v_ref[...].astype(F32))
    o_ref[...] = ((yn + bonus) * g_ref[...].astype(F32)).astype(o_ref.dtype)


def _rwkv_post(y, r, k, v, a, g, k_a, r_k, ln_w, ln_b, e_mat, et_mat, tm=128):
    t, c = r.shape
    row = pl.BlockSpec((tm, c), lambda i: (i, 0))
    row2 = pl.BlockSpec((2, tm, c), lambda i: (0, i, 0))
    par = pl.BlockSpec((1, c), lambda i: (0, 0))
    return pl.pallas_call(
        _post_kernel,
        grid=(t // tm,),
        in_specs=[row2, row, row, row, row2, row, par, par, par, par,
                  pl.BlockSpec((c, 128), lambda i: (0, 0)), pl.BlockSpec((128, c), lambda i: (0, 0))],
        out_specs=row,
        out_shape=jax.ShapeDtypeStruct((t, c), BF16),
        compiler_params=_cparams(("parallel",)),
        name="rwkv_post",
    )(y, r, k, v, a, g, k_a, r_k, ln_w, ln_b, e_mat, et_mat)


def _s5_tables(lam_re, lam_im, log_step, b_re, b_im, c_re, c_im, d_skip, n_levels):
    L, P, CH, G, GB = S5_L, S5_STATE, S5_CH, S5_GROUPS, S5_GB
    NB = G // GB
    hi = lax.Precision.HIGHEST
    lam = lax.complex(lam_re.astype(F32), lam_im.astype(F32))
    dt = jnp.exp(log_step.astype(F32))[..., None]
    lam_dt = lam * dt
    lam_bar = jnp.exp(lam_dt)
    b = lax.complex(b_re.astype(F32), b_im.astype(F32))
    b_bar = ((lam_bar - 1.0) / lam)[..., None] * b
    c = lax.complex(c_re.astype(F32), c_im.astype(F32))
    taus = jnp.arange(L + 1, dtype=F32)
    pows = jnp.exp(lam_dt[:, :, None, :] * taus[None, None, :, None])
    kern = jnp.real(jnp.einsum('dgop,dgtp,dgpi->dgtoi', c, pows[:, :, :L], b_bar, precision=hi))
    j = jnp.arange(L)[:, None]
    t = jnp.arange(L)[None, :]
    lag = jnp.arange(L)[None, None, :]
    sel_f = ((t - j)[:, :, None] == lag).astype(F32)
    sel_b = ((j - t)[:, :, None] == lag).astype(F32)
    skip = (jnp.eye(L)[None, :, None, :, None] * jnp.eye(CH)[None, None, :, None, :]
            * d_skip.astype(F32).reshape(G, 1, CH, 1, 1))
    tsmall = (jnp.einsum('jtl,gloi->gjito', sel_f, kern[0], precision=hi)
              + jnp.einsum('jtl,gloi->gjito', sel_b, kern[1], precision=hi) + skip)
    def by_block(x, lead):
        x = x.reshape((NB, GB) + x.shape[1:])
        perm = (0,) + tuple(range(2, 2 + lead)) + (1,) + tuple(range(2 + lead, x.ndim))
        return x.transpose(perm)

    a_t = by_block(tsmall.reshape(G, L, CH, L * CH), 1).reshape(NB, L * GB * CH, L * CH)
    desc = jnp.exp(lam_dt[:, :, None, :] * (L - taus[:L])[None, None, :, None])
    desc1 = jnp.exp(lam_dt[0][:, None, :] * (L - 1 - taus[:L])[None, :, None])
    bb_t = b_bar.transpose(0, 1, 3, 2)
    zf = desc1[:, :, None, :] * bb_t[0][:, None]
    zb = pows[1][:, :L][:, :, None, :] * bb_t[1][:, None]
    wzs = jnp.stack([jnp.real(zf), jnp.imag(zf), jnp.real(zb), jnp.imag(zb)], axis=3)
    a_z = by_block(wzs.reshape(G, L, CH, 4 * P), 1).reshape(NB, L * GB * CH, 4 * P)
    c_t = c.transpose(0, 1, 3, 2)
    yf = c_t[0][:, :, None, :] * pows[0][:, 1:].transpose(0, 2, 1)[:, :, :, None]
    yb = c_t[1][:, :, None, :] * desc[1].transpose(0, 2, 1)[:, :, :, None]
    wys = jnp.stack([jnp.real(yf), -jnp.imag(yf), jnp.real(yb), -jnp.imag(yb)], axis=1)
    a_y = by_block(wys.reshape(G, 4, P, L * CH), 1).reshape(NB, 4 * GB * P, L * CH)
    steps = L * (2.0 ** jnp.arange(n_levels, dtype=F32))
    lp = jnp.exp(lam_dt[:, :, None, :] * steps[None, None, :, None])
    m = jnp.stack([jnp.real(lp[0]), jnp.imag(lp[0]), jnp.real(lp[1]), jnp.imag(lp[1])], axis=0)
    mult = m.reshape(4, NB, GB, n_levels, P).transpose(1, 3, 0, 2, 4).reshape(NB, n_levels, 4, GB * P)
    return a_t.astype(BF16), a_z.astype(BF16), a_y.astype(BF16), mult


def _group_of(shape, dim, width):
    return (lax.broadcasted_iota(jnp.int32, shape, dim) // width) % S5_GB


def _s5_kernel(n_levels, x_ref, at_ref, az_ref, ay_ref, rt_ref, rz_ref, mult_ref, y_ref, t_s, wz_s, wy_s):
    nck = x_ref.shape[0] // S5_L
    lanes = x_ref.shape[1]
    w = S5_GB * S5_STATE

    @pl.when(pl.program_id(1) == 0)
    def _():
        t = jnp.dot(at_ref[...], rt_ref[...], preferred_element_type=F32)
        keep = _group_of(t.shape, 0, S5_CH) == _group_of(t.shape, 1, S5_CH)
        t_s[...] = jnp.where(keep, t, 0.0).astype(BF16)
        z = jnp.dot(az_ref[...], rz_ref[...], preferred_element_type=F32)
        keep = _group_of(z.shape, 0, S5_CH) == _group_of(z.shape, 1, S5_STATE)
        wz_s[...] = jnp.where(keep, z, 0.0).astype(BF16)
        yy = jnp.dot(ay_ref[...], rt_ref[...], preferred_element_type=F32)
        keep = _group_of(yy.shape, 0, S5_STATE) == _group_of(yy.shape, 1, S5_CH)
        wy_s[...] = jnp.where(keep, yy, 0.0).astype(BF16)

    xcat = jnp.concatenate([x_ref[pl.ds(tl, nck, stride=S5_L), :].astype(BF16) for tl in range(S5_L)],
                           axis=1)
    z = jnp.dot(xcat, wz_s[...], preferred_element_type=F32)
    fre, fim, bre, bim = z[:, :w], z[:, w:2 * w], z[:, 2 * w:3 * w], z[:, 3 * w:]
    cidx = lax.broadcasted_iota(jnp.int32, (nck, w), 0)
    for i in range(n_levels):
        sh = 1 << i
        lfr, lfi = mult_ref[i, 0:1, :], mult_ref[i, 1:2, :]
        lbr, lbi = mult_ref[i, 2:3, :], mult_ref[i, 3:4, :]
        keep = cidx >= sh
        sr = jnp.where(keep, pltpu.roll(fre, sh, 0), 0.0)
        si = jnp.where(keep, pltpu.roll(fim, sh, 0), 0.0)
        fre, fim = fre + sr * lfr - si * lfi, fim + sr * lfi + si * lfr
        keep = cidx < nck - sh
        sr = jnp.where(keep, pltpu.roll(bre, nck - sh, 0), 0.0)
        si = jnp.where(keep, pltpu.roll(bim, nck - sh, 0), 0.0)
        bre, bim = bre + sr * lbr - si * lbi, bim + sr * lbi + si * lbr
    has_prev = cidx >= 1
    has_next = cidx < nck - 1
    xin = jnp.concatenate([jnp.where(has_prev, pltpu.roll(fre, 1, 0), 0.0),
                           jnp.where(has_prev, pltpu.roll(fim, 1, 0), 0.0),
                           jnp.where(has_next, pltpu.roll(bre, nck - 1, 0), 0.0),
                           jnp.where(has_next, pltpu.roll(bim, nck - 1, 0), 0.0)], axis=1).astype(BF16)
    y = jnp.dot(xcat, t_s[...], preferred_element_type=F32)
    y = y + jnp.dot(xin, wy_s[...], preferred_element_type=F32)
    for tl in range(S5_L):
        y_ref[pl.ds(tl, nck, stride=S5_L), :] = y[:, tl * lanes:(tl + 1) * lanes]


def _s5_mix(p_main, col0, a_t, a_z, a_y, mult, batch, seq, n_levels):
    nb = a_t.shape[0]
    lanes = S5_GB * S5_CH
    kw = S5_L * lanes
    sw = 4 * S5_GB * S5_STATE
    r_t = (jnp.arange(S5_L * S5_CH)[:, None] ==
           (jnp.arange(kw)[None, :] // lanes) * S5_CH + jnp.arange(kw)[None, :] % S5_CH).astype(BF16)
    r_z = (jnp.arange(4 * S5_STATE)[:, None] ==
           (jnp.arange(sw)[None, :] // (S5_GB * S5_STATE)) * S5_STATE + jnp.arange(sw)[None, :] % S5_STATE
           ).astype(BF16)
    return pl.pallas_call(
        functools.partial(_s5_kernel, n_levels),
        grid=(nb, batch),
        in_specs=[pl.BlockSpec((seq, lanes), lambda j, b: (b, col0 + j)),
                  pl.BlockSpec((None, kw, S5_L * S5_CH), lambda j, b: (j, 0, 0)),
                  pl.BlockSpec((None, kw, 4 * S5_STATE), lambda j, b: (j, 0, 0)),
                  pl.BlockSpec((None, sw, S5_L * S5_CH), lambda j, b: (j, 0, 0)),
                  pl.BlockSpec((S5_L * S5_CH, kw), lambda j, b: (0, 0)),
                  pl.BlockSpec((4 * S5_STATE, sw), lambda j, b: (0, 0)),
                  pl.BlockSpec((None, n_levels, 4, S5_GB * S5_STATE), lambda j, b: (j, 0, 0, 0))],
        out_specs=pl.BlockSpec((seq, lanes), lambda j, b: (b, j)),
        out_shape=jax.ShapeDtypeStruct((batch * seq, D_S5), F32),
        scratch_shapes=[pltpu.VMEM((kw, kw), BF16), pltpu.VMEM((kw, sw), BF16), pltpu.VMEM((sw, kw), BF16)],
        compiler_params=_cparams(("parallel", "arbitrary")),
        name="s5_mix",
    )(p_main, a_t, a_z, a_y, r_t, r_z, mult)


def _glu_kernel(y_ref, w_ref, b_ref, o_ref):
    y = y_ref[...]
    z = jnp.dot(jax.nn.gelu(y).astype(BF16), w_ref[...], preferred_element_type=F32) + b_ref[...]
    o_ref[...] = (y * jax.nn.sigmoid(z)).astype(o_ref.dtype)


def _glu(y, w, b, tm=512):
    t, c = y.shape
    tm = min(tm, t)
    return pl.pallas_call(
        _glu_kernel,
        grid=(t // tm,),
        in_specs=[pl.BlockSpec((tm, c), lambda i: (i, 0)), pl.BlockSpec((c, c), lambda i: (0, 0)),
                  pl.BlockSpec((1, c), lambda i: (0, 0))],
        out_specs=pl.BlockSpec((tm, c), lambda i: (i, 0)),
        out_shape=jax.ShapeDtypeStruct((t, c), BF16),
        compiler_params=_cparams(("parallel",)),
        name="s5_glu",
    )(y, w, b)


def _wqk_kernel(wq_ref, k_ref, o_ref):
    acc = lax.dot_general(wq_ref[...], k_ref[...], (((1,), (1,)), ((), ())), preferred_element_type=F32)
    o_ref[...] = (acc * ATTN_SCALE).astype(o_ref.dtype)


def _vwo_kernel(v_ref, wo_ref, o_ref):
    o_ref[...] = jnp.dot(v_ref[...], wo_ref[...], preferred_element_type=F32).astype(o_ref.dtype)


def _attn_kernel(mem, x_ref, wqk_ref, vwo_ref, o_ref, p_ref):
    j = pl.program_id(2)
    tn = o_ref.shape[1]

    @pl.when(j == 0)
    def _():
        x = x_ref[...]
        rstd = lax.rsqrt(jnp.mean(x * x, axis=-1, keepdims=True) + RMS_EPS)
        s = jnp.dot(x.astype(BF16), wqk_ref[...], preferred_element_type=F32) * rstd
        for h in range(ATTN_HEADS):
            sh = s[:, h * mem:(h + 1) * mem]
            sh = sh - jnp.max(sh, axis=-1, keepdims=True)
            p = jnp.exp(sh)
            p_ref[:, h * mem:(h + 1) * mem] = (p / jnp.sum(p, axis=-1, keepdims=True)).astype(BF16)

    res = x_ref[:, pl.ds(pl.multiple_of(j * tn, tn), tn)]
    o_ref[...] = jnp.dot(p_ref[...], vwo_ref[...], preferred_element_type=F32) + res


def _cross_attention(x, kv, w_q, w_o, batch, seq, mem, tm=512, tn=1024):
    d = D_MODEL
    hm = ATTN_HEADS * mem
    tm = min(tm, seq)
    wqk = pl.pallas_call(
        _wqk_kernel,
        grid=(batch, ATTN_HEADS),
        in_specs=[pl.BlockSpec((d, ATTN_HEAD_DIM), lambda b, h: (0, h)),
                  pl.BlockSpec((mem, ATTN_HEAD_DIM), lambda b, h: (b, h))],
        out_specs=pl.BlockSpec((None, d, mem), lambda b, h: (b, 0, h)),
        out_shape=jax.ShapeDtypeStruct((batch, d, hm), BF16),
        compiler_params=_cparams(("parallel", "parallel")),
        name="attn_wqk",
    )(w_q, kv)
    vwo = pl.pallas_call(
        _vwo_kernel,
        grid=(batch, ATTN_HEADS),
        in_specs=[pl.BlockSpec((mem, ATTN_HEAD_DIM), lambda b, h: (b, ATTN_HEADS + h)),
                  pl.BlockSpec((ATTN_HEAD_DIM, d), lambda b, h: (h, 0))],
        out_specs=pl.BlockSpec((None, mem, d), lambda b, h: (b, h, 0)),
        out_shape=jax.ShapeDtypeStruct((batch, hm, d), BF16),
        compiler_params=_cparams(("parallel", "parallel")),
        name="attn_vwo",
    )(kv, w_o)
    out = pl.pallas_call(
        functools.partial(_attn_kernel, mem),
        grid=(batch, seq // tm, d // tn),
        in_specs=[pl.BlockSpec((None, tm, d), lambda b, i, j: (b, i, 0)),
                  pl.BlockSpec((None, d, hm), lambda b, i, j: (b, 0, 0)),
                  pl.BlockSpec((None, hm, tn), lambda b, i, j: (b, 0, j))],
        out_specs=pl.BlockSpec((None, tm, tn), lambda b, i, j: (b, i, j)),
        out_shape=jax.ShapeDtypeStruct((batch, seq, d), F32),
        scratch_shapes=[pltpu.VMEM((tm, hm), BF16)],
        compiler_params=_cparams(("parallel", "parallel", "arbitrary")),
        name="cross_attn",
    )(x.reshape(batch, seq, d), wqk, vwo)
    return out.reshape(batch * seq, d)


def _router_kernel(x_ref, g_ref, whi_ref, wlo_ref, b_ref, xn_ref, info_ref, cnt_ref, carry_ref):
    i = pl.program_id(0)

    @pl.when(i == 0)
    def _():
        carry_ref[...] = jnp.zeros_like(carry_ref)

    x = x_ref[...]
    tm = x.shape[0]
    ms = jnp.mean(x * x, axis=-1, keepdims=True)
    xn = x * lax.rsqrt(ms + RMS_EPS) * g_ref[...]
    half = xn.shape[1] // 2
    xn_ref[...] = _pack_bf16_pair(xn[:, :half], xn[:, half:])
    xh, xl = _split2(xn)
    whi = whi_ref[...]
    logits = (jnp.dot(xh, whi, preferred_element_type=F32) + jnp.dot(xl, whi, preferred_element_type=F32)
              + jnp.dot(xh, wlo_ref[...], preferred_element_type=F32)) + b_ref[...]
    li = lax.broadcasted_iota(jnp.int32, logits.shape, 1)
    neg = jnp.float32(-jnp.inf)
    is_g = li < N_GROUPS
    gl = jnp.where(is_g, logits, neg)
    gm = jnp.max(gl, axis=-1, keepdims=True)
    gi = jnp.min(jnp.where(is_g & (gl == gm), li, 128), axis=-1, keepdims=True)
    gp = 1.0 / jnp.sum(jnp.where(is_g, jnp.exp(gl - gm), 0.0), axis=-1, keepdims=True)
    lo_lane = N_GROUPS + EPG * gi
    sel = (li >= lo_lane) & (li < lo_lane + EPG)
    l1 = jnp.where(sel, logits, neg)
    e1 = jnp.max(l1, axis=-1, keepdims=True)
    i1 = jnp.min(jnp.where(sel & (l1 == e1), li, 128), axis=-1, keepdims=True)
    sel2 = sel & (li != i1)
    l2 = jnp.where(sel2, logits, neg)
    e2 = jnp.max(l2, axis=-1, keepdims=True)
    i2 = jnp.min(jnp.where(sel2 & (l2 == e2), li, 128), axis=-1, keepdims=True)
    ex = jnp.exp(e2 - e1)
    w1 = gp / (1.0 + ex)
    w2 = gp * ex / (1.0 + ex)
    x1 = i1 - N_GROUPS
    x2 = i2 - N_GROUPS
    oh1 = li == x1
    oh2 = li == x2
    oh = (oh1 | oh2).astype(BF16)
    ri = lax.broadcasted_iota(jnp.int32, (tm, tm), 0)
    ci = lax.broadcasted_iota(jnp.int32, (tm, tm), 1)
    before = jnp.dot((ci < ri).astype(BF16), oh, preferred_element_type=F32) + carry_ref[0:1, :]
    r1 = jnp.sum(jnp.where(oh1, before, 0.0), axis=-1, keepdims=True)
    r2 = jnp.sum(jnp.where(oh2, before, 0.0), axis=-1, keepdims=True)
    new_carry = carry_ref[0:1, :] + jnp.sum(oh.astype(F32), axis=0, keepdims=True)
    carry_ref[...] = jnp.broadcast_to(new_carry, carry_ref.shape)
    cnt_ref[...] = jnp.broadcast_to(new_carry, cnt_ref.shape)
    info = jnp.where(li == 0, x1.astype(F32), 0.0)
    info = jnp.where(li == 1, x2.astype(F32), info)
    info = jnp.where(li == 2, w1, info)
    info = jnp.where(li == 3, w2, info)
    info = jnp.where(li == 4, r1, info)
    info = jnp.where(li == 5, r2, info)
    info_ref[...] = info


def _router(x, gain, w_hi, w_lo, bias, tm=256):
    t, d = x.shape
    sd = jax.ShapeDtypeStruct
    return pl.pallas_call(
        _router_kernel,
        grid=(t // tm,),
        in_specs=[pl.BlockSpec((tm, d), lambda i: (i, 0)), pl.BlockSpec((1, d), lambda i: (0, 0)),
                  pl.BlockSpec((d, 128), lambda i: (0, 0)), pl.BlockSpec((d, 128), lambda i: (0, 0)),
                  pl.BlockSpec((1, 128), lambda i: (0, 0))],
        out_specs=[pl.BlockSpec((tm, d // 2), lambda i: (i, 0)), pl.BlockSpec((tm, 128), lambda i: (i, 0)),
                   pl.BlockSpec((8, 128), lambda i: (0, 0))],
        out_shape=[sd((t, d // 2), jnp.uint32), sd((t, 128), F32), sd((8, 128), F32)],
        scratch_shapes=[pltpu.VMEM((8, 128), F32)],
        compiler_params=_cparams(("arbitrary",)),
        name="moe_router",
    )(x, gain.reshape(1, d), w_hi, w_lo, bias)


def _gather_rows(src_hbm, idx_ref, dst_ref, sem, n_rows):
    def body(j, carry):
        pltpu.make_async_copy(src_hbm.at[pl.ds(idx_ref[0, j], 1), :], dst_ref.at[pl.ds(j, 1), :], sem).start()
        return carry
    lax.fori_loop(0, n_rows, body, 0, unroll=8)


def _wait_rows(src_hbm, dst_ref, sem, n_rows):
    pltpu.make_async_copy(src_hbm.at[pl.ds(0, n_rows), :], dst_ref, sem).wait()


def _issue_rows(src_hbm, idx_ref, dst_ref, sem, lo, hi):
    for j in range(lo, hi):
        pltpu.make_async_copy(src_hbm.at[pl.ds(idx_ref[0, j], 1), :], dst_ref.at[pl.ds(j, 1), :], sem).start()


def _expert_up_kernel(be_ref, nused_ref, tok_ref, tokn_ref, x_hbm, wg_ref, wu_ref, h_ref, xbuf, sem, wg_s, wu_s):
    i = pl.program_id(0)
    nb = pl.num_programs(0)
    n_used = nused_ref[0]
    slot = i % 2

    @pl.when(i == 0)
    def _():
        _gather_rows(x_hbm, tok_ref, xbuf.at[0], sem.at[0], MOE_BLK)

    @pl.when((i < n_used) & _new_expert(be_ref, i))
    def _():
        wg_s[...] = wg_ref[...].astype(BF16)
        wu_s[...] = wu_ref[...].astype(BF16)

    @pl.when(i < n_used)
    def _():
        _wait_rows(x_hbm, xbuf.at[slot], sem.at[slot], MOE_BLK)
        half = xbuf.shape[2]
        nk = 2 * half // EXPERT_KC
        per = MOE_BLK // nk
        hg = jnp.zeros((MOE_BLK, D_EXPERT), F32)
        hu = jnp.zeros((MOE_BLK, D_EXPERT), F32)
        for kc in range(nk):
            _issue_rows(x_hbm, tokn_ref, xbuf.at[1 - slot], sem.at[1 - slot], kc * per, (kc + 1) * per)
            ks = slice(kc * EXPERT_KC, (kc + 1) * EXPERT_KC)
            wc = (kc * EXPERT_KC) % half
            xb = _unpack_bf16_pair(xbuf[slot, :, wc:wc + EXPERT_KC])[(kc * EXPERT_KC) // half].astype(BF16)
            hg = hg + jnp.dot(xb, wg_s[ks, :], preferred_element_type=F32)
            hu = hu + jnp.dot(xb, wu_s[ks, :], preferred_element_type=F32)
        h_ref[...] = (jax.nn.silu(hg) * hu).astype(h_ref.dtype)

        @pl.when(i == nb - 1)
        def _():
            _wait_rows(x_hbm, xbuf.at[1 - slot], sem.at[1 - slot], MOE_BLK)

    @pl.when(i >= n_used)
    def _():
        @pl.when(i == n_used)
        def _():
            _wait_rows(x_hbm, xbuf.at[slot], sem.at[slot], MOE_BLK)

        h_ref[...] = jnp.zeros_like(h_ref)


def _new_expert(be_ref, i):
    return (i == 0) | (be_ref[i] != be_ref[jnp.maximum(i - 1, 0)])


def _expert_down_kernel(be_ref, nused_ref, h_ref, wd_ref, y_ref, wd_s):
    i = pl.program_id(0)
    n_used = nused_ref[0]

    @pl.when((i < n_used) & _new_expert(be_ref, i))
    def _():
        wd_s[...] = wd_ref[...].astype(BF16)

    @pl.when(i < n_used)
    def _():
        h = h_ref[...]
        half = y_ref.shape[1]
        for nc in range(half // EXPERT_KC):
            ns = slice(nc * EXPERT_KC, (nc + 1) * EXPERT_KC)
            nh = slice(half + nc * EXPERT_KC, half + (nc + 1) * EXPERT_KC)
            lo = jnp.dot(h, wd_s[:, ns], preferred_element_type=F32)
            hi = jnp.dot(h, wd_s[:, nh], preferred_element_type=F32)
            y_ref[:, ns] = _pack_bf16_pair(lo, hi)

    @pl.when(i >= n_used)
    def _():
        y_ref[...] = jnp.zeros_like(y_ref)


def _experts(block_expert, n_used, tok3, xn, wg, wu, wd):
    nb = tok3.shape[0]
    d = 2 * xn.shape[1]
    tok_spec = pl.BlockSpec((None, 1, MOE_BLK), lambda i, be, nu: (i, 0, 0), memory_space=pltpu.SMEM)
    tokn_spec = pl.BlockSpec((None, 1, MOE_BLK), lambda i, be, nu: (jnp.minimum(i + 1, nb - 1), 0, 0),
                             memory_space=pltpu.SMEM)
    h_buf = pl.pallas_call(
        _expert_up_kernel,
        grid_spec=pltpu.PrefetchScalarGridSpec(
            num_scalar_prefetch=2,
            grid=(nb,),
            in_specs=[tok_spec, tokn_spec, pl.BlockSpec(memory_space=pl.ANY),
                      pl.BlockSpec((None, d, D_EXPERT), lambda i, be, nu: (be[i], 0, 0)),
                      pl.BlockSpec((None, d, D_EXPERT), lambda i, be, nu: (be[i], 0, 0))],
            out_specs=pl.BlockSpec((MOE_BLK, D_EXPERT), lambda i, be, nu: (i, 0)),
            scratch_shapes=[pltpu.VMEM((2, MOE_BLK, d // 2), jnp.uint32), pltpu.SemaphoreType.DMA((2,)),
                            pltpu.VMEM((d, D_EXPERT), BF16), pltpu.VMEM((d, D_EXPERT), BF16)],
        ),
        out_shape=jax.ShapeDtypeStruct((nb * MOE_BLK, D_EXPERT), BF16),
        compiler_params=_cparams(("arbitrary",)),
        name="moe_up",
    )(block_expert, n_used, tok3, tok3, xn, wg, wu)
    return pl.pallas_call(
        _expert_down_kernel,
        grid_spec=pltpu.PrefetchScalarGridSpec(
            num_scalar_prefetch=2,
            grid=(nb,),
            in_specs=[pl.BlockSpec((MOE_BLK, D_EXPERT), lambda i, be, nu: (i, 0)),
                      pl.BlockSpec((None, D_EXPERT, d), lambda i, be, nu: (be[i], 0, 0))],
            out_specs=pl.BlockSpec((MOE_BLK, d // 2), lambda i, be, nu: (i, 0)),
            scratch_shapes=[pltpu.VMEM((D_EXPERT, d), BF16)],
        ),
        out_shape=jax.ShapeDtypeStruct((nb * MOE_BLK, d // 2), jnp.uint32),
        compiler_params=_cparams(("arbitrary",)),
        name="moe_down",
    )(block_expert, n_used, h_buf, wd)


def _combine_kernel(d1_ref, d1n_ref, d2_ref, d2n_ref, x_ref, info_ref, g_ref, y_hbm, o_ref, ybuf, sem):
    i = pl.program_id(0)
    n = pl.num_programs(0)
    slot = i % 2

    @pl.when(i == 0)
    def _():
        _gather_rows(y_hbm, d1_ref, ybuf.at[0, 0], sem.at[0], COMB_TM)
        _gather_rows(y_hbm, d2_ref, ybuf.at[0, 1], sem.at[0], COMB_TM)

    _wait_rows(y_hbm, ybuf.at[slot, 0], sem.at[slot], COMB_TM)
    _wait_rows(y_hbm, ybuf.at[slot, 1], sem.at[slot], COMB_TM)
    _issue_rows(y_hbm, d1n_ref, ybuf.at[1 - slot, 0], sem.at[1 - slot], 0, COMB_TM)
    _issue_rows(y_hbm, d2n_ref, ybuf.at[1 - slot, 1], sem.at[1 - slot], 0, COMB_TM)
    info = info_ref[...]
    w1 = info[:, 2:3]
    w2 = info[:, 3:4]
    half = ybuf.shape[3]
    lo1, hi1 = _unpack_bf16_pair(ybuf[slot, 0])
    lo2, hi2 = _unpack_bf16_pair(ybuf[slot, 1])
    xlo = x_ref[:, :half] + lo1 * w1 + lo2 * w2
    xhi = x_ref[:, half:] + hi1 * w1 + hi2 * w2
    ms = (jnp.sum(xlo * xlo, axis=-1, keepdims=True) + jnp.sum(xhi * xhi, axis=-1, keepdims=True)) / (2 * half)
    rstd = lax.rsqrt(ms + RMS_EPS)
    o_ref[:, :half] = xlo * rstd * g_ref[:, :half]
    o_ref[:, half:] = xhi * rstd * g_ref[:, half:]

    @pl.when(i == n - 1)
    def _():
        _wait_rows(y_hbm, ybuf.at[1 - slot, 0], sem.at[1 - slot], COMB_TM)
        _wait_rows(y_hbm, ybuf.at[1 - slot, 1], sem.at[1 - slot], COMB_TM)


def _combine(dest1, dest2, x, info, gain, y_buf):
    t, d = x.shape
    nt = t // COMB_TM
    d1 = dest1.reshape(nt, 1, COMB_TM)
    d2 = dest2.reshape(nt, 1, COMB_TM)
    cur = pl.BlockSpec((None, 1, COMB_TM), lambda i: (i, 0, 0), memory_space=pltpu.SMEM)
    nxt = pl.BlockSpec((None, 1, COMB_TM), lambda i: (jnp.minimum(i + 1, nt - 1), 0, 0),
                       memory_space=pltpu.SMEM)
    return pl.pallas_call(
        _combine_kernel,
        grid=(nt,),
        in_specs=[cur, nxt, cur, nxt,
                  pl.BlockSpec((COMB_TM, d), lambda i: (i, 0)),
                  pl.BlockSpec((COMB_TM, 128), lambda i: (i, 0)),
                  pl.BlockSpec((1, d), lambda i: (0, 0)),
                  pl.BlockSpec(memory_space=pl.ANY)],
        out_specs=pl.BlockSpec((COMB_TM, d), lambda i: (i, 0)),
        out_shape=jax.ShapeDtypeStruct((t, d), F32),
        scratch_shapes=[pltpu.VMEM((2, 2, COMB_TM, d // 2), jnp.uint32), pltpu.SemaphoreType.DMA((2,))],
        compiler_params=_cparams(("arbitrary",)),
        name="moe_combine",
    )(d1, d1, d2, d2, x, info, gain.reshape(1, d), y_buf)


def _head_matrices():
    lane = jnp.arange(D_RWKV) // HEAD
    e = (lane[:, None] == jnp.arange(128)[None, :]).astype(BF16)
    return e, e.T


def _pad_lora_rows(w):
    return jnp.pad(w, ((0, 0), (0, LORA_PAD - w.shape[1]), (0, 0)))


def _layer(x, mem, norm_mix, w_in, shift_taps, rwkv_w0, rwkv_w2, rwkv_a0, rwkv_a2, rwkv_g2, rwkv_k_k,
           rwkv_k_a, rwkv_r_k, rwkv_ln_w, rwkv_ln_b, s5_lam_re, s5_lam_im, s5_log_step, s5_b_re, s5_b_im,
           s5_c_re, s5_c_im, s5_d, s5_glu_w, s5_glu_b, w_out, norm_attn, norm_mem, w_q, w_k, w_v, w_o,
           norm_ffn, router_grp_w, router_grp_b, router_exp_w, router_exp_b, exp_w_gate, exp_w_up,
           exp_w_down, out_gain):
    batch, seq, d = x.shape
    t = batch * seq
    mem_n = mem.shape[1]
    xt = x.reshape(t, d)
    c3 = 3 * D_RWKV
    off_g = c3 + 4 * 96

    def pad_cols(w, lo, width):
        return jnp.pad(w[:, lo:lo + width], ((0, 0), (0, LORA_PAD - width)))

    w_all = jnp.concatenate([w_in[:, :c3], w_in[:, off_g + 256:]]
                            + [pad_cols(w_in, c3 + i * 96, 96) for i in range(4)]
                            + [w_in[:, off_g:off_g + 256], jnp.zeros((d, LORA_BLK - N_LORA), F32)],
                            axis=1).astype(BF16)
    taps_l = jnp.concatenate([pad_cols(shift_taps, c3 + i * 96, 96) for i in range(4)]
                             + [shift_taps[:, off_g:off_g + 256], jnp.zeros((3, LORA_BLK - N_LORA), F32)], axis=1)
    h = _rmsnorm(xt, norm_mix, BF16)
    p_all = _matmul(h, w_all, F32, 1024, LORA_BLK, name="w_in")

    e_mat, et_mat = _head_matrices()
    k_a = rwkv_k_a.reshape(1, D_RWKV)
    r, k, v, kk, lw, a, g = _rwkv_prep(
        p_all, seq, shift_taps[:, :D_RWKV], shift_taps[:, D_RWKV:2 * D_RWKV],
        shift_taps[:, 2 * D_RWKV:c3], taps_l, rwkv_w0, _pad_lora_rows(rwkv_w2).astype(BF16), rwkv_a0,
        _pad_lora_rows(rwkv_a2).astype(BF16), rwkv_g2.astype(BF16), rwkv_k_k.reshape(1, D_RWKV),
        e_mat, et_mat)
    y_scan = _rwkv_scan(r, k, v, kk, lw, a, k_a, batch, seq)
    y_rwkv = _rwkv_post(y_scan, r, k, v, a, g, k_a, rwkv_r_k.reshape(1, D_RWKV),
                        rwkv_ln_w.reshape(1, D_RWKV), rwkv_ln_b.reshape(1, D_RWKV), e_mat, et_mat)

    n_levels = int(math.log2(seq // S5_L))
    a_t, a_z, a_y, mult = _s5_tables(s5_lam_re, s5_lam_im, s5_log_step, s5_b_re, s5_b_im,
                                    s5_c_re, s5_c_im, s5_d, n_levels)
    y_s5 = _s5_mix(p_all, c3 // (S5_GB * S5_CH), a_t, a_z, a_y, mult, batch, seq, n_levels)
    y_glu = _glu(y_s5, s5_glu_w.astype(BF16), s5_glu_b.reshape(1, D_S5))

    x1 = _matmul2_res(y_rwkv, y_glu, w_out.astype(BF16), xt, 1024, 512)

    memn = _rmsnorm(mem.reshape(batch * mem_n, d), norm_mem, BF16)
    wkv = jnp.concatenate([w_k, w_v], axis=1).astype(BF16)
    kv = _matmul(memn, wkv, BF16, 1024, 512, name="w_kv")
    x2 = _cross_attention(x1, kv, (norm_attn[:, None] * w_q).astype(BF16), w_o.astype(BF16), batch, seq, mem_n)

    w_r = jnp.concatenate([router_grp_w, router_exp_w,
                           jnp.zeros((d, 128 - N_GROUPS - N_EXPERTS), F32)], axis=1)
    w_r_hi = w_r.astype(BF16)
    w_r_lo = (w_r - w_r_hi.astype(F32)).astype(BF16)
    b_r = jnp.concatenate([router_grp_b, router_exp_b,
                           jnp.zeros((128 - N_GROUPS - N_EXPERTS,), F32)]).reshape(1, 128)
    xn3, info, cnt = _router(x2, norm_ffn, w_r_hi, w_r_lo, b_r)
    eid = info[:, 0:2].astype(jnp.int32)
    rank = info[:, 4:6].astype(jnp.int32)
    counts = cnt[0, :N_EXPERTS].astype(jnp.int32)
    nblk = (counts + MOE_BLK - 1) // MOE_BLK
    bstart = jnp.cumsum(nblk) - nblk
    n_used = jnp.sum(nblk)
    first = jnp.sum(jnp.where(eid[:, :, None] == jnp.arange(N_EXPERTS)[None, None, :], bstart, 0), axis=-1)
    dest = first * MOE_BLK + rank
    nb = (t * TOP_K) // MOE_BLK + N_EXPERTS
    tok_buf = jnp.zeros((nb * MOE_BLK,), jnp.int32).at[dest.reshape(-1)].set(
        jnp.repeat(jnp.arange(t, dtype=jnp.int32), TOP_K))
    blk = jnp.arange(nb, dtype=jnp.int32)
    block_expert = jnp.sum(blk[:, None] >= (bstart + nblk)[None, :], axis=1).astype(jnp.int32)
    last_e = jnp.max(jnp.where(nblk > 0, jnp.arange(N_EXPERTS), 0)).astype(jnp.int32)
    block_expert = jnp.minimum(block_expert, last_e)
    y_buf = _experts(block_expert, n_used.reshape(1).astype(jnp.int32), tok_buf.reshape(nb, 1, MOE_BLK),
                     xn3, exp_w_gate, exp_w_up, exp_w_down)
    out = _combine(dest[:, 0], dest[:, 1], x2, info, out_gain, y_buf)
    return out.reshape(batch, seq, d)


def kernel(x, mem, norm_mix, w_in, shift_taps, rwkv_w0, rwkv_w2, rwkv_a0, rwkv_a2, rwkv_g2, rwkv_k_k, rwkv_k_a, rwkv_r_k, rwkv_ln_w, rwkv_ln_b, s5_lam_re, s5_lam_im, s5_log_step, s5_b_re, s5_b_im, s5_c_re, s5_c_im, s5_d, s5_glu_w, s5_glu_b, w_out, norm_attn, norm_mem, w_q, w_k, w_v, w_o, norm_ffn, router_grp_w, router_grp_b, router_exp_w, router_exp_b, exp_w_gate, exp_w_up, exp_w_down, norm_final):
    return _layer(x, mem, norm_mix[0], w_in[0], shift_taps[0], rwkv_w0[0], rwkv_w2[0], rwkv_a0[0],
                  rwkv_a2[0], rwkv_g2[0], rwkv_k_k[0], rwkv_k_a[0], rwkv_r_k[0], rwkv_ln_w[0],
                  rwkv_ln_b[0], s5_lam_re[0], s5_lam_im[0], s5_log_step[0], s5_b_re[0], s5_b_im[0],
                  s5_c_re[0], s5_c_im[0], s5_d[0], s5_glu_w[0], s5_glu_b[0], w_out[0], norm_attn[0],
                  norm_mem[0], w_q[0], w_k[0], w_v[0], w_o[0], norm_ffn[0], router_grp_w[0],
                  router_grp_b[0], router_exp_w[0], router_exp_b[0], exp_w_gate[0], exp_w_up[0],
                  exp_w_down[0], norm_final)
```

```python
import functools
import math

import jax
import jax.numpy as jnp
from jax import lax
from jax.experimental import pallas as pl
from jax.experimental.pallas import tpu as pltpu

F32 = jnp.float32
BF16 = jnp.bfloat16

D_MODEL = 4096
D_RWKV = 2048
D_S5 = 2048
HEAD = 64
PAIR = 2 * HEAD
LORA_PAD = 128
N_LORA = 4 * LORA_PAD + 256
LORA_BLK = 1024
N_MAIN = 3 * D_RWKV + D_S5
S5_CH = 16
S5_GROUPS = D_S5 // S5_CH
S5_STATE = 64
S5_L = 8
S5_GB = 8
ATTN_HEADS = 4
ATTN_HEAD_DIM = D_MODEL // ATTN_HEADS
ATTN_SCALE = ATTN_HEAD_DIM ** -0.5
N_GROUPS = 8
EPG = 8
N_EXPERTS = N_GROUPS * EPG
TOP_K = 2
D_EXPERT = D_MODEL // 8
RMS_EPS = 1e-6
GN_EPS = 64e-5
L2_EPS = 1e-12

CHUNK = 64
SCAN_SUB = 4
SLAB = 2048
MOE_BLK = 288
COMB_TM = 128
EXPERT_KC = 1024
VMEM_LIMIT = 56 * 1024 * 1024


def _cparams(sem):
    return pltpu.CompilerParams(dimension_semantics=sem, vmem_limit_bytes=VMEM_LIMIT)


def _bdot(a, b):
    return jnp.dot(a.astype(BF16), b.astype(BF16), preferred_element_type=F32)


def _bdot_nt(a, b):
    return lax.dot_general(a.astype(BF16), b.astype(BF16), (((1,), (1,)), ((), ())),
                           preferred_element_type=F32)


def _split2(x):
    hi = x.astype(BF16)
    lo = (x - hi.astype(F32)).astype(BF16)
    return hi, lo


def _split3(x):
    hi = x.astype(BF16)
    r1 = x - hi.astype(F32)
    mid = r1.astype(BF16)
    lo = (r1 - mid.astype(F32)).astype(BF16)
    return hi, mid, lo


def _pack_bf16_pair(lo, hi):
    lo_b = lax.bitcast_convert_type(lo.astype(BF16).astype(F32), jnp.uint32) >> 16
    hi_b = lax.bitcast_convert_type(hi.astype(BF16).astype(F32), jnp.uint32) & jnp.uint32(0xFFFF0000)
    return hi_b | lo_b


def _unpack_bf16_pair(w):
    return (lax.bitcast_convert_type(w << 16, F32),
            lax.bitcast_convert_type(w & jnp.uint32(0xFFFF0000), F32))


def _dot_split2_rhs(x, e):
    h, l = _split2(x)
    return jnp.dot(h, e, preferred_element_type=F32) + jnp.dot(l, e, preferred_element_type=F32)


def _dot_exact_lhs(e, x):
    h, m, l = _split3(x)
    return (jnp.dot(e, h, preferred_element_type=F32) + jnp.dot(e, m, preferred_element_type=F32)
            + jnp.dot(e, l, preferred_element_type=F32))


def _rms_kernel(x_ref, g_ref, o_ref):
    x = x_ref[...]
    ms = jnp.mean(x * x, axis=-1, keepdims=True)
    o_ref[...] = (x * lax.rsqrt(ms + RMS_EPS) * g_ref[...]).astype(o_ref.dtype)


def _rmsnorm(x, gain, out_dtype, tm=256):
    t, d = x.shape
    return pl.pallas_call(
        _rms_kernel,
        grid=(t // tm,),
        in_specs=[pl.BlockSpec((tm, d), lambda i: (i, 0)), pl.BlockSpec((1, d), lambda i: (0, 0))],
        out_specs=pl.BlockSpec((tm, d), lambda i: (i, 0)),
        out_shape=jax.ShapeDtypeStruct((t, d), out_dtype),
        compiler_params=_cparams(("parallel",)),
        name="rmsnorm",
    )(x, gain.reshape(1, d))


def _mm_kernel(a_ref, b_ref, o_ref):
    o_ref[...] = jnp.dot(a_ref[...], b_ref[...], preferred_element_type=F32).astype(o_ref.dtype)


def _mm2_res_kernel(a1_ref, a2_ref, b_ref, r_ref, o_ref):
    k1 = a1_ref.shape[1]
    acc = jnp.dot(a1_ref[...], b_ref[:k1, :], preferred_element_type=F32)
    acc = acc + jnp.dot(a2_ref[...], b_ref[k1:, :], preferred_element_type=F32)
    o_ref[...] = (acc + r_ref[...]).astype(o_ref.dtype)


def _matmul(a, b, out_dtype, tm, tn, name):
    m, k = a.shape
    n = b.shape[1]
    tm, tn = min(tm, m), min(tn, n)
    return pl.pallas_call(
        _mm_kernel,
        grid=(m // tm, n // tn),
        in_specs=[pl.BlockSpec((tm, k), lambda i, j: (i, 0)), pl.BlockSpec((k, tn), lambda i, j: (0, j))],
        out_specs=pl.BlockSpec((tm, tn), lambda i, j: (i, j)),
        out_shape=jax.ShapeDtypeStruct((m, n), out_dtype),
        compiler_params=_cparams(("parallel", "parallel")),
        name=name,
    )(a, b)


def _matmul2_res(a1, a2, b, res, tm, tn):
    m, k1 = a1.shape
    k2 = a2.shape[1]
    n = b.shape[1]
    tm = min(tm, m)
    return pl.pallas_call(
        _mm2_res_kernel,
        grid=(m // tm, n // tn),
        in_specs=[pl.BlockSpec((tm, k1), lambda i, j: (i, 0)),
                  pl.BlockSpec((tm, k2), lambda i, j: (i, 0)),
                  pl.BlockSpec((k1 + k2, tn), lambda i, j: (0, j)),
                  pl.BlockSpec((tm, tn), lambda i, j: (i, j))],
        out_specs=pl.BlockSpec((tm, tn), lambda i, j: (i, j)),
        out_shape=jax.ShapeDtypeStruct((m, n), F32),
        compiler_params=_cparams(("parallel", "parallel")),
        name="w_out",
    )(a1, a2, b, res)


def _head_sum_bcast(x, e_ref, et_ref):
    s = _dot_split2_rhs(x, e_ref[...])
    return _dot_split2_rhs(s, et_ref[...])


def _shift3(x, hp, hn, taps, first, last):
    tm = x.shape[0]
    row = lax.broadcasted_iota(jnp.int32, x.shape, 0)
    prev_edge = jnp.where(first, 0.0, hp[7:8, :])
    next_edge = jnp.where(last, 0.0, hn[0:1, :])
    prev = jnp.where(row == 0, prev_edge, pltpu.roll(x, 1, 0))
    nxt = jnp.where(row == tm - 1, next_edge, pltpu.roll(x, tm - 1, 0))
    return taps[0:1, :] * prev + taps[1:2, :] * x + taps[2:3, :] * nxt


def _prep_kernel(seq_tiles,
                 r_ref, rp_ref, rn_ref, k_ref, kp_ref, kn_ref, v_ref, vp_ref, vn_ref,
                 lo_ref, lop_ref, lon_ref,
                 tr_ref, tk_ref, tv_ref, tl_ref,
                 w0_ref, w2_ref, a0_ref, a2_ref, g2_ref, kk_ref_p, e_ref, et_ref,
                 ro_ref, ko_ref, vo_ref, kko_ref, lw_ref, a_ref, g_ref):
    i = pl.program_id(0)
    first = (i % seq_tiles) == 0
    last = (i % seq_tiles) == seq_tiles - 1
    r = _shift3(r_ref[...], rp_ref[...], rn_ref[...], tr_ref[...], first, last)
    k = _shift3(k_ref[...], kp_ref[...], kn_ref[...], tk_ref[...], first, last)
    v = _shift3(v_ref[...], vp_ref[...], vn_ref[...], tv_ref[...], first, last)
    lo = _shift3(lo_ref[...], lop_ref[...], lon_ref[...], tl_ref[...], first, last)
    ro_ref[...] = r.astype(ro_ref.dtype)
    ko_ref[...] = k.astype(ko_ref.dtype)
    vo_ref[...] = v.astype(vo_ref.dtype)
    for d in range(2):
        xw = lo[:, d * LORA_PAD:(d + 1) * LORA_PAD]
        xa = lo[:, (2 + d) * LORA_PAD:(3 + d) * LORA_PAD]
        wl = w0_ref[d:d + 1, :] + _bdot(jnp.tanh(xw), w2_ref[d])
        lw_ref[d] = -math.exp(-0.5) * jax.nn.sigmoid(wl)
        a_ref[d] = jax.nn.sigmoid(a0_ref[d:d + 1, :] + _bdot(xa, a2_ref[d])).astype(a_ref.dtype)
    xg = lo[:, 4 * LORA_PAD:N_LORA]
    g_ref[...] = _bdot(jax.nn.sigmoid(xg), g2_ref[...]).astype(g_ref.dtype)
    kk = k * kk_ref_p[...]
    ssq = _head_sum_bcast(kk * kk, e_ref, et_ref)
    kko_ref[...] = (kk / jnp.maximum(jnp.sqrt(ssq), L2_EPS)).astype(kko_ref.dtype)


def _rwkv_prep(p_all, seq, taps_r, taps_k, taps_v, taps_l, w0, w2p, a0, a2p, g2, k_k, e_mat, et_mat,
               tm=128):
    t = p_all.shape[0]
    nt8 = t // 8
    seq_tiles = seq // tm
    c = D_RWKV

    def main_spec(col):
        return [pl.BlockSpec((tm, c), lambda i, col=col: (i, col)),
                pl.BlockSpec((8, c), lambda i, col=col: (jnp.maximum(i * (tm // 8) - 1, 0), col)),
                pl.BlockSpec((8, c), lambda i, col=col: (jnp.minimum((i + 1) * (tm // 8), nt8 - 1), col))]

    lcol = N_MAIN // LORA_BLK
    lora_spec = [pl.BlockSpec((tm, LORA_BLK), lambda i: (i, lcol)),
                 pl.BlockSpec((8, LORA_BLK), lambda i: (jnp.maximum(i * (tm // 8) - 1, 0), lcol)),
                 pl.BlockSpec((8, LORA_BLK), lambda i: (jnp.minimum((i + 1) * (tm // 8), nt8 - 1), lcol))]

    def full(shape):
        nd = len(shape)
        return pl.BlockSpec(shape, lambda i, nd=nd: (0,) * nd)

    in_specs = (main_spec(0) + main_spec(1) + main_spec(2) + lora_spec
                + [full((3, c)), full((3, c)), full((3, c)), full((3, LORA_BLK)),
                   full((2, c)), full((2, LORA_PAD, c)), full((2, c)), full((2, LORA_PAD, c)),
                   full((256, c)), full((1, c)), full((c, 128)), full((128, c))])
    row = pl.BlockSpec((tm, c), lambda i: (i, 0))
    row2 = pl.BlockSpec((2, tm, c), lambda i: (0, i, 0))
    sd = jax.ShapeDtypeStruct
    return pl.pallas_call(
        functools.partial(_prep_kernel, seq_tiles),
        grid=(t // tm,),
        in_specs=in_specs,
        out_specs=[row, row, row, row, row2, row2, row],
        out_shape=[sd((t, c), BF16), sd((t, c), BF16), sd((t, c), BF16), sd((t, c), BF16),
                   sd((2, t, c), F32), sd((2, t, c), BF16), sd((t, c), BF16)],
        compiler_params=_cparams(("parallel",)),
        name="rwkv_prep",
    )(p_all, p_all, p_all, p_all, p_all, p_all, p_all, p_all, p_all,
      p_all, p_all, p_all,
      taps_r, taps_k, taps_v, taps_l, w0, w2p, a0, a2p, g2, k_k, e_mat, et_mat)


def _scan_kernel(r_ref, k_ref, v_ref, kk_ref, lw_ref, a_ref, ka_ref, y_ref, st_ref):
    d = pl.program_id(0)
    c = pl.program_id(3)

    @pl.when(c == 0)
    def _():
        st_ref[...] = jnp.zeros_like(st_ref)

    ti = lax.broadcasted_iota(jnp.int32, (CHUNK, PAIR), 0)
    si = lax.broadcasted_iota(jnp.int32, (CHUNK, PAIR), 1) % HEAD
    ahead = (ti - si) * (1 - 2 * d)
    incl = ahead >= 0
    strict = ahead > 0
    eye = (si == ti).astype(F32)
    tri = incl[:, :CHUNK].astype(BF16)
    ka = ka_ref[...]

    npair = SLAB // PAIR
    lane = lax.broadcasted_iota(jnp.int32, (1, PAIR), 1)
    head0 = lane < HEAD

    def bd(y):
        return jnp.concatenate([jnp.where(head0, y, 0.0), jnp.where(head0, 0.0, y)], axis=0)

    sls = [slice(p * PAIR, (p + 1) * PAIR) for p in range(npair)]
    subs = []
    for u in range(SCAN_SUB):
        j = u + d * (SCAN_SUB - 1 - 2 * u)
        rows = pl.ds(pl.multiple_of(j * CHUNK, CHUNK), CHUNK)
        lw = lw_ref[rows, :]
        cum_incl = _dot_exact_lhs(tri, lw)
        e_incl = jnp.exp(cum_incl)
        e_excl = jnp.exp(cum_incl - lw)
        e_ninc = jnp.exp(-cum_incl)
        g_tot = jnp.exp(jnp.sum(lw, axis=0, keepdims=True))
        a = a_ref[rows, :].astype(F32)
        kk = kk_ref[rows, :].astype(F32)
        kd = k_ref[rows, :].astype(F32) * (1.0 + (a - 1.0) * ka)
        bt = (kk * a) * e_ninc
        kt = kd * e_ninc
        subs.append(dict(rows=rows, at=kk * e_excl, rt=r_ref[rows, :].astype(F32) * e_incl, bt=bt, kt=kt,
                         bh=bt * g_tot, kh=kt * g_tot, v=v_ref[rows, :].astype(F32), g_tot=g_tot))
    items = [(u, p) for u in range(SCAN_SUB) for p in range(npair)]
    ars = {(u, p): jnp.concatenate([subs[u]["at"][:, sls[p]], subs[u]["rt"][:, sls[p]]], axis=0).astype(BF16)
           for (u, p) in items}
    gs = {(u, p): _bdot_nt(ars[(u, p)], jnp.concatenate([bd(subs[u]["bt"][:, sls[p]]),
                                                        bd(subs[u]["kt"][:, sls[p]])], axis=0))
          for (u, p) in items}
    ms = {it: jnp.where(strict, -gs[it][:CHUNK, :PAIR], 0.0) for it in items}
    tinvs = {it: eye + ms[it] for it in items}
    ms = {it: _bdot(ms[it], bd(ms[it])) for it in items}
    for _ in range(4):
        ps = {it: _bdot(jnp.concatenate([tinvs[it], ms[it]], axis=0), bd(ms[it])) for it in items}
        tinvs = {it: tinvs[it] + ps[it][:CHUNK] for it in items}
        ms = {it: ps[it][CHUNK:] for it in items}
    tinvs = {it: tinvs[it] + _bdot(tinvs[it], bd(ms[it])) for it in items}
    x2s = {(u, p): _bdot(jnp.concatenate([jnp.where(strict, gs[(u, p)][:CHUNK, PAIR:], 0.0),
                                          jnp.where(incl, gs[(u, p)][CHUNK:, PAIR:], 0.0)], axis=0),
                         bd(subs[u]["v"][:, sls[p]])) for (u, p) in items}
    row = lax.broadcasted_iota(jnp.int32, (PAIR, PAIR), 0)
    same_head = (row < HEAD) == (lax.broadcasted_iota(jnp.int32, (PAIR, PAIR), 1) < HEAD)
    s_ps = [st_ref[p] for p in range(npair)]
    for u in range(SCAN_SUB):
        sub = subs[u]
        x1s = [_bdot_nt(ars[(u, p)], s_ps[p]) for p in range(npair)]
        us = [-_bdot(tinvs[(u, p)], bd(x1s[p][:CHUNK] + x2s[(u, p)][:CHUNK])) for p in range(npair)]
        outs = [x1s[p][CHUNK:] + x2s[(u, p)][CHUNK:]
                + _bdot(jnp.where(incl, gs[(u, p)][CHUNK:, :PAIR], 0.0), bd(us[p])) for p in range(npair)]
        nxt = []
        for p in range(npair):
            uv = jnp.concatenate([us[p], sub["v"][:, sls[p]]], axis=0).astype(BF16)
            bk = jnp.concatenate([sub["bh"][:, sls[p]], sub["kh"][:, sls[p]]], axis=0).astype(BF16)
            upd = lax.dot_general(uv, bk, (((0,), (0,)), ((), ())), preferred_element_type=F32)
            nxt.append(s_ps[p] * sub["g_tot"][:, sls[p]] + jnp.where(same_head, upd, 0.0))
        s_ps = nxt
        y_ref[sub["rows"], :] = jnp.concatenate(outs, axis=1)
    for p in range(npair):
        st_ref[p] = s_ps[p]


def _rwkv_scan(r, k, v, kk, lw, a, k_a, batch, seq):
    nc = seq // (CHUNK * SCAN_SUB)
    ns = D_RWKV // SLAB
    r4 = r.reshape(batch, seq, D_RWKV)
    k4 = k.reshape(batch, seq, D_RWKV)
    v4 = v.reshape(batch, seq, D_RWKV)
    kk4 = kk.reshape(batch, seq, D_RWKV)
    lw5 = lw.reshape(2, batch, seq, D_RWKV)
    a5 = a.reshape(2, batch, seq, D_RWKV)

    def tchunk(d, c):
        return c + d * (nc - 1 - 2 * c)

    rows = CHUNK * SCAN_SUB
    shared = pl.BlockSpec((None, rows, SLAB), lambda d, b, s, c: (b, tchunk(d, c), s))
    perdir = pl.BlockSpec((None, None, rows, SLAB), lambda d, b, s, c: (d, b, tchunk(d, c), s))
    y = pl.pallas_call(
        _scan_kernel,
        grid=(2, batch, ns, nc),
        in_specs=[shared, shared, shared, shared, perdir, perdir,
                  pl.BlockSpec((1, SLAB), lambda d, b, s, c: (0, s))],
        out_specs=perdir,
        out_shape=jax.ShapeDtypeStruct((2, batch, seq, D_RWKV), F32),
        scratch_shapes=[pltpu.VMEM((SLAB // PAIR, PAIR, PAIR), F32)],
        compiler_params=_cparams(("parallel", "parallel", "parallel", "arbitrary")),
        name="rwkv_scan",
    )(r4, k4, v4, kk4, lw5, a5, k_a)
    return y.reshape(2, batch * seq, D_RWKV)


def _post_kernel(y_ref, r_ref, k_ref, v_ref, a_ref, g_ref, ka_ref, rk_ref, lnw_ref, lnb_ref,
                 e_ref, et_ref, o_ref):
    y = y_ref[0] + y_ref[1]
    mu = _head_sum_bcast(y, e_ref, et_ref) * (1.0 / HEAD)
    yc = y - mu
    var = _head_sum_bcast(yc * yc, e_ref, et_ref) * (1.0 / HEAD)
    yn = yc * lax.rsqrt(var + GN_EPS) * lnw_ref[...] + lnb_ref[...]
    ka = ka_ref[...]
    k = k_ref[...].astype(F32)
    kd_sum = (k * (1.0 + (a_ref[0].astype(F32) - 1.0) * ka)
              + k * (1.0 + (a_ref[1].astype(F32) - 1.0) * ka))
    bonus = (_head_sum_bcast(r_ref[...].astype(F32) * kd_sum * rk_ref[...], e_ref, et_ref)
             * v_ref[...].astype(F32))
    o_ref[...] = ((yn + bonus) * g_ref[...].astype(F32)).astype(o_ref.dtype)


def _rwkv_post(y, r, k, v, a, g, k_a, r_k, ln_w, ln_b, e_mat, et_mat, tm=128):
    t, c = r.shape
    row = pl.BlockSpec((tm, c), lambda i: (i, 0))
    row2 = pl.BlockSpec((2, tm, c), lambda i: (0, i, 0))
    par = pl.BlockSpec((1, c), lambda i: (0, 0))
    return pl.pallas_call(
        _post_kernel,
        grid=(t // tm,),
        in_specs=[row2, row, row, row, row2, row, par, par, par, par,
                  pl.BlockSpec((c, 128), lambda i: (0, 0)), pl.BlockSpec((128, c), lambda i: (0, 0))],
        out_specs=row,
        out_shape=jax.ShapeDtypeStruct((t, c), BF16),
        compiler_params=_cparams(("parallel",)),
        name="rwkv_post",
    )(y, r, k, v, a, g, k_a, r_k, ln_w, ln_b, e_mat, et_mat)


def _s5_tables(lam_re, lam_im, log_step, b_re, b_im, c_re, c_im, d_skip, n_levels):
    L, P, CH, G, GB = S5_L, S5_STATE, S5_CH, S5_GROUPS, S5_GB
    NB = G // GB
    hi = lax.Precision.HIGHEST
    lam = lax.complex(lam_re.astype(F32), lam_im.astype(F32))
    dt = jnp.exp(log_step.astype(F32))[..., None]
    lam_dt = lam * dt
    lam_bar = jnp.exp(lam_dt)
    b = lax.complex(b_re.astype(F32), b_im.astype(F32))
    b_bar = ((lam_bar - 1.0) / lam)[..., None] * b
    c = lax.complex(c_re.astype(F32), c_im.astype(F32))
    taus = jnp.arange(L + 1, dtype=F32)
    pows = jnp.exp(lam_dt[:, :, None, :] * taus[None, None, :, None])
    kern = jnp.real(jnp.einsum('dgop,dgtp,dgpi->dgtoi', c, pows[:, :, :L], b_bar, precision=hi))
    j = jnp.arange(L)[:, None]
    t = jnp.arange(L)[None, :]
    lag = jnp.arange(L)[None, None, :]
    sel_f = ((t - j)[:, :, None] == lag).astype(F32)
    sel_b = ((j - t)[:, :, None] == lag).astype(F32)
    skip = (jnp.eye(L)[None, :, None, :, None] * jnp.eye(CH)[None, None, :, None, :]
            * d_skip.astype(F32).reshape(G, 1, CH, 1, 1))
    tsmall = (jnp.einsum('jtl,gloi->gjito', sel_f, kern[0], precision=hi)
              + jnp.einsum('jtl,gloi->gjito', sel_b, kern[1], precision=hi) + skip)
    def by_block(x, lead):
        x = x.reshape((NB, GB) + x.shape[1:])
        perm = (0,) + tuple(range(2, 2 + lead)) + (1,) + tuple(range(2 + lead, x.ndim))
        return x.transpose(perm)

    a_t = by_block(tsmall.reshape(G, L, CH, L * CH), 1).reshape(NB, L * GB * CH, L * CH)
    desc = jnp.exp(lam_dt[:, :, None, :] * (L - taus[:L])[None, None, :, None])
    desc1 = jnp.exp(lam_dt[0][:, None, :] * (L - 1 - taus[:L])[None, :, None])
    bb_t = b_bar.transpose(0, 1, 3, 2)
    zf = desc1[:, :, None, :] * bb_t[0][:, None]
    zb = pows[1][:, :L][:, :, None, :] * bb_t[1][:, None]
    wzs = jnp.stack([jnp.real(zf), jnp.imag(zf), jnp.real(zb), jnp.imag(zb)], axis=3)
    a_z = by_block(wzs.reshape(G, L, CH, 4 * P), 1).reshape(NB, L * GB * CH, 4 * P)
    c_t = c.transpose(0, 1, 3, 2)
    yf = c_t[0][:, :, None, :] * pows[0][:, 1:].transpose(0, 2, 1)[:, :, :, None]
    yb = c_t[1][:, :, None, :] * desc[1].transpose(0, 2, 1)[:, :, :, None]
    wys = jnp.stack([jnp.real(yf), -jnp.imag(yf), jnp.real(yb), -jnp.imag(yb)], axis=1)
    a_y = by_block(wys.reshape(G, 4, P, L * CH), 1).reshape(NB, 4 * GB * P, L * CH)
    steps = L * (2.0 ** jnp.arange(n_levels, dtype=F32))
    lp = jnp.exp(lam_dt[:, :, None, :] * steps[None, None, :, None])
    m = jnp.stack([jnp.real(lp[0]), jnp.imag(lp[0]), jnp.real(lp[1]), jnp.imag(lp[1])], axis=0)
    mult = m.reshape(4, NB, GB, n_levels, P).transpose(1, 3, 0, 2, 4).reshape(NB, n_levels, 4, GB * P)
    return a_t.astype(BF16), a_z.astype(BF16), a_y.astype(BF16), mult


def _group_of(shape, dim, width):
    return (lax.broadcasted_iota(jnp.int32, shape, dim) // width) % S5_GB


def _s5_kernel(n_levels, x_ref, at_ref, az_ref, ay_ref, rt_ref, rz_ref, mult_ref, y_ref, t_s, wz_s, wy_s):
    nck = x_ref.shape[0] // S5_L
    lanes = x_ref.shape[1]
    w = S5_GB * S5_STATE

    @pl.when(pl.program_id(1) == 0)
    def _():
        t = jnp.dot(at_ref[...], rt_ref[...], preferred_element_type=F32)
        keep = _group_of(t.shape, 0, S5_CH) == _group_of(t.shape, 1, S5_CH)
        t_s[...] = jnp.where(keep, t, 0.0).astype(BF16)
        z = jnp.dot(az_ref[...], rz_ref[...], preferred_element_type=F32)
        keep = _group_of(z.shape, 0, S5_CH) == _group_of(z.shape, 1, S5_STATE)
        wz_s[...] = jnp.where(keep, z, 0.0).astype(BF16)
        yy = jnp.dot(ay_ref[...], rt_ref[...], preferred_element_type=F32)
        keep = _group_of(yy.shape, 0, S5_STATE) == _group_of(yy.shape, 1, S5_CH)
        wy_s[...] = jnp.where(keep, yy, 0.0).astype(BF16)

    xcat = jnp.concatenate([x_ref[pl.ds(tl, nck, stride=S5_L), :].astype(BF16) for tl in range(S5_L)],
                           axis=1)
    z = jnp.dot(xcat, wz_s[...], preferred_element_type=F32)
    fre, fim, bre, bim = z[:, :w], z[:, w:2 * w], z[:, 2 * w:3 * w], z[:, 3 * w:]
    cidx = lax.broadcasted_iota(jnp.int32, (nck, w), 0)
    for i in range(n_levels):
        sh = 1 << i
        lfr, lfi = mult_ref[i, 0:1, :], mult_ref[i, 1:2, :]
        lbr, lbi = mult_ref[i, 2:3, :], mult_ref[i, 3:4, :]
        keep = cidx >= sh
        sr = jnp.where(keep, pltpu.roll(fre, sh, 0), 0.0)
        si = jnp.where(keep, pltpu.roll(fim, sh, 0), 0.0)
        fre, fim = fre + sr * lfr - si * lfi, fim + sr * lfi + si * lfr
        keep = cidx < nck - sh
        sr = jnp.where(keep, pltpu.roll(bre, nck - sh, 0), 0.0)
        si = jnp.where(keep, pltpu.roll(bim, nck - sh, 0), 0.0)
        bre, bim = bre + sr * lbr - si * lbi, bim + sr * lbi + si * lbr
    has_prev = cidx >= 1
    has_next = cidx < nck - 1
    xin = jnp.concatenate([jnp.where(has_prev, pltpu.roll(fre, 1, 0), 0.0),
                           jnp.where(has_prev, pltpu.roll(fim, 1, 0), 0.0),
                           jnp.where(has_next, pltpu.roll(bre, nck - 1, 0), 0.0),
                           jnp.where(has_next, pltpu.roll(bim, nck - 1, 0), 0.0)], axis=1).astype(BF16)
    y = jnp.dot(xcat, t_s[...], preferred_element_type=F32)
    y = y + jnp.dot(xin, wy_s[...], preferred_element_type=F32)
    for tl in range(S5_L):
        y_ref[pl.ds(tl, nck, stride=S5_L), :] = y[:, tl * lanes:(tl + 1) * lanes]


def _s5_mix(p_main, col0, a_t, a_z, a_y, mult, batch, seq, n_levels):
    nb = a_t.shape[0]
    lanes = S5_GB * S5_CH
    kw = S5_L * lanes
    sw = 4 * S5_GB * S5_STATE
    r_t = (jnp.arange(S5_L * S5_CH)[:, None] ==
           (jnp.arange(kw)[None, :] // lanes) * S5_CH + jnp.arange(kw)[None, :] % S5_CH).astype(BF16)
    r_z = (jnp.arange(4 * S5_STATE)[:, None] ==
           (jnp.arange(sw)[None, :] // (S5_GB * S5_STATE)) * S5_STATE + jnp.arange(sw)[None, :] % S5_STATE
           ).astype(BF16)
    return pl.pallas_call(
        functools.partial(_s5_kernel, n_levels),
        grid=(nb, batch),
        in_specs=[pl.BlockSpec((seq, lanes), lambda j, b: (b, col0 + j)),
                  pl.BlockSpec((None, kw, S5_L * S5_CH), lambda j, b: (j, 0, 0)),
                  pl.BlockSpec((None, kw, 4 * S5_STATE), lambda j, b: (j, 0, 0)),
                  pl.BlockSpec((None, sw, S5_L * S5_CH), lambda j, b: (j, 0, 0)),
                  pl.BlockSpec((S5_L * S5_CH, kw), lambda j, b: (0, 0)),
                  pl.BlockSpec((4 * S5_STATE, sw), lambda j, b: (0, 0)),
                  pl.BlockSpec((None, n_levels, 4, S5_GB * S5_STATE), lambda j, b: (j, 0, 0, 0))],
        out_specs=pl.BlockSpec((seq, lanes), lambda j, b: (b, j)),
        out_shape=jax.ShapeDtypeStruct((batch * seq, D_S5), F32),
        scratch_shapes=[pltpu.VMEM((kw, kw), BF16), pltpu.VMEM((kw, sw), BF16), pltpu.VMEM((sw, kw), BF16)],
        compiler_params=_cparams(("parallel", "arbitrary")),
        name="s5_mix",
    )(p_main, a_t, a_z, a_y, r_t, r_z, mult)


def _glu_kernel(y_ref, w_ref, b_ref, o_ref):
    y = y_ref[...]
    z = jnp.dot(jax.nn.gelu(y).astype(BF16), w_ref[...], preferred_element_type=F32) + b_ref[...]
    o_ref[...] = (y * jax.nn.sigmoid(z)).astype(o_ref.dtype)


def _glu(y, w, b, tm=512):
    t, c = y.shape
    tm = min(tm, t)
    return pl.pallas_call(
        _glu_kernel,
        grid=(t // tm,),
        in_specs=[pl.BlockSpec((tm, c), lambda i: (i, 0)), pl.BlockSpec((c, c), lambda i: (0, 0)),
                  pl.BlockSpec((1, c), lambda i: (0, 0))],
        out_specs=pl.BlockSpec((tm, c), lambda i: (i, 0)),
        out_shape=jax.ShapeDtypeStruct((t, c), BF16),
        compiler_params=_cparams(("parallel",)),
        name="s5_glu",
    )(y, w, b)


def _wqk_kernel(wq_ref, k_ref, o_ref):
    acc = lax.dot_general(wq_ref[...], k_ref[...], (((1,), (1,)), ((), ())), preferred_element_type=F32)
    o_ref[...] = (acc * ATTN_SCALE).astype(o_ref.dtype)


def _vwo_kernel(v_ref, wo_ref, o_ref):
    o_ref[...] = jnp.dot(v_ref[...], wo_ref[...], preferred_element_type=F32).astype(o_ref.dtype)


def _attn_kernel(mem, x_ref, wqk_ref, vwo_ref, o_ref, p_ref):
    j = pl.program_id(2)
    tn = o_ref.shape[1]

    @pl.when(j == 0)
    def _():
        x = x_ref[...]
        rstd = lax.rsqrt(jnp.mean(x * x, axis=-1, keepdims=True) + RMS_EPS)
        s = jnp.dot(x.astype(BF16), wqk_ref[...], preferred_element_type=F32) * rstd
        for h in range(ATTN_HEADS):
            sh = s[:, h * mem:(h + 1) * mem]
            sh = sh - jnp.max(sh, axis=-1, keepdims=True)
            p = jnp.exp(sh)
            p_ref[:, h * mem:(h + 1) * mem] = (p / jnp.sum(p, axis=-1, keepdims=True)).astype(BF16)

    res = x_ref[:, pl.ds(pl.multiple_of(j * tn, tn), tn)]
    o_ref[...] = jnp.dot(p_ref[...], vwo_ref[...], preferred_element_type=F32) + res


def _cross_attention(x, kv, w_q, w_o, batch, seq, mem, tm=512, tn=1024):
    d = D_MODEL
    hm = ATTN_HEADS * mem
    tm = min(tm, seq)
    wqk = pl.pallas_call(
        _wqk_kernel,
        grid=(batch, ATTN_HEADS),
        in_specs=[pl.BlockSpec((d, ATTN_HEAD_DIM), lambda b, h: (0, h)),
                  pl.BlockSpec((mem, ATTN_HEAD_DIM), lambda b, h: (b, h))],
        out_specs=pl.BlockSpec((None, d, mem), lambda b, h: (b, 0, h)),
        out_shape=jax.ShapeDtypeStruct((batch, d, hm), BF16),
        compiler_params=_cparams(("parallel", "parallel")),
        name="attn_wqk",
    )(w_q, kv)
    vwo = pl.pallas_call(
        _vwo_kernel,
        grid=(batch, ATTN_HEADS),
        in_specs=[pl.BlockSpec((mem, ATTN_HEAD_DIM), lambda b, h: (b, ATTN_HEADS + h)),
                  pl.BlockSpec((ATTN_HEAD_DIM, d), lambda b, h: (h, 0))],
        out_specs=pl.BlockSpec((None, mem, d), lambda b, h: (b, h, 0)),
        out_shape=jax.ShapeDtypeStruct((batch, hm, d), BF16),
        compiler_params=_cparams(("parallel", "parallel")),
        name="attn_vwo",
    )(kv, w_o)
    out = pl.pallas_call(
        functools.partial(_attn_kernel, mem),
        grid=(batch, seq // tm, d // tn),
        in_specs=[pl.BlockSpec((None, tm, d), lambda b, i, j: (b, i, 0)),
                  pl.BlockSpec((None, d, hm), lambda b, i, j: (b, 0, 0)),
                  pl.BlockSpec((None, hm, tn), lambda b, i, j: (b, 0, j))],
        out_specs=pl.BlockSpec((None, tm, tn), lambda b, i, j: (b, i, j)),
        out_shape=jax.ShapeDtypeStruct((batch, seq, d), F32),
        scratch_shapes=[pltpu.VMEM((tm, hm), BF16)],
        compiler_params=_cparams(("parallel", "parallel", "arbitrary")),
        name="cross_attn",
    )(x.reshape(batch, seq, d), wqk, vwo)
    return out.reshape(batch * seq, d)


def _router_kernel(x_ref, g_ref, whi_ref, wlo_ref, b_ref, xn_ref, info_ref, cnt_ref, carry_ref):
    i = pl.program_id(0)

    @pl.when(i == 0)
    def _():
        carry_ref[...] = jnp.zeros_like(carry_ref)

    x = x_ref[...]
    tm = x.shape[0]
    ms = jnp.mean(x * x, axis=-1, keepdims=True)
    xn = x * lax.rsqrt(ms + RMS_EPS) * g_ref[...]
    half = xn.shape[1] // 2
    xn_ref[...] = _pack_bf16_pair(xn[:, :half], xn[:, half:])
    xh, xl = _split2(xn)
    whi = whi_ref[...]
    logits = (jnp.dot(xh, whi, preferred_element_type=F32) + jnp.dot(xl, whi, preferred_element_type=F32)
              + jnp.dot(xh, wlo_ref[...], preferred_element_type=F32)) + b_ref[...]
    li = lax.broadcasted_iota(jnp.int32, logits.shape, 1)
    neg = jnp.float32(-jnp.inf)
    is_g = li < N_GROUPS
    gl = jnp.where(is_g, logits, neg)
    gm = jnp.max(gl, axis=-1, keepdims=True)
    gi = jnp.min(jnp.where(is_g & (gl == gm), li, 128), axis=-1, keepdims=True)
    gp = 1.0 / jnp.sum(jnp.where(is_g, jnp.exp(gl - gm), 0.0), axis=-1, keepdims=True)
    lo_lane = N_GROUPS + EPG * gi
    sel = (li >= lo_lane) & (li < lo_lane + EPG)
    l1 = jnp.where(sel, logits, neg)
    e1 = jnp.max(l1, axis=-1, keepdims=True)
    i1 = jnp.min(jnp.where(sel & (l1 == e1), li, 128), axis=-1, keepdims=True)
    sel2 = sel & (li != i1)
    l2 = jnp.where(sel2, logits, neg)
    e2 = jnp.max(l2, axis=-1, keepdims=True)
    i2 = jnp.min(jnp.where(sel2 & (l2 == e2), li, 128), axis=-1, keepdims=True)
    ex = jnp.exp(e2 - e1)
    w1 = gp / (1.0 + ex)
    w2 = gp * ex / (1.0 + ex)
    x1 = i1 - N_GROUPS
    x2 = i2 - N_GROUPS
    oh1 = li == x1
    oh2 = li == x2
    oh = (oh1 | oh2).astype(BF16)
    ri = lax.broadcasted_iota(jnp.int32, (tm, tm), 0)
    ci = lax.broadcasted_iota(jnp.int32, (tm, tm), 1)
    before = jnp.dot((ci < ri).astype(BF16), oh, preferred_element_type=F32) + carry_ref[0:1, :]
    r1 = jnp.sum(jnp.where(oh1, before, 0.0), axis=-1, keepdims=True)
    r2 = jnp.sum(jnp.where(oh2, before, 0.0), axis=-1, keepdims=True)
    new_carry = carry_ref[0:1, :] + jnp.sum(oh.astype(F32), axis=0, keepdims=True)
    carry_ref[...] = jnp.broadcast_to(new_carry, carry_ref.shape)
    cnt_ref[...] = jnp.broadcast_to(new_carry, cnt_ref.shape)
    info = jnp.where(li == 0, x1.astype(F32), 0.0)
    info = jnp.where(li == 1, x2.astype(F32), info)
    info = jnp.where(li == 2, w1, info)
    info = jnp.where(li == 3, w2, info)
    info = jnp.where(li == 4, r1, info)
    info = jnp.where(li == 5, r2, info)
    info_ref[...] = info


def _router(x, gain, w_hi, w_lo, bias, tm=256):
    t, d = x.shape
    sd = jax.ShapeDtypeStruct
    return pl.pallas_call(
        _router_kernel,
        grid=(t // tm,),
        in_specs=[pl.BlockSpec((tm, d), lambda i: (i, 0)), pl.BlockSpec((1, d), lambda i: (0, 0)),
                  pl.BlockSpec((d, 128), lambda i: (0, 0)), pl.BlockSpec((d, 128), lambda i: (0, 0)),
                  pl.BlockSpec((1, 128), lambda i: (0, 0))],
        out_specs=[pl.BlockSpec((tm, d // 2), lambda i: (i, 0)), pl.BlockSpec((tm, 128), lambda i: (i, 0)),
                   pl.BlockSpec((8, 128), lambda i: (0, 0))],
        out_shape=[sd((t, d // 2), jnp.uint32), sd((t, 128), F32), sd((8, 128), F32)],
        scratch_shapes=[pltpu.VMEM((8, 128), F32)],
        compiler_params=_cparams(("arbitrary",)),
        name="moe_router",
    )(x, gain.reshape(1, d), w_hi, w_lo, bias)


def _gather_rows(src_hbm, idx_ref, dst_ref, sem, n_rows):
    def body(j, carry):
        pltpu.make_async_copy(src_hbm.at[pl.ds(idx_ref[0, j], 1), :], dst_ref.at[pl.ds(j, 1), :], sem).start()
        return carry
    lax.fori_loop(0, n_rows, body, 0, unroll=8)


def _wait_rows(src_hbm, dst_ref, sem, n_rows):
    pltpu.make_async_copy(src_hbm.at[pl.ds(0, n_rows), :], dst_ref, sem).wait()


def _issue_rows(src_hbm, idx_ref, dst_ref, sem, lo, hi):
    for j in range(lo, hi):
        pltpu.make_async_copy(src_hbm.at[pl.ds(idx_ref[0, j], 1), :], dst_ref.at[pl.ds(j, 1), :], sem).start()


def _expert_up_kernel(be_ref, nused_ref, tok_ref, tokn_ref, x_hbm, wg_ref, wu_ref, h_ref, xbuf, sem):
    i = pl.program_id(0)
    nb = pl.num_programs(0)
    n_used = nused_ref[0]
    slot = i % 2

    @pl.when(i == 0)
    def _():
        _gather_rows(x_hbm, tok_ref, xbuf.at[0], sem.at[0], MOE_BLK)

    @pl.when(i < n_used)
    def _():
        _wait_rows(x_hbm, xbuf.at[slot], sem.at[slot], MOE_BLK)
        half = xbuf.shape[2]
        nk = 2 * half // EXPERT_KC
        per = MOE_BLK // nk
        hg = jnp.zeros((MOE_BLK, D_EXPERT), F32)
        hu = jnp.zeros((MOE_BLK, D_EXPERT), F32)
        for kc in range(nk):
            _issue_rows(x_hbm, tokn_ref, xbuf.at[1 - slot], sem.at[1 - slot], kc * per, (kc + 1) * per)
            ks = slice(kc * EXPERT_KC, (kc + 1) * EXPERT_KC)
            wc = (kc * EXPERT_KC) % half
            xb = _unpack_bf16_pair(xbuf[slot, :, wc:wc + EXPERT_KC])[(kc * EXPERT_KC) // half].astype(BF16)
            hg = hg + jnp.dot(xb, wg_ref[ks, :].astype(BF16), preferred_element_type=F32)
            hu = hu + jnp.dot(xb, wu_ref[ks, :].astype(BF16), preferred_element_type=F32)
        h_ref[...] = (jax.nn.silu(hg) * hu).astype(h_ref.dtype)

        @pl.when(i == nb - 1)
        def _():
            _wait_rows(x_hbm, xbuf.at[1 - slot], sem.at[1 - slot], MOE_BLK)

    @pl.when(i >= n_used)
    def _():
        @pl.when(i == n_used)
        def _():
            _wait_rows(x_hbm, xbuf.at[slot], sem.at[slot], MOE_BLK)

        h_ref[...] = jnp.zeros_like(h_ref)


def _expert_down_kernel(be_ref, nused_ref, h_ref, wd_ref, y_ref):
    i = pl.program_id(0)
    n_used = nused_ref[0]

    @pl.when(i < n_used)
    def _():
        h = h_ref[...]
        half = y_ref.shape[1]
        for nc in range(half // EXPERT_KC):
            ns = slice(nc * EXPERT_KC, (nc + 1) * EXPERT_KC)
            nh = slice(half + nc * EXPERT_KC, half + (nc + 1) * EXPERT_KC)
            lo = jnp.dot(h, wd_ref[:, ns].astype(BF16), preferred_element_type=F32)
            hi = jnp.dot(h, wd_ref[:, nh].astype(BF16), preferred_element_type=F32)
            y_ref[:, ns] = _pack_bf16_pair(lo, hi)

    @pl.when(i >= n_used)
    def _():
        y_ref[...] = jnp.zeros_like(y_ref)


def _experts(block_expert, n_used, tok3, xn, wg, wu, wd):
    nb = tok3.shape[0]
    d = 2 * xn.shape[1]
    tok_spec = pl.BlockSpec((None, 1, MOE_BLK), lambda i, be, nu: (i, 0, 0), memory_space=pltpu.SMEM)
    tokn_spec = pl.BlockSpec((None, 1, MOE_BLK), lambda i, be, nu: (jnp.minimum(i + 1, nb - 1), 0, 0),
                             memory_space=pltpu.SMEM)
    h_buf = pl.pallas_call(
        _expert_up_kernel,
        grid_spec=pltpu.PrefetchScalarGridSpec(
            num_scalar_prefetch=2,
            grid=(nb,),
            in_specs=[tok_spec, tokn_spec, pl.BlockSpec(memory_space=pl.ANY),
                      pl.BlockSpec((None, d, D_EXPERT), lambda i, be, nu: (be[i], 0, 0)),
                      pl.BlockSpec((None, d, D_EXPERT), lambda i, be, nu: (be[i], 0, 0))],
            out_specs=pl.BlockSpec((MOE_BLK, D_EXPERT), lambda i, be, nu: (i, 0)),
            scratch_shapes=[pltpu.VMEM((2, MOE_BLK, d // 2), jnp.uint32), pltpu.SemaphoreType.DMA((2,))],
        ),
        out_shape=jax.ShapeDtypeStruct((nb * MOE_BLK, D_EXPERT), BF16),
        compiler_params=_cparams(("arbitrary",)),
        name="moe_up",
    )(block_expert, n_used, tok3, tok3, xn, wg, wu)
    return pl.pallas_call(
        _expert_down_kernel,
        grid_spec=pltpu.PrefetchScalarGridSpec(
            num_scalar_prefetch=2,
            grid=(nb,),
            in_specs=[pl.BlockSpec((MOE_BLK, D_EXPERT), lambda i, be, nu: (i, 0)),
                      pl.BlockSpec((None, D_EXPERT, d), lambda i, be, nu: (be[i], 0, 0))],
            out_specs=pl.BlockSpec((MOE_BLK, d // 2), lambda i, be, nu: (i, 0)),
        ),
        out_shape=jax.ShapeDtypeStruct((nb * MOE_BLK, d // 2), jnp.uint32),
        compiler_params=_cparams(("arbitrary",)),
        name="moe_down",
    )(block_expert, n_used, h_buf, wd)


def _combine_kernel(d1_ref, d1n_ref, d2_ref, d2n_ref, x_ref, info_ref, g_ref, y_hbm, o_ref, ybuf, sem):
    i = pl.program_id(0)
    n = pl.num_programs(0)
    slot = i % 2

    @pl.when(i == 0)
    def _():
        _gather_rows(y_hbm, d1_ref, ybuf.at[0, 0], sem.at[0], COMB_TM)
        _gather_rows(y_hbm, d2_ref, ybuf.at[0, 1], sem.at[0], COMB_TM)

    _wait_rows(y_hbm, ybuf.at[slot, 0], sem.at[slot], COMB_TM)
    _wait_rows(y_hbm, ybuf.at[slot, 1], sem.at[slot], COMB_TM)
    _issue_rows(y_hbm, d1n_ref, ybuf.at[1 - slot, 0], sem.at[1 - slot], 0, COMB_TM)
    _issue_rows(y_hbm, d2n_ref, ybuf.at[1 - slot, 1], sem.at[1 - slot], 0, COMB_TM)
    info = info_ref[...]
    w1 = info[:, 2:3]
    w2 = info[:, 3:4]
    half = ybuf.shape[3]
    lo1, hi1 = _unpack_bf16_pair(ybuf[slot, 0])
    lo2, hi2 = _unpack_bf16_pair(ybuf[slot, 1])
    xlo = x_ref[:, :half] + lo1 * w1 + lo2 * w2
    xhi = x_ref[:, half:] + hi1 * w1 + hi2 * w2
    ms = (jnp.sum(xlo * xlo, axis=-1, keepdims=True) + jnp.sum(xhi * xhi, axis=-1, keepdims=True)) / (2 * half)
    rstd = lax.rsqrt(ms + RMS_EPS)
    o_ref[:, :half] = xlo * rstd * g_ref[:, :half]
    o_ref[:, half:] = xhi * rstd * g_ref[:, half:]

    @pl.when(i == n - 1)
    def _():
        _wait_rows(y_hbm, ybuf.at[1 - slot, 0], sem.at[1 - slot], COMB_TM)
        _wait_rows(y_hbm, ybuf.at[1 - slot, 1], sem.at[1 - slot], COMB_TM)


def _combine(dest1, dest2, x, info, gain, y_buf):
    t, d = x.shape
    nt = t // COMB_TM
    d1 = dest1.reshape(nt, 1, COMB_TM)
    d2 = dest2.reshape(nt, 1, COMB_TM)
    cur = pl.BlockSpec((None, 1, COMB_TM), lambda i: (i, 0, 0), memory_space=pltpu.SMEM)
    nxt = pl.BlockSpec((None, 1, COMB_TM), lambda i: (jnp.minimum(i + 1, nt - 1), 0, 0),
                       memory_space=pltpu.SMEM)
    return pl.pallas_call(
        _combine_kernel,
        grid=(nt,),
        in_specs=[cur, nxt, cur, nxt,
                  pl.BlockSpec((COMB_TM, d), lambda i: (i, 0)),
                  pl.BlockSpec((COMB_TM, 128), lambda i: (i, 0)),
                  pl.BlockSpec((1, d), lambda i: (0, 0)),
                  pl.BlockSpec(memory_space=pl.ANY)],
        out_specs=pl.BlockSpec((COMB_TM, d), lambda i: (i, 0)),
        out_shape=jax.ShapeDtypeStruct((t, d), F32),
        scratch_shapes=[pltpu.VMEM((2, 2, COMB_TM, d // 2), jnp.uint32), pltpu.SemaphoreType.DMA((2,))],
        compiler_params=_cparams(("arbitrary",)),
        name="moe_combine",
    )(d1, d1, d2, d2, x, info, gain.reshape(1, d), y_buf)


def _head_matrices():
    lane = jnp.arange(D_RWKV) // HEAD
    e = (lane[:, None] == jnp.arange(128)[None, :]).astype(BF16)
    return e, e.T


def _pad_lora_rows(w):
    return jnp.pad(w, ((0, 0), (0, LORA_PAD - w.shape[1]), (0, 0)))


def _layer(x, mem, norm_mix, w_in, shift_taps, rwkv_w0, rwkv_w2, rwkv_a0, rwkv_a2, rwkv_g2, rwkv_k_k,
           rwkv_k_a, rwkv_r_k, rwkv_ln_w, rwkv_ln_b, s5_lam_re, s5_lam_im, s5_log_step, s5_b_re, s5_b_im,
           s5_c_re, s5_c_im, s5_d, s5_glu_w, s5_glu_b, w_out, norm_attn, norm_mem, w_q, w_k, w_v, w_o,
           norm_ffn, router_grp_w, router_grp_b, router_exp_w, router_exp_b, exp_w_gate, exp_w_up,
           exp_w_down, out_gain):
    batch, seq, d = x.shape
    t = batch * seq
    mem_n = mem.shape[1]
    xt = x.reshape(t, d)
    c3 = 3 * D_RWKV
    off_g = c3 + 4 * 96

    def pad_cols(w, lo, width):
        return jnp.pad(w[:, lo:lo + width], ((0, 0), (0, LORA_PAD - width)))

    w_all = jnp.concatenate([w_in[:, :c3], w_in[:, off_g + 256:]]
                            + [pad_cols(w_in, c3 + i * 96, 96) for i in range(4)]
                            + [w_in[:, off_g:off_g + 256], jnp.zeros((d, LORA_BLK - N_LORA), F32)],
                            axis=1).astype(BF16)
    taps_l = jnp.concatenate([pad_cols(shift_taps, c3 + i * 96, 96) for i in range(4)]
                             + [shift_taps[:, off_g:off_g + 256], jnp.zeros((3, LORA_BLK - N_LORA), F32)], axis=1)
    h = _rmsnorm(xt, norm_mix, BF16)
    p_all = _matmul(h, w_all, F32, 1024, LORA_BLK, name="w_in")

    e_mat, et_mat = _head_matrices()
    k_a = rwkv_k_a.reshape(1, D_RWKV)
    r, k, v, kk, lw, a, g = _rwkv_prep(
        p_all, seq, shift_taps[:, :D_RWKV], shift_taps[:, D_RWKV:2 * D_RWKV],
        shift_taps[:, 2 * D_RWKV:c3], taps_l, rwkv_w0, _pad_lora_rows(rwkv_w2).astype(BF16), rwkv_a0,
        _pad_lora_rows(rwkv_a2).astype(BF16), rwkv_g2.astype(BF16), rwkv_k_k.reshape(1, D_RWKV),
        e_mat, et_mat)
    y_scan = _rwkv_scan(r, k, v, kk, lw, a, k_a, batch, seq)
    y_rwkv = _rwkv_post(y_scan, r, k, v, a, g, k_a, rwkv_r_k.reshape(1, D_RWKV),
                        rwkv_ln_w.reshape(1, D_RWKV), rwkv_ln_b.reshape(1, D_RWKV), e_mat, et_mat)

    n_levels = int(math.log2(seq // S5_L))
    a_t, a_z, a_y, mult = _s5_tables(s5_lam_re, s5_lam_im, s5_log_step, s5_b_re, s5_b_im,
                                    s5_c_re, s5_c_im, s5_d, n_levels)
    y_s5 = _s5_mix(p_all, c3 // (S5_GB * S5_CH), a_t, a_z, a_y, mult, batch, seq, n_levels)
    y_glu = _glu(y_s5, s5_glu_w.astype(BF16), s5_glu_b.reshape(1, D_S5))

    x1 = _matmul2_res(y_rwkv, y_glu, w_out.astype(BF16), xt, 1024, 512)

    memn = _rmsnorm(mem.reshape(batch * mem_n, d), norm_mem, BF16)
    wkv = jnp.concatenate([w_k, w_v], axis=1).astype(BF16)
    kv = _matmul(memn, wkv, BF16, 1024, 512, name="w_kv")
    x2 = _cross_attention(x1, kv, (norm_attn[:, None] * w_q).astype(BF16), w_o.astype(BF16), batch, seq, mem_n)

    w_r = jnp.concatenate([router_grp_w, router_exp_w,
                           jnp.zeros((d, 128 - N_GROUPS - N_EXPERTS), F32)], axis=1)
    w_r_hi = w_r.astype(BF16)
    w_r_lo = (w_r - w_r_hi.astype(F32)).astype(BF16)
    b_r = jnp.concatenate([router_grp_b, router_exp_b,
                           jnp.zeros((128 - N_GROUPS - N_EXPERTS,), F32)]).reshape(1, 128)
    xn3, info, cnt = _router(x2, norm_ffn, w_r_hi, w_r_lo, b_r)
    eid = info[:, 0:2].astype(jnp.int32)
    rank = info[:, 4:6].astype(jnp.int32)
    counts = cnt[0, :N_EXPERTS].astype(jnp.int32)
    nblk = (counts + MOE_BLK - 1) // MOE_BLK
    bstart = jnp.cumsum(nblk) - nblk
    n_used = jnp.sum(nblk)
    first = jnp.sum(jnp.where(eid[:, :, None] == jnp.arange(N_EXPERTS)[None, None, :], bstart, 0), axis=-1)
    dest = first * MOE_BLK + rank
    nb = (t * TOP_K) // MOE_BLK + N_EXPERTS
    tok_buf = jnp.zeros((nb * MOE_BLK,), jnp.int32).at[dest.reshape(-1)].set(
        jnp.repeat(jnp.arange(t, dtype=jnp.int32), TOP_K))
    blk = jnp.arange(nb, dtype=jnp.int32)
    block_expert = jnp.sum(blk[:, None] >= (bstart + nblk)[None, :], axis=1).astype(jnp.int32)
    last_e = jnp.max(jnp.where(nblk > 0, jnp.arange(N_EXPERTS), 0)).astype(jnp.int32)
    block_expert = jnp.minimum(block_expert, last_e)
    y_buf = _experts(block_expert, n_used.reshape(1).astype(jnp.int32), tok_buf.reshape(nb, 1, MOE_BLK),
                     xn3, exp_w_gate, exp_w_up, exp_w_down)
    out = _combine(dest[:, 0], dest[:, 1], x2, info, out_gain, y_buf)
    return out.reshape(batch, seq, d)


def kernel(x, mem, norm_mix, w_in, shift_taps, rwkv_w0, rwkv_w2, rwkv_a0, rwkv_a2, rwkv_g2, rwkv_k_k, rwkv_k_a, rwkv_r_k, rwkv_ln_w, rwkv_ln_b, s5_lam_re, s5_lam_im, s5_log_step, s5_b_re, s5_b_im, s5_c_re, s5_c_im, s5_d, s5_glu_w, s5_glu_b, w_out, norm_attn, norm_mem, w_q, w_k, w_v, w_o, norm_ffn, router_grp_w, router_grp_b, router_exp_w, router_exp_b, exp_w_gate, exp_w_up, exp_w_down, norm_final):
    return _layer(x, mem, norm_mix[0], w_in[0], shift_taps[0], rwkv_w0[0], rwkv_w2[0], rwkv_a0[0],
                  rwkv_a2[0], rwkv_g2[0], rwkv_k_k[0], rwkv_k_a[0], rwkv_r_k[0], rwkv_ln_w[0],
                  rwkv_ln_b[0], s5_lam_re[0], s5_lam_im[0], s5_log_step[0], s5_b_re[0], s5_b_im[0],
                  s5_c_re[0], s5_c_im[0], s5_d[0], s5_glu_w[0], s5_glu_b[0], w_out[0], norm_attn[0],
                  norm_mem[0], w_q[0], w_k[0], w_v[0], w_o[0], norm_ffn[0], router_grp_w[0],
                  router_grp_b[0], router_exp_w[0], router_exp_b[0], exp_w_gate[0], exp_w_up[0],
                  exp_w_down[0], norm_final)
```

```python
import functools
import math

import jax
import jax.numpy as jnp
from jax import lax
from jax.experimental import pallas as pl
from jax.experimental.pallas import tpu as pltpu

F32 = jnp.float32
BF16 = jnp.bfloat16

D_MODEL = 4096
D_RWKV = 2048
D_S5 = 2048
HEAD = 64
PAIR = 2 * HEAD
LORA_PAD = 128
N_LORA = 4 * LORA_PAD + 256
LORA_BLK = 1024
N_MAIN = 3 * D_RWKV + D_S5
S5_CH = 16
S5_GROUPS = D_S5 // S5_CH
S5_STATE = 64
S5_L = 8
S5_GB = 8
ATTN_HEADS = 4
ATTN_HEAD_DIM = D_MODEL // ATTN_HEADS
ATTN_SCALE = ATTN_HEAD_DIM ** -0.5
N_GROUPS = 8
EPG = 8
N_EXPERTS = N_GROUPS * EPG
TOP_K = 2
D_EXPERT = D_MODEL // 8
RMS_EPS = 1e-6
GN_EPS = 64e-5
L2_EPS = 1e-12

CHUNK = 64
SCAN_SUB = 4
SLAB = 2048
MOE_BLK = 288
COMB_TM = 256
EXPERT_KC = 1024
VMEM_LIMIT = 56 * 1024 * 1024


def _cparams(sem):
    return pltpu.CompilerParams(dimension_semantics=sem, vmem_limit_bytes=VMEM_LIMIT)


def _bdot(a, b):
    return jnp.dot(a.astype(BF16), b.astype(BF16), preferred_element_type=F32)


def _bdot_nt(a, b):
    return lax.dot_general(a.astype(BF16), b.astype(BF16), (((1,), (1,)), ((), ())),
                           preferred_element_type=F32)


def _split2(x):
    hi = x.astype(BF16)
    lo = (x - hi.astype(F32)).astype(BF16)
    return hi, lo


def _split3(x):
    hi = x.astype(BF16)
    r1 = x - hi.astype(F32)
    mid = r1.astype(BF16)
    lo = (r1 - mid.astype(F32)).astype(BF16)
    return hi, mid, lo


def _pack_bf16_pair(lo, hi):
    lo_b = lax.bitcast_convert_type(lo.astype(BF16).astype(F32), jnp.uint32) >> 16
    hi_b = lax.bitcast_convert_type(hi.astype(BF16).astype(F32), jnp.uint32) & jnp.uint32(0xFFFF0000)
    return hi_b | lo_b


def _unpack_bf16_pair(w):
    return (lax.bitcast_convert_type(w << 16, F32),
            lax.bitcast_convert_type(w & jnp.uint32(0xFFFF0000), F32))


def _dot_split2_rhs(x, e):
    h, l = _split2(x)
    return jnp.dot(h, e, preferred_element_type=F32) + jnp.dot(l, e, preferred_element_type=F32)


def _dot_exact_lhs(e, x):
    h, m, l = _split3(x)
    return (jnp.dot(e, h, preferred_element_type=F32) + jnp.dot(e, m, preferred_element_type=F32)
            + jnp.dot(e, l, preferred_element_type=F32))


def _rms_kernel(x_ref, g_ref, o_ref):
    x = x_ref[...]
    ms = jnp.mean(x * x, axis=-1, keepdims=True)
    o_ref[...] = (x * lax.rsqrt(ms + RMS_EPS) * g_ref[...]).astype(o_ref.dtype)


def _rmsnorm(x, gain, out_dtype, tm=256):
    t, d = x.shape
    return pl.pallas_call(
        _rms_kernel,
        grid=(t // tm,),
        in_specs=[pl.BlockSpec((tm, d), lambda i: (i, 0)), pl.BlockSpec((1, d), lambda i: (0, 0))],
        out_specs=pl.BlockSpec((tm, d), lambda i: (i, 0)),
        out_shape=jax.ShapeDtypeStruct((t, d), out_dtype),
        compiler_params=_cparams(("parallel",)),
        name="rmsnorm",
    )(x, gain.reshape(1, d))


def _mm_kernel(a_ref, b_ref, o_ref):
    o_ref[...] = jnp.dot(a_ref[...], b_ref[...], preferred_element_type=F32).astype(o_ref.dtype)


def _mm2_res_kernel(a1_ref, a2_ref, b_ref, r_ref, o_ref):
    k1 = a1_ref.shape[1]
    acc = jnp.dot(a1_ref[...], b_ref[:k1, :], preferred_element_type=F32)
    acc = acc + jnp.dot(a2_ref[...], b_ref[k1:, :], preferred_element_type=F32)
    o_ref[...] = (acc + r_ref[...]).astype(o_ref.dtype)


def _matmul(a, b, out_dtype, tm, tn, name):
    m, k = a.shape
    n = b.shape[1]
    tm, tn = min(tm, m), min(tn, n)
    return pl.pallas_call(
        _mm_kernel,
        grid=(m // tm, n // tn),
        in_specs=[pl.BlockSpec((tm, k), lambda i, j: (i, 0)), pl.BlockSpec((k, tn), lambda i, j: (0, j))],
        out_specs=pl.BlockSpec((tm, tn), lambda i, j: (i, j)),
        out_shape=jax.ShapeDtypeStruct((m, n), out_dtype),
        compiler_params=_cparams(("parallel", "parallel")),
        name=name,
    )(a, b)


def _matmul2_res(a1, a2, b, res, tm, tn):
    m, k1 = a1.shape
    k2 = a2.shape[1]
    n = b.shape[1]
    tm = min(tm, m)
    return pl.pallas_call(
        _mm2_res_kernel,
        grid=(m // tm, n // tn),
        in_specs=[pl.BlockSpec((tm, k1), lambda i, j: (i, 0)),
                  pl.BlockSpec((tm, k2), lambda i, j: (i, 0)),
                  pl.BlockSpec((k1 + k2, tn), lambda i, j: (0, j)),
                  pl.BlockSpec((tm, tn), lambda i, j: (i, j))],
        out_specs=pl.BlockSpec((tm, tn), lambda i, j: (i, j)),
        out_shape=jax.ShapeDtypeStruct((m, n), F32),
        compiler_params=_cparams(("parallel", "parallel")),
        name="w_out",
    )(a1, a2, b, res)


def _head_sum_bcast(x, e_ref, et_ref):
    s = _dot_split2_rhs(x, e_ref[...])
    return _dot_split2_rhs(s, et_ref[...])


def _shift3(x, hp, hn, taps, first, last):
    tm = x.shape[0]
    row = lax.broadcasted_iota(jnp.int32, x.shape, 0)
    prev_edge = jnp.where(first, 0.0, hp[7:8, :])
    next_edge = jnp.where(last, 0.0, hn[0:1, :])
    prev = jnp.where(row == 0, prev_edge, pltpu.roll(x, 1, 0))
    nxt = jnp.where(row == tm - 1, next_edge, pltpu.roll(x, tm - 1, 0))
    return taps[0:1, :] * prev + taps[1:2, :] * x + taps[2:3, :] * nxt


def _prep_kernel(seq_tiles,
                 r_ref, rp_ref, rn_ref, k_ref, kp_ref, kn_ref, v_ref, vp_ref, vn_ref,
                 lo_ref, lop_ref, lon_ref,
                 tr_ref, tk_ref, tv_ref, tl_ref,
                 w0_ref, w2_ref, a0_ref, a2_ref, g2_ref, kk_ref_p, e_ref, et_ref,
                 ro_ref, ko_ref, vo_ref, kko_ref, lw_ref, a_ref, g_ref):
    i = pl.program_id(0)
    first = (i % seq_tiles) == 0
    last = (i % seq_tiles) == seq_tiles - 1
    r = _shift3(r_ref[...], rp_ref[...], rn_ref[...], tr_ref[...], first, last)
    k = _shift3(k_ref[...], kp_ref[...], kn_ref[...], tk_ref[...], first, last)
    v = _shift3(v_ref[...], vp_ref[...], vn_ref[...], tv_ref[...], first, last)
    lo = _shift3(lo_ref[...], lop_ref[...], lon_ref[...], tl_ref[...], first, last)
    ro_ref[...] = r.astype(ro_ref.dtype)
    ko_ref[...] = k.astype(ko_ref.dtype)
    vo_ref[...] = v.astype(vo_ref.dtype)
    for d in range(2):
        xw = lo[:, d * LORA_PAD:(d + 1) * LORA_PAD]
        xa = lo[:, (2 + d) * LORA_PAD:(3 + d) * LORA_PAD]
        wl = w0_ref[d:d + 1, :] + _bdot(jnp.tanh(xw), w2_ref[d])
        lw_ref[d] = -math.exp(-0.5) * jax.nn.sigmoid(wl)
        a_ref[d] = jax.nn.sigmoid(a0_ref[d:d + 1, :] + _bdot(xa, a2_ref[d])).astype(a_ref.dtype)
    xg = lo[:, 4 * LORA_PAD:N_LORA]
    g_ref[...] = _bdot(jax.nn.sigmoid(xg), g2_ref[...]).astype(g_ref.dtype)
    kk = k * kk_ref_p[...]
    ssq = _head_sum_bcast(kk * kk, e_ref, et_ref)
    kko_ref[...] = (kk / jnp.maximum(jnp.sqrt(ssq), L2_EPS)).astype(kko_ref.dtype)


def _rwkv_prep(p_all, seq, taps_r, taps_k, taps_v, taps_l, w0, w2p, a0, a2p, g2, k_k, e_mat, et_mat,
               tm=128):
    t = p_all.shape[0]
    nt8 = t // 8
    seq_tiles = seq // tm
    c = D_RWKV

    def main_spec(col):
        return [pl.BlockSpec((tm, c), lambda i, col=col: (i, col)),
                pl.BlockSpec((8, c), lambda i, col=col: (jnp.maximum(i * (tm // 8) - 1, 0), col)),
                pl.BlockSpec((8, c), lambda i, col=col: (jnp.minimum((i + 1) * (tm // 8), nt8 - 1), col))]

    lcol = N_MAIN // LORA_BLK
    lora_spec = [pl.BlockSpec((tm, LORA_BLK), lambda i: (i, lcol)),
                 pl.BlockSpec((8, LORA_BLK), lambda i: (jnp.maximum(i * (tm // 8) - 1, 0), lcol)),
                 pl.BlockSpec((8, LORA_BLK), lambda i: (jnp.minimum((i + 1) * (tm // 8), nt8 - 1), lcol))]

    def full(shape):
        nd = len(shape)
        return pl.BlockSpec(shape, lambda i, nd=nd: (0,) * nd)

    in_specs = (main_spec(0) + main_spec(1) + main_spec(2) + lora_spec
                + [full((3, c)), full((3, c)), full((3, c)), full((3, LORA_BLK)),
                   full((2, c)), full((2, LORA_PAD, c)), full((2, c)), full((2, LORA_PAD, c)),
                   full((256, c)), full((1, c)), full((c, 128)), full((128, c))])
    row = pl.BlockSpec((tm, c), lambda i: (i, 0))
    row2 = pl.BlockSpec((2, tm, c), lambda i: (0, i, 0))
    sd = jax.ShapeDtypeStruct
    return pl.pallas_call(
        functools.partial(_prep_kernel, seq_tiles),
        grid=(t // tm,),
        in_specs=in_specs,
        out_specs=[row, row, row, row, row2, row2, row],
        out_shape=[sd((t, c), BF16), sd((t, c), BF16), sd((t, c), BF16), sd((t, c), BF16),
                   sd((2, t, c), F32), sd((2, t, c), BF16), sd((t, c), BF16)],
        compiler_params=_cparams(("parallel",)),
        name="rwkv_prep",
    )(p_all, p_all, p_all, p_all, p_all, p_all, p_all, p_all, p_all,
      p_all, p_all, p_all,
      taps_r, taps_k, taps_v, taps_l, w0, w2p, a0, a2p, g2, k_k, e_mat, et_mat)


def _scan_kernel(r_ref, k_ref, v_ref, kk_ref, lw_ref, a_ref, ka_ref, y_ref, st_ref):
    d = pl.program_id(0)
    c = pl.program_id(3)

    @pl.when(c == 0)
    def _():
        st_ref[...] = jnp.zeros_like(st_ref)

    ti = lax.broadcasted_iota(jnp.int32, (CHUNK, PAIR), 0)
    si = lax.broadcasted_iota(jnp.int32, (CHUNK, PAIR), 1) % HEAD
    ahead = (ti - si) * (1 - 2 * d)
    incl = ahead >= 0
    strict = ahead > 0
    eye = (si == ti).astype(F32)
    tri = incl[:, :CHUNK].astype(BF16)
    ka = ka_ref[...]

    npair = SLAB // PAIR
    lane = lax.broadcasted_iota(jnp.int32, (1, PAIR), 1)
    head0 = lane < HEAD

    def bd(y):
        return jnp.concatenate([jnp.where(head0, y, 0.0), jnp.where(head0, 0.0, y)], axis=0)

    sls = [slice(p * PAIR, (p + 1) * PAIR) for p in range(npair)]
    subs = []
    for u in range(SCAN_SUB):
        j = u + d * (SCAN_SUB - 1 - 2 * u)
        rows = pl.ds(pl.multiple_of(j * CHUNK, CHUNK), CHUNK)
        lw = lw_ref[rows, :]
        cum_incl = _dot_exact_lhs(tri, lw)
        e_incl = jnp.exp(cum_incl)
        e_excl = jnp.exp(cum_incl - lw)
        e_ninc = jnp.exp(-cum_incl)
        g_tot = jnp.exp(jnp.sum(lw, axis=0, keepdims=True))
        a = a_ref[rows, :].astype(F32)
        kk = kk_ref[rows, :].astype(F32)
        kd = k_ref[rows, :].astype(F32) * (1.0 + (a - 1.0) * ka)
        bt = (kk * a) * e_ninc
        kt = kd * e_ninc
        subs.append(dict(rows=rows, at=kk * e_excl, rt=r_ref[rows, :].astype(F32) * e_incl, bt=bt, kt=kt,
                         bh=bt * g_tot, kh=kt * g_tot, v=v_ref[rows, :].astype(F32), g_tot=g_tot))
    items = [(u, p) for u in range(SCAN_SUB) for p in range(npair)]
    ars = {(u, p): jnp.concatenate([subs[u]["at"][:, sls[p]], subs[u]["rt"][:, sls[p]]], axis=0).astype(BF16)
           for (u, p) in items}
    gs = {(u, p): _bdot_nt(ars[(u, p)], jnp.concatenate([bd(subs[u]["bt"][:, sls[p]]),
                                                        bd(subs[u]["kt"][:, sls[p]])], axis=0))
          for (u, p) in items}
    ms = {it: jnp.where(strict, -gs[it][:CHUNK, :PAIR], 0.0) for it in items}
    tinvs = {it: eye + ms[it] for it in items}
    ms = {it: _bdot(ms[it], bd(ms[it])) for it in items}
    for _ in range(4):
        ps = {it: _bdot(jnp.concatenate([tinvs[it], ms[it]], axis=0), bd(ms[it])) for it in items}
        tinvs = {it: tinvs[it] + ps[it][:CHUNK] for it in items}
        ms = {it: ps[it][CHUNK:] for it in items}
    tinvs = {it: tinvs[it] + _bdot(tinvs[it], bd(ms[it])) for it in items}
    x2s = {(u, p): _bdot(jnp.concatenate([jnp.where(strict, gs[(u, p)][:CHUNK, PAIR:], 0.0),
                                          jnp.where(incl, gs[(u, p)][CHUNK:, PAIR:], 0.0)], axis=0),
                         bd(subs[u]["v"][:, sls[p]])) for (u, p) in items}
    row = lax.broadcasted_iota(jnp.int32, (PAIR, PAIR), 0)
    same_head = (row < HEAD) == (lax.broadcasted_iota(jnp.int32, (PAIR, PAIR), 1) < HEAD)
    s_ps = [st_ref[p] for p in range(npair)]
    for u in range(SCAN_SUB):
        sub = subs[u]
        x1s = [_bdot_nt(ars[(u, p)], s_ps[p]) for p in range(npair)]
        us = [-_bdot(tinvs[(u, p)], bd(x1s[p][:CHUNK] + x2s[(u, p)][:CHUNK])) for p in range(npair)]
        outs = [x1s[p][CHUNK:] + x2s[(u, p)][CHUNK:]
                + _bdot(jnp.where(incl, gs[(u, p)][CHUNK:, :PAIR], 0.0), bd(us[p])) for p in range(npair)]
        nxt = []
        for p in range(npair):
            uv = jnp.concatenate([us[p], sub["v"][:, sls[p]]], axis=0).astype(BF16)
            bk = jnp.concatenate([sub["bh"][:, sls[p]], sub["kh"][:, sls[p]]], axis=0).astype(BF16)
            upd = lax.dot_general(uv, bk, (((0,), (0,)), ((), ())), preferred_element_type=F32)
            nxt.append(s_ps[p] * sub["g_tot"][:, sls[p]] + jnp.where(same_head, upd, 0.0))
        s_ps = nxt
        y_ref[sub["rows"], :] = jnp.concatenate(outs, axis=1)
    for p in range(npair):
        st_ref[p] = s_ps[p]


def _rwkv_scan(r, k, v, kk, lw, a, k_a, batch, seq):
    nc = seq // (CHUNK * SCAN_SUB)
    ns = D_RWKV // SLAB
    r4 = r.reshape(batch, seq, D_RWKV)
    k4 = k.reshape(batch, seq, D_RWKV)
    v4 = v.reshape(batch, seq, D_RWKV)
    kk4 = kk.reshape(batch, seq, D_RWKV)
    lw5 = lw.reshape(2, batch, seq, D_RWKV)
    a5 = a.reshape(2, batch, seq, D_RWKV)

    def tchunk(d, c):
        return c + d * (nc - 1 - 2 * c)

    rows = CHUNK * SCAN_SUB
    shared = pl.BlockSpec((None, rows, SLAB), lambda d, b, s, c: (b, tchunk(d, c), s))
    perdir = pl.BlockSpec((None, None, rows, SLAB), lambda d, b, s, c: (d, b, tchunk(d, c), s))
    y = pl.pallas_call(
        _scan_kernel,
        grid=(2, batch, ns, nc),
        in_specs=[shared, shared, shared, shared, perdir, perdir,
                  pl.BlockSpec((1, SLAB), lambda d, b, s, c: (0, s))],
        out_specs=perdir,
        out_shape=jax.ShapeDtypeStruct((2, batch, seq, D_RWKV), F32),
        scratch_shapes=[pltpu.VMEM((SLAB // PAIR, PAIR, PAIR), F32)],
        compiler_params=_cparams(("parallel", "parallel", "parallel", "arbitrary")),
        name="rwkv_scan",
    )(r4, k4, v4, kk4, lw5, a5, k_a)
    return y.reshape(2, batch * seq, D_RWKV)


def _post_kernel(y_ref, r_ref, k_ref, v_ref, a_ref, g_ref, ka_ref, rk_ref, lnw_ref, lnb_ref,
                 e_ref, et_ref, o_ref):
    y = y_ref[0] + y_ref[1]
    mu = _head_sum_bcast(y, e_ref, et_ref) * (1.0 / HEAD)
    yc = y - mu
    var = _head_sum_bcast(yc * yc, e_ref, et_ref) * (1.0 / HEAD)
    yn = yc * lax.rsqrt(var + GN_EPS) * lnw_ref[...] + lnb_ref[...]
    ka = ka_ref[...]
    k = k_ref[...].astype(F32)
    kd_sum = (k * (1.0 + (a_ref[0].astype(F32) - 1.0) * ka)
              + k * (1.0 + (a_ref[1].astype(F32) - 1.0) * ka))
    bonus = (_head_sum_bcast(r_ref[...].astype(F32) * kd_sum * rk_ref[...], e_ref, et_ref)
             * v_ref[...].astype(F32))
    o_ref[...] = ((yn + bonus) * g_ref[...].astype(F32)).astype(o_ref.dtype)


def _rwkv_post(y, r, k, v, a, g, k_a, r_k, ln_w, ln_b, e_mat, et_mat, tm=256):
    t, c = r.shape
    row = pl.BlockSpec((tm, c), lambda i: (i, 0))
    row2 = pl.BlockSpec((2, tm, c), lambda i: (0, i, 0))
    par = pl.BlockSpec((1, c), lambda i: (0, 0))
    return pl.pallas_call(
        _post_kernel,
        grid=(t // tm,),
        in_specs=[row2, row, row, row, row2, row, par, par, par, par,
                  pl.BlockSpec((c, 128), lambda i: (0, 0)), pl.BlockSpec((128, c), lambda i: (0, 0))],
        out_specs=row,
        out_shape=jax.ShapeDtypeStruct((t, c), BF16),
        compiler_params=_cparams(("parallel",)),
        name="rwkv_post",
    )(y, r, k, v, a, g, k_a, r_k, ln_w, ln_b, e_mat, et_mat)


def _s5_tables(lam_re, lam_im, log_step, b_re, b_im, c_re, c_im, d_skip, n_levels):
    L, P, CH, G, GB = S5_L, S5_STATE, S5_CH, S5_GROUPS, S5_GB
    NB = G // GB
    hi = lax.Precision.HIGHEST
    lam = lax.complex(lam_re.astype(F32), lam_im.astype(F32))
    dt = jnp.exp(log_step.astype(F32))[..., None]
    lam_dt = lam * dt
    lam_bar = jnp.exp(lam_dt)
    b = lax.complex(b_re.astype(F32), b_im.astype(F32))
    b_bar = ((lam_bar - 1.0) / lam)[..., None] * b
    c = lax.complex(c_re.astype(F32), c_im.astype(F32))
    taus = jnp.arange(L + 1, dtype=F32)
    pows = jnp.exp(lam_dt[:, :, None, :] * taus[None, None, :, None])
    kern = jnp.real(jnp.einsum('dgop,dgtp,dgpi->dgtoi', c, pows[:, :, :L], b_bar, precision=hi))
    j = jnp.arange(L)[:, None]
    t = jnp.arange(L)[None, :]
    lag = jnp.arange(L)[None, None, :]
    sel_f = ((t - j)[:, :, None] == lag).astype(F32)
    sel_b = ((j - t)[:, :, None] == lag).astype(F32)
    skip = (jnp.eye(L)[None, :, None, :, None] * jnp.eye(CH)[None, None, :, None, :]
            * d_skip.astype(F32).reshape(G, 1, CH, 1, 1))
    tsmall = (jnp.einsum('jtl,gloi->gjito', sel_f, kern[0], precision=hi)
              + jnp.einsum('jtl,gloi->gjito', sel_b, kern[1], precision=hi) + skip)
    def by_block(x, lead):
        x = x.reshape((NB, GB) + x.shape[1:])
        perm = (0,) + tuple(range(2, 2 + lead)) + (1,) + tuple(range(2 + lead, x.ndim))
        return x.transpose(perm)

    a_t = by_block(tsmall.reshape(G, L, CH, L * CH), 1).reshape(NB, L * GB * CH, L * CH)
    desc = jnp.exp(lam_dt[:, :, None, :] * (L - taus[:L])[None, None, :, None])
    desc1 = jnp.exp(lam_dt[0][:, None, :] * (L - 1 - taus[:L])[None, :, None])
    bb_t = b_bar.transpose(0, 1, 3, 2)
    zf = desc1[:, :, None, :] * bb_t[0][:, None]
    zb = pows[1][:, :L][:, :, None, :] * bb_t[1][:, None]
    wzs = jnp.stack([jnp.real(zf), jnp.imag(zf), jnp.real(zb), jnp.imag(zb)], axis=3)
    a_z = by_block(wzs.reshape(G, L, CH, 4 * P), 1).reshape(NB, L * GB * CH, 4 * P)
    c_t = c.transpose(0, 1, 3, 2)
    yf = c_t[0][:, :, None, :] * pows[0][:, 1:].transpose(0, 2, 1)[:, :, :, None]
    yb = c_t[1][:, :, None, :] * desc[1].transpose(0, 2, 1)[:, :, :, None]
    wys = jnp.stack([jnp.real(yf), -jnp.imag(yf), jnp.real(yb), -jnp.imag(yb)], axis=1)
    a_y = by_block(wys.reshape(G, 4, P, L * CH), 1).reshape(NB, 4 * GB * P, L * CH)
    steps = L * (2.0 ** jnp.arange(n_levels, dtype=F32))
    lp = jnp.exp(lam_dt[:, :, None, :] * steps[None, None, :, None])
    m = jnp.stack([jnp.real(lp[0]), jnp.imag(lp[0]), jnp.real(lp[1]), jnp.imag(lp[1])], axis=0)
    mult = m.reshape(4, NB, GB, n_levels, P).transpose(1, 3, 0, 2, 4).reshape(NB, n_levels, 4, GB * P)
    return a_t.astype(BF16), a_z.astype(BF16), a_y.astype(BF16), mult


def _group_of(shape, dim, width):
    return (lax.broadcasted_iota(jnp.int32, shape, dim) // width) % S5_GB


def _s5_kernel(n_levels, x_ref, at_ref, az_ref, ay_ref, rt_ref, rz_ref, mult_ref, y_ref, t_s, wz_s, wy_s):
    nck = x_ref.shape[0] // S5_L
    lanes = x_ref.shape[1]
    w = S5_GB * S5_STATE

    @pl.when(pl.program_id(1) == 0)
    def _():
        t = jnp.dot(at_ref[...], rt_ref[...], preferred_element_type=F32)
        keep = _group_of(t.shape, 0, S5_CH) == _group_of(t.shape, 1, S5_CH)
        t_s[...] = jnp.where(keep, t, 0.0).astype(BF16)
        z = jnp.dot(az_ref[...], rz_ref[...], preferred_element_type=F32)
        keep = _group_of(z.shape, 0, S5_CH) == _group_of(z.shape, 1, S5_STATE)
        wz_s[...] = jnp.where(keep, z, 0.0).astype(BF16)
        yy = jnp.dot(ay_ref[...], rt_ref[...], preferred_element_type=F32)
        keep = _group_of(yy.shape, 0, S5_STATE) == _group_of(yy.shape, 1, S5_CH)
        wy_s[...] = jnp.where(keep, yy, 0.0).astype(BF16)

    xcat = jnp.concatenate([x_ref[pl.ds(tl, nck, stride=S5_L), :].astype(BF16) for tl in range(S5_L)],
                           axis=1)
    z = jnp.dot(xcat, wz_s[...], preferred_element_type=F32)
    fre, fim, bre, bim = z[:, :w], z[:, w:2 * w], z[:, 2 * w:3 * w], z[:, 3 * w:]
    cidx = lax.broadcasted_iota(jnp.int32, (nck, w), 0)
    for i in range(n_levels):
        sh = 1 << i
        lfr, lfi = mult_ref[i, 0:1, :], mult_ref[i, 1:2, :]
        lbr, lbi = mult_ref[i, 2:3, :], mult_ref[i, 3:4, :]
        keep = cidx >= sh
        sr = jnp.where(keep, pltpu.roll(fre, sh, 0), 0.0)
        si = jnp.where(keep, pltpu.roll(fim, sh, 0), 0.0)
        fre, fim = fre + sr * lfr - si * lfi, fim + sr * lfi + si * lfr
        keep = cidx < nck - sh
        sr = jnp.where(keep, pltpu.roll(bre, nck - sh, 0), 0.0)
        si = jnp.where(keep, pltpu.roll(bim, nck - sh, 0), 0.0)
        bre, bim = bre + sr * lbr - si * lbi, bim + sr * lbi + si * lbr
    has_prev = cidx >= 1
    has_next = cidx < nck - 1
    xin = jnp.concatenate([jnp.where(has_prev, pltpu.roll(fre, 1, 0), 0.0),
                           jnp.where(has_prev, pltpu.roll(fim, 1, 0), 0.0),
                           jnp.where(has_next, pltpu.roll(bre, nck - 1, 0), 0.0),
                           jnp.where(has_next, pltpu.roll(bim, nck - 1, 0), 0.0)], axis=1).astype(BF16)
    y = jnp.dot(xcat, t_s[...], preferred_element_type=F32)
    y = y + jnp.dot(xin, wy_s[...], preferred_element_type=F32)
    for tl in range(S5_L):
        y_ref[pl.ds(tl, nck, stride=S5_L), :] = y[:, tl * lanes:(tl + 1) * lanes]


def _s5_mix(p_main, col0, a_t, a_z, a_y, mult, batch, seq, n_levels):
    nb = a_t.shape[0]
    lanes = S5_GB * S5_CH
    kw = S5_L * lanes
    sw = 4 * S5_GB * S5_STATE
    r_t = (jnp.arange(S5_L * S5_CH)[:, None] ==
           (jnp.arange(kw)[None, :] // lanes) * S5_CH + jnp.arange(kw)[None, :] % S5_CH).astype(BF16)
    r_z = (jnp.arange(4 * S5_STATE)[:, None] ==
           (jnp.arange(sw)[None, :] // (S5_GB * S5_STATE)) * S5_STATE + jnp.arange(sw)[None, :] % S5_STATE
           ).astype(BF16)
    return pl.pallas_call(
        functools.partial(_s5_kernel, n_levels),
        grid=(nb, batch),
        in_specs=[pl.BlockSpec((seq, lanes), lambda j, b: (b, col0 + j)),
                  pl.BlockSpec((None, kw, S5_L * S5_CH), lambda j, b: (j, 0, 0)),
                  pl.BlockSpec((None, kw, 4 * S5_STATE), lambda j, b: (j, 0, 0)),
                  pl.BlockSpec((None, sw, S5_L * S5_CH), lambda j, b: (j, 0, 0)),
                  pl.BlockSpec((S5_L * S5_CH, kw), lambda j, b: (0, 0)),
                  pl.BlockSpec((4 * S5_STATE, sw), lambda j, b: (0, 0)),
                  pl.BlockSpec((None, n_levels, 4, S5_GB * S5_STATE), lambda j, b: (j, 0, 0, 0))],
        out_specs=pl.BlockSpec((seq, lanes), lambda j, b: (b, j)),
        out_shape=jax.ShapeDtypeStruct((batch * seq, D_S5), F32),
        scratch_shapes=[pltpu.VMEM((kw, kw), BF16), pltpu.VMEM((kw, sw), BF16), pltpu.VMEM((sw, kw), BF16)],
        compiler_params=_cparams(("parallel", "arbitrary")),
        name="s5_mix",
    )(p_main, a_t, a_z, a_y, r_t, r_z, mult)


def _glu_kernel(y_ref, w_ref, b_ref, o_ref):
    y = y_ref[...]
    z = jnp.dot(jax.nn.gelu(y).astype(BF16), w_ref[...], preferred_element_type=F32) + b_ref[...]
    o_ref[...] = (y * jax.nn.sigmoid(z)).astype(o_ref.dtype)


def _glu(y, w, b, tm=1024):
    t, c = y.shape
    tm = min(tm, t)
    return pl.pallas_call(
        _glu_kernel,
        grid=(t // tm,),
        in_specs=[pl.BlockSpec((tm, c), lambda i: (i, 0)), pl.BlockSpec((c, c), lambda i: (0, 0)),
                  pl.BlockSpec((1, c), lambda i: (0, 0))],
        out_specs=pl.BlockSpec((tm, c), lambda i: (i, 0)),
        out_shape=jax.ShapeDtypeStruct((t, c), BF16),
        compiler_params=_cparams(("parallel",)),
        name="s5_glu",
    )(y, w, b)


def _wqk_kernel(wq_ref, k_ref, o_ref):
    acc = lax.dot_general(wq_ref[...], k_ref[...], (((1,), (1,)), ((), ())), preferred_element_type=F32)
    o_ref[...] = (acc * ATTN_SCALE).astype(o_ref.dtype)


def _vwo_kernel(v_ref, wo_ref, o_ref):
    o_ref[...] = jnp.dot(v_ref[...], wo_ref[...], preferred_element_type=F32).astype(o_ref.dtype)


def _attn_kernel(mem, x_ref, wqk_ref, vwo_ref, o_ref, p_ref):
    j = pl.program_id(2)
    tn = o_ref.shape[1]

    @pl.when(j == 0)
    def _():
        x = x_ref[...]
        rstd = lax.rsqrt(jnp.mean(x * x, axis=-1, keepdims=True) + RMS_EPS)
        s = jnp.dot(x.astype(BF16), wqk_ref[...], preferred_element_type=F32) * rstd
        for h in range(ATTN_HEADS):
            sh = s[:, h * mem:(h + 1) * mem]
            sh = sh - jnp.max(sh, axis=-1, keepdims=True)
            p = jnp.exp(sh)
            p_ref[:, h * mem:(h + 1) * mem] = (p / jnp.sum(p, axis=-1, keepdims=True)).astype(BF16)

    res = x_ref[:, pl.ds(pl.multiple_of(j * tn, tn), tn)]
    o_ref[...] = jnp.dot(p_ref[...], vwo_ref[...], preferred_element_type=F32) + res


def _cross_attention(x, kv, w_q, w_o, batch, seq, mem, tm=512, tn=1024):
    d = D_MODEL
    hm = ATTN_HEADS * mem
    tm = min(tm, seq)
    wqk = pl.pallas_call(
        _wqk_kernel,
        grid=(batch, ATTN_HEADS),
        in_specs=[pl.BlockSpec((d, ATTN_HEAD_DIM), lambda b, h: (0, h)),
                  pl.BlockSpec((mem, ATTN_HEAD_DIM), lambda b, h: (b, h))],
        out_specs=pl.BlockSpec((None, d, mem), lambda b, h: (b, 0, h)),
        out_shape=jax.ShapeDtypeStruct((batch, d, hm), BF16),
        compiler_params=_cparams(("parallel", "parallel")),
        name="attn_wqk",
    )(w_q, kv)
    vwo = pl.pallas_call(
        _vwo_kernel,
        grid=(batch, ATTN_HEADS),
        in_specs=[pl.BlockSpec((mem, ATTN_HEAD_DIM), lambda b, h: (b, ATTN_HEADS + h)),
                  pl.BlockSpec((ATTN_HEAD_DIM, d), lambda b, h: (h, 0))],
        out_specs=pl.BlockSpec((None, mem, d), lambda b, h: (b, h, 0)),
        out_shape=jax.ShapeDtypeStruct((batch, hm, d), BF16),
        compiler_params=_cparams(("parallel", "parallel")),
        name="attn_vwo",
    )(kv, w_o)
    out = pl.pallas_call(
        functools.partial(_attn_kernel, mem),
        grid=(batch, seq // tm, d // tn),
        in_specs=[pl.BlockSpec((None, tm, d), lambda b, i, j: (b, i, 0)),
                  pl.BlockSpec((None, d, hm), lambda b, i, j: (b, 0, 0)),
                  pl.BlockSpec((None, hm, tn), lambda b, i, j: (b, 0, j))],
        out_specs=pl.BlockSpec((None, tm, tn), lambda b, i, j: (b, i, j)),
        out_shape=jax.ShapeDtypeStruct((batch, seq, d), F32),
        scratch_shapes=[pltpu.VMEM((tm, hm), BF16)],
        compiler_params=_cparams(("parallel", "parallel", "arbitrary")),
        name="cross_attn",
    )(x.reshape(batch, seq, d), wqk, vwo)
    return out.reshape(batch * seq, d)


def _router_kernel(x_ref, g_ref, whi_ref, wlo_ref, b_ref, xn_ref, info_ref, cnt_ref, carry_ref):
    i = pl.program_id(0)

    @pl.when(i == 0)
    def _():
        carry_ref[...] = jnp.zeros_like(carry_ref)

    x = x_ref[...]
    tm = x.shape[0]
    ms = jnp.mean(x * x, axis=-1, keepdims=True)
    xn = x * lax.rsqrt(ms + RMS_EPS) * g_ref[...]
    half = xn.shape[1] // 2
    xn_ref[...] = _pack_bf16_pair(xn[:, :half], xn[:, half:])
    xh, xl = _split2(xn)
    whi = whi_ref[...]
    logits = (jnp.dot(xh, whi, preferred_element_type=F32) + jnp.dot(xl, whi, preferred_element_type=F32)
              + jnp.dot(xh, wlo_ref[...], preferred_element_type=F32)) + b_ref[...]
    li = lax.broadcasted_iota(jnp.int32, logits.shape, 1)
    neg = jnp.float32(-jnp.inf)
    is_g = li < N_GROUPS
    gl = jnp.where(is_g, logits, neg)
    gm = jnp.max(gl, axis=-1, keepdims=True)
    gi = jnp.min(jnp.where(is_g & (gl == gm), li, 128), axis=-1, keepdims=True)
    gp = 1.0 / jnp.sum(jnp.where(is_g, jnp.exp(gl - gm), 0.0), axis=-1, keepdims=True)
    lo_lane = N_GROUPS + EPG * gi
    sel = (li >= lo_lane) & (li < lo_lane + EPG)
    l1 = jnp.where(sel, logits, neg)
    e1 = jnp.max(l1, axis=-1, keepdims=True)
    i1 = jnp.min(jnp.where(sel & (l1 == e1), li, 128), axis=-1, keepdims=True)
    sel2 = sel & (li != i1)
    l2 = jnp.where(sel2, logits, neg)
    e2 = jnp.max(l2, axis=-1, keepdims=True)
    i2 = jnp.min(jnp.where(sel2 & (l2 == e2), li, 128), axis=-1, keepdims=True)
    ex = jnp.exp(e2 - e1)
    w1 = gp / (1.0 + ex)
    w2 = gp * ex / (1.0 + ex)
    x1 = i1 - N_GROUPS
    x2 = i2 - N_GROUPS
    oh1 = li == x1
    oh2 = li == x2
    oh = (oh1 | oh2).astype(BF16)
    ri = lax.broadcasted_iota(jnp.int32, (tm, tm), 0)
    ci = lax.broadcasted_iota(jnp.int32, (tm, tm), 1)
    before = jnp.dot((ci < ri).astype(BF16), oh, preferred_element_type=F32) + carry_ref[0:1, :]
    r1 = jnp.sum(jnp.where(oh1, before, 0.0), axis=-1, keepdims=True)
    r2 = jnp.sum(jnp.where(oh2, before, 0.0), axis=-1, keepdims=True)
    new_carry = carry_ref[0:1, :] + jnp.sum(oh.astype(F32), axis=0, keepdims=True)
    carry_ref[...] = jnp.broadcast_to(new_carry, carry_ref.shape)
    cnt_ref[...] = jnp.broadcast_to(new_carry, cnt_ref.shape)
    info = jnp.where(li == 0, x1.astype(F32), 0.0)
    info = jnp.where(li == 1, x2.astype(F32), info)
    info = jnp.where(li == 2, w1, info)
    info = jnp.where(li == 3, w2, info)
    info = jnp.where(li == 4, r1, info)
    info = jnp.where(li == 5, r2, info)
    info_ref[...] = info


def _router(x, gain, w_hi, w_lo, bias, tm=256):
    t, d = x.shape
    sd = jax.ShapeDtypeStruct
    return pl.pallas_call(
        _router_kernel,
        grid=(t // tm,),
        in_specs=[pl.BlockSpec((tm, d), lambda i: (i, 0)), pl.BlockSpec((1, d), lambda i: (0, 0)),
                  pl.BlockSpec((d, 128), lambda i: (0, 0)), pl.BlockSpec((d, 128), lambda i: (0, 0)),
                  pl.BlockSpec((1, 128), lambda i: (0, 0))],
        out_specs=[pl.BlockSpec((tm, d // 2), lambda i: (i, 0)), pl.BlockSpec((tm, 128), lambda i: (i, 0)),
                   pl.BlockSpec((8, 128), lambda i: (0, 0))],
        out_shape=[sd((t, d // 2), jnp.uint32), sd((t, 128), F32), sd((8, 128), F32)],
        scratch_shapes=[pltpu.VMEM((8, 128), F32)],
        compiler_params=_cparams(("arbitrary",)),
        name="moe_router",
    )(x, gain.reshape(1, d), w_hi, w_lo, bias)


def _gather_rows(src_hbm, idx_ref, dst_ref, sem, n_rows):
    def body(j, carry):
        pltpu.make_async_copy(src_hbm.at[pl.ds(idx_ref[0, j], 1), :], dst_ref.at[pl.ds(j, 1), :], sem).start()
        return carry
    lax.fori_loop(0, n_rows, body, 0, unroll=8)


def _wait_rows(src_hbm, dst_ref, sem, n_rows):
    pltpu.make_async_copy(src_hbm.at[pl.ds(0, n_rows), :], dst_ref, sem).wait()


def _issue_rows(src_hbm, idx_ref, dst_ref, sem, lo, hi):
    for j in range(lo, hi):
        pltpu.make_async_copy(src_hbm.at[pl.ds(idx_ref[0, j], 1), :], dst_ref.at[pl.ds(j, 1), :], sem).start()


def _expert_up_kernel(be_ref, nused_ref, tok_ref, tokn_ref, x_hbm, wg_ref, wu_ref, h_ref, xbuf, sem):
    i = pl.program_id(0)
    nb = pl.num_programs(0)
    n_used = nused_ref[0]
    slot = i % 2

    @pl.when(i == 0)
    def _():
        _gather_rows(x_hbm, tok_ref, xbuf.at[0], sem.at[0], MOE_BLK)

    @pl.when(i < n_used)
    def _():
        _wait_rows(x_hbm, xbuf.at[slot], sem.at[slot], MOE_BLK)
        half = xbuf.shape[2]
        nk = 2 * half // EXPERT_KC
        per = MOE_BLK // nk
        hg = jnp.zeros((MOE_BLK, D_EXPERT), F32)
        hu = jnp.zeros((MOE_BLK, D_EXPERT), F32)
        for kc in range(nk):
            _issue_rows(x_hbm, tokn_ref, xbuf.at[1 - slot], sem.at[1 - slot], kc * per, (kc + 1) * per)
            ks = slice(kc * EXPERT_KC, (kc + 1) * EXPERT_KC)
            wc = (kc * EXPERT_KC) % half
            xb = _unpack_bf16_pair(xbuf[slot, :, wc:wc + EXPERT_KC])[(kc * EXPERT_KC) // half].astype(BF16)
            hg = hg + jnp.dot(xb, wg_ref[ks, :].astype(BF16), preferred_element_type=F32)
            hu = hu + jnp.dot(xb, wu_ref[ks, :].astype(BF16), preferred_element_type=F32)
        h_ref[...] = (jax.nn.silu(hg) * hu).astype(h_ref.dtype)

        @pl.when(i == nb - 1)
        def _():
            _wait_rows(x_hbm, xbuf.at[1 - slot], sem.at[1 - slot], MOE_BLK)

    @pl.when(i >= n_used)
    def _():
        @pl.when(i == n_used)
        def _():
            _wait_rows(x_hbm, xbuf.at[slot], sem.at[slot], MOE_BLK)

        h_ref[...] = jnp.zeros_like(h_ref)


def _expert_down_kernel(be_ref, nused_ref, h_ref, wd_ref, y_ref):
    i = pl.program_id(0)
    n_used = nused_ref[0]

    @pl.when(i < n_used)
    def _():
        h = h_ref[...]
        half = y_ref.shape[1]
        for nc in range(half // EXPERT_KC):
            ns = slice(nc * EXPERT_KC, (nc + 1) * EXPERT_KC)
            nh = slice(half + nc * EXPERT_KC, half + (nc + 1) * EXPERT_KC)
            lo = jnp.dot(h, wd_ref[:, ns].astype(BF16), preferred_element_type=F32)
            hi = jnp.dot(h, wd_ref[:, nh].astype(BF16), preferred_element_type=F32)
            y_ref[:, ns] = _pack_bf16_pair(lo, hi)

    @pl.when(i >= n_used)
    def _():
        y_ref[...] = jnp.zeros_like(y_ref)


def _experts(block_expert, n_used, tok3, xn, wg, wu, wd):
    nb = tok3.shape[0]
    d = 2 * xn.shape[1]
    tok_spec = pl.BlockSpec((None, 1, MOE_BLK), lambda i, be, nu: (i, 0, 0), memory_space=pltpu.SMEM)
    tokn_spec = pl.BlockSpec((None, 1, MOE_BLK), lambda i, be, nu: (jnp.minimum(i + 1, nb - 1), 0, 0),
                             memory_space=pltpu.SMEM)
    h_buf = pl.pallas_call(
        _expert_up_kernel,
        grid_spec=pltpu.PrefetchScalarGridSpec(
            num_scalar_prefetch=2,
            grid=(nb,),
            in_specs=[tok_spec, tokn_spec, pl.BlockSpec(memory_space=pl.ANY),
                      pl.BlockSpec((None, d, D_EXPERT), lambda i, be, nu: (be[i], 0, 0)),
                      pl.BlockSpec((None, d, D_EXPERT), lambda i, be, nu: (be[i], 0, 0))],
            out_specs=pl.BlockSpec((MOE_BLK, D_EXPERT), lambda i, be, nu: (i, 0)),
            scratch_shapes=[pltpu.VMEM((2, MOE_BLK, d // 2), jnp.uint32), pltpu.SemaphoreType.DMA((2,))],
        ),
        out_shape=jax.ShapeDtypeStruct((nb * MOE_BLK, D_EXPERT), BF16),
        compiler_params=_cparams(("arbitrary",)),
        name="moe_up",
    )(block_expert, n_used, tok3, tok3, xn, wg, wu)
    return pl.pallas_call(
        _expert_down_kernel,
        grid_spec=pltpu.PrefetchScalarGridSpec(
            num_scalar_prefetch=2,
            grid=(nb,),
            in_specs=[pl.BlockSpec((MOE_BLK, D_EXPERT), lambda i, be, nu: (i, 0)),
                      pl.BlockSpec((None, D_EXPERT, d), lambda i, be, nu: (be[i], 0, 0))],
            out_specs=pl.BlockSpec((MOE_BLK, d // 2), lambda i, be, nu: (i, 0)),
        ),
        out_shape=jax.ShapeDtypeStruct((nb * MOE_BLK, d // 2), jnp.uint32),
        compiler_params=_cparams(("arbitrary",)),
        name="moe_down",
    )(block_expert, n_used, h_buf, wd)


def _combine_kernel(d1_ref, d1n_ref, d2_ref, d2n_ref, x_ref, info_ref, g_ref, y_hbm, o_ref, ybuf, sem):
    i = pl.program_id(0)
    n = pl.num_programs(0)
    slot = i % 2

    @pl.when(i == 0)
    def _():
        _gather_rows(y_hbm, d1_ref, ybuf.at[0, 0], sem.at[0], COMB_TM)
        _gather_rows(y_hbm, d2_ref, ybuf.at[0, 1], sem.at[0], COMB_TM)

    _wait_rows(y_hbm, ybuf.at[slot, 0], sem.at[slot], COMB_TM)
    _wait_rows(y_hbm, ybuf.at[slot, 1], sem.at[slot], COMB_TM)
    _issue_rows(y_hbm, d1n_ref, ybuf.at[1 - slot, 0], sem.at[1 - slot], 0, COMB_TM)
    _issue_rows(y_hbm, d2n_ref, ybuf.at[1 - slot, 1], sem.at[1 - slot], 0, COMB_TM)
    info = info_ref[...]
    w1 = info[:, 2:3]
    w2 = info[:, 3:4]
    half = ybuf.shape[3]
    lo1, hi1 = _unpack_bf16_pair(ybuf[slot, 0])
    lo2, hi2 = _unpack_bf16_pair(ybuf[slot, 1])
    xlo = x_ref[:, :half] + lo1 * w1 + lo2 * w2
    xhi = x_ref[:, half:] + hi1 * w1 + hi2 * w2
    ms = (jnp.sum(xlo * xlo, axis=-1, keepdims=True) + jnp.sum(xhi * xhi, axis=-1, keepdims=True)) / (2 * half)
    rstd = lax.rsqrt(ms + RMS_EPS)
    o_ref[:, :half] = xlo * rstd * g_ref[:, :half]
    o_ref[:, half:] = xhi * rstd * g_ref[:, half:]

    @pl.when(i == n - 1)
    def _():
        _wait_rows(y_hbm, ybuf.at[1 - slot, 0], sem.at[1 - slot], COMB_TM)
        _wait_rows(y_hbm, ybuf.at[1 - slot, 1], sem.at[1 - slot], COMB_TM)


def _combine(dest1, dest2, x, info, gain, y_buf):
    t, d = x.shape
    nt = t // COMB_TM
    d1 = dest1.reshape(nt, 1, COMB_TM)
    d2 = dest2.reshape(nt, 1, COMB_TM)
    cur = pl.BlockSpec((None, 1, COMB_TM), lambda i: (i, 0, 0), memory_space=pltpu.SMEM)
    nxt = pl.BlockSpec((None, 1, COMB_TM), lambda i: (jnp.minimum(i + 1, nt - 1), 0, 0),
                       memory_space=pltpu.SMEM)
    return pl.pallas_call(
        _combine_kernel,
        grid=(nt,),
        in_specs=[cur, nxt, cur, nxt,
                  pl.BlockSpec((COMB_TM, d), lambda i: (i, 0)),
                  pl.BlockSpec((COMB_TM, 128), lambda i: (i, 0)),
                  pl.BlockSpec((1, d), lambda i: (0, 0)),
                  pl.BlockSpec(memory_space=pl.ANY)],
        out_specs=pl.BlockSpec((COMB_TM, d), lambda i: (i, 0)),
        out_shape=jax.ShapeDtypeStruct((t, d), F32),
        scratch_shapes=[pltpu.VMEM((2, 2, COMB_TM, d // 2), jnp.uint32), pltpu.SemaphoreType.DMA((2,))],
        compiler_params=_cparams(("arbitrary",)),
        name="moe_combine",
    )(d1, d1, d2, d2, x, info, gain.reshape(1, d), y_buf)


def _head_matrices():
    lane = jnp.arange(D_RWKV) // HEAD
    e = (lane[:, None] == jnp.arange(128)[None, :]).astype(BF16)
    return e, e.T


def _pad_lora_rows(w):
    return jnp.pad(w, ((0, 0), (0, LORA_PAD - w.shape[1]), (0, 0)))


def _layer(x, mem, norm_mix, w_in, shift_taps, rwkv_w0, rwkv_w2, rwkv_a0, rwkv_a2, rwkv_g2, rwkv_k_k,
           rwkv_k_a, rwkv_r_k, rwkv_ln_w, rwkv_ln_b, s5_lam_re, s5_lam_im, s5_log_step, s5_b_re, s5_b_im,
           s5_c_re, s5_c_im, s5_d, s5_glu_w, s5_glu_b, w_out, norm_attn, norm_mem, w_q, w_k, w_v, w_o,
           norm_ffn, router_grp_w, router_grp_b, router_exp_w, router_exp_b, exp_w_gate, exp_w_up,
           exp_w_down, out_gain):
    batch, seq, d = x.shape
    t = batch * seq
    mem_n = mem.shape[1]
    xt = x.reshape(t, d)
    c3 = 3 * D_RWKV
    off_g = c3 + 4 * 96

    def pad_cols(w, lo, width):
        return jnp.pad(w[:, lo:lo + width], ((0, 0), (0, LORA_PAD - width)))

    w_all = jnp.concatenate([w_in[:, :c3], w_in[:, off_g + 256:]]
                            + [pad_cols(w_in, c3 + i * 96, 96) for i in range(4)]
                            + [w_in[:, off_g:off_g + 256], jnp.zeros((d, LORA_BLK - N_LORA), F32)],
                            axis=1).astype(BF16)
    taps_l = jnp.concatenate([pad_cols(shift_taps, c3 + i * 96, 96) for i in range(4)]
                             + [shift_taps[:, off_g:off_g + 256], jnp.zeros((3, LORA_BLK - N_LORA), F32)], axis=1)
    h = _rmsnorm(xt, norm_mix, BF16)
    p_all = _matmul(h, w_all, F32, 1024, LORA_BLK, name="w_in")

    e_mat, et_mat = _head_matrices()
    k_a = rwkv_k_a.reshape(1, D_RWKV)
    r, k, v, kk, lw, a, g = _rwkv_prep(
        p_all, seq, shift_taps[:, :D_RWKV], shift_taps[:, D_RWKV:2 * D_RWKV],
        shift_taps[:, 2 * D_RWKV:c3], taps_l, rwkv_w0, _pad_lora_rows(rwkv_w2).astype(BF16), rwkv_a0,
        _pad_lora_rows(rwkv_a2).astype(BF16), rwkv_g2.astype(BF16), rwkv_k_k.reshape(1, D_RWKV),
        e_mat, et_mat)
    y_scan = _rwkv_scan(r, k, v, kk, lw, a, k_a, batch, seq)
    y_rwkv = _rwkv_post(y_scan, r, k, v, a, g, k_a, rwkv_r_k.reshape(1, D_RWKV),
                        rwkv_ln_w.reshape(1, D_RWKV), rwkv_ln_b.reshape(1, D_RWKV), e_mat, et_mat)

    n_levels = int(math.log2(seq // S5_L))
    a_t, a_z, a_y, mult = _s5_tables(s5_lam_re, s5_lam_im, s5_log_step, s5_b_re, s5_b_im,
                                    s5_c_re, s5_c_im, s5_d, n_levels)
    y_s5 = _s5_mix(p_all, c3 // (S5_GB * S5_CH), a_t, a_z, a_y, mult, batch, seq, n_levels)
    y_glu = _glu(y_s5, s5_glu_w.astype(BF16), s5_glu_b.reshape(1, D_S5))

    x1 = _matmul2_res(y_rwkv, y_glu, w_out.astype(BF16), xt, 1024, 512)

    memn = _rmsnorm(mem.reshape(batch * mem_n, d), norm_mem, BF16)
    wkv = jnp.concatenate([w_k, w_v], axis=1).astype(BF16)
    kv = _matmul(memn, wkv, BF16, 1024, 512, name="w_kv")
    x2 = _cross_attention(x1, kv, (norm_attn[:, None] * w_q).astype(BF16), w_o.astype(BF16), batch, seq, mem_n)

    w_r = jnp.concatenate([router_grp_w, router_exp_w,
                           jnp.zeros((d, 128 - N_GROUPS - N_EXPERTS), F32)], axis=1)
    w_r_hi = w_r.astype(BF16)
    w_r_lo = (w_r - w_r_hi.astype(F32)).astype(BF16)
    b_r = jnp.concatenate([router_grp_b, router_exp_b,
                           jnp.zeros((128 - N_GROUPS - N_EXPERTS,), F32)]).reshape(1, 128)
    xn3, info, cnt = _router(x2, norm_ffn, w_r_hi, w_r_lo, b_r)
    eid = info[:, 0:2].astype(jnp.int32)
    rank = info[:, 4:6].astype(jnp.int32)
    counts = cnt[0, :N_EXPERTS].astype(jnp.int32)
    nblk = (counts + MOE_BLK - 1) // MOE_BLK
    bstart = jnp.cumsum(nblk) - nblk
    n_used = jnp.sum(nblk)
    first = jnp.sum(jnp.where(eid[:, :, None] == jnp.arange(N_EXPERTS)[None, None, :], bstart, 0), axis=-1)
    dest = first * MOE_BLK + rank
    nb = (t * TOP_K) // MOE_BLK + N_EXPERTS
    tok_buf = jnp.zeros((nb * MOE_BLK,), jnp.int32).at[dest.reshape(-1)].set(
        jnp.repeat(jnp.arange(t, dtype=jnp.int32), TOP_K))
    blk = jnp.arange(nb, dtype=jnp.int32)
    block_expert = jnp.sum(blk[:, None] >= (bstart + nblk)[None, :], axis=1).astype(jnp.int32)
    last_e = jnp.max(jnp.where(nblk > 0, jnp.arange(N_EXPERTS), 0)).astype(jnp.int32)
    block_expert = jnp.minimum(block_expert, last_e)
    y_buf = _experts(block_expert, n_used.reshape(1).astype(jnp.int32), tok_buf.reshape(nb, 1, MOE_BLK),
                     xn3, exp_w_gate, exp_w_up, exp_w_down)
    out = _combine(dest[:, 0], dest[:, 1], x2, info, out_gain, y_buf)
    return out.reshape(batch, seq, d)


def kernel(x, mem, norm_mix, w_in, shift_taps, rwkv_w0, rwkv_w2, rwkv_a0, rwkv_a2, rwkv_g2, rwkv_k_k, rwkv_k_a, rwkv_r_k, rwkv_ln_w, rwkv_ln_b, s5_lam_re, s5_lam_im, s5_log_step, s5_b_re, s5_b_im, s5_c_re, s5_c_im, s5_d, s5_glu_w, s5_glu_b, w_out, norm_attn, norm_mem, w_q, w_k, w_v, w_o, norm_ffn, router_grp_w, router_grp_b, router_exp_w, router_exp_b, exp_w_gate, exp_w_up, exp_w_down, norm_final):
    return _layer(x, mem, norm_mix[0], w_in[0], shift_taps[0], rwkv_w0[0], rwkv_w2[0], rwkv_a0[0],
                  rwkv_a2[0], rwkv_g2[0], rwkv_k_k[0], rwkv_k_a[0], rwkv_r_k[0], rwkv_ln_w[0],
                  rwkv_ln_b[0], s5_lam_re[0], s5_lam_im[0], s5_log_step[0], s5_b_re[0], s5_b_im[0],
                  s5_c_re[0], s5_c_im[0], s5_d[0], s5_glu_w[0], s5_glu_b[0], w_out[0], norm_attn[0],
                  norm_mem[0], w_q[0], w_k[0], w_v[0], w_o[0], norm_ffn[0], router_grp_w[0],
                  router_grp_b[0], router_exp_w[0], router_exp_b[0], exp_w_gate[0], exp_w_up[0],
                  exp_w_down[0], norm_final)
```

```python
import functools
import math

import jax
import jax.numpy as jnp
from jax import lax
from jax.experimental import pallas as pl
from jax.experimental.pallas import tpu as pltpu

F32 = jnp.float32
BF16 = jnp.bfloat16

D_MODEL = 4096
D_RWKV = 2048
D_S5 = 2048
HEAD = 64
PAIR = 2 * HEAD
LORA_PAD = 128
N_LORA = 4 * LORA_PAD + 256
LORA_BLK = 1024
N_MAIN = 3 * D_RWKV + D_S5
S5_CH = 16
S5_GROUPS = D_S5 // S5_CH
S5_STATE = 64
S5_L = 8
S5_GB = 8
ATTN_HEADS = 4
ATTN_HEAD_DIM = D_MODEL // ATTN_HEADS
ATTN_SCALE = ATTN_HEAD_DIM ** -0.5
N_GROUPS = 8
EPG = 8
N_EXPERTS = N_GROUPS * EPG
TOP_K = 2
D_EXPERT = D_MODEL // 8
RMS_EPS = 1e-6
GN_EPS = 64e-5
L2_EPS = 1e-12

CHUNK = 64
SCAN_SUB = 4
SLAB = 2048
MOE_BLK = 288
COMB_TM = 256
EXPERT_KC = 1024
VMEM_LIMIT = 56 * 1024 * 1024


def _cparams(sem):
    return pltpu.CompilerParams(dimension_semantics=sem, vmem_limit_bytes=VMEM_LIMIT)


def _bdot(a, b):
    return jnp.dot(a.astype(BF16), b.astype(BF16), preferred_element_type=F32)


def _bdot_nt(a, b):
    return lax.dot_general(a.astype(BF16), b.astype(BF16), (((1,), (1,)), ((), ())),
                           preferred_element_type=F32)


def _split2(x):
    hi = x.astype(BF16)
    lo = (x - hi.astype(F32)).astype(BF16)
    return hi, lo


def _split3(x):
    hi = x.astype(BF16)
    r1 = x - hi.astype(F32)
    mid = r1.astype(BF16)
    lo = (r1 - mid.astype(F32)).astype(BF16)
    return hi, mid, lo


def _pack_bf16_pair(lo, hi):
    lo_b = lax.bitcast_convert_type(lo.astype(BF16).astype(F32), jnp.uint32) >> 16
    hi_b = lax.bitcast_convert_type(hi.astype(BF16).astype(F32), jnp.uint32) & jnp.uint32(0xFFFF0000)
    return hi_b | lo_b


def _unpack_bf16_pair(w):
    return (lax.bitcast_convert_type(w << 16, F32),
            lax.bitcast_convert_type(w & jnp.uint32(0xFFFF0000), F32))


def _dot_split2_rhs(x, e):
    h, l = _split2(x)
    return jnp.dot(h, e, preferred_element_type=F32) + jnp.dot(l, e, preferred_element_type=F32)


def _dot_exact_lhs(e, x):
    h, m, l = _split3(x)
    return (jnp.dot(e, h, preferred_element_type=F32) + jnp.dot(e, m, preferred_element_type=F32)
            + jnp.dot(e, l, preferred_element_type=F32))


def _rms_kernel(x_ref, g_ref, o_ref):
    x = x_ref[...]
    ms = jnp.mean(x * x, axis=-1, keepdims=True)
    o_ref[...] = (x * lax.rsqrt(ms + RMS_EPS) * g_ref[...]).astype(o_ref.dtype)


def _rmsnorm(x, gain, out_dtype, tm=256):
    t, d = x.shape
    return pl.pallas_call(
        _rms_kernel,
        grid=(t // tm,),
        in_specs=[pl.BlockSpec((tm, d), lambda i: (i, 0)), pl.BlockSpec((1, d), lambda i: (0, 0))],
        out_specs=pl.BlockSpec((tm, d), lambda i: (i, 0)),
        out_shape=jax.ShapeDtypeStruct((t, d), out_dtype),
        compiler_params=_cparams(("parallel",)),
        name="rmsnorm",
    )(x, gain.reshape(1, d))


def _mm_kernel(a_ref, b_ref, o_ref):
    o_ref[...] = jnp.dot(a_ref[...], b_ref[...], preferred_element_type=F32).astype(o_ref.dtype)


def _mm2_res_kernel(a1_ref, a2_ref, b_ref, r_ref, o_ref):
    k1 = a1_ref.shape[1]
    acc = jnp.dot(a1_ref[...], b_ref[:k1, :], preferred_element_type=F32)
    acc = acc + jnp.dot(a2_ref[...], b_ref[k1:, :], preferred_element_type=F32)
    o_ref[...] = (acc + r_ref[...]).astype(o_ref.dtype)


def _matmul(a, b, out_dtype, tm, tn, name):
    m, k = a.shape
    n = b.shape[1]
    tm, tn = min(tm, m), min(tn, n)
    return pl.pallas_call(
        _mm_kernel,
        grid=(m // tm, n // tn),
        in_specs=[pl.BlockSpec((tm, k), lambda i, j: (i, 0)), pl.BlockSpec((k, tn), lambda i, j: (0, j))],
        out_specs=pl.BlockSpec((tm, tn), lambda i, j: (i, j)),
        out_shape=jax.ShapeDtypeStruct((m, n), out_dtype),
        compiler_params=_cparams(("parallel", "parallel")),
        name=name,
    )(a, b)


def _matmul2_res(a1, a2, b, res, tm, tn):
    m, k1 = a1.shape
    k2 = a2.shape[1]
    n = b.shape[1]
    tm = min(tm, m)
    return pl.pallas_call(
        _mm2_res_kernel,
        grid=(m // tm, n // tn),
        in_specs=[pl.BlockSpec((tm, k1), lambda i, j: (i, 0)),
                  pl.BlockSpec((tm, k2), lambda i, j: (i, 0)),
                  pl.BlockSpec((k1 + k2, tn), lambda i, j: (0, j)),
                  pl.BlockSpec((tm, tn), lambda i, j: (i, j))],
        out_specs=pl.BlockSpec((tm, tn), lambda i, j: (i, j)),
        out_shape=jax.ShapeDtypeStruct((m, n), F32),
        compiler_params=_cparams(("parallel", "parallel")),
        name="w_out",
    )(a1, a2, b, res)


def _head_sum_bcast(x, e_ref, et_ref):
    s = _dot_split2_rhs(x, e_ref[...])
    return _dot_split2_rhs(s, et_ref[...])


def _shift3(x, hp, hn, taps, first, last):
    tm = x.shape[0]
    row = lax.broadcasted_iota(jnp.int32, x.shape, 0)
    prev_edge = jnp.where(first, 0.0, hp[7:8, :])
    next_edge = jnp.where(last, 0.0, hn[0:1, :])
    prev = jnp.where(row == 0, prev_edge, pltpu.roll(x, 1, 0))
    nxt = jnp.where(row == tm - 1, next_edge, pltpu.roll(x, tm - 1, 0))
    return taps[0:1, :] * prev + taps[1:2, :] * x + taps[2:3, :] * nxt


def _prep_kernel(seq_tiles,
                 r_ref, rp_ref, rn_ref, k_ref, kp_ref, kn_ref, v_ref, vp_ref, vn_ref,
                 lo_ref, lop_ref, lon_ref,
                 tr_ref, tk_ref, tv_ref, tl_ref,
                 w0_ref, w2_ref, a0_ref, a2_ref, g2_ref, kk_ref_p, e_ref, et_ref,
                 ro_ref, ko_ref, vo_ref, kko_ref, lw_ref, a_ref, g_ref):
    i = pl.program_id(0)
    first = (i % seq_tiles) == 0
    last = (i % seq_tiles) == seq_tiles - 1
    r = _shift3(r_ref[...], rp_ref[...], rn_ref[...], tr_ref[...], first, last)
    k = _shift3(k_ref[...], kp_ref[...], kn_ref[...], tk_ref[...], first, last)
    v = _shift3(v_ref[...], vp_ref[...], vn_ref[...], tv_ref[...], first, last)
    lo = _shift3(lo_ref[...], lop_ref[...], lon_ref[...], tl_ref[...], first, last)
    ro_ref[...] = r.astype(ro_ref.dtype)
    ko_ref[...] = k.astype(ko_ref.dtype)
    vo_ref[...] = v.astype(vo_ref.dtype)
    for d in range(2):
        xw = lo[:, d * LORA_PAD:(d + 1) * LORA_PAD]
        xa = lo[:, (2 + d) * LORA_PAD:(3 + d) * LORA_PAD]
        wl = w0_ref[d:d + 1, :] + _bdot(jnp.tanh(xw), w2_ref[d])
        lw_ref[d] = -math.exp(-0.5) * jax.nn.sigmoid(wl)
        a_ref[d] = jax.nn.sigmoid(a0_ref[d:d + 1, :] + _bdot(xa, a2_ref[d])).astype(a_ref.dtype)
    xg = lo[:, 4 * LORA_PAD:N_LORA]
    g_ref[...] = _bdot(jax.nn.sigmoid(xg), g2_ref[...]).astype(g_ref.dtype)
    kk = k * kk_ref_p[...]
    ssq = _head_sum_bcast(kk * kk, e_ref, et_ref)
    kko_ref[...] = (kk / jnp.maximum(jnp.sqrt(ssq), L2_EPS)).astype(kko_ref.dtype)


def _rwkv_prep(p_all, seq, taps_r, taps_k, taps_v, taps_l, w0, w2p, a0, a2p, g2, k_k, e_mat, et_mat,
               tm=128):
    t = p_all.shape[0]
    nt8 = t // 8
    seq_tiles = seq // tm
    c = D_RWKV

    def main_spec(col):
        return [pl.BlockSpec((tm, c), lambda i, col=col: (i, col)),
                pl.BlockSpec((8, c), lambda i, col=col: (jnp.maximum(i * (tm // 8) - 1, 0), col)),
                pl.BlockSpec((8, c), lambda i, col=col: (jnp.minimum((i + 1) * (tm // 8), nt8 - 1), col))]

    lcol = N_MAIN // LORA_BLK
    lora_spec = [pl.BlockSpec((tm, LORA_BLK), lambda i: (i, lcol)),
                 pl.BlockSpec((8, LORA_BLK), lambda i: (jnp.maximum(i * (tm // 8) - 1, 0), lcol)),
                 pl.BlockSpec((8, LORA_BLK), lambda i: (jnp.minimum((i + 1) * (tm // 8), nt8 - 1), lcol))]

    def full(shape):
        nd = len(shape)
        return pl.BlockSpec(shape, lambda i, nd=nd: (0,) * nd)

    in_specs = (main_spec(0) + main_spec(1) + main_spec(2) + lora_spec
                + [full((3, c)), full((3, c)), full((3, c)), full((3, LORA_BLK)),
                   full((2, c)), full((2, LORA_PAD, c)), full((2, c)), full((2, LORA_PAD, c)),
                   full((256, c)), full((1, c)), full((c, 128)), full((128, c))])
    row = pl.BlockSpec((tm, c), lambda i: (i, 0))
    row2 = pl.BlockSpec((2, tm, c), lambda i: (0, i, 0))
    sd = jax.ShapeDtypeStruct
    return pl.pallas_call(
        functools.partial(_prep_kernel, seq_tiles),
        grid=(t // tm,),
        in_specs=in_specs,
        out_specs=[row, row, row, row, row2, row2, row],
        out_shape=[sd((t, c), BF16), sd((t, c), BF16), sd((t, c), BF16), sd((t, c), BF16),
                   sd((2, t, c), F32), sd((2, t, c), BF16), sd((t, c), BF16)],
        compiler_params=_cparams(("parallel",)),
        name="rwkv_prep",
    )(p_all, p_all, p_all, p_all, p_all, p_all, p_all, p_all, p_all,
      p_all, p_all, p_all,
      taps_r, taps_k, taps_v, taps_l, w0, w2p, a0, a2p, g2, k_k, e_mat, et_mat)


def _scan_kernel(r_ref, k_ref, v_ref, kk_ref, lw_ref, a_ref, ka_ref, y_ref, st_ref):
    d = pl.program_id(0)
    c = pl.program_id(3)

    @pl.when(c == 0)
    def _():
        st_ref[...] = jnp.zeros_like(st_ref)

    ti = lax.broadcasted_iota(jnp.int32, (CHUNK, PAIR), 0)
    si = lax.broadcasted_iota(jnp.int32, (CHUNK, PAIR), 1) % HEAD
    ahead = (ti - si) * (1 - 2 * d)
    incl = ahead >= 0
    strict = ahead > 0
    eye = (si == ti).astype(F32)
    tri = incl[:, :CHUNK].astype(BF16)
    ka = ka_ref[...]

    npair = SLAB // PAIR
    lane = lax.broadcasted_iota(jnp.int32, (1, PAIR), 1)
    head0 = lane < HEAD

    def bd(y):
        return jnp.concatenate([jnp.where(head0, y, 0.0), jnp.where(head0, 0.0, y)], axis=0)

    sls = [slice(p * PAIR, (p + 1) * PAIR) for p in range(npair)]
    subs = []
    for u in range(SCAN_SUB):
        j = u + d * (SCAN_SUB - 1 - 2 * u)
        rows = pl.ds(pl.multiple_of(j * CHUNK, CHUNK), CHUNK)
        lw = lw_ref[rows, :]
        cum_incl = _dot_exact_lhs(tri, lw)
        e_incl = jnp.exp(cum_incl)
        e_excl = jnp.exp(cum_incl - lw)
        e_ninc = jnp.exp(-cum_incl)
        g_tot = jnp.exp(jnp.sum(lw, axis=0, keepdims=True))
        a = a_ref[rows, :].astype(F32)
        kk = kk_ref[rows, :].astype(F32)
        kd = k_ref[rows, :].astype(F32) * (1.0 + (a - 1.0) * ka)
        bt = (kk * a) * e_ninc
        kt = kd * e_ninc
        subs.append(dict(rows=rows, at=kk * e_excl, rt=r_ref[rows, :].astype(F32) * e_incl, bt=bt, kt=kt,
                         bh=bt * g_tot, kh=kt * g_tot, v=v_ref[rows, :].astype(F32), g_tot=g_tot))
    items = [(u, p) for u in range(SCAN_SUB) for p in range(npair)]
    ars = {(u, p): jnp.concatenate([subs[u]["at"][:, sls[p]], subs[u]["rt"][:, sls[p]]], axis=0).astype(BF16)
           for (u, p) in items}
    gs = {(u, p): _bdot_nt(ars[(u, p)], jnp.concatenate([bd(subs[u]["bt"][:, sls[p]]),
                                                        bd(subs[u]["kt"][:, sls[p]])], axis=0))
          for (u, p) in items}
    ms = {it: jnp.where(strict, -gs[it][:CHUNK, :PAIR], 0.0) for it in items}
    tinvs = {it: eye + ms[it] for it in items}
    ms = {it: _bdot(ms[it], bd(ms[it])) for it in items}
    for _ in range(4):
        ps = {it: _bdot(jnp.concatenate([tinvs[it], ms[it]], axis=0), bd(ms[it])) for it in items}
        tinvs = {it: tinvs[it] + ps[it][:CHUNK] for it in items}
        ms = {it: ps[it][CHUNK:] for it in items}
    tinvs = {it: tinvs[it] + _bdot(tinvs[it], bd(ms[it])) for it in items}
    x2s = {(u, p): _bdot(jnp.concatenate([jnp.where(strict, gs[(u, p)][:CHUNK, PAIR:], 0.0),
                                          jnp.where(incl, gs[(u, p)][CHUNK:, PAIR:], 0.0)], axis=0),
                         bd(subs[u]["v"][:, sls[p]])) for (u, p) in items}
    row = lax.broadcasted_iota(jnp.int32, (PAIR, PAIR), 0)
    same_head = (row < HEAD) == (lax.broadcasted_iota(jnp.int32, (PAIR, PAIR), 1) < HEAD)
    s_ps = [st_ref[p] for p in range(npair)]
    for u in range(SCAN_SUB):
        sub = subs[u]
        x1s = [_bdot_nt(ars[(u, p)], s_ps[p]) for p in range(npair)]
        us = [-_bdot(tinvs[(u, p)], bd(x1s[p][:CHUNK] + x2s[(u, p)][:CHUNK])) for p in range(npair)]
        outs = [x1s[p][CHUNK:] + x2s[(u, p)][CHUNK:]
                + _bdot(jnp.where(incl, gs[(u, p)][CHUNK:, :PAIR], 0.0), bd(us[p])) for p in range(npair)]
        nxt = []
        for p in range(npair):
            uv = jnp.concatenate([us[p], sub["v"][:, sls[p]]], axis=0).astype(BF16)
            bk = jnp.concatenate([sub["bh"][:, sls[p]], sub["kh"][:, sls[p]]], axis=0).astype(BF16)
            upd = lax.dot_general(uv, bk, (((0,), (0,)), ((), ())), preferred_element_type=F32)
            nxt.append(s_ps[p] * sub["g_tot"][:, sls[p]] + jnp.where(same_head, upd, 0.0))
        s_ps = nxt
        y_ref[sub["rows"], :] = jnp.concatenate(outs, axis=1)
    for p in range(npair):
        st_ref[p] = s_ps[p]


def _rwkv_scan(r, k, v, kk, lw, a, k_a, batch, seq):
    nc = seq // (CHUNK * SCAN_SUB)
    ns = D_RWKV // SLAB
    r4 = r.reshape(batch, seq, D_RWKV)
    k4 = k.reshape(batch, seq, D_RWKV)
    v4 = v.reshape(batch, seq, D_RWKV)
    kk4 = kk.reshape(batch, seq, D_RWKV)
    lw5 = lw.reshape(2, batch, seq, D_RWKV)
    a5 = a.reshape(2, batch, seq, D_RWKV)

    def tchunk(d, c):
        return c + d * (nc - 1 - 2 * c)

    rows = CHUNK * SCAN_SUB
    shared = pl.BlockSpec((None, rows, SLAB), lambda d, b, s, c: (b, tchunk(d, c), s))
    perdir = pl.BlockSpec((None, None, rows, SLAB), lambda d, b, s, c: (d, b, tchunk(d, c), s))
    y = pl.pallas_call(
        _scan_kernel,
        grid=(2, batch, ns, nc),
        in_specs=[shared, shared, shared, shared, perdir, perdir,
                  pl.BlockSpec((1, SLAB), lambda d, b, s, c: (0, s))],
        out_specs=perdir,
        out_shape=jax.ShapeDtypeStruct((2, batch, seq, D_RWKV), F32),
        scratch_shapes=[pltpu.VMEM((SLAB // PAIR, PAIR, PAIR), F32)],
        compiler_params=_cparams(("parallel", "parallel", "parallel", "arbitrary")),
        name="rwkv_scan",
    )(r4, k4, v4, kk4, lw5, a5, k_a)
    return y.reshape(2, batch * seq, D_RWKV)


def _post_kernel(y_ref, r_ref, k_ref, v_ref, a_ref, g_ref, ka_ref, rk_ref, lnw_ref, lnb_ref,
                 e_ref, et_ref, o_ref):
    y = y_ref[0] + y_ref[1]
    mu = _head_sum_bcast(y, e_ref, et_ref) * (1.0 / HEAD)
    yc = y - mu
    var = _head_sum_bcast(yc * yc, e_ref, et_ref) * (1.0 / HEAD)
    yn = yc * lax.rsqrt(var + GN_EPS) * lnw_ref[...] + lnb_ref[...]
    ka = ka_ref[...]
    k = k_ref[...].astype(F32)
    kd_sum = (k * (1.0 + (a_ref[0].astype(F32) - 1.0) * ka)
              + k * (1.0 + (a_ref[1].astype(F32) - 1.0) * ka))
    bonus = (_head_sum_bcast(r_ref[...].astype(F32) * kd_sum * rk_ref[...], e_ref, et_ref)
             * v_ref[...].astype(F32))
    o_ref[...] = ((yn + bonus) * g_ref[...].astype(F32)).astype(o_ref.dtype)


def _rwkv_post(y, r, k, v, a, g, k_a, r_k, ln_w, ln_b, e_mat, et_mat, tm=256):
    t, c = r.shape
    row = pl.BlockSpec((tm, c), lambda i: (i, 0))
    row2 = pl.BlockSpec((2, tm, c), lambda i: (0, i, 0))
    par = pl.BlockSpec((1, c), lambda i: (0, 0))
    return pl.pallas_call(
        _post_kernel,
        grid=(t // tm,),
        in_specs=[row2, row, row, row, row2, row, par, par, par, par,
                  pl.BlockSpec((c, 128), lambda i: (0, 0)), pl.BlockSpec((128, c), lambda i: (0, 0))],
        out_specs=row,
        out_shape=jax.ShapeDtypeStruct((t, c), BF16),
        compiler_params=_cparams(("parallel",)),
        name="rwkv_post",
    )(y, r, k, v, a, g, k_a, r_k, ln_w, ln_b, e_mat, et_mat)


def _s5_tables(lam_re, lam_im, log_step, b_re, b_im, c_re, c_im, d_skip, n_levels):
    L, P, CH, G, GB = S5_L, S5_STATE, S5_CH, S5_GROUPS, S5_GB
    NB = G // GB
    hi = lax.Precision.HIGHEST
    lam = lax.complex(lam_re.astype(F32), lam_im.astype(F32))
    dt = jnp.exp(log_step.astype(F32))[..., None]
    lam_dt = lam * dt
    lam_bar = jnp.exp(lam_dt)
    b = lax.complex(b_re.astype(F32), b_im.astype(F32))
    b_bar = ((lam_bar - 1.0) / lam)[..., None] * b
    c = lax.complex(c_re.astype(F32), c_im.astype(F32))
    taus = jnp.arange(L + 1, dtype=F32)
    pows = jnp.exp(lam_dt[:, :, None, :] * taus[None, None, :, None])
    kern = jnp.real(jnp.einsum('dgop,dgtp,dgpi->dgtoi', c, pows[:, :, :L], b_bar, precision=hi))
    j = jnp.arange(L)[:, None]
    t = jnp.arange(L)[None, :]
    lag = jnp.arange(L)[None, None, :]
    sel_f = ((t - j)[:, :, None] == lag).astype(F32)
    sel_b = ((j - t)[:, :, None] == lag).astype(F32)
    skip = (jnp.eye(L)[None, :, None, :, None] * jnp.eye(CH)[None, None, :, None, :]
            * d_skip.astype(F32).reshape(G, 1, CH, 1, 1))
    tsmall = (jnp.einsum('jtl,gloi->gjito', sel_f, kern[0], precision=hi)
              + jnp.einsum('jtl,gloi->gjito', sel_b, kern[1], precision=hi) + skip)
    def by_block(x, lead):
        x = x.reshape((NB, GB) + x.shape[1:])
        perm = (0,) + tuple(range(2, 2 + lead)) + (1,) + tuple(range(2 + lead, x.ndim))
        return x.transpose(perm)

    a_t = by_block(tsmall.reshape(G, L, CH, L * CH), 1).reshape(NB, L * GB * CH, L * CH)
    desc = jnp.exp(lam_dt[:, :, None, :] * (L - taus[:L])[None, None, :, None])
    desc1 = jnp.exp(lam_dt[0][:, None, :] * (L - 1 - taus[:L])[None, :, None])
    bb_t = b_bar.transpose(0, 1, 3, 2)
    zf = desc1[:, :, None, :] * bb_t[0][:, None]
    zb = pows[1][:, :L][:, :, None, :] * bb_t[1][:, None]
    wzs = jnp.stack([jnp.real(zf), jnp.imag(zf), jnp.real(zb), jnp.imag(zb)], axis=3)
    a_z = by_block(wzs.reshape(G, L, CH, 4 * P), 1).reshape(NB, L * GB * CH, 4 * P)
    c_t = c.transpose(0, 1, 3, 2)
    yf = c_t[0][:, :, None, :] * pows[0][:, 1:].transpose(0, 2, 1)[:, :, :, None]
    yb = c_t[1][:, :, None, :] * desc[1].transpose(0, 2, 1)[:, :, :, None]
    wys = jnp.stack([jnp.real(yf), -jnp.imag(yf), jnp.real(yb), -jnp.imag(yb)], axis=1)
    a_y = by_block(wys.reshape(G, 4, P, L * CH), 1).reshape(NB, 4 * GB * P, L * CH)
    steps = L * (2.0 ** jnp.arange(n_levels, dtype=F32))
    lp = jnp.exp(lam_dt[:, :, None, :] * steps[None, None, :, None])
    m = jnp.stack([jnp.real(lp[0]), jnp.imag(lp[0]), jnp.real(lp[1]), jnp.imag(lp[1])], axis=0)
    mult = m.reshape(4, NB, GB, n_levels, P).transpose(1, 3, 0, 2, 4).reshape(NB, n_levels, 4, GB * P)
    return a_t.astype(BF16), a_z.astype(BF16), a_y.astype(BF16), mult


def _group_of(shape, dim, width):
    return (lax.broadcasted_iota(jnp.int32, shape, dim) // width) % S5_GB


def _s5_kernel(n_levels, x_ref, at_ref, az_ref, ay_ref, rt_ref, rz_ref, mult_ref, y_ref, t_s, wz_s, wy_s):
    nck = x_ref.shape[0] // S5_L
    lanes = x_ref.shape[1]
    w = S5_GB * S5_STATE

    @pl.when(pl.program_id(1) == 0)
    def _():
        t = jnp.dot(at_ref[...], rt_ref[...], preferred_element_type=F32)
        keep = _group_of(t.shape, 0, S5_CH) == _group_of(t.shape, 1, S5_CH)
        t_s[...] = jnp.where(keep, t, 0.0).astype(BF16)
        z = jnp.dot(az_ref[...], rz_ref[...], preferred_element_type=F32)
        keep = _group_of(z.shape, 0, S5_CH) == _group_of(z.shape, 1, S5_STATE)
        wz_s[...] = jnp.where(keep, z, 0.0).astype(BF16)
        yy = jnp.dot(ay_ref[...], rt_ref[...], preferred_element_type=F32)
        keep = _group_of(yy.shape, 0, S5_STATE) == _group_of(yy.shape, 1, S5_CH)
        wy_s[...] = jnp.where(keep, yy, 0.0).astype(BF16)

    xcat = jnp.concatenate([x_ref[pl.ds(tl, nck, stride=S5_L), :].astype(BF16) for tl in range(S5_L)],
                           axis=1)
    z = jnp.dot(xcat, wz_s[...], preferred_element_type=F32)
    fre, fim, bre, bim = z[:, :w], z[:, w:2 * w], z[:, 2 * w:3 * w], z[:, 3 * w:]
    cidx = lax.broadcasted_iota(jnp.int32, (nck, w), 0)
    for i in range(n_levels):
        sh = 1 << i
        lfr, lfi = mult_ref[i, 0:1, :], mult_ref[i, 1:2, :]
        lbr, lbi = mult_ref[i, 2:3, :], mult_ref[i, 3:4, :]
        keep = cidx >= sh
        sr = jnp.where(keep, pltpu.roll(fre, sh, 0), 0.0)
        si = jnp.where(keep, pltpu.roll(fim, sh, 0), 0.0)
        fre, fim = fre + sr * lfr - si * lfi, fim + sr * lfi + si * lfr
        keep = cidx < nck - sh
        sr = jnp.where(keep, pltpu.roll(bre, nck - sh, 0), 0.0)
        si = jnp.where(keep, pltpu.roll(bim, nck - sh, 0), 0.0)
        bre, bim = bre + sr * lbr - si * lbi, bim + sr * lbi + si * lbr
    has_prev = cidx >= 1
    has_next = cidx < nck - 1
    xin = jnp.concatenate([jnp.where(has_prev, pltpu.roll(fre, 1, 0), 0.0),
                           jnp.where(has_prev, pltpu.roll(fim, 1, 0), 0.0),
                           jnp.where(has_next, pltpu.roll(bre, nck - 1, 0), 0.0),
                           jnp.where(has_next, pltpu.roll(bim, nck - 1, 0), 0.0)], axis=1).astype(BF16)
    y = jnp.dot(xcat, t_s[...], preferred_element_type=F32)
    y = y + jnp.dot(xin, wy_s[...], preferred_element_type=F32)
    for tl in range(S5_L):
        y_ref[pl.ds(tl, nck, stride=S5_L), :] = y[:, tl * lanes:(tl + 1) * lanes]


def _s5_mix(p_main, col0, a_t, a_z, a_y, mult, batch, seq, n_levels):
    nb = a_t.shape[0]
    lanes = S5_GB * S5_CH
    kw = S5_L * lanes
    sw = 4 * S5_GB * S5_STATE
    r_t = (jnp.arange(S5_L * S5_CH)[:, None] ==
           (jnp.arange(kw)[None, :] // lanes) * S5_CH + jnp.arange(kw)[None, :] % S5_CH).astype(BF16)
    r_z = (jnp.arange(4 * S5_STATE)[:, None] ==
           (jnp.arange(sw)[None, :] // (S5_GB * S5_STATE)) * S5_STATE + jnp.arange(sw)[None, :] % S5_STATE
           ).astype(BF16)
    return pl.pallas_call(
        functools.partial(_s5_kernel, n_levels),
        grid=(nb, batch),
        in_specs=[pl.BlockSpec((seq, lanes), lambda j, b: (b, col0 + j)),
                  pl.BlockSpec((None, kw, S5_L * S5_CH), lambda j, b: (j, 0, 0)),
                  pl.BlockSpec((None, kw, 4 * S5_STATE), lambda j, b: (j, 0, 0)),
                  pl.BlockSpec((None, sw, S5_L * S5_CH), lambda j, b: (j, 0, 0)),
                  pl.BlockSpec((S5_L * S5_CH, kw), lambda j, b: (0, 0)),
                  pl.BlockSpec((4 * S5_STATE, sw), lambda j, b: (0, 0)),
                  pl.BlockSpec((None, n_levels, 4, S5_GB * S5_STATE), lambda j, b: (j, 0, 0, 0))],
        out_specs=pl.BlockSpec((seq, lanes), lambda j, b: (b, j)),
        out_shape=jax.ShapeDtypeStruct((batch * seq, D_S5), F32),
        scratch_shapes=[pltpu.VMEM((kw, kw), BF16), pltpu.VMEM((kw, sw), BF16), pltpu.VMEM((sw, kw), BF16)],
        compiler_params=_cparams(("parallel", "arbitrary")),
        name="s5_mix",
    )(p_main, a_t, a_z, a_y, r_t, r_z, mult)


def _glu_kernel(y_ref, w_ref, b_ref, o_ref):
    y = y_ref[...]
    z = jnp.dot(jax.nn.gelu(y).astype(BF16), w_ref[...], preferred_element_type=F32) + b_ref[...]
    o_ref[...] = (y * jax.nn.sigmoid(z)).astype(o_ref.dtype)


def _glu(y, w, b, tm=1024):
    t, c = y.shape
    tm = min(tm, t)
    return pl.pallas_call(
        _glu_kernel,
        grid=(t // tm,),
        in_specs=[pl.BlockSpec((tm, c), lambda i: (i, 0)), pl.BlockSpec((c, c), lambda i: (0, 0)),
                  pl.BlockSpec((1, c), lambda i: (0, 0))],
        out_specs=pl.BlockSpec((tm, c), lambda i: (i, 0)),
        out_shape=jax.ShapeDtypeStruct((t, c), BF16),
        compiler_params=_cparams(("parallel",)),
        name="s5_glu",
    )(y, w, b)


def _wqk_kernel(wq_ref, k_ref, o_ref):
    acc = lax.dot_general(wq_ref[...], k_ref[...], (((1,), (1,)), ((), ())), preferred_element_type=F32)
    o_ref[...] = (acc * ATTN_SCALE).astype(o_ref.dtype)


def _vwo_kernel(v_ref, wo_ref, o_ref):
    o_ref[...] = jnp.dot(v_ref[...], wo_ref[...], preferred_element_type=F32).astype(o_ref.dtype)


def _attn_kernel(mem, x_ref, wqk_ref, vwo_ref, o_ref, p_ref):
    j = pl.program_id(2)
    tn = o_ref.shape[1]

    @pl.when(j == 0)
    def _():
        x = x_ref[...]
        rstd = lax.rsqrt(jnp.mean(x * x, axis=-1, keepdims=True) + RMS_EPS)
        s = jnp.dot(x.astype(BF16), wqk_ref[...], preferred_element_type=F32) * rstd
        for h in range(ATTN_HEADS):
            sh = s[:, h * mem:(h + 1) * mem]
            sh = sh - jnp.max(sh, axis=-1, keepdims=True)
            p = jnp.exp(sh)
            p_ref[:, h * mem:(h + 1) * mem] = (p / jnp.sum(p, axis=-1, keepdims=True)).astype(BF16)

    res = x_ref[:, pl.ds(pl.multiple_of(j * tn, tn), tn)]
    o_ref[...] = jnp.dot(p_ref[...], vwo_ref[...], preferred_element_type=F32) + res


def _cross_attention(x, kv, w_q, w_o, batch, seq, mem, tm=512, tn=1024):
    d = D_MODEL
    hm = ATTN_HEADS * mem
    tm = min(tm, seq)
    wqk = pl.pallas_call(
        _wqk_kernel,
        grid=(batch, ATTN_HEADS),
        in_specs=[pl.BlockSpec((d, ATTN_HEAD_DIM), lambda b, h: (0, h)),
                  pl.BlockSpec((mem, ATTN_HEAD_DIM), lambda b, h: (b, h))],
        out_specs=pl.BlockSpec((None, d, mem), lambda b, h: (b, 0, h)),
        out_shape=jax.ShapeDtypeStruct((batch, d, hm), BF16),
        compiler_params=_cparams(("parallel", "parallel")),
        name="attn_wqk",
    )(w_q, kv)
    vwo = pl.pallas_call(
        _vwo_kernel,
        grid=(batch, ATTN_HEADS),
        in_specs=[pl.BlockSpec((mem, ATTN_HEAD_DIM), lambda b, h: (b, ATTN_HEADS + h)),
                  pl.BlockSpec((ATTN_HEAD_DIM, d), lambda b, h: (h, 0))],
        out_specs=pl.BlockSpec((None, mem, d), lambda b, h: (b, h, 0)),
        out_shape=jax.ShapeDtypeStruct((batch, hm, d), BF16),
        compiler_params=_cparams(("parallel", "parallel")),
        name="attn_vwo",
    )(kv, w_o)
    out = pl.pallas_call(
        functools.partial(_attn_kernel, mem),
        grid=(batch, seq // tm, d // tn),
        in_specs=[pl.BlockSpec((None, tm, d), lambda b, i, j: (b, i, 0)),
                  pl.BlockSpec((None, d, hm), lambda b, i, j: (b, 0, 0)),
                  pl.BlockSpec((None, hm, tn), lambda b, i, j: (b, 0, j))],
        out_specs=pl.BlockSpec((None, tm, tn), lambda b, i, j: (b, i, j)),
        out_shape=jax.ShapeDtypeStruct((batch, seq, d), F32),
        scratch_shapes=[pltpu.VMEM((tm, hm), BF16)],
        compiler_params=_cparams(("parallel", "parallel", "arbitrary")),
        name="cross_attn",
    )(x.reshape(batch, seq, d), wqk, vwo)
    return out.reshape(batch * seq, d)


def _router_kernel(x_ref, g_ref, whi_ref, wlo_ref, b_ref, xn_ref, info_ref, cnt_ref, carry_ref):
    i = pl.program_id(0)

    @pl.when(i == 0)
    def _():
        carry_ref[...] = jnp.zeros_like(carry_ref)

    x = x_ref[...]
    tm = x.shape[0]
    ms = jnp.mean(x * x, axis=-1, keepdims=True)
    xn = x * lax.rsqrt(ms + RMS_EPS) * g_ref[...]
    half = xn.shape[1] // 2
    xn_ref[...] = _pack_bf16_pair(xn[:, :half], xn[:, half:])
    xh, xl = _split2(xn)
    whi = whi_ref[...]
    logits = (jnp.dot(xh, whi, preferred_element_type=F32) + jnp.dot(xl, whi, preferred_element_type=F32)
              + jnp.dot(xh, wlo_ref[...], preferred_element_type=F32)) + b_ref[...]
    li = lax.broadcasted_iota(jnp.int32, logits.shape, 1)
    neg = jnp.float32(-jnp.inf)
    is_g = li < N_GROUPS
    gl = jnp.where(is_g, logits, neg)
    gm = jnp.max(gl, axis=-1, keepdims=True)
    gi = jnp.min(jnp.where(is_g & (gl == gm), li, 128), axis=-1, keepdims=True)
    gp = 1.0 / jnp.sum(jnp.where(is_g, jnp.exp(gl - gm), 0.0), axis=-1, keepdims=True)
    lo_lane = N_GROUPS + EPG * gi
    sel = (li >= lo_lane) & (li < lo_lane + EPG)
    l1 = jnp.where(sel, logits, neg)
    e1 = jnp.max(l1, axis=-1, keepdims=True)
    i1 = jnp.min(jnp.where(sel & (l1 == e1), li, 128), axis=-1, keepdims=True)
    sel2 = sel & (li != i1)
    l2 = jnp.where(sel2, logits, neg)
    e2 = jnp.max(l2, axis=-1, keepdims=True)
    i2 = jnp.min(jnp.where(sel2 & (l2 == e2), li, 128), axis=-1, keepdims=True)
    ex = jnp.exp(e2 - e1)
    w1 = gp / (1.0 + ex)
    w2 = gp * ex / (1.0 + ex)
    x1 = i1 - N_GROUPS
    x2 = i2 - N_GROUPS
    oh1 = li == x1
    oh2 = li == x2
    oh = (oh1 | oh2).astype(BF16)
    ri = lax.broadcasted_iota(jnp.int32, (tm, tm), 0)
    ci = lax.broadcasted_iota(jnp.int32, (tm, tm), 1)
    before = jnp.dot((ci < ri).astype(BF16), oh, preferred_element_type=F32) + carry_ref[0:1, :]
    r1 = jnp.sum(jnp.where(oh1, before, 0.0), axis=-1, keepdims=True)
    r2 = jnp.sum(jnp.where(oh2, before, 0.0), axis=-1, keepdims=True)
    new_carry = carry_ref[0:1, :] + jnp.sum(oh.astype(F32), axis=0, keepdims=True)
    carry_ref[...] = jnp.broadcast_to(new_carry, carry_ref.shape)
    cnt_ref[...] = jnp.broadcast_to(new_carry, cnt_ref.shape)
    info = jnp.where(li == 0, x1.astype(F32), 0.0)
    info = jnp.where(li == 1, x2.astype(F32), info)
    info = jnp.where(li == 2, w1, info)
    info = jnp.where(li == 3, w2, info)
    info = jnp.where(li == 4, r1, info)
    info = jnp.where(li == 5, r2, info)
    info_ref[...] = info


def _router(x, gain, w_hi, w_lo, bias, tm=256):
    t, d = x.shape
    sd = jax.ShapeDtypeStruct
    return pl.pallas_call(
        _router_kernel,
        grid=(t // tm,),
        in_specs=[pl.BlockSpec((tm, d), lambda i: (i, 0)), pl.BlockSpec((1, d), lambda i: (0, 0)),
                  pl.BlockSpec((d, 128), lambda i: (0, 0)), pl.BlockSpec((d, 128), lambda i: (0, 0)),
                  pl.BlockSpec((1, 128), lambda i: (0, 0))],
        out_specs=[pl.BlockSpec((tm, d // 2), lambda i: (i, 0)), pl.BlockSpec((tm, 128), lambda i: (i, 0)),
                   pl.BlockSpec((8, 128), lambda i: (0, 0))],
        out_shape=[sd((t, d // 2), jnp.uint32), sd((t, 128), F32), sd((8, 128), F32)],
        scratch_shapes=[pltpu.VMEM((8, 128), F32)],
        compiler_params=_cparams(("arbitrary",)),
        name="moe_router",
    )(x, gain.reshape(1, d), w_hi, w_lo, bias)


def _gather_rows(src_hbm, idx_ref, dst_ref, sem, n_rows):
    def body(j, carry):
        pltpu.make_async_copy(src_hbm.at[pl.ds(idx_ref[0, j], 1), :], dst_ref.at[pl.ds(j, 1), :], sem).start()
        return carry
    lax.fori_loop(0, n_rows, body, 0, unroll=8)


def _wait_rows(src_hbm, dst_ref, sem, n_rows):
    pltpu.make_async_copy(src_hbm.at[pl.ds(0, n_rows), :], dst_ref, sem).wait()


def _issue_rows(src_hbm, idx_ref, dst_ref, sem, lo, hi):
    for j in range(lo, hi):
        pltpu.make_async_copy(src_hbm.at[pl.ds(idx_ref[0, j], 1), :], dst_ref.at[pl.ds(j, 1), :],
                              sem).start(priority=j % 2)


def _expert_up_kernel(be_ref, nused_ref, tok_ref, tokn_ref, x_hbm, wg_ref, wu_ref, h_ref, xbuf, sem):
    i = pl.program_id(0)
    nb = pl.num_programs(0)
    n_used = nused_ref[0]
    slot = i % 2

    @pl.when(i == 0)
    def _():
        _gather_rows(x_hbm, tok_ref, xbuf.at[0], sem.at[0], MOE_BLK)

    @pl.when(i < n_used)
    def _():
        _wait_rows(x_hbm, xbuf.at[slot], sem.at[slot], MOE_BLK)
        half = xbuf.shape[2]
        nk = 2 * half // EXPERT_KC
        per = MOE_BLK // nk
        hg = jnp.zeros((MOE_BLK, D_EXPERT), F32)
        hu = jnp.zeros((MOE_BLK, D_EXPERT), F32)
        for kc in range(nk):
            _issue_rows(x_hbm, tokn_ref, xbuf.at[1 - slot], sem.at[1 - slot], kc * per, (kc + 1) * per)
            ks = slice(kc * EXPERT_KC, (kc + 1) * EXPERT_KC)
            wc = (kc * EXPERT_KC) % half
            xb = _unpack_bf16_pair(xbuf[slot, :, wc:wc + EXPERT_KC])[(kc * EXPERT_KC) // half].astype(BF16)
            hg = hg + jnp.dot(xb, wg_ref[ks, :].astype(BF16), preferred_element_type=F32)
            hu = hu + jnp.dot(xb, wu_ref[ks, :].astype(BF16), preferred_element_type=F32)
        h_ref[...] = (jax.nn.silu(hg) * hu).astype(h_ref.dtype)

        @pl.when(i == nb - 1)
        def _():
            _wait_rows(x_hbm, xbuf.at[1 - slot], sem.at[1 - slot], MOE_BLK)

    @pl.when(i >= n_used)
    def _():
        @pl.when(i == n_used)
        def _():
            _wait_rows(x_hbm, xbuf.at[slot], sem.at[slot], MOE_BLK)

        h_ref[...] = jnp.zeros_like(h_ref)


def _expert_down_kernel(be_ref, nused_ref, h_ref, wd_ref, y_ref):
    i = pl.program_id(0)
    n_used = nused_ref[0]

    @pl.when(i < n_used)
    def _():
        h = h_ref[...]
        half = y_ref.shape[1]
        for nc in range(half // EXPERT_KC):
            ns = slice(nc * EXPERT_KC, (nc + 1) * EXPERT_KC)
            nh = slice(half + nc * EXPERT_KC, half + (nc + 1) * EXPERT_KC)
            lo = jnp.dot(h, wd_ref[:, ns].astype(BF16), preferred_element_type=F32)
            hi = jnp.dot(h, wd_ref[:, nh].astype(BF16), preferred_element_type=F32)
            y_ref[:, ns] = _pack_bf16_pair(lo, hi)

    @pl.when(i >= n_used)
    def _():
        y_ref[...] = jnp.zeros_like(y_ref)


def _experts(block_expert, n_used, tok3, xn, wg, wu, wd):
    nb = tok3.shape[0]
    d = 2 * xn.shape[1]
    tok_spec = pl.BlockSpec((None, 1, MOE_BLK), lambda i, be, nu: (i, 0, 0), memory_space=pltpu.SMEM)
    tokn_spec = pl.BlockSpec((None, 1, MOE_BLK), lambda i, be, nu: (jnp.minimum(i + 1, nb - 1), 0, 0),
                             memory_space=pltpu.SMEM)
    h_buf = pl.pallas_call(
        _expert_up_kernel,
        grid_spec=pltpu.PrefetchScalarGridSpec(
            num_scalar_prefetch=2,
            grid=(nb,),
            in_specs=[tok_spec, tokn_spec, pl.BlockSpec(memory_space=pl.ANY),
                      pl.BlockSpec((None, d, D_EXPERT), lambda i, be, nu: (be[i], 0, 0)),
                      pl.BlockSpec((None, d, D_EXPERT), lambda i, be, nu: (be[i], 0, 0))],
            out_specs=pl.BlockSpec((MOE_BLK, D_EXPERT), lambda i, be, nu: (i, 0)),
            scratch_shapes=[pltpu.VMEM((2, MOE_BLK, d // 2), jnp.uint32), pltpu.SemaphoreType.DMA((2,))],
        ),
        out_shape=jax.ShapeDtypeStruct((nb * MOE_BLK, D_EXPERT), BF16),
        compiler_params=_cparams(("arbitrary",)),
        name="moe_up",
    )(block_expert, n_used, tok3, tok3, xn, wg, wu)
    return pl.pallas_call(
        _expert_down_kernel,
        grid_spec=pltpu.PrefetchScalarGridSpec(
            num_scalar_prefetch=2,
            grid=(nb,),
            in_specs=[pl.BlockSpec((MOE_BLK, D_EXPERT), lambda i, be, nu: (i, 0)),
                      pl.BlockSpec((None, D_EXPERT, d), lambda i, be, nu: (be[i], 0, 0))],
            out_specs=pl.BlockSpec((MOE_BLK, d // 2), lambda i, be, nu: (i, 0)),
        ),
        out_shape=jax.ShapeDtypeStruct((nb * MOE_BLK, d // 2), jnp.uint32),
        compiler_params=_cparams(("arbitrary",)),
        name="moe_down",
    )(block_expert, n_used, h_buf, wd)


def _combine_kernel(d1_ref, d1n_ref, d2_ref, d2n_ref, x_ref, info_ref, g_ref, y_hbm, o_ref, ybuf, sem):
    i = pl.program_id(0)
    n = pl.num_programs(0)
    slot = i % 2

    @pl.when(i == 0)
    def _():
        _gather_rows(y_hbm, d1_ref, ybuf.at[0, 0], sem.at[0], COMB_TM)
        _gather_rows(y_hbm, d2_ref, ybuf.at[0, 1], sem.at[0], COMB_TM)

    _wait_rows(y_hbm, ybuf.at[slot, 0], sem.at[slot], COMB_TM)
    _wait_rows(y_hbm, ybuf.at[slot, 1], sem.at[slot], COMB_TM)
    _issue_rows(y_hbm, d1n_ref, ybuf.at[1 - slot, 0], sem.at[1 - slot], 0, COMB_TM)
    _issue_rows(y_hbm, d2n_ref, ybuf.at[1 - slot, 1], sem.at[1 - slot], 0, COMB_TM)
    info = info_ref[...]
    w1 = info[:, 2:3]
    w2 = info[:, 3:4]
    half = ybuf.shape[3]
    lo1, hi1 = _unpack_bf16_pair(ybuf[slot, 0])
    lo2, hi2 = _unpack_bf16_pair(ybuf[slot, 1])
    xlo = x_ref[:, :half] + lo1 * w1 + lo2 * w2
    xhi = x_ref[:, half:] + hi1 * w1 + hi2 * w2
    ms = (jnp.sum(xlo * xlo, axis=-1, keepdims=True) + jnp.sum(xhi * xhi, axis=-1, keepdims=True)) / (2 * half)
    rstd = lax.rsqrt(ms + RMS_EPS)
    o_ref[:, :half] = xlo * rstd * g_ref[:, :half]
    o_ref[:, half:] = xhi * rstd * g_ref[:, half:]

    @pl.when(i == n - 1)
    def _():
        _wait_rows(y_hbm, ybuf.at[1 - slot, 0], sem.at[1 - slot], COMB_TM)
        _wait_rows(y_hbm, ybuf.at[1 - slot, 1], sem.at[1 - slot], COMB_TM)


def _combine(dest1, dest2, x, info, gain, y_buf):
    t, d = x.shape
    nt = t // COMB_TM
    d1 = dest1.reshape(nt, 1, COMB_TM)
    d2 = dest2.reshape(nt, 1, COMB_TM)
    cur = pl.BlockSpec((None, 1, COMB_TM), lambda i: (i, 0, 0), memory_space=pltpu.SMEM)
    nxt = pl.BlockSpec((None, 1, COMB_TM), lambda i: (jnp.minimum(i + 1, nt - 1), 0, 0),
                       memory_space=pltpu.SMEM)
    return pl.pallas_call(
        _combine_kernel,
        grid=(nt,),
        in_specs=[cur, nxt, cur, nxt,
                  pl.BlockSpec((COMB_TM, d), lambda i: (i, 0)),
                  pl.BlockSpec((COMB_TM, 128), lambda i: (i, 0)),
                  pl.BlockSpec((1, d), lambda i: (0, 0)),
                  pl.BlockSpec(memory_space=pl.ANY)],
        out_specs=pl.BlockSpec((COMB_TM, d), lambda i: (i, 0)),
        out_shape=jax.ShapeDtypeStruct((t, d), F32),
        scratch_shapes=[pltpu.VMEM((2, 2, COMB_TM, d // 2), jnp.uint32), pltpu.SemaphoreType.DMA((2,))],
        compiler_params=_cparams(("arbitrary",)),
        name="moe_combine",
    )(d1, d1, d2, d2, x, info, gain.reshape(1, d), y_buf)


def _head_matrices():
    lane = jnp.arange(D_RWKV) // HEAD
    e = (lane[:, None] == jnp.arange(128)[None, :]).astype(BF16)
    return e, e.T


def _pad_lora_rows(w):
    return jnp.pad(w, ((0, 0), (0, LORA_PAD - w.shape[1]), (0, 0)))


def _layer(x, mem, norm_mix, w_in, shift_taps, rwkv_w0, rwkv_w2, rwkv_a0, rwkv_a2, rwkv_g2, rwkv_k_k,
           rwkv_k_a, rwkv_r_k, rwkv_ln_w, rwkv_ln_b, s5_lam_re, s5_lam_im, s5_log_step, s5_b_re, s5_b_im,
           s5_c_re, s5_c_im, s5_d, s5_glu_w, s5_glu_b, w_out, norm_attn, norm_mem, w_q, w_k, w_v, w_o,
           norm_ffn, router_grp_w, router_grp_b, router_exp_w, router_exp_b, exp_w_gate, exp_w_up,
           exp_w_down, out_gain):
    batch, seq, d = x.shape
    t = batch * seq
    mem_n = mem.shape[1]
    xt = x.reshape(t, d)
    c3 = 3 * D_RWKV
    off_g = c3 + 4 * 96

    def pad_cols(w, lo, width):
        return jnp.pad(w[:, lo:lo + width], ((0, 0), (0, LORA_PAD - width)))

    w_all = jnp.concatenate([w_in[:, :c3], w_in[:, off_g + 256:]]
                            + [pad_cols(w_in, c3 + i * 96, 96) for i in range(4)]
                            + [w_in[:, off_g:off_g + 256], jnp.zeros((d, LORA_BLK - N_LORA), F32)],
                            axis=1).astype(BF16)
    taps_l = jnp.concatenate([pad_cols(shift_taps, c3 + i * 96, 96) for i in range(4)]
                             + [shift_taps[:, off_g:off_g + 256], jnp.zeros((3, LORA_BLK - N_LORA), F32)], axis=1)
    h = _rmsnorm(xt, norm_mix, BF16)
    p_all = _matmul(h, w_all, F32, 1024, LORA_BLK, name="w_in")

    e_mat, et_mat = _head_matrices()
    k_a = rwkv_k_a.reshape(1, D_RWKV)
    r, k, v, kk, lw, a, g = _rwkv_prep(
        p_all, seq, shift_taps[:, :D_RWKV], shift_taps[:, D_RWKV:2 * D_RWKV],
        shift_taps[:, 2 * D_RWKV:c3], taps_l, rwkv_w0, _pad_lora_rows(rwkv_w2).astype(BF16), rwkv_a0,
        _pad_lora_rows(rwkv_a2).astype(BF16), rwkv_g2.astype(BF16), rwkv_k_k.reshape(1, D_RWKV),
        e_mat, et_mat)
    y_scan = _rwkv_scan(r, k, v, kk, lw, a, k_a, batch, seq)
    y_rwkv = _rwkv_post(y_scan, r, k, v, a, g, k_a, rwkv_r_k.reshape(1, D_RWKV),
                        rwkv_ln_w.reshape(1, D_RWKV), rwkv_ln_b.reshape(1, D_RWKV), e_mat, et_mat)

    n_levels = int(math.log2(seq // S5_L))
    a_t, a_z, a_y, mult = _s5_tables(s5_lam_re, s5_lam_im, s5_log_step, s5_b_re, s5_b_im,
                                    s5_c_re, s5_c_im, s5_d, n_levels)
    y_s5 = _s5_mix(p_all, c3 // (S5_GB * S5_CH), a_t, a_z, a_y, mult, batch, seq, n_levels)
    y_glu = _glu(y_s5, s5_glu_w.astype(BF16), s5_glu_b.reshape(1, D_S5))

    x1 = _matmul2_res(y_rwkv, y_glu, w_out.astype(BF16), xt, 1024, 512)

    memn = _rmsnorm(mem.reshape(batch * mem_n, d), norm_mem, BF16)
    wkv = jnp.concatenate([w_k, w_v], axis=1).astype(BF16)
    kv = _matmul(memn, wkv, BF16, 1024, 512, name="w_kv")
    x2 = _cross_attention(x1, kv, (norm_attn[:, None] * w_q).astype(BF16), w_o.astype(BF16), batch, seq, mem_n)

    w_r = jnp.concatenate([router_grp_w, router_exp_w,
                           jnp.zeros((d, 128 - N_GROUPS - N_EXPERTS), F32)], axis=1)
    w_r_hi = w_r.astype(BF16)
    w_r_lo = (w_r - w_r_hi.astype(F32)).astype(BF16)
    b_r = jnp.concatenate([router_grp_b, router_exp_b,
                           jnp.zeros((128 - N_GROUPS - N_EXPERTS,), F32)]).reshape(1, 128)
    xn3, info, cnt = _router(x2, norm_ffn, w_r_hi, w_r_lo, b_r)
    eid = info[:, 0:2].astype(jnp.int32)
    rank = info[:, 4:6].astype(jnp.int32)
    counts = cnt[0, :N_EXPERTS].astype(jnp.int32)
    nblk = (counts + MOE_BLK - 1) // MOE_BLK
    bstart = jnp.cumsum(nblk) - nblk
    n_used = jnp.sum(nblk)
    first = jnp.sum(jnp.where(eid[:, :, None] == jnp.arange(N_EXPERTS)[None, None, :], bstart, 0), axis=-1)
    dest = first * MOE_BLK + rank
    nb = (t * TOP_K) // MOE_BLK + N_EXPERTS
    tok_buf = jnp.zeros((nb * MOE_BLK,), jnp.int32).at[dest.reshape(-1)].set(
        jnp.repeat(jnp.arange(t, dtype=jnp.int32), TOP_K))
    blk = jnp.arange(nb, dtype=jnp.int32)
    block_expert = jnp.sum(blk[:, None] >= (bstart + nblk)[None, :], axis=1).astype(jnp.int32)
    last_e = jnp.max(jnp.where(nblk > 0, jnp.arange(N_EXPERTS), 0)).astype(jnp.int32)
    block_expert = jnp.minimum(block_expert, last_e)
    y_buf = _experts(block_expert, n_used.reshape(1).astype(jnp.int32), tok_buf.reshape(nb, 1, MOE_BLK),
                     xn3, exp_w_gate, exp_w_up, exp_w_down)
    out = _combine(dest[:, 0], dest[:, 1], x2, info, out_gain, y_buf)
    return out.reshape(batch, seq, d)


def kernel(x, mem, norm_mix, w_in, shift_taps, rwkv_w0, rwkv_w2, rwkv_a0, rwkv_a2, rwkv_g2, rwkv_k_k, rwkv_k_a, rwkv_r_k, rwkv_ln_w, rwkv_ln_b, s5_lam_re, s5_lam_im, s5_log_step, s5_b_re, s5_b_im, s5_c_re, s5_c_im, s5_d, s5_glu_w, s5_glu_b, w_out, norm_attn, norm_mem, w_q, w_k, w_v, w_o, norm_ffn, router_grp_w, router_grp_b, router_exp_w, router_exp_b, exp_w_gate, exp_w_up, exp_w_down, norm_final):
    return _layer(x, mem, norm_mix[0], w_in[0], shift_taps[0], rwkv_w0[0], rwkv_w2[0], rwkv_a0[0],
                  rwkv_a2[0], rwkv_g2[0], rwkv_k_k[0], rwkv_k_a[0], rwkv_r_k[0], rwkv_ln_w[0],
                  rwkv_ln_b[0], s5_lam_re[0], s5_lam_im[0], s5_log_step[0], s5_b_re[0], s5_b_im[0],
                  s5_c_re[0], s5_c_im[0], s5_d[0], s5_glu_w[0], s5_glu_b[0], w_out[0], norm_attn[0],
                  norm_mem[0], w_q[0], w_k[0], w_v[0], w_o[0], norm_ffn[0], router_grp_w[0],
                  router_grp_b[0], router_exp_w[0], router_exp_b[0], exp_w_gate[0], exp_w_up[0],
                  exp_w_down[0], norm_final)
```
